```python
import jax
import jax.numpy as jnp
from jax import lax
import numpy as np

D_MODEL = 2048
BATCH = 1
SEQ = 8192
DEPTH = 4

A_HEADS = 8
A_KV_GROUPS = 2
A_HPG = A_HEADS // A_KV_GROUPS
HEAD_DIM = 128
A_WIDTH = A_HEADS * HEAD_DIM
A_KV_WIDTH = A_KV_GROUPS * HEAD_DIM
CMP_BLOCK = 32
CMP_STRIDE = 16
SEL_BLOCK = 64
SEL_TOPK = 16
WINDOW = 512
Q_BLOCK = 128
ROPE_THETA = 10000.0
HG_HEADS = 8
HG_DIM = 128
HG_WIDTH = HG_HEADS * HG_DIM
HG_CHUNK = 64
D_FF = 5632
CONV_WIDTH = 3
NORM_EPS = 1e-6
N_ADA = 6
IN_SIZES = (A_WIDTH, 6 * A_KV_WIDTH, 3 * A_HEADS, HG_WIDTH, HG_WIDTH, HG_WIDTH, HG_WIDTH, 2 * D_MODEL)
IN_COLS = A_WIDTH + 6 * A_KV_WIDTH + 3 * A_HEADS + 4 * HG_WIDTH + 2 * D_MODEL

kernel_name = 'hybrid_nsa_hgrn2_convffn_adaln_trunk'


def split_cols(z, sizes):
    out = []
    off = 0
    for s in sizes:
        out.append(z[..., off:off + s])
        off += s
    return out


def rms_norm(t, g):
    tf = t.astype(jnp.float32)
    y = tf * lax.rsqrt(jnp.mean(tf * tf, axis=-1, keepdims=True) + NORM_EPS)
    return (y * g.astype(jnp.float32)).astype(t.dtype)


def rope_tables(positions):
    inv_freq = 1.0 / (ROPE_THETA ** (jnp.arange(0, HEAD_DIM, 2, dtype=jnp.float32) / HEAD_DIM))
    ang = positions.astype(jnp.float32)[..., None] * inv_freq
    return jnp.cos(ang)[:, :, None, :], jnp.sin(ang)[:, :, None, :]


def apply_rope(t, cos, sin):
    tf = t.astype(jnp.float32)
    t1, t2 = jnp.split(tf, 2, axis=-1)
    return jnp.concatenate([t1 * cos - t2 * sin, t2 * cos + t1 * sin], axis=-1).astype(t.dtype)


def compress_blocks(t, pos_emb, w):
    B, S, G, hd = t.shape
    n = (S - CMP_BLOCK) // CMP_STRIDE + 1
    idx = jnp.arange(n)[:, None] * CMP_STRIDE + jnp.arange(CMP_BLOCK)[None, :]
    blocks = t[:, idx] + pos_emb[None, None, :, None, :]
    blocks = blocks.transpose(0, 1, 3, 2, 4).reshape(B, n, G, CMP_BLOCK * hd)
    return blocks @ w


def masked_softmax(s, mask):
    s = jnp.where(mask, s, -jnp.inf)
    m = jnp.max(s, axis=-1, keepdims=True)
    m = jnp.where(jnp.isfinite(m), m, 0.0)
    p = jnp.exp(s - m)
    den = jnp.sum(p, axis=-1, keepdims=True)
    return p / jnp.where(den > 0, den, 1.0)


def nsa_sequence(q, k_cmp, v_cmp, k_sel, v_sel, k_win, v_win, gates):
    S = q.shape[0]
    n_cmp = k_cmp.shape[0]
    n_sel = S // SEL_BLOCK
    topk = min(SEL_TOPK, n_sel)
    scale = HEAD_DIM ** -0.5
    cmp_start = jnp.arange(n_cmp) * CMP_STRIDE
    cmp_last = cmp_start + CMP_BLOCK - 1
    sel_start = jnp.arange(n_sel) * SEL_BLOCK
    overlap = jnp.maximum(
        jnp.minimum(cmp_start[:, None] + CMP_BLOCK, sel_start[None, :] + SEL_BLOCK)
        - jnp.maximum(cmp_start[:, None], sel_start[None, :]), 0).astype(jnp.float32) / CMP_BLOCK
    ks_blk = k_sel.reshape(n_sel, SEL_BLOCK, A_KV_GROUPS, HEAD_DIM).transpose(2, 0, 1, 3)
    vs_blk = v_sel.reshape(n_sel, SEL_BLOCK, A_KV_GROUPS, HEAD_DIM).transpose(2, 0, 1, 3)
    kw_pad = jnp.pad(k_win, ((WINDOW, 0), (0, 0), (0, 0)))
    vw_pad = jnp.pad(v_win, ((WINDOW, 0), (0, 0), (0, 0)))
    blk_ids = jnp.arange(n_sel)
    in_blk = jnp.arange(SEL_BLOCK)
    win_off = jnp.arange(Q_BLOCK + WINDOW) - WINDOW

    def one_block(bi):
        t0 = bi * Q_BLOCK
        tpos = t0 + jnp.arange(Q_BLOCK)
        qb = lax.dynamic_slice_in_dim(q, t0, Q_BLOCK, 0).reshape(Q_BLOCK, A_KV_GROUPS, A_HPG, HEAD_DIM) * scale
        gb = lax.dynamic_slice_in_dim(gates, t0, Q_BLOCK, 0).reshape(Q_BLOCK, A_KV_GROUPS, A_HPG, 3)
        s_c = jnp.einsum('qghd,ngd->ghqn', qb, k_cmp)
        p_c = masked_softmax(s_c, (cmp_last[None, :] <= tpos[:, None])[None, None])
        o_c = jnp.einsum('ghqn,ngd->qghd', p_c, v_cmp)
        imp = jnp.einsum('ghqn,nm->gqm', p_c, overlap)
        cur = tpos // SEL_BLOCK
        forced = (blk_ids[None] == 0) | (blk_ids[None] == cur[:, None]) | (blk_ids[None] == cur[:, None] - 1)
        valid = sel_start[None] <= tpos[:, None]
        imp = jnp.where(forced[None], jnp.inf, jnp.where(valid[None], imp, -jnp.inf))
        _, idx = lax.top_k(imp, topk)
        kg = jax.vmap(lambda kb, ix: kb[ix])(ks_blk, idx).reshape(A_KV_GROUPS, Q_BLOCK, topk * SEL_BLOCK, HEAD_DIM)
        vg = jax.vmap(lambda vb, ix: vb[ix])(vs_blk, idx).reshape(A_KV_GROUPS, Q_BLOCK, topk * SEL_BLOCK, HEAD_DIM)
        kpos = (idx[..., None] * SEL_BLOCK + in_blk).reshape(A_KV_GROUPS, Q_BLOCK, topk * SEL_BLOCK)
        s_s = jnp.einsum('qghd,gqkd->ghqk', qb, kg)
        p_s = masked_softmax(s_s, (kpos <= tpos[None, :, None])[:, None])
        o_s = jnp.einsum('ghqk,gqkd->qghd', p_s, vg)
        kwb = lax.dynamic_slice_in_dim(kw_pad, t0, Q_BLOCK + WINDOW, 0)
        vwb = lax.dynamic_slice_in_dim(vw_pad, t0, Q_BLOCK + WINDOW, 0)
        wpos = t0 + win_off
        dist = tpos[:, None] - wpos[None, :]
        mask_w = (dist >= 0) & (dist < WINDOW) & (wpos[None, :] >= 0)
        s_w = jnp.einsum('qghd,kgd->ghqk', qb, kwb)
        p_w = masked_softmax(s_w, mask_w[None, None])
        o_w = jnp.einsum('ghqk,kgd->qghd', p_w, vwb)
        o = gb[..., 0:1] * o_c + gb[..., 1:2] * o_s + gb[..., 2:3] * o_w
        return o.reshape(Q_BLOCK, A_WIDTH)

    out = lax.map(one_block, jnp.arange(S // Q_BLOCK))
    return out.reshape(S, A_WIDTH)


def hgrn2_recurrence(q, f_pre, v, lb):
    B, S, H, d = q.shape
    lb = lb.reshape(H, d)
    log_f = jnp.logaddexp(jnp.log(lb), jnp.log1p(-lb) + jax.nn.log_sigmoid(f_pre))
    k = -jnp.expm1(log_f)
    nc = S // HG_CHUNK

    def to_chunks(t):
        return t.reshape(B, nc, HG_CHUNK, H, t.shape[-1]).transpose(1, 0, 3, 2, 4)

    causal = jnp.tril(jnp.ones((HG_CHUNK, HG_CHUNK), dtype=bool))[:, :, None]

    def step(state, inp):
        qc, kc, vc, gc = inp
        b = jnp.cumsum(gc, axis=2)
        diff = b[:, :, :, None, :] - b[:, :, None, :, :]
        decay = jnp.where(causal, jnp.exp(jnp.where(causal, diff, 0.0)), 0.0)
        scores = jnp.einsum('bhtd,bhsd,bhtsd->bhts', qc, kc, decay)
        o = jnp.einsum('bhts,bhsv->bhtv', scores, vc) + jnp.einsum('bhtd,bhdv->bhtv', qc * jnp.exp(b), state)
        b_last = b[:, :, -1:, :]
        state = jnp.exp(b_last[:, :, 0, :])[..., None] * state + jnp.einsum('bhsd,bhsv->bhdv', kc * jnp.exp(b_last - b), vc)
        return state, o

    state0 = jnp.zeros((B, H, d, v.shape[-1]), jnp.float32)
    _, o = lax.scan(step, state0, (to_chunks(q), to_chunks(k), to_chunks(v), to_chunks(log_f)))
    return o.transpose(1, 0, 3, 2, 4).reshape(B, S, H, v.shape[-1])


def causal_depthwise_conv(u, w, b):
    S = u.shape[1]
    u_pad = jnp.pad(u, ((0, 0), (CONV_WIDTH - 1, 0), (0, 0)))
    out = b + w[0] * u_pad[:, 0:S]
    for j in range(1, CONV_WIDTH):
        out = out + w[j] * u_pad[:, j:j + S]
    return out


def setup_inputs(seed: int = 0) -> dict:
    key = jax.random.key(seed)
    k = jax.random.split(key, 24)

    def nrm(kk, shape, scale):
        return jax.random.normal(kk, shape, jnp.float32) * scale

    L = DEPTH
    D = D_MODEL
    return {
        'x': nrm(k[0], (BATCH, SEQ, D), 1.0),
        'c': nrm(k[1], (BATCH, D), 1.0),
        'positions': jnp.broadcast_to(jnp.arange(SEQ, dtype=jnp.int32), (BATCH, SEQ)),
        'w_ada': nrm(k[2], (L, D, N_ADA * D), 0.5 * D ** -0.5),
        'b_ada': nrm(k[3], (L, N_ADA * D), 0.02),
        'g_pre_mix': 1.0 + nrm(k[4], (L, D), 0.05),
        'w_in': nrm(k[5], (L, D, IN_COLS), D ** -0.5),
        'pe_kc': nrm(k[6], (L, CMP_BLOCK, HEAD_DIM), 0.1),
        'w_kc': nrm(k[7], (L, CMP_BLOCK * HEAD_DIM, HEAD_DIM), (CMP_BLOCK * HEAD_DIM) ** -0.5),
        'pe_vc': nrm(k[8], (L, CMP_BLOCK, HEAD_DIM), 0.1),
        'w_vc': nrm(k[9], (L, CMP_BLOCK * HEAD_DIM, HEAD_DIM), (CMP_BLOCK * HEAD_DIM) ** -0.5),
        'lb_logits': nrm(k[10], (L, HG_WIDTH), 1.0),
        'g_hg_norm': 1.0 + nrm(k[11], (L, HG_WIDTH), 0.05),
        'w_br_attn': nrm(k[12], (L, A_WIDTH, D), A_WIDTH ** -0.5),
        'w_br_hgrn': nrm(k[13], (L, HG_WIDTH, D), HG_WIDTH ** -0.5),
        'w_out': nrm(k[14], (L, D, D), D ** -0.5),
        'g_post_mix': 1.0 + nrm(k[15], (L, D), 0.05),
        'g_pre_ffn': 1.0 + nrm(k[16], (L, D), 0.05),
        'w_up': nrm(k[17], (L, D, 2 * D_FF), D ** -0.5),
        'conv_w': nrm(k[18], (L, CONV_WIDTH, 2 * D_FF), CONV_WIDTH ** -0.5),
        'conv_b': nrm(k[19], (L, 2 * D_FF), 0.02),
        'w_down': nrm(k[20], (L, D_FF, D), D_FF ** -0.5),
        'g_post_ffn': 1.0 + nrm(k[21], (L, D), 0.05),
    }


def reference(x, c, positions, w_ada, b_ada, g_pre_mix, w_in, pe_kc, w_kc, pe_vc, w_vc, lb_logits, g_hg_norm,
              w_br_attn, w_br_hgrn, w_out, g_post_mix, g_pre_ffn, w_up, conv_w, conv_b, w_down, g_post_ffn):
    B, S, D = x.shape
    dt = x.dtype
    cos, sin = rope_tables(positions)
    lb_cum = jnp.cumsum(jax.nn.softmax(lb_logits.astype(jnp.float32), axis=0), axis=0)
    lower_bounds = lb_cum - lb_cum[0:1]
    c_act = jax.nn.silu(c)

    def f32(t):
        return t.astype(jnp.float32)

    for l in range(DEPTH):
        ada = (c_act @ w_ada[l] + b_ada[l])[:, None, :]
        sh1, sc1, gt1, sh2, sc2, gt2 = jnp.split(ada, N_ADA, axis=-1)

        h = rms_norm(x, g_pre_mix[l]) * (1.0 + sc1) + sh1
        z = h @ w_in[l]
        q_a, kv, a_gate, q_h, f_h, i_h, og_h, m_gate = split_cols(z, IN_SIZES)

        q_a = apply_rope(q_a.reshape(B, S, A_HEADS, HEAD_DIM), cos, sin)
        kc_raw, vc_raw, k_sel, v_sel, k_win, v_win = [
            t.reshape(B, S, A_KV_GROUPS, HEAD_DIM) for t in jnp.split(kv, 6, axis=-1)]
        kc_raw = apply_rope(kc_raw, cos, sin)
        k_sel = apply_rope(k_sel, cos, sin)
        k_win = apply_rope(k_win, cos, sin)
        k_cmp = compress_blocks(kc_raw, pe_kc[l], w_kc[l])
        v_cmp = compress_blocks(vc_raw, pe_vc[l], w_vc[l])
        branch_gates = jax.nn.sigmoid(f32(a_gate)).reshape(B, S, A_HEADS, 3)
        attn = jax.vmap(nsa_sequence)(f32(q_a), f32(k_cmp), f32(v_cmp), f32(k_sel), f32(v_sel),
                                      f32(k_win), f32(v_win), branch_gates).astype(dt)

        hg = hgrn2_recurrence(f32(q_h).reshape(B, S, HG_HEADS, HG_DIM),
                              f32(f_h).reshape(B, S, HG_HEADS, HG_DIM),
                              f32(i_h).reshape(B, S, HG_HEADS, HG_DIM),
                              lower_bounds[l]).astype(dt)
        hg = rms_norm(hg, g_hg_norm[l].reshape(HG_HEADS, HG_DIM)) * jax.nn.silu(og_h.reshape(B, S, HG_HEADS, HG_DIM))
        hg = hg.reshape(B, S, HG_WIDTH)

        g_attn, g_hgrn = jnp.split(jax.nn.sigmoid(m_gate), 2, axis=-1)
        mixed = g_attn * (attn @ w_br_attn[l]) + g_hgrn * (hg @ w_br_hgrn[l])
        x = x + gt1 * rms_norm(mixed @ w_out[l], g_post_mix[l])

        h = rms_norm(x, g_pre_ffn[l]) * (1.0 + sc2) + sh2
        u = causal_depthwise_conv(h @ w_up[l], conv_w[l], conv_b[l])
        u_gate, u_val = jnp.split(u, 2, axis=-1)
        x = x + gt2 * rms_norm((jax.nn.silu(u_gate) * u_val) @ w_down[l], g_post_ffn[l])
    return x
```

```python
import functools

import jax
import jax.numpy as jnp
import numpy as np
from jax import lax
from jax.experimental import pallas as pl
from jax.experimental.pallas import tpu as pltpu

F32 = jnp.float32
BF16 = jnp.bfloat16

A_HEADS = 8
A_KV_GROUPS = 2
A_HPG = A_HEADS // A_KV_GROUPS
HEAD_DIM = 128
A_WIDTH = A_HEADS * HEAD_DIM
A_KV_WIDTH = A_KV_GROUPS * HEAD_DIM
CMP_BLOCK = 32
CMP_STRIDE = 16
SEL_BLOCK = 64
SEL_TOPK = 16
WINDOW = 512
Q_BLOCK = 128
ROPE_THETA = 10000.0
HG_HEADS = 8
HG_DIM = 128
HG_WIDTH = HG_HEADS * HG_DIM
HG_CHUNK = 64
CONV_WIDTH = 3
NORM_EPS = 1e-6
N_ADA = 6

LANES = 128
SEL_SLOTS = LANES
KEY_TILE = 256
NEG_BIG = -1e30
VMEM_LIMIT = 56 * 1024 * 1024


def _cparams(sem):
    return pltpu.CompilerParams(dimension_semantics=sem, vmem_limit_bytes=VMEM_LIMIT)


def _split3(a):
    hi = a.astype(BF16)
    r1 = a - hi.astype(F32)
    mid = r1.astype(BF16)
    lo = (r1 - mid.astype(F32)).astype(BF16)
    return hi, mid, lo


def _dot(a, b):
    return jnp.dot(a, b, preferred_element_type=F32)


def _dot_nt(a, b):
    return lax.dot_general(a, b, (((1,), (1,)), ((), ())), preferred_element_type=F32)


def _dot_tn(a, b):
    return lax.dot_general(a, b, (((0,), (0,)), ((), ())), preferred_element_type=F32)


def _ada_kernel(c_ref, w_ref, b_ref, o_ref):
    c = c_ref[...]
    ca = c * jax.nn.sigmoid(c)
    o_ref[0] = jnp.sum(ca * w_ref[0], axis=0, keepdims=True) + b_ref[0]


def _ada_all(c, w_ada, b_ada):
    L, D, N = w_ada.shape
    tn = 1024
    return pl.pallas_call(
        _ada_kernel,
        grid=(L, N // tn),
        in_specs=[
            pl.BlockSpec((D, 1), lambda l, n: (0, 0)),
            pl.BlockSpec((1, D, tn), lambda l, n: (l, 0, n)),
            pl.BlockSpec((1, 1, tn), lambda l, n: (l, 0, n)),
        ],
        out_specs=pl.BlockSpec((1, 1, tn), lambda l, n: (l, 0, n)),
        out_shape=jax.ShapeDtypeStruct((L, 1, N), F32),
        compiler_params=_cparams(("arbitrary", "arbitrary")),
        name="ada",
    )(c.reshape(D, 1), w_ada, b_ada.reshape(L, 1, N))


def _prenorm_kernel(x_ref, g_ref, sc_ref, sh_ref, o_ref):
    x = x_ref[...]
    ms = jnp.mean(x * x, axis=-1, keepdims=True)
    y = x * lax.rsqrt(ms + NORM_EPS) * g_ref[...]
    o_ref[...] = (y * (1.0 + sc_ref[...]) + sh_ref[...]).astype(o_ref.dtype)


def _prenorm(x, g, sc, sh):
    S, D = x.shape
    tm = 512
    row = pl.BlockSpec((1, D), lambda m: (0, 0))
    return pl.pallas_call(
        _prenorm_kernel,
        grid=(S // tm,),
        in_specs=[pl.BlockSpec((tm, D), lambda m: (m, 0)), row, row, row],
        out_specs=pl.BlockSpec((tm, D), lambda m: (m, 0)),
        out_shape=jax.ShapeDtypeStruct((S, D), BF16),
        compiler_params=_cparams(("arbitrary",)),
        name="prenorm",
    )(x, g.reshape(1, D), sc, sh)


def _rope(t, cosf, sinf):
    return t * cosf + pltpu.roll(t, HEAD_DIM // 2, axis=1) * sinf


def _proj_kernel(*refs, rope_tiles, act, tn):
    if rope_tiles:
        h_ref, w_ref, cos_ref, sin_ref, o_ref = refs
    else:
        h_ref, w_ref, o_ref = refs
    acc = _dot(h_ref[...], w_ref[...])
    if act == "sigmoid":
        acc = jax.nn.sigmoid(acc)
    if not rope_tiles:
        o_ref[...] = acc.astype(o_ref.dtype)
        return
    n = pl.program_id(1)

    @pl.when(n < rope_tiles)
    def _():
        cosf = cos_ref[...]
        sinf = sin_ref[...]
        for s in range(tn // HEAD_DIM):
            sl = slice(s * HEAD_DIM, (s + 1) * HEAD_DIM)
            o_ref[:, sl] = _rope(acc[:, sl], cosf, sinf).astype(o_ref.dtype)

    @pl.when(n >= rope_tiles)
    def _():
        o_ref[...] = acc.astype(o_ref.dtype)


def _proj(h, w, out_dtype, *, tn, rope_cols=0, tables=None, act=None, name="proj"):
    S, D = h.shape
    N = w.shape[1]
    tm = 1024 if S % 1024 == 0 else S
    rope_tiles = rope_cols // tn
    in_specs = [pl.BlockSpec((tm, D), lambda m, n: (m, 0)),
                pl.BlockSpec((D, tn), lambda m, n: (0, n))]
    args = [h, w]
    if rope_tiles:
        tab = pl.BlockSpec((tm, HEAD_DIM), lambda m, n: (m, 0))
        in_specs += [tab, tab]
        args += list(tables)
    return pl.pallas_call(
        functools.partial(_proj_kernel, rope_tiles=rope_tiles, act=act, tn=tn),
        grid=(S // tm, N // tn),
        in_specs=in_specs,
        out_specs=pl.BlockSpec((tm, tn), lambda m, n: (m, n)),
        out_shape=jax.ShapeDtypeStruct((S, N), out_dtype),
        compiler_params=_cparams(("arbitrary", "arbitrary")),
        name=name,
    )(*args)


def _compress_kernel(t_ref, pe_ref, w_ref, o_ref, *, n_chunk):
    half = CMP_BLOCK // 2
    top, bot = [], []
    for j in range(half):
        a = t_ref[pl.ds(j, n_chunk, stride=CMP_STRIDE), :]
        top.append(a + pe_ref[0, j:j + 1, :])
        bot.append(a + pe_ref[0, half + j:half + j + 1, :])
    a_top = jnp.concatenate(top, axis=1)
    a_bot = jnp.concatenate(bot, axis=1)
    kw = half * HEAD_DIM
    w = w_ref[0]
    w_top = w[:kw].astype(BF16)
    w_bot = w[kw:].astype(BF16)
    y_top = jnp.zeros((n_chunk, HEAD_DIM), F32)
    y_bot = jnp.zeros((n_chunk, HEAD_DIM), F32)
    for piece in _split3(a_top):
        y_top = y_top + _dot(piece, w_top)
    for piece in _split3(a_bot):
        y_bot = y_bot + _dot(piece, w_bot)
    o_ref[0, 0] = y_top + pltpu.roll(y_bot, n_chunk - 1, axis=0)


def _compress(cmp_raw, pe2, w2):
    S = cmp_raw.shape[0]
    n_chunk = S // CMP_STRIDE
    G = A_KV_GROUPS
    return pl.pallas_call(
        functools.partial(_compress_kernel, n_chunk=n_chunk),
        grid=(2, G),
        in_specs=[
            pl.BlockSpec((S, HEAD_DIM), lambda kv, g: (0, kv * G + g)),
            pl.BlockSpec((1, CMP_BLOCK, HEAD_DIM), lambda kv, g: (kv, 0, 0)),
            pl.BlockSpec((1, CMP_BLOCK * HEAD_DIM, HEAD_DIM), lambda kv, g: (kv, 0, 0)),
        ],
        out_specs=pl.BlockSpec((1, 1, n_chunk, HEAD_DIM), lambda kv, g: (kv, g, 0, 0)),
        out_shape=jax.ShapeDtypeStruct((2, G, n_chunk, HEAD_DIM), F32),
        compiler_params=_cparams(("arbitrary", "arbitrary")),
        name="compress",
    )(cmp_raw, pe2, w2)


def _nsa_kernel(q_ref, ks_ref, kw_ref, vs_ref, vw_ref, kc_ref, vc_ref, ov_ref, gate_ref,
                o_ref, kaug_ref, *, seq):
    i = pl.program_id(1)
    t0 = i * Q_BLOCK
    n_cmp = seq // CMP_STRIDE
    rows = A_HPG * Q_BLOCK

    @pl.when(i == 0)
    def _():
        def build(r, c):
            r0 = pl.multiple_of(r * KEY_TILE, KEY_TILE)
            kaug_ref[pl.ds(r0, KEY_TILE), 0:HEAD_DIM] = ks_ref[pl.ds(r0, KEY_TILE), :]
            blk = (r0 + lax.broadcasted_iota(jnp.int32, (KEY_TILE, SEL_SLOTS), 0)) // SEL_BLOCK
            slot = lax.broadcasted_iota(jnp.int32, (KEY_TILE, SEL_SLOTS), 1)
            kaug_ref[pl.ds(r0, KEY_TILE), HEAD_DIM:HEAD_DIM + SEL_SLOTS] = jnp.where(
                blk == slot, 1.0, 0.0).astype(BF16)
            return c
        lax.fori_loop(0, seq // KEY_TILE, build, 0)

    q4 = jnp.concatenate(
        [q_ref[:, h * HEAD_DIM:(h + 1) * HEAD_DIM] for h in range(A_HPG)], axis=0)
    tq = t0 + lax.broadcasted_iota(jnp.int32, (rows, 1), 0) % Q_BLOCK

    kc = kc_ref[0, 0].astype(BF16)
    vc = vc_ref[0, 0].astype(BF16)
    s_c = _dot_nt(q4, kc)
    ncol = lax.broadcasted_iota(jnp.int32, (1, n_cmp), 1)
    vis = (ncol * CMP_STRIDE + (CMP_BLOCK - 1)) <= tq
    s_c = jnp.where(vis, s_c, NEG_BIG)
    m_c = jnp.max(s_c, axis=1, keepdims=True)
    p_c = jnp.where(vis, jnp.exp(s_c - m_c), 0.0)
    den = jnp.sum(p_c, axis=1, keepdims=True)
    p_c = p_c * (1.0 / jnp.where(den > 0, den, 1.0))
    o_c = _dot(p_c.astype(BF16), vc)

    psum = (p_c[0:Q_BLOCK] + p_c[Q_BLOCK:2 * Q_BLOCK]
            + p_c[2 * Q_BLOCK:3 * Q_BLOCK] + p_c[3 * Q_BLOCK:4 * Q_BLOCK])
    ov = ov_ref[...]
    imp = jnp.zeros((Q_BLOCK, SEL_SLOTS), F32)
    for piece in _split3(psum):
        imp = imp + _dot(piece, ov)
    imp_t = imp.T
    slot_i = lax.broadcasted_iota(jnp.int32, (SEL_SLOTS, Q_BLOCK), 0)
    tq_l = t0 + lax.broadcasted_iota(jnp.int32, (SEL_SLOTS, Q_BLOCK), 1)
    cur = tq_l // SEL_BLOCK
    forced = jnp.where(slot_i == 0, 1, 0) + jnp.where(slot_i == cur, 1, 0) + jnp.where(slot_i == cur - 1, 1, 0)
    valid = slot_i * SEL_BLOCK <= tq_l
    key = jnp.where(forced > 0, -NEG_BIG, jnp.where(valid, imp_t, NEG_BIG))
    slot_f = slot_i.astype(F32)
    bias_t = jnp.full((SEL_SLOTS, Q_BLOCK), NEG_BIG, F32)
    for _ in range(SEL_TOPK):
        mx = jnp.max(key, axis=0, keepdims=True)
        first = jnp.min(jnp.where(key == mx, slot_f, 1e9), axis=0, keepdims=True)
        pick = slot_f == first
        bias_t = jnp.where(pick, 0.0, bias_t)
        key = jnp.where(pick, -3e38, key)
    bias = bias_t.T.astype(BF16)
    q_aug = jnp.concatenate([q4, jnp.concatenate([bias] * A_HPG, axis=0)], axis=1)

    n_tiles = (i + 2) // 2

    def sel_tile(j, carry, causal):
        m, l, acc = carry
        k0 = pl.multiple_of(j * KEY_TILE, KEY_TILE)
        s = _dot_nt(q_aug, kaug_ref[pl.ds(k0, KEY_TILE), :])
        if causal:
            kpos = k0 + lax.broadcasted_iota(jnp.int32, (1, KEY_TILE), 1)
            s = jnp.where(kpos <= tq, s, NEG_BIG)
        m_new = jnp.maximum(m, jnp.max(s, axis=1, keepdims=True))
        alpha = jnp.exp(m - m_new)
        p = jnp.exp(s - m_new)
        l = alpha * l + jnp.sum(p, axis=1, keepdims=True)
        acc = alpha * acc + _dot(p.astype(BF16), vs_ref[pl.ds(k0, KEY_TILE), :])
        return m_new, l, acc

    init = (jnp.full((rows, 1), NEG_BIG, F32), jnp.zeros((rows, 1), F32),
            jnp.zeros((rows, HEAD_DIM), F32))
    carry = lax.fori_loop(0, n_tiles - 1, lambda j, c: sel_tile(j, c, False), init)
    _, l_s, acc_s = sel_tile(n_tiles - 1, carry, True)
    o_s = acc_s * (1.0 / l_s)

    wlen = WINDOW + Q_BLOCK
    w0 = pl.multiple_of(jnp.maximum(t0 - WINDOW, 0), Q_BLOCK)
    s_w = _dot_nt(q4, kw_ref[pl.ds(w0, wlen), :])
    dist = tq - (w0 + lax.broadcasted_iota(jnp.int32, (1, wlen), 1))
    s_w = jnp.where((dist >= 0) & (dist < WINDOW), s_w, NEG_BIG)
    m_w = jnp.max(s_w, axis=1, keepdims=True)
    p_w = jnp.exp(s_w - m_w)
    l_w = jnp.sum(p_w, axis=1, keepdims=True)
    o_w = _dot(p_w.astype(BF16), vw_ref[pl.ds(w0, wlen), :]) * (1.0 / l_w)

    gates = gate_ref[...]
    for h in range(A_HPG):
        r = slice(h * Q_BLOCK, (h + 1) * Q_BLOCK)
        o = (gates[:, 3 * h:3 * h + 1] * o_c[r] + gates[:, 3 * h + 1:3 * h + 2] * o_s[r]
             + gates[:, 3 * h + 2:3 * h + 3] * o_w[r])
        o_ref[:, h * HEAD_DIM:(h + 1) * HEAD_DIM] = o.astype(o_ref.dtype)


def _overlap_matrix(seq):
    n_chunk = seq // CMP_STRIDE
    cs = np.arange(n_chunk)[:, None] * CMP_STRIDE
    ss = np.arange(SEL_SLOTS)[None, :] * SEL_BLOCK
    ov = np.maximum(np.minimum(cs + CMP_BLOCK, ss + SEL_BLOCK) - np.maximum(cs, ss), 0) / CMP_BLOCK
    n_cmp = (seq - CMP_BLOCK) // CMP_STRIDE + 1
    ov[n_cmp:] = 0.0
    ov[:, seq // SEL_BLOCK:] = 0.0
    return ov.astype(np.float32)


def _nsa(q, kv, cmp_kv, gates):
    S = q.shape[0]
    G = A_KV_GROUPS
    assert S // SEL_BLOCK <= SEL_SLOTS and S % KEY_TILE == 0 and S >= WINDOW + Q_BLOCK
    n_chunk = S // CMP_STRIDE
    ov = jnp.asarray(_overlap_matrix(S), BF16)
    col = lambda c: pl.BlockSpec((S, HEAD_DIM), lambda g, i, c=c: (0, c * G + g))
    return pl.pallas_call(
        functools.partial(_nsa_kernel, seq=S),
        grid=(G, S // Q_BLOCK),
        in_specs=[
            pl.BlockSpec((Q_BLOCK, A_HPG * HEAD_DIM), lambda g, i: (i, g)),
            col(0), col(1), col(2), col(3),
            pl.BlockSpec((1, 1, n_chunk, HEAD_DIM), lambda g, i: (0, g, 0, 0)),
            pl.BlockSpec((1, 1, n_chunk, HEAD_DIM), lambda g, i: (1, g, 0, 0)),
            pl.BlockSpec((n_chunk, SEL_SLOTS), lambda g, i: (0, 0)),
            pl.BlockSpec((Q_BLOCK, LANES), lambda g, i: (i, g)),
        ],
        out_specs=pl.BlockSpec((Q_BLOCK, A_HPG * HEAD_DIM), lambda g, i: (i, g)),
        out_shape=jax.ShapeDtypeStruct((S, A_WIDTH), BF16),
        scratch_shapes=[pltpu.VMEM((S, HEAD_DIM + SEL_SLOTS), BF16)],
        compiler_params=_cparams(("arbitrary", "arbitrary")),
        name="nsa",
    )(q, kv, kv, kv, kv, cmp_kv, cmp_kv, ov, gates)


_ROW_MASKS = 18
_SCORE_MASKS = 10


def _hgrn_tables():
    t = np.arange(HG_CHUNK)
    p4, i4, i16 = t % 4, (t // 4) % 4, t // 16
    conds = [p4 >= 1, p4 >= 2, p4 >= 3, p4 <= 2, p4 <= 1, p4 == 0,
             i4 >= 1, i4 >= 2, i4 >= 3, i4 <= 2, i4 <= 1, i4 == 0,
             i16 >= 1, i16 >= 2, i16 >= 3, i16 <= 2, i16 <= 1, i16 == 0]
    rm = np.stack([np.broadcast_to(c[:, None], (HG_CHUNK, HG_DIM)) for c in conds]).astype(np.float32)
    tt, ss = t[:, None], t[None, :]
    lag16 = tt // 16 - ss // 16
    lag4 = tt // 4 - ss // 4
    same16, same4 = lag16 == 0, lag4 == 0
    sm = [lag16 == 1, lag16 == 2, lag16 == 3,
          same16 & (lag4 == 1), same16 & (lag4 == 2), same16 & (lag4 == 3),
          same4 & (tt - ss == 0), same4 & (tt - ss == 1), same4 & (tt - ss == 2), same4 & (tt - ss == 3)]
    return rm, np.stack(sm).astype(np.float32)


def _hgrn_kernel(q_ref, f_ref, i_ref, og_ref, loglb_ref, log1m_ref, gn_ref, rm_ref, sm_ref,
                 o_ref, st_ref, g_scr, k_scr, o_scr, *, tb):
    C = HG_CHUNK

    @pl.when(pl.program_id(1) == 0)
    def _():
        st_ref[...] = jnp.zeros_like(st_ref)

    x = f_ref[...]
    log_sig = jnp.minimum(x, 0.0) - jnp.log(1.0 + jnp.exp(-jnp.abs(x)))
    a = loglb_ref[...]
    c = log1m_ref[...] + log_sig
    log_f = jnp.maximum(a, c) + jnp.log(1.0 + jnp.exp(-jnp.abs(a - c)))
    g_scr[...] = log_f
    k_scr[...] = 1.0 - jnp.exp(log_f)

    def rmask(n):
        return rm_ref[n]

    def roll(v, s):
        return pltpu.roll(v, s % C, axis=0)

    def chunk(ci, carry):
        r0 = pl.multiple_of(ci * C, C)
        g = g_scr[pl.ds(r0, C), :]
        q = q_ref[pl.ds(r0, C), :]
        k = k_scr[pl.ds(r0, C), :]
        v = i_ref[pl.ds(r0, C), :].astype(BF16)
        pre4 = g + rmask(0) * roll(g, 1) + rmask(1) * roll(g, 2) + rmask(2) * roll(g, 3)
        suf4 = rmask(3) * roll(g, -1) + rmask(4) * roll(g, -2) + rmask(5) * roll(g, -3)
        tot4 = pre4 + suf4
        a1, a2, a3 = roll(tot4, 4), roll(tot4, 8), roll(tot4, 12)
        pre16 = pre4 + rmask(6) * a1 + rmask(7) * a2 + rmask(8) * a3
        suf16 = suf4 + rmask(9) * roll(tot4, -4) + rmask(10) * roll(tot4, -8) + rmask(11) * roll(tot4, -12)
        tot16 = pre16 + suf16
        d1, d2, d3 = roll(tot16, 16), roll(tot16, 32), roll(tot16, 48)
        pre64 = pre16 + rmask(12) * d1 + rmask(13) * d2 + rmask(14) * d3
        suf64 = suf16 + rmask(15) * roll(tot16, -16) + rmask(16) * roll(tot16, -32) + rmask(17) * roll(tot16, -48)
        tot64 = pre64 + suf64
        g1, g2 = roll(g, 1), roll(g, 2)

        def b16(z):
            return z.astype(BF16)

        q_a = q * jnp.exp(pre16)
        lhs_a = jnp.concatenate([b16(q_a), b16(q_a * jnp.exp(d1)), b16(q_a * jnp.exp(d1 + d2))], axis=0)
        s_a = _dot_nt(lhs_a, b16(k * jnp.exp(suf16)))
        q_b = q * jnp.exp(pre4)
        lhs_b = jnp.concatenate([b16(q_b), b16(q_b * jnp.exp(a1)), b16(q_b * jnp.exp(a1 + a2))], axis=0)
        s_b = _dot_nt(lhs_b, b16(k * jnp.exp(suf4)))
        q_c = q * jnp.exp(g)
        lhs_c = jnp.concatenate([b16(q), b16(q_c), b16(q_c * jnp.exp(g1)), b16(q_c * jnp.exp(g1 + g2))], axis=0)
        s_c = _dot_nt(lhs_c, b16(k))
        scores = (sm_ref[0] * s_a[0:C] + sm_ref[1] * s_a[C:2 * C] + sm_ref[2] * s_a[2 * C:3 * C]
                  + sm_ref[3] * s_b[0:C] + sm_ref[4] * s_b[C:2 * C] + sm_ref[5] * s_b[2 * C:3 * C]
                  + sm_ref[6] * s_c[0:C] + sm_ref[7] * s_c[C:2 * C] + sm_ref[8] * s_c[2 * C:3 * C]
                  + sm_ref[9] * s_c[3 * C:4 * C])
        st = st_ref[...]
        o = _dot(b16(scores), v) + _dot_nt(b16(q * jnp.exp(pre64)), b16(st))
        o_scr[pl.ds(r0, C), :] = o
        k_st = b16(k * jnp.exp(suf64))
        st_ref[...] = st * jnp.exp(tot64[0:1, :]) + _dot_tn(v, k_st)
        return carry

    lax.fori_loop(0, tb // C, chunk, 0)

    o = o_scr[...]
    ms = jnp.mean(o * o, axis=-1, keepdims=True)
    og = og_ref[...]
    y = o * lax.rsqrt(ms + NORM_EPS) * gn_ref[...] * (og * jax.nn.sigmoid(og))
    o_ref[...] = y.astype(o_ref.dtype)


def _hgrn(hz, loglb, log1mlb, gnorm):
    S = hz.shape[0]
    tb = 512 if S % 512 == 0 else S
    H = HG_HEADS
    rm, sm = _hgrn_tables()
    blk = lambda part: pl.BlockSpec((tb, HG_DIM), lambda h, c, part=part: (c, part * H + h))
    row = pl.BlockSpec((1, HG_DIM), lambda h, c: (0, h))
    return pl.pallas_call(
        functools.partial(_hgrn_kernel, tb=tb),
        grid=(H, S // tb),
        in_specs=[blk(0), blk(1), blk(2), blk(3), row, row, row,
                  pl.BlockSpec((_ROW_MASKS, HG_CHUNK, HG_DIM), lambda h, c: (0, 0, 0)),
                  pl.BlockSpec((_SCORE_MASKS, HG_CHUNK, HG_CHUNK), lambda h, c: (0, 0, 0))],
        out_specs=pl.BlockSpec((tb, HG_DIM), lambda h, c: (c, h)),
        out_shape=jax.ShapeDtypeStruct((S, HG_WIDTH), BF16),
        scratch_shapes=[pltpu.VMEM((HG_DIM, HG_DIM), F32), pltpu.VMEM((tb, HG_DIM), F32),
                        pltpu.VMEM((tb, HG_DIM), F32), pltpu.VMEM((tb, HG_DIM), F32)],
        compiler_params=_cparams(("arbitrary", "arbitrary")),
        name="hgrn",
    )(hz, hz, hz, hz, loglb, log1mlb, gnorm, jnp.asarray(rm), jnp.asarray(sm))


def _merge_kernel(at_ref, hg_ref, ga_ref, gh_ref, wa_ref, wh_ref, wo_ref, x_ref, gt_ref, gp_ref, o_ref):
    mixed = (ga_ref[...].astype(F32) * _dot(at_ref[...], wa_ref[...])
             + gh_ref[...].astype(F32) * _dot(hg_ref[...], wh_ref[...]))
    y = _dot(mixed.astype(BF16), wo_ref[...])
    ms = jnp.mean(y * y, axis=-1, keepdims=True)
    yn = y * lax.rsqrt(ms + NORM_EPS) * gp_ref[...]
    o_ref[...] = x_ref[...] + gt_ref[...] * yn


def _merge(attn, hg, mg, wa, wh, wo, x, gt, gpost):
    S, D = x.shape
    tm = 256
    const = lambda shape: pl.BlockSpec(shape, lambda m: (0, 0), pipeline_mode=pl.Buffered(1))
    row = pl.BlockSpec((1, D), lambda m: (0, 0))
    return pl.pallas_call(
        _merge_kernel,
        grid=(S // tm,),
        in_specs=[
            pl.BlockSpec((tm, A_WIDTH), lambda m: (m, 0)),
            pl.BlockSpec((tm, HG_WIDTH), lambda m: (m, 0)),
            pl.BlockSpec((tm, D), lambda m: (m, 0)),
            pl.BlockSpec((tm, D), lambda m: (m, 1)),
            const((A_WIDTH, D)), const((HG_WIDTH, D)), const((D, D)),
            pl.BlockSpec((tm, D), lambda m: (m, 0)),
            row, row,
        ],
        out_specs=pl.BlockSpec((tm, D), lambda m: (m, 0)),
        out_shape=jax.ShapeDtypeStruct((S, D), F32),
        compiler_params=_cparams(("arbitrary",)),
        name="merge",
    )(attn, hg, mg, mg, wa, wh, wo, x, gt, gpost.reshape(1, D))


_HALO = 16


def _ffn_kernel(x_ref, xh_ref, g_ref, sc_ref, sh_ref, wg_ref, wv_ref, cwg_ref, cwv_ref, cbg_ref, cbv_ref,
                wd_ref, gt_ref, gp_ref, o_ref, h_scr, acc_scr, ug_scr, uv_scr, *, tm):
    m = pl.program_id(0)
    f = pl.program_id(1)

    def norm_mod(x):
        ms = jnp.mean(x * x, axis=-1, keepdims=True)
        y = x * lax.rsqrt(ms + NORM_EPS) * g_ref[...]
        return (y * (1.0 + sc_ref[...]) + sh_ref[...]).astype(BF16)

    @pl.when(f == 0)
    def _():
        h_scr[0:_HALO, :] = norm_mod(xh_ref[...])
        h_scr[_HALO:_HALO + tm, :] = norm_mod(x_ref[...])
        acc_scr[...] = jnp.zeros_like(acc_scr)

    h = h_scr[...]
    ug_scr[...] = _dot(h, wg_ref[...])
    uv_scr[...] = _dot(h, wv_ref[...])

    @pl.when(m == 0)
    def _():
        ug_scr[0:_HALO, :] = jnp.zeros((_HALO, ug_scr.shape[1]), F32)
        uv_scr[0:_HALO, :] = jnp.zeros((_HALO, uv_scr.shape[1]), F32)

    def conv(u_scr, cw_ref, cb_ref):
        out = cb_ref[...] + cw_ref[0:1, :] * u_scr[pl.ds(_HALO - 2, tm), :]
        out = out + cw_ref[1:2, :] * u_scr[pl.ds(_HALO - 1, tm), :]
        return out + cw_ref[2:3, :] * u_scr[pl.ds(_HALO, tm), :]

    cg = conv(ug_scr, cwg_ref, cbg_ref)
    cv = conv(uv_scr, cwv_ref, cbv_ref)
    act = (cg * jax.nn.sigmoid(cg) * cv).astype(BF16)
    acc_scr[...] += _dot(act, wd_ref[...])

    @pl.when(f == pl.num_programs(1) - 1)
    def _():
        y = acc_scr[...]
        ms = jnp.mean(y * y, axis=-1, keepdims=True)
        yn = y * lax.rsqrt(ms + NORM_EPS) * gp_ref[...]
        o_ref[...] = x_ref[...] + gt_ref[...] * yn


def _ffn(x, gpre, sc, sh, wup, convw, convb, wdown, gt, gpost):
    S, D = x.shape
    F = wdown.shape[0]
    tm = 512 if S % 512 == 0 else S
    tf = 512
    nf = F // tf
    hb = tm // _HALO
    row = pl.BlockSpec((1, D), lambda m, f: (0, 0))
    return pl.pallas_call(
        functools.partial(_ffn_kernel, tm=tm),
        grid=(S // tm, nf),
        in_specs=[
            pl.BlockSpec((tm, D), lambda m, f: (m, 0)),
            pl.BlockSpec((_HALO, D), lambda m, f: (jnp.maximum(m * hb - 1, 0), 0)),
            row, row, row,
            pl.BlockSpec((D, tf), lambda m, f: (0, f)),
            pl.BlockSpec((D, tf), lambda m, f: (0, nf + f)),
            pl.BlockSpec((CONV_WIDTH, tf), lambda m, f: (0, f)),
            pl.BlockSpec((CONV_WIDTH, tf), lambda m, f: (0, nf + f)),
            pl.BlockSpec((1, tf), lambda m, f: (0, f)),
            pl.BlockSpec((1, tf), lambda m, f: (0, nf + f)),
            pl.BlockSpec((tf, D), lambda m, f: (f, 0)),
            row, row,
        ],
        out_specs=pl.BlockSpec((tm, D), lambda m, f: (m, 0)),
        out_shape=jax.ShapeDtypeStruct((S, D), F32),
        scratch_shapes=[pltpu.VMEM((tm + _HALO, D), BF16), pltpu.VMEM((tm, D), F32),
                        pltpu.VMEM((tm + _HALO, tf), F32), pltpu.VMEM((tm + _HALO, tf), F32)],
        compiler_params=_cparams(("arbitrary", "arbitrary")),
        name="ffn",
    )(x, x, gpre.reshape(1, D), sc, sh, wup, wup, convw, convw, convb.reshape(1, -1), convb.reshape(1, -1),
      wdown, gt, gpost.reshape(1, D))


def _rope_tables(positions):
    inv_freq = 1.0 / (ROPE_THETA ** (jnp.arange(0, HEAD_DIM, 2, dtype=F32) / HEAD_DIM))
    ang = positions.astype(F32)[:, None] * inv_freq
    cos, sin = jnp.cos(ang), jnp.sin(ang)
    return jnp.concatenate([cos, cos], axis=-1), jnp.concatenate([-sin, sin], axis=-1)


def _mixer_weights(w_in_l):
    D = w_in_l.shape[0]
    o = 0
    wq = w_in_l[:, o:o + A_WIDTH]; o += A_WIDTH
    kvs = []
    for _ in range(6):
        kvs.append(w_in_l[:, o:o + A_KV_WIDTH]); o += A_KV_WIDTH
    kc, vc, ks, vs, kw, vw = kvs
    wgate = w_in_l[:, o:o + 3 * A_HEADS]; o += 3 * A_HEADS
    whg = w_in_l[:, o:o + 4 * HG_WIDTH]; o += 4 * HG_WIDTH
    wmg = w_in_l[:, o:]
    per_group = 3 * A_HPG
    pad = jnp.zeros((D, LANES - per_group), w_in_l.dtype)
    wgate = jnp.concatenate(
        [t for g in range(A_KV_GROUPS) for t in (wgate[:, g * per_group:(g + 1) * per_group], pad)], axis=1)
    return (wq.astype(BF16), jnp.concatenate([ks, kw, vs, vw], axis=1).astype(BF16),
            jnp.concatenate([kc, vc], axis=1).astype(BF16), wgate.astype(BF16),
            whg.astype(BF16), wmg.astype(BF16))


def kernel(x, c, positions, w_ada, b_ada, g_pre_mix, w_in, pe_kc, w_kc, pe_vc, w_vc, lb_logits, g_hg_norm,
           w_br_attn, w_br_hgrn, w_out, g_post_mix, g_pre_ffn, w_up, conv_w, conv_b, w_down, g_post_ffn):
    B, S, D = x.shape
    assert B == 1, "kernel is written for one sequence"
    L = w_ada.shape[0]
    xs = x[0]
    cosf, sinf = _rope_tables(positions[0])
    scale = HEAD_DIM ** -0.5
    q_tabs = (cosf * scale, sinf * scale)
    k_tabs = (cosf, sinf)
    lb_cum = jnp.cumsum(jax.nn.softmax(lb_logits.astype(F32), axis=0), axis=0)
    lower = lb_cum - lb_cum[0:1]
    log_lb = jnp.log(lower)
    log_1m = jnp.log1p(-lower)
    ada = _ada_all(c, w_ada, b_ada)

    for l in range(L):
        sh1, sc1, gt1, sh2, sc2, gt2 = [ada[l, :, j * D:(j + 1) * D] for j in range(N_ADA)]
        wq, wkv, wcmp, wgate, whg, wmg = _mixer_weights(w_in[l])

        h = _prenorm(xs, g_pre_mix[l], sc1, sh1)
        q = _proj(h, wq, BF16, tn=512, rope_cols=A_WIDTH, tables=q_tabs, name="proj_q")
        kv = _proj(h, wkv, BF16, tn=512, rope_cols=2 * A_KV_WIDTH, tables=k_tabs, name="proj_kv")
        cmp_raw = _proj(h, wcmp, F32, tn=256, rope_cols=A_KV_WIDTH, tables=k_tabs, name="proj_cmp")
        gates = _proj(h, wgate, F32, tn=256, act="sigmoid", name="proj_gate")
        hz = _proj(h, whg, F32, tn=512, name="proj_hgrn")
        mg = _proj(h, wmg, BF16, tn=512, act="sigmoid", name="proj_merge_gate")

        cmp_kv = _compress(cmp_raw, jnp.stack([pe_kc[l], pe_vc[l]]), jnp.stack([w_kc[l], w_vc[l]]))
        attn = _nsa(q, kv, cmp_kv, gates)
        hg = _hgrn(hz, log_lb[l:l + 1], log_1m[l:l + 1], g_hg_norm[l:l + 1])
        xs = _merge(attn, hg, mg, w_br_attn[l].astype(BF16), w_br_hgrn[l].astype(BF16),
                    w_out[l].astype(BF16), xs, gt1, g_post_mix[l])

        xs = _ffn(xs, g_pre_ffn[l], sc2, sh2, w_up[l].astype(BF16), conv_w[l], conv_b[l],
                  w_down[l].astype(BF16), gt2, g_post_ffn[l])
    return xs[None]
```

```python
import functools

import jax
import jax.numpy as jnp
import numpy as np
from jax import lax
from jax.experimental import pallas as pl
from jax.experimental.pallas import tpu as pltpu

F32 = jnp.float32
BF16 = jnp.bfloat16

A_HEADS = 8
A_KV_GROUPS = 2
A_HPG = A_HEADS // A_KV_GROUPS
HEAD_DIM = 128
A_WIDTH = A_HEADS * HEAD_DIM
A_KV_WIDTH = A_KV_GROUPS * HEAD_DIM
CMP_BLOCK = 32
CMP_STRIDE = 16
SEL_BLOCK = 64
SEL_TOPK = 16
WINDOW = 512
Q_BLOCK = 128
ROPE_THETA = 10000.0
HG_HEADS = 8
HG_DIM = 128
HG_WIDTH = HG_HEADS * HG_DIM
HG_CHUNK = 64
CONV_WIDTH = 3
NORM_EPS = 1e-6
N_ADA = 6

LANES = 128
SEL_SLOTS = LANES
KEY_TILE = 256
NEG_BIG = -1e30
VMEM_LIMIT = 56 * 1024 * 1024


def _cparams(sem):
    return pltpu.CompilerParams(dimension_semantics=sem, vmem_limit_bytes=VMEM_LIMIT)


def _split3(a):
    hi = a.astype(BF16)
    r1 = a - hi.astype(F32)
    mid = r1.astype(BF16)
    lo = (r1 - mid.astype(F32)).astype(BF16)
    return hi, mid, lo


def _dot(a, b):
    return jnp.dot(a, b, preferred_element_type=F32)


def _dot_nt(a, b):
    return lax.dot_general(a, b, (((1,), (1,)), ((), ())), preferred_element_type=F32)


def _dot_tn(a, b):
    return lax.dot_general(a, b, (((0,), (0,)), ((), ())), preferred_element_type=F32)


def _ada_kernel(c_ref, w_ref, b_ref, o_ref):
    c = c_ref[...]
    ca = c * jax.nn.sigmoid(c)
    o_ref[0] = jnp.sum(ca * w_ref[0], axis=0, keepdims=True) + b_ref[0]


def _ada_all(c, w_ada, b_ada):
    L, D, N = w_ada.shape
    tn = 1024
    return pl.pallas_call(
        _ada_kernel,
        grid=(L, N // tn),
        in_specs=[
            pl.BlockSpec((D, 1), lambda l, n: (0, 0)),
            pl.BlockSpec((1, D, tn), lambda l, n: (l, 0, n)),
            pl.BlockSpec((1, 1, tn), lambda l, n: (l, 0, n)),
        ],
        out_specs=pl.BlockSpec((1, 1, tn), lambda l, n: (l, 0, n)),
        out_shape=jax.ShapeDtypeStruct((L, 1, N), F32),
        compiler_params=_cparams(("arbitrary", "arbitrary")),
        name="ada",
    )(c.reshape(D, 1), w_ada, b_ada.reshape(L, 1, N))


def _prenorm_kernel(x_ref, g_ref, sc_ref, sh_ref, o_ref):
    x = x_ref[...]
    ms = jnp.mean(x * x, axis=-1, keepdims=True)
    y = x * lax.rsqrt(ms + NORM_EPS) * g_ref[...]
    o_ref[...] = (y * (1.0 + sc_ref[...]) + sh_ref[...]).astype(o_ref.dtype)


def _prenorm(x, g, sc, sh):
    S, D = x.shape
    tm = 512
    row = pl.BlockSpec((1, D), lambda m: (0, 0))
    return pl.pallas_call(
        _prenorm_kernel,
        grid=(S // tm,),
        in_specs=[pl.BlockSpec((tm, D), lambda m: (m, 0)), row, row, row],
        out_specs=pl.BlockSpec((tm, D), lambda m: (m, 0)),
        out_shape=jax.ShapeDtypeStruct((S, D), BF16),
        compiler_params=_cparams(("arbitrary",)),
        name="prenorm",
    )(x, g.reshape(1, D), sc, sh)


def _rope(t, cosf, sinf):
    return t * cosf + pltpu.roll(t, HEAD_DIM // 2, axis=1) * sinf


def _proj_kernel(*refs, rope_tiles, act, tn):
    if rope_tiles:
        h_ref, w_ref, cos_ref, sin_ref, o_ref = refs
    else:
        h_ref, w_ref, o_ref = refs
    acc = _dot(h_ref[...], w_ref[...])
    if act == "sigmoid":
        acc = jax.nn.sigmoid(acc)
    if not rope_tiles:
        o_ref[...] = acc.astype(o_ref.dtype)
        return
    n = pl.program_id(1)

    @pl.when(n < rope_tiles)
    def _():
        cosf = cos_ref[...]
        sinf = sin_ref[...]
        for s in range(tn // HEAD_DIM):
            sl = slice(s * HEAD_DIM, (s + 1) * HEAD_DIM)
            o_ref[:, sl] = _rope(acc[:, sl], cosf, sinf).astype(o_ref.dtype)

    @pl.when(n >= rope_tiles)
    def _():
        o_ref[...] = acc.astype(o_ref.dtype)


def _proj(h, w, out_dtype, *, tn, rope_cols=0, tables=None, act=None, name="proj"):
    S, D = h.shape
    N = w.shape[1]
    tm = 1024 if S % 1024 == 0 else S
    rope_tiles = rope_cols // tn
    in_specs = [pl.BlockSpec((tm, D), lambda m, n: (m, 0)),
                pl.BlockSpec((D, tn), lambda m, n: (0, n))]
    args = [h, w]
    if rope_tiles:
        tab = pl.BlockSpec((tm, HEAD_DIM), lambda m, n: (m, 0))
        in_specs += [tab, tab]
        args += list(tables)
    return pl.pallas_call(
        functools.partial(_proj_kernel, rope_tiles=rope_tiles, act=act, tn=tn),
        grid=(S // tm, N // tn),
        in_specs=in_specs,
        out_specs=pl.BlockSpec((tm, tn), lambda m, n: (m, n)),
        out_shape=jax.ShapeDtypeStruct((S, N), out_dtype),
        compiler_params=_cparams(("arbitrary", "arbitrary")),
        name=name,
    )(*args)


def _compress_kernel(t_ref, pe_ref, w_ref, o_ref, *, n_chunk):
    half = CMP_BLOCK // 2
    top, bot = [], []
    for j in range(half):
        a = t_ref[pl.ds(j, n_chunk, stride=CMP_STRIDE), :]
        top.append(a + pe_ref[0, j:j + 1, :])
        bot.append(a + pe_ref[0, half + j:half + j + 1, :])
    a_top = jnp.concatenate(top, axis=1)
    a_bot = jnp.concatenate(bot, axis=1)
    kw = half * HEAD_DIM
    w = w_ref[0]
    w_top = w[:kw].astype(BF16)
    w_bot = w[kw:].astype(BF16)
    y_top = jnp.zeros((n_chunk, HEAD_DIM), F32)
    y_bot = jnp.zeros((n_chunk, HEAD_DIM), F32)
    for piece in _split3(a_top):
        y_top = y_top + _dot(piece, w_top)
    for piece in _split3(a_bot):
        y_bot = y_bot + _dot(piece, w_bot)
    o_ref[0, 0] = y_top + pltpu.roll(y_bot, n_chunk - 1, axis=0)


def _compress(cmp_raw, pe2, w2):
    S = cmp_raw.shape[0]
    n_chunk = S // CMP_STRIDE
    G = A_KV_GROUPS
    return pl.pallas_call(
        functools.partial(_compress_kernel, n_chunk=n_chunk),
        grid=(2, G),
        in_specs=[
            pl.BlockSpec((S, HEAD_DIM), lambda kv, g: (0, kv * G + g)),
            pl.BlockSpec((1, CMP_BLOCK, HEAD_DIM), lambda kv, g: (kv, 0, 0)),
            pl.BlockSpec((1, CMP_BLOCK * HEAD_DIM, HEAD_DIM), lambda kv, g: (kv, 0, 0)),
        ],
        out_specs=pl.BlockSpec((1, 1, n_chunk, HEAD_DIM), lambda kv, g: (kv, g, 0, 0)),
        out_shape=jax.ShapeDtypeStruct((2, G, n_chunk, HEAD_DIM), F32),
        compiler_params=_cparams(("arbitrary", "arbitrary")),
        name="compress",
    )(cmp_raw, pe2, w2)


def _nsa_kernel(q_ref, ks_ref, kw_ref, vs_ref, vw_ref, kc_ref, vc_ref, ov_ref, gate_ref,
                o_ref, kaug_ref, vt_ref, *, seq):
    i = pl.program_id(1)
    t0 = i * Q_BLOCK
    n_cmp = seq // CMP_STRIDE
    rows = A_HPG * Q_BLOCK

    @pl.when(i == 0)
    def _():
        def build(r, c):
            r0 = pl.multiple_of(r * KEY_TILE, KEY_TILE)
            kaug_ref[r, :, 0:HEAD_DIM] = ks_ref[pl.ds(r0, KEY_TILE), :]
            blk = (r0 + lax.broadcasted_iota(jnp.int32, (KEY_TILE, SEL_SLOTS), 0)) // SEL_BLOCK
            slot = lax.broadcasted_iota(jnp.int32, (KEY_TILE, SEL_SLOTS), 1)
            kaug_ref[r, :, HEAD_DIM:HEAD_DIM + SEL_SLOTS] = jnp.where(blk == slot, 1.0, 0.0).astype(BF16)
            vt_ref[r] = vs_ref[pl.ds(r0, KEY_TILE), :].astype(F32).T.astype(BF16)
            return c
        lax.fori_loop(0, seq // KEY_TILE, build, 0)

    q_heads = [q_ref[:, h * HEAD_DIM:(h + 1) * HEAD_DIM] for h in range(A_HPG)]
    q4 = jnp.concatenate(q_heads, axis=0)
    tq = t0 + lax.broadcasted_iota(jnp.int32, (rows, 1), 0) % Q_BLOCK

    kc = kc_ref[0, 0].astype(BF16)
    vc = vc_ref[0, 0].astype(BF16)
    s_c = _dot_nt(q4, kc)
    ncol = lax.broadcasted_iota(jnp.int32, (1, n_cmp), 1)
    vis = (ncol * CMP_STRIDE + (CMP_BLOCK - 1)) <= tq
    s_c = jnp.where(vis, s_c, NEG_BIG)
    m_c = jnp.max(s_c, axis=1, keepdims=True)
    p_c = jnp.where(vis, jnp.exp(s_c - m_c), 0.0)
    den = jnp.sum(p_c, axis=1, keepdims=True)
    p_c = p_c * (1.0 / jnp.where(den > 0, den, 1.0))
    o_c = _dot(p_c.astype(BF16), vc)

    psum = (p_c[0:Q_BLOCK] + p_c[Q_BLOCK:2 * Q_BLOCK]
            + p_c[2 * Q_BLOCK:3 * Q_BLOCK] + p_c[3 * Q_BLOCK:4 * Q_BLOCK])
    ov = ov_ref[...]
    imp = jnp.zeros((Q_BLOCK, SEL_SLOTS), F32)
    for piece in _split3(psum):
        imp = imp + _dot(piece, ov)
    imp_t = imp.T
    slot_i = lax.broadcasted_iota(jnp.int32, (SEL_SLOTS, Q_BLOCK), 0)
    tq_l = t0 + lax.broadcasted_iota(jnp.int32, (SEL_SLOTS, Q_BLOCK), 1)
    cur = tq_l // SEL_BLOCK
    forced = jnp.where(slot_i == 0, 1, 0) + jnp.where(slot_i == cur, 1, 0) + jnp.where(slot_i == cur - 1, 1, 0)
    valid = slot_i * SEL_BLOCK <= tq_l
    key = jnp.where(forced > 0, -NEG_BIG, jnp.where(valid, imp_t, NEG_BIG))
    slot_f = slot_i.astype(F32)
    bias_t = jnp.full((SEL_SLOTS, Q_BLOCK), NEG_BIG, F32)
    for _ in range(SEL_TOPK):
        mx = jnp.max(key, axis=0, keepdims=True)
        first = jnp.min(jnp.where(key == mx, slot_f, 1e9), axis=0, keepdims=True)
        pick = slot_f == first
        bias_t = jnp.where(pick, 0.0, bias_t)
        key = jnp.where(pick, -3e38, key)

    qt = jnp.concatenate([qh.astype(F32).T.astype(BF16) for qh in q_heads], axis=1)
    qt_aug = jnp.concatenate([qt, jnp.concatenate([bias_t.astype(BF16)] * A_HPG, axis=1)], axis=0)
    tq_row = t0 + lax.broadcasted_iota(jnp.int32, (1, rows), 1) % Q_BLOCK

    def sel_tile(j, state, causal):
        m, l, acc = state
        s = _dot(kaug_ref[j], qt_aug)
        if causal:
            kpos = j * KEY_TILE + lax.broadcasted_iota(jnp.int32, (KEY_TILE, 1), 0)
            s = jnp.where(kpos <= tq_row, s, NEG_BIG)
        m_new = jnp.maximum(m, jnp.max(s, axis=0, keepdims=True))
        alpha = jnp.exp(m - m_new)
        p = jnp.exp(s - m_new)
        l = alpha * l + jnp.sum(p, axis=0, keepdims=True)
        acc = alpha * acc + _dot(vt_ref[j], p.astype(BF16))
        return m_new, l, acc

    def fresh():
        return (jnp.full((1, rows), 0.5 * NEG_BIG, F32), jnp.zeros((1, rows), F32),
                jnp.zeros((HEAD_DIM, rows), F32))

    n_tiles = (i + 2) // 2
    n_pairs = (n_tiles - 1) // 2

    def pair(pj, st):
        return sel_tile(2 * pj, st[0], False), sel_tile(2 * pj + 1, st[1], False)

    st_a, st_b = lax.fori_loop(0, n_pairs, pair, (fresh(), fresh()))
    (m_a, l_a, acc_a) = sel_tile(2 * n_pairs, st_a, True)
    (m_b, l_b, acc_b) = sel_tile(2 * n_pairs + 1, st_b, True)
    m_s = jnp.maximum(m_a, m_b)
    w_a = jnp.exp(m_a - m_s)
    w_b = jnp.exp(m_b - m_s)
    o_st = (w_a * acc_a + w_b * acc_b) * (1.0 / (w_a * l_a + w_b * l_b))

    wlen = WINDOW + Q_BLOCK
    w0 = pl.multiple_of(jnp.maximum(t0 - WINDOW, 0), Q_BLOCK)
    s_w = _dot_nt(q4, kw_ref[pl.ds(w0, wlen), :])
    dist = tq - (w0 + lax.broadcasted_iota(jnp.int32, (1, wlen), 1))
    s_w = jnp.where((dist >= 0) & (dist < WINDOW), s_w, NEG_BIG)
    m_w = jnp.max(s_w, axis=1, keepdims=True)
    p_w = jnp.exp(s_w - m_w)
    l_w = jnp.sum(p_w, axis=1, keepdims=True)
    o_w = _dot(p_w.astype(BF16), vw_ref[pl.ds(w0, wlen), :]) * (1.0 / l_w)

    gates = gate_ref[...]
    for h in range(A_HPG):
        r = slice(h * Q_BLOCK, (h + 1) * Q_BLOCK)
        o_s = o_st[:, r].T
        o = (gates[:, 3 * h:3 * h + 1] * o_c[r] + gates[:, 3 * h + 1:3 * h + 2] * o_s
             + gates[:, 3 * h + 2:3 * h + 3] * o_w[r])
        o_ref[:, h * HEAD_DIM:(h + 1) * HEAD_DIM] = o.astype(o_ref.dtype)


def _overlap_matrix(seq):
    n_chunk = seq // CMP_STRIDE
    cs = np.arange(n_chunk)[:, None] * CMP_STRIDE
    ss = np.arange(SEL_SLOTS)[None, :] * SEL_BLOCK
    ov = np.maximum(np.minimum(cs + CMP_BLOCK, ss + SEL_BLOCK) - np.maximum(cs, ss), 0) / CMP_BLOCK
    n_cmp = (seq - CMP_BLOCK) // CMP_STRIDE + 1
    ov[n_cmp:] = 0.0
    ov[:, seq // SEL_BLOCK:] = 0.0
    return ov.astype(np.float32)


def _nsa(q, kv, cmp_kv, gates):
    S = q.shape[0]
    G = A_KV_GROUPS
    assert S // SEL_BLOCK <= SEL_SLOTS and S % KEY_TILE == 0 and S >= WINDOW + Q_BLOCK
    n_chunk = S // CMP_STRIDE
    ov = jnp.asarray(_overlap_matrix(S), BF16)
    col = lambda c: pl.BlockSpec((S, HEAD_DIM), lambda g, i, c=c: (0, c * G + g))
    return pl.pallas_call(
        functools.partial(_nsa_kernel, seq=S),
        grid=(G, S // Q_BLOCK),
        in_specs=[
            pl.BlockSpec((Q_BLOCK, A_HPG * HEAD_DIM), lambda g, i: (i, g)),
            col(0), col(1), col(2), col(3),
            pl.BlockSpec((1, 1, n_chunk, HEAD_DIM), lambda g, i: (0, g, 0, 0)),
            pl.BlockSpec((1, 1, n_chunk, HEAD_DIM), lambda g, i: (1, g, 0, 0)),
            pl.BlockSpec((n_chunk, SEL_SLOTS), lambda g, i: (0, 0)),
            pl.BlockSpec((Q_BLOCK, LANES), lambda g, i: (i, g)),
        ],
        out_specs=pl.BlockSpec((Q_BLOCK, A_HPG * HEAD_DIM), lambda g, i: (i, g)),
        out_shape=jax.ShapeDtypeStruct((S, A_WIDTH), BF16),
        scratch_shapes=[pltpu.VMEM((S // KEY_TILE, KEY_TILE, HEAD_DIM + SEL_SLOTS), BF16),
                        pltpu.VMEM((S // KEY_TILE, HEAD_DIM, KEY_TILE), BF16)],
        compiler_params=_cparams(("arbitrary", "arbitrary")),
        name="nsa",
    )(q, kv, kv, kv, kv, cmp_kv, cmp_kv, ov, gates)


NSA_QB = 256
SEL_BODY = 4
QK_AHEAD = 2


def _nsa2_kernel(q_ref, ks_ref, kw_ref, vs_ref, vw_ref, kc_ref, vc_ref, ovt_ref, gate_ref,
                 o_ref, kaug_ref, vt_ref, vwt_ref, vct_ref, acc_ref, *, seq):
    i = pl.program_id(1)
    t0 = i * NSA_QB
    n_cmp = seq // CMP_STRIDE
    cols = A_HPG * NSA_QB

    @pl.when(i == 0)
    def _():
        def build(r, c):
            r0 = pl.multiple_of(r * KEY_TILE, KEY_TILE)
            kaug_ref[r, :, 0:HEAD_DIM] = ks_ref[pl.ds(r0, KEY_TILE), :]
            blk = (r0 + lax.broadcasted_iota(jnp.int32, (KEY_TILE, SEL_SLOTS), 0)) // SEL_BLOCK
            slot = lax.broadcasted_iota(jnp.int32, (KEY_TILE, SEL_SLOTS), 1)
            kaug_ref[r, :, HEAD_DIM:HEAD_DIM + SEL_SLOTS] = jnp.where(blk == slot, 1.0, 0.0).astype(BF16)
            vt_ref[r] = vs_ref[pl.ds(r0, KEY_TILE), :].astype(F32).T.astype(BF16)
            vwt_ref[r] = vw_ref[pl.ds(r0, KEY_TILE), :].astype(F32).T.astype(BF16)
            return c
        lax.fori_loop(0, seq // KEY_TILE, build, 0)
        vct_ref[...] = vc_ref[0, 0].T.astype(BF16)

    qt = jnp.concatenate(
        [q_ref[:, h * HEAD_DIM:(h + 1) * HEAD_DIM].astype(F32).T.astype(BF16) for h in range(A_HPG)],
        axis=1)
    tq = t0 + lax.broadcasted_iota(jnp.int32, (1, cols), 1) % NSA_QB

    def softmax0(s):
        m = jnp.max(s, axis=0, keepdims=True)
        p = jnp.exp(s - m)
        return p, jnp.sum(p, axis=0, keepdims=True)

    s_c = _dot(kc_ref[0, 0].astype(BF16), qt)
    nrow = lax.broadcasted_iota(jnp.int32, (n_cmp, 1), 0)
    vis = (nrow * CMP_STRIDE + (CMP_BLOCK - 1)) <= tq
    p_c, den = softmax0(jnp.where(vis, s_c, NEG_BIG))
    p_c = jnp.where(vis, p_c, 0.0) * (1.0 / jnp.where(den > 0, den, 1.0))
    o_ct = _dot(vct_ref[...], p_c.astype(BF16))

    wlen = WINDOW + NSA_QB
    w0 = pl.multiple_of(jnp.maximum(t0 - WINDOW, 0), KEY_TILE)
    wt = jnp.maximum(i - WINDOW // KEY_TILE, 0)
    s_w = _dot(kw_ref[pl.ds(w0, wlen), :], qt)
    dist = tq - (w0 + lax.broadcasted_iota(jnp.int32, (wlen, 1), 0))
    p_w, l_w = softmax0(jnp.where((dist >= 0) & (dist < WINDOW), s_w, NEG_BIG))
    p_w = p_w.astype(BF16)
    o_wt = _dot(vwt_ref[wt], p_w[0:KEY_TILE])
    for k in range(1, wlen // KEY_TILE):
        o_wt = o_wt + _dot(vwt_ref[wt + k], p_w[k * KEY_TILE:(k + 1) * KEY_TILE])
    o_wt = o_wt * (1.0 / l_w)

    psum = p_c[:, 0:NSA_QB]
    for h in range(1, A_HPG):
        psum = psum + p_c[:, h * NSA_QB:(h + 1) * NSA_QB]
    ovt = ovt_ref[...]
    imp_t = jnp.zeros((SEL_SLOTS, NSA_QB), F32)
    for piece in _split3(psum):
        imp_t = imp_t + _dot(ovt, piece)
    slot_i = lax.broadcasted_iota(jnp.int32, (SEL_SLOTS, NSA_QB), 0)
    tq_l = t0 + lax.broadcasted_iota(jnp.int32, (SEL_SLOTS, NSA_QB), 1)
    cur = tq_l // SEL_BLOCK
    forced = jnp.where(slot_i == 0, 1, 0) + jnp.where(slot_i == cur, 1, 0) + jnp.where(slot_i == cur - 1, 1, 0)
    valid = slot_i * SEL_BLOCK <= tq_l
    key = jnp.where(forced > 0, -NEG_BIG, jnp.where(valid, imp_t, NEG_BIG))
    slot_f = slot_i.astype(F32)
    bias_t = jnp.full((SEL_SLOTS, NSA_QB), NEG_BIG, F32)
    for _ in range(SEL_TOPK):
        mx = jnp.max(key, axis=0, keepdims=True)
        first = jnp.min(jnp.where(key == mx, slot_f, 1e9), axis=0, keepdims=True)
        pick = slot_f == first
        bias_t = jnp.where(pick, 0.0, bias_t)
        key = jnp.where(pick, -3e38, key)

    qt_aug = jnp.concatenate([qt, jnp.concatenate([bias_t.astype(BF16)] * A_HPG, axis=1)], axis=0)
    acc_ref[...] = jnp.zeros_like(acc_ref)

    def sel_body(base, n, state, causal):
        m, l = state
        s = {}
        for k in range(min(QK_AHEAD, n)):
            s[k] = _dot(kaug_ref[base + k], qt_aug)
        for k in range(n):
            if k + QK_AHEAD < n:
                s[k + QK_AHEAD] = _dot(kaug_ref[base + k + QK_AHEAD], qt_aug)
            sk = s.pop(k)
            if causal:
                kpos = (base + k) * KEY_TILE + lax.broadcasted_iota(jnp.int32, (KEY_TILE, 1), 0)
                sk = jnp.where(kpos <= tq, sk, NEG_BIG)
            m_new = jnp.maximum(m, jnp.max(sk, axis=0, keepdims=True))
            alpha = jnp.exp(m - m_new)
            p = jnp.exp(sk - m_new)
            l = alpha * l + jnp.sum(p, axis=0, keepdims=True)
            acc_ref[...] = alpha * acc_ref[...] + _dot(vt_ref[base + k], p.astype(BF16))
            m = m_new
        return m, l

    n_bulk = i // SEL_BODY
    state = (jnp.full((1, cols), 0.5 * NEG_BIG, F32), jnp.zeros((1, cols), F32))
    state = lax.fori_loop(0, n_bulk, lambda b, st: sel_body(b * SEL_BODY, SEL_BODY, st, False), state)
    _, l_s = sel_body(n_bulk * SEL_BODY, SEL_BODY, state, True)
    o_st = acc_ref[...] * (1.0 / l_s)

    gates_t = gate_ref[...].T
    for h in range(A_HPG):
        c = slice(h * NSA_QB, (h + 1) * NSA_QB)
        o = (gates_t[3 * h:3 * h + 1] * o_ct[:, c] + gates_t[3 * h + 1:3 * h + 2] * o_st[:, c]
             + gates_t[3 * h + 2:3 * h + 3] * o_wt[:, c])
        o_ref[:, h * HEAD_DIM:(h + 1) * HEAD_DIM] = o.T.astype(o_ref.dtype)


def _nsa2(q, kv, cmp_kv, gates):
    S = q.shape[0]
    G = A_KV_GROUPS
    n_tiles = S // KEY_TILE
    assert S // SEL_BLOCK <= SEL_SLOTS and S % (KEY_TILE * SEL_BODY) == 0 and S >= WINDOW + NSA_QB
    assert NSA_QB == KEY_TILE
    n_chunk = S // CMP_STRIDE
    ovt = jnp.asarray(_overlap_matrix(S).T, BF16)
    col = lambda c: pl.BlockSpec((S, HEAD_DIM), lambda g, i, c=c: (0, c * G + g))
    return pl.pallas_call(
        functools.partial(_nsa2_kernel, seq=S),
        grid=(G, S // NSA_QB),
        in_specs=[
            pl.BlockSpec((NSA_QB, A_HPG * HEAD_DIM), lambda g, i: (i, g)),
            col(0), col(1), col(2), col(3),
            pl.BlockSpec((1, 1, n_chunk, HEAD_DIM), lambda g, i: (0, g, 0, 0)),
            pl.BlockSpec((1, 1, n_chunk, HEAD_DIM), lambda g, i: (1, g, 0, 0)),
            pl.BlockSpec((SEL_SLOTS, n_chunk), lambda g, i: (0, 0)),
            pl.BlockSpec((NSA_QB, LANES), lambda g, i: (i, g)),
        ],
        out_specs=pl.BlockSpec((NSA_QB, A_HPG * HEAD_DIM), lambda g, i: (i, g)),
        out_shape=jax.ShapeDtypeStruct((S, A_WIDTH), BF16),
        scratch_shapes=[pltpu.VMEM((n_tiles, KEY_TILE, HEAD_DIM + SEL_SLOTS), BF16),
                        pltpu.VMEM((n_tiles, HEAD_DIM, KEY_TILE), BF16),
                        pltpu.VMEM((n_tiles, HEAD_DIM, KEY_TILE), BF16),
                        pltpu.VMEM((HEAD_DIM, n_chunk), BF16),
                        pltpu.VMEM((HEAD_DIM, A_HPG * NSA_QB), F32)],
        compiler_params=_cparams(("arbitrary", "arbitrary")),
        name="nsa",
    )(q, kv, kv, kv, kv, cmp_kv, cmp_kv, ovt, gates)


_ROW_MASKS = 18
_SCORE_MASKS = 10


def _hgrn_tables():
    t = np.arange(HG_CHUNK)
    p4, i4, i16 = t % 4, (t // 4) % 4, t // 16
    conds = [p4 >= 1, p4 >= 2, p4 >= 3, p4 <= 2, p4 <= 1, p4 == 0,
             i4 >= 1, i4 >= 2, i4 >= 3, i4 <= 2, i4 <= 1, i4 == 0,
             i16 >= 1, i16 >= 2, i16 >= 3, i16 <= 2, i16 <= 1, i16 == 0]
    rm = np.stack([np.broadcast_to(c[:, None], (HG_CHUNK, HG_DIM)) for c in conds]).astype(np.float32)
    tt, ss = t[:, None], t[None, :]
    lag16 = tt // 16 - ss // 16
    lag4 = tt // 4 - ss // 4
    same16, same4 = lag16 == 0, lag4 == 0
    sm = [lag16 == 1, lag16 == 2, lag16 == 3,
          same16 & (lag4 == 1), same16 & (lag4 == 2), same16 & (lag4 == 3),
          same4 & (tt - ss == 0), same4 & (tt - ss == 1), same4 & (tt - ss == 2), same4 & (tt - ss == 3)]
    return rm, np.stack(sm).astype(np.float32)


def _hgrn_kernel(q_ref, f_ref, i_ref, og_ref, loglb_ref, log1m_ref, gn_ref, rm_ref, sm_ref,
                 o_ref, st_ref, g_scr, k_scr, o_scr, *, tb):
    C = HG_CHUNK

    @pl.when(pl.program_id(1) == 0)
    def _():
        st_ref[...] = jnp.zeros_like(st_ref)

    x = f_ref[...]
    log_sig = jnp.minimum(x, 0.0) - jnp.log(1.0 + jnp.exp(-jnp.abs(x)))
    a = loglb_ref[...]
    c = log1m_ref[...] + log_sig
    log_f = jnp.maximum(a, c) + jnp.log(1.0 + jnp.exp(-jnp.abs(a - c)))
    g_scr[...] = log_f
    k_scr[...] = 1.0 - jnp.exp(log_f)

    def rmask(n):
        return rm_ref[n]

    def roll(v, s):
        return pltpu.roll(v, s % C, axis=0)

    def chunk(ci, carry):
        r0 = pl.multiple_of(ci * C, C)
        g = g_scr[pl.ds(r0, C), :]
        q = q_ref[pl.ds(r0, C), :]
        k = k_scr[pl.ds(r0, C), :]
        v = i_ref[pl.ds(r0, C), :].astype(BF16)
        pre4 = g + rmask(0) * roll(g, 1) + rmask(1) * roll(g, 2) + rmask(2) * roll(g, 3)
        suf4 = rmask(3) * roll(g, -1) + rmask(4) * roll(g, -2) + rmask(5) * roll(g, -3)
        tot4 = pre4 + suf4
        a1, a2, a3 = roll(tot4, 4), roll(tot4, 8), roll(tot4, 12)
        pre16 = pre4 + rmask(6) * a1 + rmask(7) * a2 + rmask(8) * a3
        suf16 = suf4 + rmask(9) * roll(tot4, -4) + rmask(10) * roll(tot4, -8) + rmask(11) * roll(tot4, -12)
        tot16 = pre16 + suf16
        d1, d2, d3 = roll(tot16, 16), roll(tot16, 32), roll(tot16, 48)
        pre64 = pre16 + rmask(12) * d1 + rmask(13) * d2 + rmask(14) * d3
        suf64 = suf16 + rmask(15) * roll(tot16, -16) + rmask(16) * roll(tot16, -32) + rmask(17) * roll(tot16, -48)
        tot64 = pre64 + suf64
        g1, g2 = roll(g, 1), roll(g, 2)

        def b16(z):
            return z.astype(BF16)

        q_a = q * jnp.exp(pre16)
        lhs_a = jnp.concatenate([b16(q_a), b16(q_a * jnp.exp(d1)), b16(q_a * jnp.exp(d1 + d2))], axis=0)
        s_a = _dot_nt(lhs_a, b16(k * jnp.exp(suf16)))
        q_b = q * jnp.exp(pre4)
        lhs_b = jnp.concatenate([b16(q_b), b16(q_b * jnp.exp(a1)), b16(q_b * jnp.exp(a1 + a2))], axis=0)
        s_b = _dot_nt(lhs_b, b16(k * jnp.exp(suf4)))
        q_c = q * jnp.exp(g)
        lhs_c = jnp.concatenate([b16(q), b16(q_c), b16(q_c * jnp.exp(g1)), b16(q_c * jnp.exp(g1 + g2))], axis=0)
        s_c = _dot_nt(lhs_c, b16(k))
        scores = (sm_ref[0] * s_a[0:C] + sm_ref[1] * s_a[C:2 * C] + sm_ref[2] * s_a[2 * C:3 * C]
                  + sm_ref[3] * s_b[0:C] + sm_ref[4] * s_b[C:2 * C] + sm_ref[5] * s_b[2 * C:3 * C]
                  + sm_ref[6] * s_c[0:C] + sm_ref[7] * s_c[C:2 * C] + sm_ref[8] * s_c[2 * C:3 * C]
                  + sm_ref[9] * s_c[3 * C:4 * C])
        st = st_ref[...]
        o = _dot(b16(scores), v) + _dot_nt(b16(q * jnp.exp(pre64)), b16(st))
        o_scr[pl.ds(r0, C), :] = o
        k_st = b16(k * jnp.exp(suf64))
        st_ref[...] = st * jnp.exp(tot64[0:1, :]) + _dot_tn(v, k_st)
        return carry

    lax.fori_loop(0, tb // C, chunk, 0)

    o = o_scr[...]
    ms = jnp.mean(o * o, axis=-1, keepdims=True)
    og = og_ref[...]
    y = o * lax.rsqrt(ms + NORM_EPS) * gn_ref[...] * (og * jax.nn.sigmoid(og))
    o_ref[...] = y.astype(o_ref.dtype)


def _hgrn(hz, loglb, log1mlb, gnorm):
    S = hz.shape[0]
    tb = 512 if S % 512 == 0 else S
    H = HG_HEADS
    rm, sm = _hgrn_tables()
    blk = lambda part: pl.BlockSpec((tb, HG_DIM), lambda h, c, part=part: (c, part * H + h))
    row = pl.BlockSpec((1, HG_DIM), lambda h, c: (0, h))
    return pl.pallas_call(
        functools.partial(_hgrn_kernel, tb=tb),
        grid=(H, S // tb),
        in_specs=[blk(0), blk(1), blk(2), blk(3), row, row, row,
                  pl.BlockSpec((_ROW_MASKS, HG_CHUNK, HG_DIM), lambda h, c: (0, 0, 0)),
                  pl.BlockSpec((_SCORE_MASKS, HG_CHUNK, HG_CHUNK), lambda h, c: (0, 0, 0))],
        out_specs=pl.BlockSpec((tb, HG_DIM), lambda h, c: (c, h)),
        out_shape=jax.ShapeDtypeStruct((S, HG_WIDTH), BF16),
        scratch_shapes=[pltpu.VMEM((HG_DIM, HG_DIM), F32), pltpu.VMEM((tb, HG_DIM), F32),
                        pltpu.VMEM((tb, HG_DIM), F32), pltpu.VMEM((tb, HG_DIM), F32)],
        compiler_params=_cparams(("arbitrary", "arbitrary")),
        name="hgrn",
    )(hz, hz, hz, hz, loglb, log1mlb, gnorm, jnp.asarray(rm), jnp.asarray(sm))


def _merge_kernel(at_ref, hg_ref, ga_ref, gh_ref, wa_ref, wh_ref, wo_ref, x_ref, gt_ref, gp_ref, o_ref):
    mixed = (ga_ref[...].astype(F32) * _dot(at_ref[...], wa_ref[...])
             + gh_ref[...].astype(F32) * _dot(hg_ref[...], wh_ref[...]))
    y = _dot(mixed.astype(BF16), wo_ref[...])
    ms = jnp.mean(y * y, axis=-1, keepdims=True)
    yn = y * lax.rsqrt(ms + NORM_EPS) * gp_ref[...]
    o_ref[...] = x_ref[...] + gt_ref[...] * yn


def _merge(attn, hg, mg, wa, wh, wo, x, gt, gpost):
    S, D = x.shape
    tm = 256
    const = lambda shape: pl.BlockSpec(shape, lambda m: (0, 0), pipeline_mode=pl.Buffered(1))
    row = pl.BlockSpec((1, D), lambda m: (0, 0))
    return pl.pallas_call(
        _merge_kernel,
        grid=(S // tm,),
        in_specs=[
            pl.BlockSpec((tm, A_WIDTH), lambda m: (m, 0)),
            pl.BlockSpec((tm, HG_WIDTH), lambda m: (m, 0)),
            pl.BlockSpec((tm, D), lambda m: (m, 0)),
            pl.BlockSpec((tm, D), lambda m: (m, 1)),
            const((A_WIDTH, D)), const((HG_WIDTH, D)), const((D, D)),
            pl.BlockSpec((tm, D), lambda m: (m, 0)),
            row, row,
        ],
        out_specs=pl.BlockSpec((tm, D), lambda m: (m, 0)),
        out_shape=jax.ShapeDtypeStruct((S, D), F32),
        compiler_params=_cparams(("arbitrary",)),
        name="merge",
    )(attn, hg, mg, mg, wa, wh, wo, x, gt, gpost.reshape(1, D))


_HALO = 16


def _ffn_kernel(x_ref, xh_ref, g_ref, sc_ref, sh_ref, wg_ref, wv_ref, cwg_ref, cwv_ref, cbg_ref, cbv_ref,
                wd_ref, gt_ref, gp_ref, o_ref, h_scr, acc_scr, ug_scr, uv_scr, *, tm):
    m = pl.program_id(0)
    f = pl.program_id(1)

    def norm_mod(x):
        ms = jnp.mean(x * x, axis=-1, keepdims=True)
        y = x * lax.rsqrt(ms + NORM_EPS) * g_ref[...]
        return (y * (1.0 + sc_ref[...]) + sh_ref[...]).astype(BF16)

    @pl.when(f == 0)
    def _():
        h_scr[0:_HALO, :] = norm_mod(xh_ref[...])
        h_scr[_HALO:_HALO + tm, :] = norm_mod(x_ref[...])
        acc_scr[...] = jnp.zeros_like(acc_scr)

    h = h_scr[...]
    ug_scr[...] = _dot(h, wg_ref[...])
    uv_scr[...] = _dot(h, wv_ref[...])

    @pl.when(m == 0)
    def _():
        ug_scr[0:_HALO, :] = jnp.zeros((_HALO, ug_scr.shape[1]), F32)
        uv_scr[0:_HALO, :] = jnp.zeros((_HALO, uv_scr.shape[1]), F32)

    def conv(u_scr, cw_ref, cb_ref):
        out = cb_ref[...] + cw_ref[0:1, :] * u_scr[pl.ds(_HALO - 2, tm), :]
        out = out + cw_ref[1:2, :] * u_scr[pl.ds(_HALO - 1, tm), :]
        return out + cw_ref[2:3, :] * u_scr[pl.ds(_HALO, tm), :]

    cg = conv(ug_scr, cwg_ref, cbg_ref)
    cv = conv(uv_scr, cwv_ref, cbv_ref)
    act = (cg * jax.nn.sigmoid(cg) * cv).astype(BF16)
    acc_scr[...] += _dot(act, wd_ref[...])

    @pl.when(f == pl.num_programs(1) - 1)
    def _():
        y = acc_scr[...]
        ms = jnp.mean(y * y, axis=-1, keepdims=True)
        yn = y * lax.rsqrt(ms + NORM_EPS) * gp_ref[...]
        o_ref[...] = x_ref[...] + gt_ref[...] * yn


def _ffn(x, gpre, sc, sh, wup, convw, convb, wdown, gt, gpost):
    S, D = x.shape
    F = wdown.shape[0]
    tm = 512 if S % 512 == 0 else S
    tf = 512
    nf = F // tf
    hb = tm // _HALO
    row = pl.BlockSpec((1, D), lambda m, f: (0, 0))
    return pl.pallas_call(
        functools.partial(_ffn_kernel, tm=tm),
        grid=(S // tm, nf),
        in_specs=[
            pl.BlockSpec((tm, D), lambda m, f: (m, 0)),
            pl.BlockSpec((_HALO, D), lambda m, f: (jnp.maximum(m * hb - 1, 0), 0)),
            row, row, row,
            pl.BlockSpec((D, tf), lambda m, f: (0, f)),
            pl.BlockSpec((D, tf), lambda m, f: (0, nf + f)),
            pl.BlockSpec((CONV_WIDTH, tf), lambda m, f: (0, f)),
            pl.BlockSpec((CONV_WIDTH, tf), lambda m, f: (0, nf + f)),
            pl.BlockSpec((1, tf), lambda m, f: (0, f)),
            pl.BlockSpec((1, tf), lambda m, f: (0, nf + f)),
            pl.BlockSpec((tf, D), lambda m, f: (f, 0)),
            row, row,
        ],
        out_specs=pl.BlockSpec((tm, D), lambda m, f: (m, 0)),
        out_shape=jax.ShapeDtypeStruct((S, D), F32),
        scratch_shapes=[pltpu.VMEM((tm + _HALO, D), BF16), pltpu.VMEM((tm, D), F32),
                        pltpu.VMEM((tm + _HALO, tf), F32), pltpu.VMEM((tm + _HALO, tf), F32)],
        compiler_params=_cparams(("arbitrary", "arbitrary")),
        name="ffn",
    )(x, x, gpre.reshape(1, D), sc, sh, wup, wup, convw, convw, convb.reshape(1, -1), convb.reshape(1, -1),
      wdown, gt, gpost.reshape(1, D))


def _rope_tables(positions):
    inv_freq = 1.0 / (ROPE_THETA ** (jnp.arange(0, HEAD_DIM, 2, dtype=F32) / HEAD_DIM))
    ang = positions.astype(F32)[:, None] * inv_freq
    cos, sin = jnp.cos(ang), jnp.sin(ang)
    return jnp.concatenate([cos, cos], axis=-1), jnp.concatenate([-sin, sin], axis=-1)


def _mixer_weights(w_in_l):
    D = w_in_l.shape[0]
    o = 0
    wq = w_in_l[:, o:o + A_WIDTH]; o += A_WIDTH
    kvs = []
    for _ in range(6):
        kvs.append(w_in_l[:, o:o + A_KV_WIDTH]); o += A_KV_WIDTH
    kc, vc, ks, vs, kw, vw = kvs
    wgate = w_in_l[:, o:o + 3 * A_HEADS]; o += 3 * A_HEADS
    whg = w_in_l[:, o:o + 4 * HG_WIDTH]; o += 4 * HG_WIDTH
    wmg = w_in_l[:, o:]
    per_group = 3 * A_HPG
    pad = jnp.zeros((D, LANES - per_group), w_in_l.dtype)
    wgate = jnp.concatenate(
        [t for g in range(A_KV_GROUPS) for t in (wgate[:, g * per_group:(g + 1) * per_group], pad)], axis=1)
    return (wq.astype(BF16), jnp.concatenate([ks, kw, vs, vw], axis=1).astype(BF16),
            jnp.concatenate([kc, vc], axis=1).astype(BF16), wgate.astype(BF16),
            whg.astype(BF16), wmg.astype(BF16))


def kernel(x, c, positions, w_ada, b_ada, g_pre_mix, w_in, pe_kc, w_kc, pe_vc, w_vc, lb_logits, g_hg_norm,
           w_br_attn, w_br_hgrn, w_out, g_post_mix, g_pre_ffn, w_up, conv_w, conv_b, w_down, g_post_ffn):
    B, S, D = x.shape
    assert B == 1, "kernel is written for one sequence"
    L = w_ada.shape[0]
    xs = x[0]
    cosf, sinf = _rope_tables(positions[0])
    scale = HEAD_DIM ** -0.5
    q_tabs = (cosf * scale, sinf * scale)
    k_tabs = (cosf, sinf)
    lb_cum = jnp.cumsum(jax.nn.softmax(lb_logits.astype(F32), axis=0), axis=0)
    lower = lb_cum - lb_cum[0:1]
    log_lb = jnp.log(lower)
    log_1m = jnp.log1p(-lower)
    ada = _ada_all(c, w_ada, b_ada)

    for l in range(L):
        sh1, sc1, gt1, sh2, sc2, gt2 = [ada[l, :, j * D:(j + 1) * D] for j in range(N_ADA)]
        wq, wkv, wcmp, wgate, whg, wmg = _mixer_weights(w_in[l])

        h = _prenorm(xs, g_pre_mix[l], sc1, sh1)
        q = _proj(h, wq, BF16, tn=512, rope_cols=A_WIDTH, tables=q_tabs, name="proj_q")
        kv = _proj(h, wkv, BF16, tn=512, rope_cols=2 * A_KV_WIDTH, tables=k_tabs, name="proj_kv")
        cmp_raw = _proj(h, wcmp, F32, tn=256, rope_cols=A_KV_WIDTH, tables=k_tabs, name="proj_cmp")
        gates = _proj(h, wgate, F32, tn=256, act="sigmoid", name="proj_gate")
        hz = _proj(h, whg, F32, tn=512, name="proj_hgrn")
        mg = _proj(h, wmg, BF16, tn=512, act="sigmoid", name="proj_merge_gate")

        cmp_kv = _compress(cmp_raw, jnp.stack([pe_kc[l], pe_vc[l]]), jnp.stack([w_kc[l], w_vc[l]]))
        attn = _nsa2(q, kv, cmp_kv, gates)
        hg = _hgrn(hz, log_lb[l:l + 1], log_1m[l:l + 1], g_hg_norm[l:l + 1])
        xs = _merge(attn, hg, mg, w_br_attn[l].astype(BF16), w_br_hgrn[l].astype(BF16),
                    w_out[l].astype(BF16), xs, gt1, g_post_mix[l])

        xs = _ffn(xs, g_pre_ffn[l], sc2, sh2, w_up[l].astype(BF16), conv_w[l], conv_b[l],
                  w_down[l].astype(BF16), gt2, g_post_ffn[l])
    return xs[None]
```

```python
import functools

import jax
import jax.numpy as jnp
import numpy as np
from jax import lax
from jax.experimental import pallas as pl
from jax.experimental.pallas import tpu as pltpu

F32 = jnp.float32
BF16 = jnp.bfloat16

A_HEADS = 8
A_KV_GROUPS = 2
A_HPG = A_HEADS // A_KV_GROUPS
HEAD_DIM = 128
A_WIDTH = A_HEADS * HEAD_DIM
A_KV_WIDTH = A_KV_GROUPS * HEAD_DIM
CMP_BLOCK = 32
CMP_STRIDE = 16
SEL_BLOCK = 64
SEL_TOPK = 16
WINDOW = 512
Q_BLOCK = 128
ROPE_THETA = 10000.0
HG_HEADS = 8
HG_DIM = 128
HG_WIDTH = HG_HEADS * HG_DIM
HG_CHUNK = 64
CONV_WIDTH = 3
NORM_EPS = 1e-6
N_ADA = 6

LANES = 128
SEL_SLOTS = LANES
KEY_TILE = 256
NEG_BIG = -1e30
VMEM_LIMIT = 56 * 1024 * 1024


def _cparams(sem):
    return pltpu.CompilerParams(dimension_semantics=sem, vmem_limit_bytes=VMEM_LIMIT)


def _split3(a):
    hi = a.astype(BF16)
    r1 = a - hi.astype(F32)
    mid = r1.astype(BF16)
    lo = (r1 - mid.astype(F32)).astype(BF16)
    return hi, mid, lo


def _dot(a, b):
    return jnp.dot(a, b, preferred_element_type=F32)


def _dot_nt(a, b):
    return lax.dot_general(a, b, (((1,), (1,)), ((), ())), preferred_element_type=F32)


def _dot_tn(a, b):
    return lax.dot_general(a, b, (((0,), (0,)), ((), ())), preferred_element_type=F32)


def _ada_kernel(c_ref, w_ref, b_ref, o_ref):
    c = c_ref[...]
    ca = c * jax.nn.sigmoid(c)
    o_ref[0] = jnp.sum(ca * w_ref[0], axis=0, keepdims=True) + b_ref[0]


def _ada_all(c, w_ada, b_ada):
    L, D, N = w_ada.shape
    tn = 1024
    return pl.pallas_call(
        _ada_kernel,
        grid=(L, N // tn),
        in_specs=[
            pl.BlockSpec((D, 1), lambda l, n: (0, 0)),
            pl.BlockSpec((1, D, tn), lambda l, n: (l, 0, n)),
            pl.BlockSpec((1, 1, tn), lambda l, n: (l, 0, n)),
        ],
        out_specs=pl.BlockSpec((1, 1, tn), lambda l, n: (l, 0, n)),
        out_shape=jax.ShapeDtypeStruct((L, 1, N), F32),
        compiler_params=_cparams(("arbitrary", "arbitrary")),
        name="ada",
    )(c.reshape(D, 1), w_ada, b_ada.reshape(L, 1, N))


def _prenorm_kernel(x_ref, g_ref, sc_ref, sh_ref, o_ref):
    x = x_ref[...]
    ms = jnp.mean(x * x, axis=-1, keepdims=True)
    y = x * lax.rsqrt(ms + NORM_EPS) * g_ref[...]
    o_ref[...] = (y * (1.0 + sc_ref[...]) + sh_ref[...]).astype(o_ref.dtype)


def _prenorm(x, g, sc, sh):
    S, D = x.shape
    tm = 512
    row = pl.BlockSpec((1, D), lambda m: (0, 0))
    return pl.pallas_call(
        _prenorm_kernel,
        grid=(S // tm,),
        in_specs=[pl.BlockSpec((tm, D), lambda m: (m, 0)), row, row, row],
        out_specs=pl.BlockSpec((tm, D), lambda m: (m, 0)),
        out_shape=jax.ShapeDtypeStruct((S, D), BF16),
        compiler_params=_cparams(("arbitrary",)),
        name="prenorm",
    )(x, g.reshape(1, D), sc, sh)


def _rope(t, cosf, sinf):
    return t * cosf + pltpu.roll(t, HEAD_DIM // 2, axis=1) * sinf


def _proj_kernel(*refs, rope_tiles, act, tn):
    if rope_tiles:
        h_ref, w_ref, cos_ref, sin_ref, o_ref = refs
    else:
        h_ref, w_ref, o_ref = refs
    acc = _dot(h_ref[...], w_ref[...])
    if act == "sigmoid":
        acc = jax.nn.sigmoid(acc)
    if not rope_tiles:
        o_ref[...] = acc.astype(o_ref.dtype)
        return
    n = pl.program_id(1)

    @pl.when(n < rope_tiles)
    def _():
        cosf = cos_ref[...]
        sinf = sin_ref[...]
        for s in range(tn // HEAD_DIM):
            sl = slice(s * HEAD_DIM, (s + 1) * HEAD_DIM)
            o_ref[:, sl] = _rope(acc[:, sl], cosf, sinf).astype(o_ref.dtype)

    @pl.when(n >= rope_tiles)
    def _():
        o_ref[...] = acc.astype(o_ref.dtype)


def _proj(h, w, out_dtype, *, tn, rope_cols=0, tables=None, act=None, name="proj"):
    S, D = h.shape
    N = w.shape[1]
    tm = 1024 if S % 1024 == 0 else S
    rope_tiles = rope_cols // tn
    in_specs = [pl.BlockSpec((tm, D), lambda m, n: (m, 0)),
                pl.BlockSpec((D, tn), lambda m, n: (0, n))]
    args = [h, w]
    if rope_tiles:
        tab = pl.BlockSpec((tm, HEAD_DIM), lambda m, n: (m, 0))
        in_specs += [tab, tab]
        args += list(tables)
    return pl.pallas_call(
        functools.partial(_proj_kernel, rope_tiles=rope_tiles, act=act, tn=tn),
        grid=(S // tm, N // tn),
        in_specs=in_specs,
        out_specs=pl.BlockSpec((tm, tn), lambda m, n: (m, n)),
        out_shape=jax.ShapeDtypeStruct((S, N), out_dtype),
        compiler_params=_cparams(("arbitrary", "arbitrary")),
        name=name,
    )(*args)


def _compress_kernel(t_ref, pe_ref, w_ref, o_ref, *, n_chunk):
    half = CMP_BLOCK // 2
    top, bot = [], []
    for j in range(half):
        a = t_ref[pl.ds(j, n_chunk, stride=CMP_STRIDE), :]
        top.append(a + pe_ref[0, j:j + 1, :])
        bot.append(a + pe_ref[0, half + j:half + j + 1, :])
    a_top = jnp.concatenate(top, axis=1)
    a_bot = jnp.concatenate(bot, axis=1)
    kw = half * HEAD_DIM
    w = w_ref[0]
    w_top = w[:kw].astype(BF16)
    w_bot = w[kw:].astype(BF16)
    y_top = jnp.zeros((n_chunk, HEAD_DIM), F32)
    y_bot = jnp.zeros((n_chunk, HEAD_DIM), F32)
    for piece in _split3(a_top):
        y_top = y_top + _dot(piece, w_top)
    for piece in _split3(a_bot):
        y_bot = y_bot + _dot(piece, w_bot)
    o_ref[0, 0] = y_top + pltpu.roll(y_bot, n_chunk - 1, axis=0)


def _compress(cmp_raw, pe2, w2):
    S = cmp_raw.shape[0]
    n_chunk = S // CMP_STRIDE
    G = A_KV_GROUPS
    return pl.pallas_call(
        functools.partial(_compress_kernel, n_chunk=n_chunk),
        grid=(2, G),
        in_specs=[
            pl.BlockSpec((S, HEAD_DIM), lambda kv, g: (0, kv * G + g)),
            pl.BlockSpec((1, CMP_BLOCK, HEAD_DIM), lambda kv, g: (kv, 0, 0)),
            pl.BlockSpec((1, CMP_BLOCK * HEAD_DIM, HEAD_DIM), lambda kv, g: (kv, 0, 0)),
        ],
        out_specs=pl.BlockSpec((1, 1, n_chunk, HEAD_DIM), lambda kv, g: (kv, g, 0, 0)),
        out_shape=jax.ShapeDtypeStruct((2, G, n_chunk, HEAD_DIM), F32),
        compiler_params=_cparams(("arbitrary", "arbitrary")),
        name="compress",
    )(cmp_raw, pe2, w2)


def _nsa_kernel(q_ref, ks_ref, kw_ref, vs_ref, vw_ref, kc_ref, vc_ref, ov_ref, gate_ref,
                o_ref, kaug_ref, vt_ref, *, seq):
    i = pl.program_id(1)
    t0 = i * Q_BLOCK
    n_cmp = seq // CMP_STRIDE
    rows = A_HPG * Q_BLOCK

    @pl.when(i == 0)
    def _():
        def build(r, c):
            r0 = pl.multiple_of(r * KEY_TILE, KEY_TILE)
            kaug_ref[r, :, 0:HEAD_DIM] = ks_ref[pl.ds(r0, KEY_TILE), :]
            blk = (r0 + lax.broadcasted_iota(jnp.int32, (KEY_TILE, SEL_SLOTS), 0)) // SEL_BLOCK
            slot = lax.broadcasted_iota(jnp.int32, (KEY_TILE, SEL_SLOTS), 1)
            kaug_ref[r, :, HEAD_DIM:HEAD_DIM + SEL_SLOTS] = jnp.where(blk == slot, 1.0, 0.0).astype(BF16)
            vt_ref[r] = vs_ref[pl.ds(r0, KEY_TILE), :].astype(F32).T.astype(BF16)
            return c
        lax.fori_loop(0, seq // KEY_TILE, build, 0)

    q_heads = [q_ref[:, h * HEAD_DIM:(h + 1) * HEAD_DIM] for h in range(A_HPG)]
    q4 = jnp.concatenate(q_heads, axis=0)
    tq = t0 + lax.broadcasted_iota(jnp.int32, (rows, 1), 0) % Q_BLOCK

    kc = kc_ref[0, 0].astype(BF16)
    vc = vc_ref[0, 0].astype(BF16)
    s_c = _dot_nt(q4, kc)
    ncol = lax.broadcasted_iota(jnp.int32, (1, n_cmp), 1)
    vis = (ncol * CMP_STRIDE + (CMP_BLOCK - 1)) <= tq
    s_c = jnp.where(vis, s_c, NEG_BIG)
    m_c = jnp.max(s_c, axis=1, keepdims=True)
    p_c = jnp.where(vis, jnp.exp(s_c - m_c), 0.0)
    den = jnp.sum(p_c, axis=1, keepdims=True)
    p_c = p_c * (1.0 / jnp.where(den > 0, den, 1.0))
    o_c = _dot(p_c.astype(BF16), vc)

    psum = (p_c[0:Q_BLOCK] + p_c[Q_BLOCK:2 * Q_BLOCK]
            + p_c[2 * Q_BLOCK:3 * Q_BLOCK] + p_c[3 * Q_BLOCK:4 * Q_BLOCK])
    ov = ov_ref[...]
    imp = jnp.zeros((Q_BLOCK, SEL_SLOTS), F32)
    for piece in _split3(psum):
        imp = imp + _dot(piece, ov)
    imp_t = imp.T
    slot_i = lax.broadcasted_iota(jnp.int32, (SEL_SLOTS, Q_BLOCK), 0)
    tq_l = t0 + lax.broadcasted_iota(jnp.int32, (SEL_SLOTS, Q_BLOCK), 1)
    cur = tq_l // SEL_BLOCK
    forced = jnp.where(slot_i == 0, 1, 0) + jnp.where(slot_i == cur, 1, 0) + jnp.where(slot_i == cur - 1, 1, 0)
    valid = slot_i * SEL_BLOCK <= tq_l
    key = jnp.where(forced > 0, -NEG_BIG, jnp.where(valid, imp_t, NEG_BIG))
    slot_f = slot_i.astype(F32)
    bias_t = jnp.full((SEL_SLOTS, Q_BLOCK), NEG_BIG, F32)
    for _ in range(SEL_TOPK):
        mx = jnp.max(key, axis=0, keepdims=True)
        first = jnp.min(jnp.where(key == mx, slot_f, 1e9), axis=0, keepdims=True)
        pick = slot_f == first
        bias_t = jnp.where(pick, 0.0, bias_t)
        key = jnp.where(pick, -3e38, key)

    qt = jnp.concatenate([qh.astype(F32).T.astype(BF16) for qh in q_heads], axis=1)
    qt_aug = jnp.concatenate([qt, jnp.concatenate([bias_t.astype(BF16)] * A_HPG, axis=1)], axis=0)
    tq_row = t0 + lax.broadcasted_iota(jnp.int32, (1, rows), 1) % Q_BLOCK

    def sel_tile(j, state, causal):
        m, l, acc = state
        s = _dot(kaug_ref[j], qt_aug)
        if causal:
            kpos = j * KEY_TILE + lax.broadcasted_iota(jnp.int32, (KEY_TILE, 1), 0)
            s = jnp.where(kpos <= tq_row, s, NEG_BIG)
        m_new = jnp.maximum(m, jnp.max(s, axis=0, keepdims=True))
        alpha = jnp.exp(m - m_new)
        p = jnp.exp(s - m_new)
        l = alpha * l + jnp.sum(p, axis=0, keepdims=True)
        acc = alpha * acc + _dot(vt_ref[j], p.astype(BF16))
        return m_new, l, acc

    def fresh():
        return (jnp.full((1, rows), 0.5 * NEG_BIG, F32), jnp.zeros((1, rows), F32),
                jnp.zeros((HEAD_DIM, rows), F32))

    n_tiles = (i + 2) // 2
    n_pairs = (n_tiles - 1) // 2

    def pair(pj, st):
        return sel_tile(2 * pj, st[0], False), sel_tile(2 * pj + 1, st[1], False)

    st_a, st_b = lax.fori_loop(0, n_pairs, pair, (fresh(), fresh()))
    (m_a, l_a, acc_a) = sel_tile(2 * n_pairs, st_a, True)
    (m_b, l_b, acc_b) = sel_tile(2 * n_pairs + 1, st_b, True)
    m_s = jnp.maximum(m_a, m_b)
    w_a = jnp.exp(m_a - m_s)
    w_b = jnp.exp(m_b - m_s)
    o_st = (w_a * acc_a + w_b * acc_b) * (1.0 / (w_a * l_a + w_b * l_b))

    wlen = WINDOW + Q_BLOCK
    w0 = pl.multiple_of(jnp.maximum(t0 - WINDOW, 0), Q_BLOCK)
    s_w = _dot_nt(q4, kw_ref[pl.ds(w0, wlen), :])
    dist = tq - (w0 + lax.broadcasted_iota(jnp.int32, (1, wlen), 1))
    s_w = jnp.where((dist >= 0) & (dist < WINDOW), s_w, NEG_BIG)
    m_w = jnp.max(s_w, axis=1, keepdims=True)
    p_w = jnp.exp(s_w - m_w)
    l_w = jnp.sum(p_w, axis=1, keepdims=True)
    o_w = _dot(p_w.astype(BF16), vw_ref[pl.ds(w0, wlen), :]) * (1.0 / l_w)

    gates = gate_ref[...]
    for h in range(A_HPG):
        r = slice(h * Q_BLOCK, (h + 1) * Q_BLOCK)
        o_s = o_st[:, r].T
        o = (gates[:, 3 * h:3 * h + 1] * o_c[r] + gates[:, 3 * h + 1:3 * h + 2] * o_s
             + gates[:, 3 * h + 2:3 * h + 3] * o_w[r])
        o_ref[:, h * HEAD_DIM:(h + 1) * HEAD_DIM] = o.astype(o_ref.dtype)


def _overlap_matrix(seq):
    n_chunk = seq // CMP_STRIDE
    cs = np.arange(n_chunk)[:, None] * CMP_STRIDE
    ss = np.arange(SEL_SLOTS)[None, :] * SEL_BLOCK
    ov = np.maximum(np.minimum(cs + CMP_BLOCK, ss + SEL_BLOCK) - np.maximum(cs, ss), 0) / CMP_BLOCK
    n_cmp = (seq - CMP_BLOCK) // CMP_STRIDE + 1
    ov[n_cmp:] = 0.0
    ov[:, seq // SEL_BLOCK:] = 0.0
    return ov.astype(np.float32)


def _nsa(q, kv, cmp_kv, gates):
    S = q.shape[0]
    G = A_KV_GROUPS
    assert S // SEL_BLOCK <= SEL_SLOTS and S % KEY_TILE == 0 and S >= WINDOW + Q_BLOCK
    n_chunk = S // CMP_STRIDE
    ov = jnp.asarray(_overlap_matrix(S), BF16)
    col = lambda c: pl.BlockSpec((S, HEAD_DIM), lambda g, i, c=c: (0, c * G + g))
    return pl.pallas_call(
        functools.partial(_nsa_kernel, seq=S),
        grid=(G, S // Q_BLOCK),
        in_specs=[
            pl.BlockSpec((Q_BLOCK, A_HPG * HEAD_DIM), lambda g, i: (i, g)),
            col(0), col(1), col(2), col(3),
            pl.BlockSpec((1, 1, n_chunk, HEAD_DIM), lambda g, i: (0, g, 0, 0)),
            pl.BlockSpec((1, 1, n_chunk, HEAD_DIM), lambda g, i: (1, g, 0, 0)),
            pl.BlockSpec((n_chunk, SEL_SLOTS), lambda g, i: (0, 0)),
            pl.BlockSpec((Q_BLOCK, LANES), lambda g, i: (i, g)),
        ],
        out_specs=pl.BlockSpec((Q_BLOCK, A_HPG * HEAD_DIM), lambda g, i: (i, g)),
        out_shape=jax.ShapeDtypeStruct((S, A_WIDTH), BF16),
        scratch_shapes=[pltpu.VMEM((S // KEY_TILE, KEY_TILE, HEAD_DIM + SEL_SLOTS), BF16),
                        pltpu.VMEM((S // KEY_TILE, HEAD_DIM, KEY_TILE), BF16)],
        compiler_params=_cparams(("arbitrary", "arbitrary")),
        name="nsa",
    )(q, kv, kv, kv, kv, cmp_kv, cmp_kv, ov, gates)


NSA_QB = 256
SEL_BODY = 4
QK_AHEAD = 2


def _nsa2_kernel(q_ref, ks_ref, kw_ref, vs_ref, vw_ref, kc_ref, vc_ref, ovt_ref, gate_ref,
                 o_ref, kaug_ref, vt_ref, vwt_ref, vct_ref, acc_ref, m_ref, l_ref, *, seq):
    i = pl.program_id(1)
    t0 = i * NSA_QB
    n_cmp = seq // CMP_STRIDE
    cols = A_HPG * NSA_QB

    @pl.when(i == 0)
    def _():
        def build(r, c):
            r0 = pl.multiple_of(r * KEY_TILE, KEY_TILE)
            kaug_ref[r, :, 0:HEAD_DIM] = ks_ref[pl.ds(r0, KEY_TILE), :]
            blk = (r0 + lax.broadcasted_iota(jnp.int32, (KEY_TILE, SEL_SLOTS), 0)) // SEL_BLOCK
            slot = lax.broadcasted_iota(jnp.int32, (KEY_TILE, SEL_SLOTS), 1)
            kaug_ref[r, :, HEAD_DIM:HEAD_DIM + SEL_SLOTS] = jnp.where(blk == slot, 1.0, 0.0).astype(BF16)
            vt_ref[r] = vs_ref[pl.ds(r0, KEY_TILE), :].astype(F32).T.astype(BF16)
            vwt_ref[r] = vw_ref[pl.ds(r0, KEY_TILE), :].astype(F32).T.astype(BF16)
            return c
        lax.fori_loop(0, seq // KEY_TILE, build, 0)
        vct_ref[...] = vc_ref[0, 0].T.astype(BF16)

    qt = jnp.concatenate(
        [q_ref[:, h * HEAD_DIM:(h + 1) * HEAD_DIM].astype(F32).T.astype(BF16) for h in range(A_HPG)],
        axis=1)
    tq = t0 + lax.broadcasted_iota(jnp.int32, (1, cols), 1) % NSA_QB

    def softmax0(s):
        m = jnp.max(s, axis=0, keepdims=True)
        p = jnp.exp2(s - m)
        return p, jnp.sum(p, axis=0, keepdims=True)

    s_c = _dot(kc_ref[0, 0].astype(BF16), qt)
    nrow = lax.broadcasted_iota(jnp.int32, (n_cmp, 1), 0)
    vis = (nrow * CMP_STRIDE + (CMP_BLOCK - 1)) <= tq
    p_c, den = softmax0(jnp.where(vis, s_c, NEG_BIG))
    p_c = p_c * jnp.where(tq >= CMP_BLOCK - 1, 1.0 / den, 0.0)
    o_ct = _dot(vct_ref[...], p_c.astype(BF16))

    wlen = WINDOW + NSA_QB
    w0 = pl.multiple_of(jnp.maximum(t0 - WINDOW, 0), KEY_TILE)
    wt = jnp.maximum(i - WINDOW // KEY_TILE, 0)
    s_w = _dot(kw_ref[pl.ds(w0, wlen), :], qt)
    dist = tq - (w0 + lax.broadcasted_iota(jnp.int32, (wlen, 1), 0))
    in_window = dist.astype(jnp.uint32) < WINDOW
    p_w, l_w = softmax0(jnp.where(in_window, s_w, NEG_BIG))
    p_w = p_w.astype(BF16)
    o_wt = _dot(vwt_ref[wt], p_w[0:KEY_TILE])
    for k in range(1, wlen // KEY_TILE):
        o_wt = o_wt + _dot(vwt_ref[wt + k], p_w[k * KEY_TILE:(k + 1) * KEY_TILE])
    o_wt = o_wt * (1.0 / l_w)

    psum = p_c[:, 0:NSA_QB]
    for h in range(1, A_HPG):
        psum = psum + p_c[:, h * NSA_QB:(h + 1) * NSA_QB]
    ovt = ovt_ref[...]
    imp_t = jnp.zeros((SEL_SLOTS, NSA_QB), F32)
    for piece in _split3(psum):
        imp_t = imp_t + _dot(ovt, piece)
    slot_i = lax.broadcasted_iota(jnp.int32, (SEL_SLOTS, NSA_QB), 0)
    tq_l = t0 + lax.broadcasted_iota(jnp.int32, (SEL_SLOTS, NSA_QB), 1)
    cur = tq_l // SEL_BLOCK
    forced = jnp.where(slot_i == 0, 1, 0) + jnp.where(slot_i == cur, 1, 0) + jnp.where(slot_i == cur - 1, 1, 0)
    valid = slot_i * SEL_BLOCK <= tq_l
    key = jnp.where(forced > 0, -NEG_BIG, jnp.where(valid, imp_t, NEG_BIG))
    slot_f = slot_i.astype(F32)
    bias_t = jnp.full((SEL_SLOTS, NSA_QB), NEG_BIG, F32)
    for _ in range(SEL_TOPK):
        mx = jnp.max(key, axis=0, keepdims=True)
        first = jnp.min(jnp.where(key == mx, slot_f, 1e9), axis=0, keepdims=True)
        pick = slot_f == first
        bias_t = jnp.where(pick, 0.0, bias_t)
        key = jnp.where(pick, -3e38, key)

    qt_aug = jnp.concatenate([qt, jnp.concatenate([bias_t.astype(BF16)] * A_HPG, axis=1)], axis=0)
    acc_ref[...] = jnp.zeros_like(acc_ref)
    m_ref[...] = jnp.full(m_ref.shape, 0.5 * NEG_BIG, F32)
    l_ref[...] = jnp.zeros_like(l_ref)

    def sel_body(base, causal):
        n = len(causal)
        m, l = m_ref[...], l_ref[...]
        s = {}
        for k in range(min(QK_AHEAD, n)):
            s[k] = _dot(kaug_ref[base + k], qt_aug)
        for k in range(n):
            if k + QK_AHEAD < n:
                s[k + QK_AHEAD] = _dot(kaug_ref[base + k + QK_AHEAD], qt_aug)
            sk = s.pop(k)
            if causal[k]:
                kpos = (base + k) * KEY_TILE + lax.broadcasted_iota(jnp.int32, (KEY_TILE, 1), 0)
                sk = jnp.where(kpos <= tq, sk, NEG_BIG)
            m_new = jnp.maximum(m, jnp.max(sk, axis=0, keepdims=True))
            alpha = jnp.exp2(m - m_new)
            p = jnp.exp2(sk - m_new)
            l = alpha * l + jnp.sum(p, axis=0, keepdims=True)
            acc_ref[...] = alpha * acc_ref[...] + _dot(vt_ref[base + k], p.astype(BF16))
            m = m_new
        m_ref[...] = m
        l_ref[...] = l

    n_bulk = i // SEL_BODY
    rem = i % SEL_BODY

    def bulk(b, c):
        sel_body(b * SEL_BODY, (False,) * SEL_BODY)
        return c
    lax.fori_loop(0, n_bulk, bulk, 0)

    @pl.when(rem >= 2)
    def _():
        sel_body(n_bulk * SEL_BODY, (False, False))

    @pl.when(rem % 2 == 1)
    def _():
        sel_body(i - 1, (False, True))

    @pl.when(rem % 2 == 0)
    def _():
        sel_body(i, (True,))

    o_st = acc_ref[...] * (1.0 / l_ref[...])

    gates_t = gate_ref[...].T
    for h in range(A_HPG):
        c = slice(h * NSA_QB, (h + 1) * NSA_QB)
        o = (gates_t[3 * h:3 * h + 1] * o_ct[:, c] + gates_t[3 * h + 1:3 * h + 2] * o_st[:, c]
             + gates_t[3 * h + 2:3 * h + 3] * o_wt[:, c])
        o_ref[:, h * HEAD_DIM:(h + 1) * HEAD_DIM] = o.T.astype(o_ref.dtype)


def _nsa2(q, kv, cmp_kv, gates):
    S = q.shape[0]
    G = A_KV_GROUPS
    n_tiles = S // KEY_TILE
    assert S // SEL_BLOCK <= SEL_SLOTS and S % (KEY_TILE * SEL_BODY) == 0 and S >= WINDOW + NSA_QB
    assert NSA_QB == KEY_TILE
    n_chunk = S // CMP_STRIDE
    ovt = jnp.asarray(_overlap_matrix(S).T, BF16)
    col = lambda c: pl.BlockSpec((S, HEAD_DIM), lambda g, i, c=c: (0, c * G + g))
    return pl.pallas_call(
        functools.partial(_nsa2_kernel, seq=S),
        grid=(G, S // NSA_QB),
        in_specs=[
            pl.BlockSpec((NSA_QB, A_HPG * HEAD_DIM), lambda g, i: (i, g)),
            col(0), col(1), col(2), col(3),
            pl.BlockSpec((1, 1, n_chunk, HEAD_DIM), lambda g, i: (0, g, 0, 0)),
            pl.BlockSpec((1, 1, n_chunk, HEAD_DIM), lambda g, i: (1, g, 0, 0)),
            pl.BlockSpec((SEL_SLOTS, n_chunk), lambda g, i: (0, 0)),
            pl.BlockSpec((NSA_QB, LANES), lambda g, i: (i, g)),
        ],
        out_specs=pl.BlockSpec((NSA_QB, A_HPG * HEAD_DIM), lambda g, i: (i, g)),
        out_shape=jax.ShapeDtypeStruct((S, A_WIDTH), BF16),
        scratch_shapes=[pltpu.VMEM((n_tiles, KEY_TILE, HEAD_DIM + SEL_SLOTS), BF16),
                        pltpu.VMEM((n_tiles, HEAD_DIM, KEY_TILE), BF16),
                        pltpu.VMEM((n_tiles, HEAD_DIM, KEY_TILE), BF16),
                        pltpu.VMEM((HEAD_DIM, n_chunk), BF16),
                        pltpu.VMEM((HEAD_DIM, A_HPG * NSA_QB), F32),
                        pltpu.VMEM((1, A_HPG * NSA_QB), F32),
                        pltpu.VMEM((1, A_HPG * NSA_QB), F32)],
        compiler_params=_cparams(("arbitrary", "arbitrary")),
        name="nsa",
    )(q, kv, kv, kv, kv, cmp_kv, cmp_kv, ovt, gates)


_ROW_MASKS = 18
_SCORE_MASKS = 10


def _hgrn_tables():
    t = np.arange(HG_CHUNK)
    p4, i4, i16 = t % 4, (t // 4) % 4, t // 16
    conds = [p4 >= 1, p4 >= 2, p4 >= 3, p4 <= 2, p4 <= 1, p4 == 0,
             i4 >= 1, i4 >= 2, i4 >= 3, i4 <= 2, i4 <= 1, i4 == 0,
             i16 >= 1, i16 >= 2, i16 >= 3, i16 <= 2, i16 <= 1, i16 == 0]
    rm = np.stack([np.broadcast_to(c[:, None], (HG_CHUNK, HG_DIM)) for c in conds]).astype(np.float32)
    tt, ss = t[:, None], t[None, :]
    lag16 = tt // 16 - ss // 16
    lag4 = tt // 4 - ss // 4
    same16, same4 = lag16 == 0, lag4 == 0
    sm = [lag16 == 1, lag16 == 2, lag16 == 3,
          same16 & (lag4 == 1), same16 & (lag4 == 2), same16 & (lag4 == 3),
          same4 & (tt - ss == 0), same4 & (tt - ss == 1), same4 & (tt - ss == 2), same4 & (tt - ss == 3)]
    return rm, np.stack(sm).astype(np.float32)


def _hgrn_kernel(q_ref, f_ref, i_ref, og_ref, loglb_ref, log1m_ref, gn_ref, rm_ref, sm_ref,
                 o_ref, st_ref, g_scr, k_scr, o_scr, *, tb):
    C = HG_CHUNK

    @pl.when(pl.program_id(1) == 0)
    def _():
        st_ref[...] = jnp.zeros_like(st_ref)

    x = f_ref[...]
    log_sig = jnp.minimum(x, 0.0) - jnp.log(1.0 + jnp.exp(-jnp.abs(x)))
    a = loglb_ref[...]
    c = log1m_ref[...] + log_sig
    log_f = jnp.maximum(a, c) + jnp.log(1.0 + jnp.exp(-jnp.abs(a - c)))
    g_scr[...] = log_f
    k_scr[...] = 1.0 - jnp.exp(log_f)

    def rmask(n):
        return rm_ref[n]

    def roll(v, s):
        return pltpu.roll(v, s % C, axis=0)

    def chunk(ci, carry):
        r0 = pl.multiple_of(ci * C, C)
        g = g_scr[pl.ds(r0, C), :]
        q = q_ref[pl.ds(r0, C), :]
        k = k_scr[pl.ds(r0, C), :]
        v = i_ref[pl.ds(r0, C), :].astype(BF16)
        pre4 = g + rmask(0) * roll(g, 1) + rmask(1) * roll(g, 2) + rmask(2) * roll(g, 3)
        suf4 = rmask(3) * roll(g, -1) + rmask(4) * roll(g, -2) + rmask(5) * roll(g, -3)
        tot4 = pre4 + suf4
        a1, a2, a3 = roll(tot4, 4), roll(tot4, 8), roll(tot4, 12)
        pre16 = pre4 + rmask(6) * a1 + rmask(7) * a2 + rmask(8) * a3
        suf16 = suf4 + rmask(9) * roll(tot4, -4) + rmask(10) * roll(tot4, -8) + rmask(11) * roll(tot4, -12)
        tot16 = pre16 + suf16
        d1, d2, d3 = roll(tot16, 16), roll(tot16, 32), roll(tot16, 48)
        pre64 = pre16 + rmask(12) * d1 + rmask(13) * d2 + rmask(14) * d3
        suf64 = suf16 + rmask(15) * roll(tot16, -16) + rmask(16) * roll(tot16, -32) + rmask(17) * roll(tot16, -48)
        tot64 = pre64 + suf64
        g1, g2 = roll(g, 1), roll(g, 2)

        def b16(z):
            return z.astype(BF16)

        q_a = q * jnp.exp(pre16)
        lhs_a = jnp.concatenate([b16(q_a), b16(q_a * jnp.exp(d1)), b16(q_a * jnp.exp(d1 + d2))], axis=0)
        s_a = _dot_nt(lhs_a, b16(k * jnp.exp(suf16)))
        q_b = q * jnp.exp(pre4)
        lhs_b = jnp.concatenate([b16(q_b), b16(q_b * jnp.exp(a1)), b16(q_b * jnp.exp(a1 + a2))], axis=0)
        s_b = _dot_nt(lhs_b, b16(k * jnp.exp(suf4)))
        q_c = q * jnp.exp(g)
        lhs_c = jnp.concatenate([b16(q), b16(q_c), b16(q_c * jnp.exp(g1)), b16(q_c * jnp.exp(g1 + g2))], axis=0)
        s_c = _dot_nt(lhs_c, b16(k))
        scores = (sm_ref[0] * s_a[0:C] + sm_ref[1] * s_a[C:2 * C] + sm_ref[2] * s_a[2 * C:3 * C]
                  + sm_ref[3] * s_b[0:C] + sm_ref[4] * s_b[C:2 * C] + sm_ref[5] * s_b[2 * C:3 * C]
                  + sm_ref[6] * s_c[0:C] + sm_ref[7] * s_c[C:2 * C] + sm_ref[8] * s_c[2 * C:3 * C]
                  + sm_ref[9] * s_c[3 * C:4 * C])
        st = st_ref[...]
        o = _dot(b16(scores), v) + _dot_nt(b16(q * jnp.exp(pre64)), b16(st))
        o_scr[pl.ds(r0, C), :] = o
        k_st = b16(k * jnp.exp(suf64))
        st_ref[...] = st * jnp.exp(tot64[0:1, :]) + _dot_tn(v, k_st)
        return carry

    lax.fori_loop(0, tb // C, chunk, 0)

    o = o_scr[...]
    ms = jnp.mean(o * o, axis=-1, keepdims=True)
    og = og_ref[...]
    y = o * lax.rsqrt(ms + NORM_EPS) * gn_ref[...] * (og * jax.nn.sigmoid(og))
    o_ref[...] = y.astype(o_ref.dtype)


def _hgrn(hz, loglb, log1mlb, gnorm):
    S = hz.shape[0]
    tb = 512 if S % 512 == 0 else S
    H = HG_HEADS
    rm, sm = _hgrn_tables()
    blk = lambda part: pl.BlockSpec((tb, HG_DIM), lambda h, c, part=part: (c, part * H + h))
    row = pl.BlockSpec((1, HG_DIM), lambda h, c: (0, h))
    return pl.pallas_call(
        functools.partial(_hgrn_kernel, tb=tb),
        grid=(H, S // tb),
        in_specs=[blk(0), blk(1), blk(2), blk(3), row, row, row,
                  pl.BlockSpec((_ROW_MASKS, HG_CHUNK, HG_DIM), lambda h, c: (0, 0, 0)),
                  pl.BlockSpec((_SCORE_MASKS, HG_CHUNK, HG_CHUNK), lambda h, c: (0, 0, 0))],
        out_specs=pl.BlockSpec((tb, HG_DIM), lambda h, c: (c, h)),
        out_shape=jax.ShapeDtypeStruct((S, HG_WIDTH), BF16),
        scratch_shapes=[pltpu.VMEM((HG_DIM, HG_DIM), F32), pltpu.VMEM((tb, HG_DIM), F32),
                        pltpu.VMEM((tb, HG_DIM), F32), pltpu.VMEM((tb, HG_DIM), F32)],
        compiler_params=_cparams(("arbitrary", "arbitrary")),
        name="hgrn",
    )(hz, hz, hz, hz, loglb, log1mlb, gnorm, jnp.asarray(rm), jnp.asarray(sm))


def _hgrn2_kernel(q_ref, f_ref, i_ref, og_ref, loglb_ref, log1m_ref, gn_ref, rm_ref, sm_ref,
                  o_ref, st_ref, g_scr, k_scr, o_scr, *, tb, nh):
    C = HG_CHUNK
    heads = [slice(h * HG_DIM, (h + 1) * HG_DIM) for h in range(nh)]

    @pl.when(pl.program_id(1) == 0)
    def _():
        st_ref[...] = jnp.zeros_like(st_ref)

    x = f_ref[...]
    log_sig = jnp.minimum(x, 0.0) - jnp.log(1.0 + jnp.exp(-jnp.abs(x)))
    a = loglb_ref[...]
    c = log1m_ref[...] + log_sig
    log_f = jnp.maximum(a, c) + jnp.log(1.0 + jnp.exp(-jnp.abs(a - c)))
    g_scr[...] = log_f
    k_scr[...] = 1.0 - jnp.exp(log_f)

    def rmask(n):
        return jnp.concatenate([rm_ref[n]] * nh, axis=1)

    def roll(v, s):
        return pltpu.roll(v, s % C, axis=0)

    def b16(z):
        return z.astype(BF16)

    def chunk(ci, carry):
        r0 = pl.multiple_of(ci * C, C)
        g = g_scr[pl.ds(r0, C), :]
        q = q_ref[pl.ds(r0, C), :]
        k = k_scr[pl.ds(r0, C), :]
        v = i_ref[pl.ds(r0, C), :].astype(BF16)
        pre4 = g + rmask(0) * roll(g, 1) + rmask(1) * roll(g, 2) + rmask(2) * roll(g, 3)
        suf4 = rmask(3) * roll(g, -1) + rmask(4) * roll(g, -2) + rmask(5) * roll(g, -3)
        tot4 = pre4 + suf4
        a1, a2, a3 = roll(tot4, 4), roll(tot4, 8), roll(tot4, 12)
        pre16 = pre4 + rmask(6) * a1 + rmask(7) * a2 + rmask(8) * a3
        suf16 = suf4 + rmask(9) * roll(tot4, -4) + rmask(10) * roll(tot4, -8) + rmask(11) * roll(tot4, -12)
        tot16 = pre16 + suf16
        d1, d2, d3 = roll(tot16, 16), roll(tot16, 32), roll(tot16, 48)
        pre64 = pre16 + rmask(12) * d1 + rmask(13) * d2 + rmask(14) * d3
        suf64 = suf16 + rmask(15) * roll(tot16, -16) + rmask(16) * roll(tot16, -32) + rmask(17) * roll(tot16, -48)
        tot64 = pre64 + suf64
        g1, g2 = roll(g, 1), roll(g, 2)

        q_a = q * jnp.exp(pre16)
        lhs_a = [b16(q_a), b16(q_a * jnp.exp(d1)), b16(q_a * jnp.exp(d1 + d2))]
        k_a = b16(k * jnp.exp(suf16))
        q_b = q * jnp.exp(pre4)
        lhs_b = [b16(q_b), b16(q_b * jnp.exp(a1)), b16(q_b * jnp.exp(a1 + a2))]
        k_b = b16(k * jnp.exp(suf4))
        q_c = q * jnp.exp(g)
        lhs_c = [b16(q), b16(q_c), b16(q_c * jnp.exp(g1)), b16(q_c * jnp.exp(g1 + g2))]
        k_c = b16(k)
        q_s = b16(q * jnp.exp(pre64))
        k_s = b16(k * jnp.exp(suf64))
        decay = jnp.exp(tot64[0:1, :])

        def level(lhs, rhs):
            return [_dot_nt(jnp.concatenate([z[:, hh] for z in lhs], axis=0), rhs[:, hh]) for hh in heads]

        s_a, s_b, s_c = level(lhs_a, k_a), level(lhs_b, k_b), level(lhs_c, k_c)
        o_inter = [_dot_nt(q_s[:, hh], b16(st_ref[h])) for h, hh in enumerate(heads)]
        kv = [_dot_tn(v[:, hh], k_s[:, hh]) for hh in heads]
        for h, hh in enumerate(heads):
            sa, sb, sc = s_a[h], s_b[h], s_c[h]
            scores = (sm_ref[0] * sa[0:C] + sm_ref[1] * sa[C:2 * C] + sm_ref[2] * sa[2 * C:3 * C]
                      + sm_ref[3] * sb[0:C] + sm_ref[4] * sb[C:2 * C] + sm_ref[5] * sb[2 * C:3 * C]
                      + sm_ref[6] * sc[0:C] + sm_ref[7] * sc[C:2 * C] + sm_ref[8] * sc[2 * C:3 * C]
                      + sm_ref[9] * sc[3 * C:4 * C])
            o_scr[pl.ds(r0, C), hh] = _dot(b16(scores), v[:, hh]) + o_inter[h]
            st_ref[h] = st_ref[h] * decay[:, hh] + kv[h]
        return carry

    lax.fori_loop(0, tb // C, chunk, 0)

    for hh in heads:
        o = o_scr[:, hh]
        ms = jnp.mean(o * o, axis=-1, keepdims=True)
        og = og_ref[:, hh]
        y = o * lax.rsqrt(ms + NORM_EPS) * gn_ref[:, hh] * (og * jax.nn.sigmoid(og))
        o_ref[:, hh] = y.astype(o_ref.dtype)


HG_HEADS_PER_STEP = 4


def _hgrn2(hz, loglb, log1mlb, gnorm):
    S = hz.shape[0]
    tb = 512 if S % 512 == 0 else S
    nh = HG_HEADS_PER_STEP
    ng = HG_HEADS // nh
    w = nh * HG_DIM
    rm, sm = _hgrn_tables()
    blk = lambda part: pl.BlockSpec((tb, w), lambda h, c, part=part: (c, part * ng + h))
    row = pl.BlockSpec((1, w), lambda h, c: (0, h))
    return pl.pallas_call(
        functools.partial(_hgrn2_kernel, tb=tb, nh=nh),
        grid=(ng, S // tb),
        in_specs=[blk(0), blk(1), blk(2), blk(3), row, row, row,
                  pl.BlockSpec((_ROW_MASKS, HG_CHUNK, HG_DIM), lambda h, c: (0, 0, 0)),
                  pl.BlockSpec((_SCORE_MASKS, HG_CHUNK, HG_CHUNK), lambda h, c: (0, 0, 0))],
        out_specs=pl.BlockSpec((tb, w), lambda h, c: (c, h)),
        out_shape=jax.ShapeDtypeStruct((S, HG_WIDTH), BF16),
        scratch_shapes=[pltpu.VMEM((nh, HG_DIM, HG_DIM), F32), pltpu.VMEM((tb, w), F32),
                        pltpu.VMEM((tb, w), F32), pltpu.VMEM((tb, w), F32)],
        compiler_params=_cparams(("arbitrary", "arbitrary")),
        name="hgrn",
    )(hz, hz, hz, hz, loglb, log1mlb, gnorm, jnp.asarray(rm), jnp.asarray(sm))


def _merge_kernel(at_ref, hg_ref, ga_ref, gh_ref, wa_ref, wh_ref, wo_ref, x_ref, gt_ref, gp_ref, o_ref):
    mixed = (ga_ref[...].astype(F32) * _dot(at_ref[...], wa_ref[...])
             + gh_ref[...].astype(F32) * _dot(hg_ref[...], wh_ref[...]))
    y = _dot(mixed.astype(BF16), wo_ref[...])
    ms = jnp.mean(y * y, axis=-1, keepdims=True)
    yn = y * lax.rsqrt(ms + NORM_EPS) * gp_ref[...]
    o_ref[...] = x_ref[...] + gt_ref[...] * yn


def _merge(attn, hg, mg, wa, wh, wo, x, gt, gpost):
    S, D = x.shape
    tm = 256
    const = lambda shape: pl.BlockSpec(shape, lambda m: (0, 0), pipeline_mode=pl.Buffered(1))
    row = pl.BlockSpec((1, D), lambda m: (0, 0))
    return pl.pallas_call(
        _merge_kernel,
        grid=(S // tm,),
        in_specs=[
            pl.BlockSpec((tm, A_WIDTH), lambda m: (m, 0)),
            pl.BlockSpec((tm, HG_WIDTH), lambda m: (m, 0)),
            pl.BlockSpec((tm, D), lambda m: (m, 0)),
            pl.BlockSpec((tm, D), lambda m: (m, 1)),
            const((A_WIDTH, D)), const((HG_WIDTH, D)), const((D, D)),
            pl.BlockSpec((tm, D), lambda m: (m, 0)),
            row, row,
        ],
        out_specs=pl.BlockSpec((tm, D), lambda m: (m, 0)),
        out_shape=jax.ShapeDtypeStruct((S, D), F32),
        compiler_params=_cparams(("arbitrary",)),
        name="merge",
    )(attn, hg, mg, mg, wa, wh, wo, x, gt, gpost.reshape(1, D))


_HALO = 16


def _ffn_kernel(x_ref, xh_ref, g_ref, sc_ref, sh_ref, wg_ref, wv_ref, cwg_ref, cwv_ref, cbg_ref, cbv_ref,
                wd_ref, gt_ref, gp_ref, o_ref, h_scr, acc_scr, ug_scr, uv_scr, *, tm):
    m = pl.program_id(0)
    f = pl.program_id(1)

    def norm_mod(x):
        ms = jnp.mean(x * x, axis=-1, keepdims=True)
        y = x * lax.rsqrt(ms + NORM_EPS) * g_ref[...]
        return (y * (1.0 + sc_ref[...]) + sh_ref[...]).astype(BF16)

    @pl.when(f == 0)
    def _():
        h_scr[0:_HALO, :] = norm_mod(xh_ref[...])
        h_scr[_HALO:_HALO + tm, :] = norm_mod(x_ref[...])
        acc_scr[...] = jnp.zeros_like(acc_scr)

    h = h_scr[...]
    ug_scr[...] = _dot(h, wg_ref[...])
    uv_scr[...] = _dot(h, wv_ref[...])

    @pl.when(m == 0)
    def _():
        ug_scr[0:_HALO, :] = jnp.zeros((_HALO, ug_scr.shape[1]), F32)
        uv_scr[0:_HALO, :] = jnp.zeros((_HALO, uv_scr.shape[1]), F32)

    def conv(u_scr, cw_ref, cb_ref):
        out = cb_ref[...] + cw_ref[0:1, :] * u_scr[pl.ds(_HALO - 2, tm), :]
        out = out + cw_ref[1:2, :] * u_scr[pl.ds(_HALO - 1, tm), :]
        return out + cw_ref[2:3, :] * u_scr[pl.ds(_HALO, tm), :]

    cg = conv(ug_scr, cwg_ref, cbg_ref)
    cv = conv(uv_scr, cwv_ref, cbv_ref)
    act = (cg * jax.nn.sigmoid(cg) * cv).astype(BF16)
    acc_scr[...] += _dot(act, wd_ref[...])

    @pl.when(f == pl.num_programs(1) - 1)
    def _():
        y = acc_scr[...]
        ms = jnp.mean(y * y, axis=-1, keepdims=True)
        yn = y * lax.rsqrt(ms + NORM_EPS) * gp_ref[...]
        o_ref[...] = x_ref[...] + gt_ref[...] * yn


def _ffn(x, gpre, sc, sh, wup, convw, convb, wdown, gt, gpost):
    S, D = x.shape
    F = wdown.shape[0]
    tm = 512 if S % 512 == 0 else S
    tf = 512
    nf = F // tf
    hb = tm // _HALO
    row = pl.BlockSpec((1, D), lambda m, f: (0, 0))
    return pl.pallas_call(
        functools.partial(_ffn_kernel, tm=tm),
        grid=(S // tm, nf),
        in_specs=[
            pl.BlockSpec((tm, D), lambda m, f: (m, 0)),
            pl.BlockSpec((_HALO, D), lambda m, f: (jnp.maximum(m * hb - 1, 0), 0)),
            row, row, row,
            pl.BlockSpec((D, tf), lambda m, f: (0, f)),
            pl.BlockSpec((D, tf), lambda m, f: (0, nf + f)),
            pl.BlockSpec((CONV_WIDTH, tf), lambda m, f: (0, f)),
            pl.BlockSpec((CONV_WIDTH, tf), lambda m, f: (0, nf + f)),
            pl.BlockSpec((1, tf), lambda m, f: (0, f)),
            pl.BlockSpec((1, tf), lambda m, f: (0, nf + f)),
            pl.BlockSpec((tf, D), lambda m, f: (f, 0)),
            row, row,
        ],
        out_specs=pl.BlockSpec((tm, D), lambda m, f: (m, 0)),
        out_shape=jax.ShapeDtypeStruct((S, D), F32),
        scratch_shapes=[pltpu.VMEM((tm + _HALO, D), BF16), pltpu.VMEM((tm, D), F32),
                        pltpu.VMEM((tm + _HALO, tf), F32), pltpu.VMEM((tm + _HALO, tf), F32)],
        compiler_params=_cparams(("arbitrary", "arbitrary")),
        name="ffn",
    )(x, x, gpre.reshape(1, D), sc, sh, wup, wup, convw, convw, convb.reshape(1, -1), convb.reshape(1, -1),
      wdown, gt, gpost.reshape(1, D))


def _rope_tables(positions):
    inv_freq = 1.0 / (ROPE_THETA ** (jnp.arange(0, HEAD_DIM, 2, dtype=F32) / HEAD_DIM))
    ang = positions.astype(F32)[:, None] * inv_freq
    cos, sin = jnp.cos(ang), jnp.sin(ang)
    return jnp.concatenate([cos, cos], axis=-1), jnp.concatenate([-sin, sin], axis=-1)


def _mixer_weights(w_in_l):
    D = w_in_l.shape[0]
    o = 0
    wq = w_in_l[:, o:o + A_WIDTH]; o += A_WIDTH
    kvs = []
    for _ in range(6):
        kvs.append(w_in_l[:, o:o + A_KV_WIDTH]); o += A_KV_WIDTH
    kc, vc, ks, vs, kw, vw = kvs
    wgate = w_in_l[:, o:o + 3 * A_HEADS]; o += 3 * A_HEADS
    whg = w_in_l[:, o:o + 4 * HG_WIDTH]; o += 4 * HG_WIDTH
    wmg = w_in_l[:, o:]
    per_group = 3 * A_HPG
    pad = jnp.zeros((D, LANES - per_group), w_in_l.dtype)
    wgate = jnp.concatenate(
        [t for g in range(A_KV_GROUPS) for t in (wgate[:, g * per_group:(g + 1) * per_group], pad)], axis=1)
    return (wq.astype(BF16), jnp.concatenate([ks, kw, vs, vw], axis=1).astype(BF16),
            jnp.concatenate([kc, vc], axis=1).astype(BF16), wgate.astype(BF16),
            whg.astype(BF16), wmg.astype(BF16))


def kernel(x, c, positions, w_ada, b_ada, g_pre_mix, w_in, pe_kc, w_kc, pe_vc, w_vc, lb_logits, g_hg_norm,
           w_br_attn, w_br_hgrn, w_out, g_post_mix, g_pre_ffn, w_up, conv_w, conv_b, w_down, g_post_ffn):
    B, S, D = x.shape
    assert B == 1, "kernel is written for one sequence"
    L = w_ada.shape[0]
    xs = x[0]
    cosf, sinf = _rope_tables(positions[0])
    scale = HEAD_DIM ** -0.5 * np.log2(np.e)
    q_tabs = (cosf * scale, sinf * scale)
    k_tabs = (cosf, sinf)
    lb_cum = jnp.cumsum(jax.nn.softmax(lb_logits.astype(F32), axis=0), axis=0)
    lower = lb_cum - lb_cum[0:1]
    log_lb = jnp.log(lower)
    log_1m = jnp.log1p(-lower)
    ada = _ada_all(c, w_ada, b_ada)

    for l in range(L):
        sh1, sc1, gt1, sh2, sc2, gt2 = [ada[l, :, j * D:(j + 1) * D] for j in range(N_ADA)]
        wq, wkv, wcmp, wgate, whg, wmg = _mixer_weights(w_in[l])

        h = _prenorm(xs, g_pre_mix[l], sc1, sh1)
        q = _proj(h, wq, BF16, tn=512, rope_cols=A_WIDTH, tables=q_tabs, name="proj_q")
        kv = _proj(h, wkv, BF16, tn=512, rope_cols=2 * A_KV_WIDTH, tables=k_tabs, name="proj_kv")
        cmp_raw = _proj(h, wcmp, F32, tn=256, rope_cols=A_KV_WIDTH, tables=k_tabs, name="proj_cmp")
        gates = _proj(h, wgate, F32, tn=256, act="sigmoid", name="proj_gate")
        hz = _proj(h, whg, F32, tn=512, name="proj_hgrn")
        mg = _proj(h, wmg, BF16, tn=512, act="sigmoid", name="proj_merge_gate")

        cmp_kv = _compress(cmp_raw, jnp.stack([pe_kc[l], pe_vc[l]]), jnp.stack([w_kc[l], w_vc[l]]))
        attn = _nsa2(q, kv, cmp_kv, gates)
        hg = _hgrn2(hz, log_lb[l:l + 1], log_1m[l:l + 1], g_hg_norm[l:l + 1])
        xs = _merge(attn, hg, mg, w_br_attn[l].astype(BF16), w_br_hgrn[l].astype(BF16),
                    w_out[l].astype(BF16), xs, gt1, g_post_mix[l])

        xs = _ffn(xs, g_pre_ffn[l], sc2, sh2, w_up[l].astype(BF16), conv_w[l], conv_b[l],
                  w_down[l].astype(BF16), gt2, g_post_ffn[l])
    return xs[None]
```

```python
import functools

import jax
import jax.numpy as jnp
import numpy as np
from jax import lax
from jax.experimental import pallas as pl
from jax.experimental.pallas import tpu as pltpu

F32 = jnp.float32
BF16 = jnp.bfloat16

A_HEADS = 8
A_KV_GROUPS = 2
A_HPG = A_HEADS // A_KV_GROUPS
HEAD_DIM = 128
A_WIDTH = A_HEADS * HEAD_DIM
A_KV_WIDTH = A_KV_GROUPS * HEAD_DIM
CMP_BLOCK = 32
CMP_STRIDE = 16
SEL_BLOCK = 64
SEL_TOPK = 16
WINDOW = 512
ROPE_THETA = 10000.0
HG_HEADS = 8
HG_DIM = 128
HG_WIDTH = HG_HEADS * HG_DIM
HG_CHUNK = 64
CONV_WIDTH = 3
NORM_EPS = 1e-6
N_ADA = 6
A_COLS = A_WIDTH + 6 * A_KV_WIDTH + 3 * A_HEADS

LANES = 128
SEL_SLOTS = LANES
KEY_TILE = 256
NEG_BIG = -1e30
VMEM_LIMIT = 56 * 1024 * 1024


def _cparams(sem):
    return pltpu.CompilerParams(dimension_semantics=sem, vmem_limit_bytes=VMEM_LIMIT)


def _split3(a):
    hi = a.astype(BF16)
    r1 = a - hi.astype(F32)
    mid = r1.astype(BF16)
    lo = (r1 - mid.astype(F32)).astype(BF16)
    return hi, mid, lo


def _dot(a, b):
    return jnp.dot(a, b, preferred_element_type=F32)


def _dot_nt(a, b):
    return lax.dot_general(a, b, (((1,), (1,)), ((), ())), preferred_element_type=F32)


def _dot_tn(a, b):
    return lax.dot_general(a, b, (((0,), (0,)), ((), ())), preferred_element_type=F32)


def _ada_kernel(c_ref, w_ref, b_ref, o_ref):
    c = c_ref[...]
    ca = c * jax.nn.sigmoid(c)
    o_ref[0] = jnp.sum(ca * w_ref[0], axis=0, keepdims=True) + b_ref[0]


def _ada_all(c, w_ada, b_ada):
    L, D, N = w_ada.shape
    tn = 1024
    return pl.pallas_call(
        _ada_kernel,
        grid=(L, N // tn),
        in_specs=[
            pl.BlockSpec((D, 1), lambda l, n: (0, 0)),
            pl.BlockSpec((1, D, tn), lambda l, n: (l, 0, n)),
            pl.BlockSpec((1, 1, tn), lambda l, n: (l, 0, n)),
        ],
        out_specs=pl.BlockSpec((1, 1, tn), lambda l, n: (l, 0, n)),
        out_shape=jax.ShapeDtypeStruct((L, 1, N), F32),
        compiler_params=_cparams(("arbitrary", "arbitrary")),
        name="ada",
    )(c.reshape(D, 1), w_ada, b_ada.reshape(L, 1, N))


def _prenorm_kernel(x_ref, g_ref, sc_ref, sh_ref, o_ref):
    x = x_ref[...]
    ms = jnp.mean(x * x, axis=-1, keepdims=True)
    y = x * lax.rsqrt(ms + NORM_EPS) * g_ref[...]
    o_ref[...] = (y * (1.0 + sc_ref[...]) + sh_ref[...]).astype(o_ref.dtype)


def _prenorm(x, g, sc, sh):
    S, D = x.shape
    tm = 512
    row = pl.BlockSpec((1, D), lambda m: (0, 0))
    return pl.pallas_call(
        _prenorm_kernel,
        grid=(S // tm,),
        in_specs=[pl.BlockSpec((tm, D), lambda m: (m, 0)), row, row, row],
        out_specs=pl.BlockSpec((tm, D), lambda m: (m, 0)),
        out_shape=jax.ShapeDtypeStruct((S, D), BF16),
        compiler_params=_cparams(("arbitrary",)),
        name="prenorm",
    )(x, g.reshape(1, D), sc, sh)


def _rope(t, cosf, sinf):
    return t * cosf + pltpu.roll(t, HEAD_DIM // 2, axis=1) * sinf


def _proj_kernel(h_ref, w_ref, *rest, roles, tn):
    o_ref = rest[-1]
    tabs = rest[:-1]
    acc = _dot(h_ref[...], w_ref[0])
    n = pl.program_id(1)

    def emit(kind):
        if kind == "id":
            o_ref[...] = acc.astype(o_ref.dtype)
        elif kind == "sg":
            o_ref[...] = jax.nn.sigmoid(acc).astype(o_ref.dtype)
        else:
            cos_ref, sin_ref = tabs[0:2] if kind == "rq" else tabs[-2:]
            cosf, sinf = cos_ref[...], sin_ref[...]
            for s in range(tn // HEAD_DIM):
                sl = slice(s * HEAD_DIM, (s + 1) * HEAD_DIM)
                o_ref[:, sl] = _rope(acc[:, sl], cosf, sinf).astype(o_ref.dtype)

    kinds = sorted(set(roles))
    if len(kinds) == 1:
        emit(kinds[0])
        return
    for kind in kinds:
        hit = functools.reduce(jnp.logical_or, [n == t for t, r in enumerate(roles) if r == kind])
        pl.when(hit)(functools.partial(emit, kind))


def _proj(h, w3, layer, col0, roles, out_dtype, *, tn, tables=(), name="proj"):
    S, D = h.shape
    tm = 1024 if S % 1024 == 0 else S
    nt = len(roles)
    tab = pl.BlockSpec((tm, HEAD_DIM), lambda m, n: (m, 0))
    return pl.pallas_call(
        functools.partial(_proj_kernel, roles=tuple(roles), tn=tn),
        grid=(S // tm, nt),
        in_specs=[pl.BlockSpec((tm, D), lambda m, n: (m, 0)),
                  pl.BlockSpec((1, D, tn), lambda m, n: (layer, 0, col0 + n))] + [tab] * len(tables),
        out_specs=pl.BlockSpec((tm, tn), lambda m, n: (m, n)),
        out_shape=jax.ShapeDtypeStruct((S, nt * tn), out_dtype),
        compiler_params=_cparams(("arbitrary", "arbitrary")),
        name=name,
    )(h, w3, *tables)


A_TILE = 256
A_ROLES = ("rq",) * (A_WIDTH // A_TILE) + ("rk", "id", "rk", "id", "rk", "id", "sg")
_KV_BLOCK0 = A_WIDTH // LANES
BLK_KC, BLK_VC, BLK_KS, BLK_VS, BLK_KW, BLK_VW = [_KV_BLOCK0 + j * A_KV_GROUPS for j in range(6)]
BLK_GATE = _KV_BLOCK0 + 6 * A_KV_GROUPS


def _compress_kernel(t_ref, pe_ref, w_ref, o_ref, t32_ref, *, n_chunk):
    half = CMP_BLOCK // 2
    t32_ref[...] = t_ref[...].astype(F32)
    top, bot = [], []
    for j in range(half):
        a = t32_ref[pl.ds(j, n_chunk, stride=CMP_STRIDE), :]
        top.append(a + pe_ref[0, j:j + 1, :])
        bot.append(a + pe_ref[0, half + j:half + j + 1, :])
    a_top = jnp.concatenate(top, axis=1)
    a_bot = jnp.concatenate(bot, axis=1)
    kw = half * HEAD_DIM
    w = w_ref[0]
    w_top = w[:kw].astype(BF16)
    w_bot = w[kw:].astype(BF16)
    y_top = jnp.zeros((n_chunk, HEAD_DIM), F32)
    y_bot = jnp.zeros((n_chunk, HEAD_DIM), F32)
    for piece in _split3(a_top):
        y_top = y_top + _dot(piece, w_top)
    for piece in _split3(a_bot):
        y_bot = y_bot + _dot(piece, w_bot)
    o_ref[0, 0] = y_top + pltpu.roll(y_bot, n_chunk - 1, axis=0)


def _compress(a, pe2, w2):
    S = a.shape[0]
    n_chunk = S // CMP_STRIDE
    G = A_KV_GROUPS
    return pl.pallas_call(
        functools.partial(_compress_kernel, n_chunk=n_chunk),
        grid=(2, G),
        in_specs=[
            pl.BlockSpec((S, HEAD_DIM), lambda kv, g: (0, BLK_KC + kv * G + g)),
            pl.BlockSpec((1, CMP_BLOCK, HEAD_DIM), lambda kv, g: (kv, 0, 0)),
            pl.BlockSpec((1, CMP_BLOCK * HEAD_DIM, HEAD_DIM), lambda kv, g: (kv, 0, 0)),
        ],
        out_specs=pl.BlockSpec((1, 1, n_chunk, HEAD_DIM), lambda kv, g: (kv, g, 0, 0)),
        out_shape=jax.ShapeDtypeStruct((2, G, n_chunk, HEAD_DIM), F32),
        scratch_shapes=[pltpu.VMEM((S, HEAD_DIM), F32)],
        compiler_params=_cparams(("arbitrary", "arbitrary")),
        name="compress",
    )(a, pe2, w2)


NSA_QB = 256
SEL_BODY = 4
QK_AHEAD = 2


def _nsa_kernel(q_ref, ks_ref, kw_ref, vs_ref, vw_ref, kc_ref, vc_ref, ovt_ref, gate_ref,
                o_ref, kaug_ref, vt_ref, vwt_ref, vct_ref, acc_ref, m_ref, l_ref, *, seq):
    grp = pl.program_id(0)
    i = pl.program_id(1)
    t0 = i * NSA_QB
    n_cmp = seq // CMP_STRIDE
    cols = A_HPG * NSA_QB

    @pl.when(i == 0)
    def _():
        def build(r, c):
            r0 = pl.multiple_of(r * KEY_TILE, KEY_TILE)
            kaug_ref[r, :, 0:HEAD_DIM] = ks_ref[pl.ds(r0, KEY_TILE), :]
            blk = (r0 + lax.broadcasted_iota(jnp.int32, (KEY_TILE, SEL_SLOTS), 0)) // SEL_BLOCK
            slot = lax.broadcasted_iota(jnp.int32, (KEY_TILE, SEL_SLOTS), 1)
            kaug_ref[r, :, HEAD_DIM:HEAD_DIM + SEL_SLOTS] = jnp.where(blk == slot, 1.0, 0.0).astype(BF16)
            vt_ref[r] = vs_ref[pl.ds(r0, KEY_TILE), :].astype(F32).T.astype(BF16)
            vwt_ref[r] = vw_ref[pl.ds(r0, KEY_TILE), :].astype(F32).T.astype(BF16)
            return c
        lax.fori_loop(0, seq // KEY_TILE, build, 0)
        vct_ref[...] = vc_ref[0, 0].T.astype(BF16)

    qt = jnp.concatenate(
        [q_ref[:, h * HEAD_DIM:(h + 1) * HEAD_DIM].astype(F32).T.astype(BF16) for h in range(A_HPG)],
        axis=1)
    tq = t0 + lax.broadcasted_iota(jnp.int32, (1, cols), 1) % NSA_QB

    def softmax0(s):
        m = jnp.max(s, axis=0, keepdims=True)
        p = jnp.exp2(s - m)
        return p, jnp.sum(p, axis=0, keepdims=True)

    s_c = _dot(kc_ref[0, 0].astype(BF16), qt)
    nrow = lax.broadcasted_iota(jnp.int32, (n_cmp, 1), 0)
    vis = (nrow * CMP_STRIDE + (CMP_BLOCK - 1)) <= tq
    p_c, den = softmax0(jnp.where(vis, s_c, NEG_BIG))
    p_c = p_c * jnp.where(tq >= CMP_BLOCK - 1, 1.0 / den, 0.0)
    o_ct = _dot(vct_ref[...], p_c.astype(BF16))

    wlen = WINDOW + NSA_QB
    w0 = pl.multiple_of(jnp.maximum(t0 - WINDOW, 0), KEY_TILE)
    wt = jnp.maximum(i - WINDOW // KEY_TILE, 0)
    s_w = _dot(kw_ref[pl.ds(w0, wlen), :], qt)
    dist = tq - (w0 + lax.broadcasted_iota(jnp.int32, (wlen, 1), 0))
    in_window = dist.astype(jnp.uint32) < WINDOW
    p_w, l_w = softmax0(jnp.where(in_window, s_w, NEG_BIG))
    p_w = p_w.astype(BF16)
    o_wt = _dot(vwt_ref[wt], p_w[0:KEY_TILE])
    for k in range(1, wlen // KEY_TILE):
        o_wt = o_wt + _dot(vwt_ref[wt + k], p_w[k * KEY_TILE:(k + 1) * KEY_TILE])
    o_wt = o_wt * (1.0 / l_w)

    psum = p_c[:, 0:NSA_QB]
    for h in range(1, A_HPG):
        psum = psum + p_c[:, h * NSA_QB:(h + 1) * NSA_QB]
    ovt = ovt_ref[...]
    imp_t = jnp.zeros((SEL_SLOTS, NSA_QB), F32)
    for piece in _split3(psum):
        imp_t = imp_t + _dot(ovt, piece)
    slot_i = lax.broadcasted_iota(jnp.int32, (SEL_SLOTS, NSA_QB), 0)
    tq_l = t0 + lax.broadcasted_iota(jnp.int32, (SEL_SLOTS, NSA_QB), 1)
    cur = tq_l // SEL_BLOCK
    forced = jnp.where(slot_i == 0, 1, 0) + jnp.where(slot_i == cur, 1, 0) + jnp.where(slot_i == cur - 1, 1, 0)
    valid = slot_i * SEL_BLOCK <= tq_l
    key = jnp.where(forced > 0, -NEG_BIG, jnp.where(valid, imp_t, NEG_BIG))
    slot_f = slot_i.astype(F32)
    bias_t = jnp.full((SEL_SLOTS, NSA_QB), NEG_BIG, F32)
    for _ in range(SEL_TOPK):
        mx = jnp.max(key, axis=0, keepdims=True)
        first = jnp.min(jnp.where(key == mx, slot_f, 1e9), axis=0, keepdims=True)
        pick = slot_f == first
        bias_t = jnp.where(pick, 0.0, bias_t)
        key = jnp.where(pick, -3e38, key)

    qt_aug = jnp.concatenate([qt, jnp.concatenate([bias_t.astype(BF16)] * A_HPG, axis=1)], axis=0)
    acc_ref[...] = jnp.zeros_like(acc_ref)
    m_ref[...] = jnp.full(m_ref.shape, 0.5 * NEG_BIG, F32)
    l_ref[...] = jnp.zeros_like(l_ref)

    def sel_body(base, causal):
        n = len(causal)
        m, l = m_ref[...], l_ref[...]
        s = {}
        for k in range(min(QK_AHEAD, n)):
            s[k] = _dot(kaug_ref[base + k], qt_aug)
        for k in range(n):
            if k + QK_AHEAD < n:
                s[k + QK_AHEAD] = _dot(kaug_ref[base + k + QK_AHEAD], qt_aug)
            sk = s.pop(k)
            if causal[k]:
                kpos = (base + k) * KEY_TILE + lax.broadcasted_iota(jnp.int32, (KEY_TILE, 1), 0)
                sk = jnp.where(kpos <= tq, sk, NEG_BIG)
            m_new = jnp.maximum(m, jnp.max(sk, axis=0, keepdims=True))
            alpha = jnp.exp2(m - m_new)
            p = jnp.exp2(sk - m_new)
            l = alpha * l + jnp.sum(p, axis=0, keepdims=True)
            acc_ref[...] = alpha * acc_ref[...] + _dot(vt_ref[base + k], p.astype(BF16))
            m = m_new
        m_ref[...] = m
        l_ref[...] = l

    n_bulk = i // SEL_BODY
    rem = i % SEL_BODY

    def bulk(b, c):
        sel_body(b * SEL_BODY, (False,) * SEL_BODY)
        return c
    lax.fori_loop(0, n_bulk, bulk, 0)

    @pl.when(rem >= 2)
    def _():
        sel_body(n_bulk * SEL_BODY, (False, False))

    @pl.when(rem % 2 == 1)
    def _():
        sel_body(i - 1, (False, True))

    @pl.when(rem % 2 == 0)
    def _():
        sel_body(i, (True,))

    o_st = acc_ref[...] * (1.0 / l_ref[...])

    gates_t = gate_ref[...].astype(F32).T
    per_group = 3 * A_HPG

    def gate(h, b):
        r = 3 * h + b
        return jnp.where(grp == 0, gates_t[r:r + 1], gates_t[per_group + r:per_group + r + 1])

    for h in range(A_HPG):
        c = slice(h * NSA_QB, (h + 1) * NSA_QB)
        o = gate(h, 0) * o_ct[:, c] + gate(h, 1) * o_st[:, c] + gate(h, 2) * o_wt[:, c]
        o_ref[:, h * HEAD_DIM:(h + 1) * HEAD_DIM] = o.T.astype(o_ref.dtype)


def _overlap_matrix(seq):
    n_chunk = seq // CMP_STRIDE
    cs = np.arange(n_chunk)[:, None] * CMP_STRIDE
    ss = np.arange(SEL_SLOTS)[None, :] * SEL_BLOCK
    ov = np.maximum(np.minimum(cs + CMP_BLOCK, ss + SEL_BLOCK) - np.maximum(cs, ss), 0) / CMP_BLOCK
    n_cmp = (seq - CMP_BLOCK) // CMP_STRIDE + 1
    ov[n_cmp:] = 0.0
    ov[:, seq // SEL_BLOCK:] = 0.0
    return ov.astype(np.float32)


def _nsa(a, cmp_kv):
    S = a.shape[0]
    G = A_KV_GROUPS
    n_tiles = S // KEY_TILE
    assert S // SEL_BLOCK <= SEL_SLOTS and S % (KEY_TILE * SEL_BODY) == 0 and S >= WINDOW + NSA_QB
    assert NSA_QB == KEY_TILE and G == 2
    n_chunk = S // CMP_STRIDE
    ovt = jnp.asarray(_overlap_matrix(S).T, BF16)
    col = lambda b: pl.BlockSpec((S, HEAD_DIM), lambda g, i, b=b: (0, b + g))
    return pl.pallas_call(
        functools.partial(_nsa_kernel, seq=S),
        grid=(G, S // NSA_QB),
        in_specs=[
            pl.BlockSpec((NSA_QB, A_HPG * HEAD_DIM), lambda g, i: (i, g)),
            col(BLK_KS), col(BLK_KW), col(BLK_VS), col(BLK_VW),
            pl.BlockSpec((1, 1, n_chunk, HEAD_DIM), lambda g, i: (0, g, 0, 0)),
            pl.BlockSpec((1, 1, n_chunk, HEAD_DIM), lambda g, i: (1, g, 0, 0)),
            pl.BlockSpec((SEL_SLOTS, n_chunk), lambda g, i: (0, 0)),
            pl.BlockSpec((NSA_QB, LANES), lambda g, i: (i, BLK_GATE)),
        ],
        out_specs=pl.BlockSpec((NSA_QB, A_HPG * HEAD_DIM), lambda g, i: (i, g)),
        out_shape=jax.ShapeDtypeStruct((S, A_WIDTH), BF16),
        scratch_shapes=[pltpu.VMEM((n_tiles, KEY_TILE, HEAD_DIM + SEL_SLOTS), BF16),
                        pltpu.VMEM((n_tiles, HEAD_DIM, KEY_TILE), BF16),
                        pltpu.VMEM((n_tiles, HEAD_DIM, KEY_TILE), BF16),
                        pltpu.VMEM((HEAD_DIM, n_chunk), BF16),
                        pltpu.VMEM((HEAD_DIM, A_HPG * NSA_QB), F32),
                        pltpu.VMEM((1, A_HPG * NSA_QB), F32),
                        pltpu.VMEM((1, A_HPG * NSA_QB), F32)],
        compiler_params=_cparams(("arbitrary", "arbitrary")),
        name="nsa",
    )(a, a, a, a, a, cmp_kv, cmp_kv, ovt, a)


_ROW_MASKS = 18
_SCORE_MASKS = 10
HG_HEADS_PER_STEP = 4


def _hgrn_tables():
    t = np.arange(HG_CHUNK)
    p4, i4, i16 = t % 4, (t // 4) % 4, t // 16
    conds = [p4 >= 1, p4 >= 2, p4 >= 3, p4 <= 2, p4 <= 1, p4 == 0,
             i4 >= 1, i4 >= 2, i4 >= 3, i4 <= 2, i4 <= 1, i4 == 0,
             i16 >= 1, i16 >= 2, i16 >= 3, i16 <= 2, i16 <= 1, i16 == 0]
    rm = np.stack([np.broadcast_to(c[:, None], (HG_CHUNK, HG_DIM)) for c in conds]).astype(np.float32)
    tt, ss = t[:, None], t[None, :]
    lag16 = tt // 16 - ss // 16
    lag4 = tt // 4 - ss // 4
    same16, same4 = lag16 == 0, lag4 == 0
    sm = [lag16 == 1, lag16 == 2, lag16 == 3,
          same16 & (lag4 == 1), same16 & (lag4 == 2), same16 & (lag4 == 3),
          same4 & (tt - ss == 0), same4 & (tt - ss == 1), same4 & (tt - ss == 2), same4 & (tt - ss == 3)]
    return rm, np.stack(sm).astype(np.float32)


def _hgrn_kernel(q_ref, f_ref, i_ref, og_ref, loglb_ref, log1m_ref, gn_ref, rm_ref, sm_ref,
                 o_ref, st_ref, g_scr, k_scr, o_scr, *, tb, nh):
    C = HG_CHUNK
    heads = [slice(h * HG_DIM, (h + 1) * HG_DIM) for h in range(nh)]

    @pl.when(pl.program_id(1) == 0)
    def _():
        st_ref[...] = jnp.zeros_like(st_ref)

    x = f_ref[...]
    log_sig = jnp.minimum(x, 0.0) - jnp.log(1.0 + jnp.exp(-jnp.abs(x)))
    a = loglb_ref[...]
    c = log1m_ref[...] + log_sig
    log_f = jnp.maximum(a, c) + jnp.log(1.0 + jnp.exp(-jnp.abs(a - c)))
    g_scr[...] = log_f
    k_scr[...] = 1.0 - jnp.exp(log_f)

    def rmask(n):
        return jnp.concatenate([rm_ref[n]] * nh, axis=1)

    def roll(v, s):
        return pltpu.roll(v, s % C, axis=0)

    def b16(z):
        return z.astype(BF16)

    def chunk(ci, carry):
        r0 = pl.multiple_of(ci * C, C)
        g = g_scr[pl.ds(r0, C), :]
        q = q_ref[pl.ds(r0, C), :]
        k = k_scr[pl.ds(r0, C), :]
        v = i_ref[pl.ds(r0, C), :].astype(BF16)
        pre4 = g + rmask(0) * roll(g, 1) + rmask(1) * roll(g, 2) + rmask(2) * roll(g, 3)
        suf4 = rmask(3) * roll(g, -1) + rmask(4) * roll(g, -2) + rmask(5) * roll(g, -3)
        tot4 = pre4 + suf4
        a1, a2, a3 = roll(tot4, 4), roll(tot4, 8), roll(tot4, 12)
        pre16 = pre4 + rmask(6) * a1 + rmask(7) * a2 + rmask(8) * a3
        suf16 = suf4 + rmask(9) * roll(tot4, -4) + rmask(10) * roll(tot4, -8) + rmask(11) * roll(tot4, -12)
        tot16 = pre16 + suf16
        d1, d2, d3 = roll(tot16, 16), roll(tot16, 32), roll(tot16, 48)
        pre64 = pre16 + rmask(12) * d1 + rmask(13) * d2 + rmask(14) * d3
        suf64 = suf16 + rmask(15) * roll(tot16, -16) + rmask(16) * roll(tot16, -32) + rmask(17) * roll(tot16, -48)
        tot64 = pre64 + suf64
        g1, g2 = roll(g, 1), roll(g, 2)

        q_a = q * jnp.exp(pre16)
        lhs_a = [b16(q_a), b16(q_a * jnp.exp(d1)), b16(q_a * jnp.exp(d1 + d2))]
        k_a = b16(k * jnp.exp(suf16))
        q_b = q * jnp.exp(pre4)
        lhs_b = [b16(q_b), b16(q_b * jnp.exp(a1)), b16(q_b * jnp.exp(a1 + a2))]
        k_b = b16(k * jnp.exp(suf4))
        q_c = q * jnp.exp(g)
        lhs_c = [b16(q), b16(q_c), b16(q_c * jnp.exp(g1)), b16(q_c * jnp.exp(g1 + g2))]
        k_c = b16(k)
        q_s = b16(q * jnp.exp(pre64))
        k_s = b16(k * jnp.exp(suf64))
        decay = jnp.exp(tot64[0:1, :])

        def level(lhs, rhs):
            return [_dot_nt(jnp.concatenate([z[:, hh] for z in lhs], axis=0), rhs[:, hh]) for hh in heads]

        s_a, s_b, s_c = level(lhs_a, k_a), level(lhs_b, k_b), level(lhs_c, k_c)
        o_inter = [_dot_nt(q_s[:, hh], b16(st_ref[h])) for h, hh in enumerate(heads)]
        kv = [_dot_tn(v[:, hh], k_s[:, hh]) for hh in heads]
        for h, hh in enumerate(heads):
            sa, sb, sc = s_a[h], s_b[h], s_c[h]
            scores = (sm_ref[0] * sa[0:C] + sm_ref[1] * sa[C:2 * C] + sm_ref[2] * sa[2 * C:3 * C]
                      + sm_ref[3] * sb[0:C] + sm_ref[4] * sb[C:2 * C] + sm_ref[5] * sb[2 * C:3 * C]
                      + sm_ref[6] * sc[0:C] + sm_ref[7] * sc[C:2 * C] + sm_ref[8] * sc[2 * C:3 * C]
                      + sm_ref[9] * sc[3 * C:4 * C])
            o_scr[pl.ds(r0, C), hh] = _dot(b16(scores), v[:, hh]) + o_inter[h]
            st_ref[h] = st_ref[h] * decay[:, hh] + kv[h]
        return carry

    lax.fori_loop(0, tb // C, chunk, 0)

    for hh in heads:
        o = o_scr[:, hh]
        ms = jnp.mean(o * o, axis=-1, keepdims=True)
        og = og_ref[:, hh]
        y = o * lax.rsqrt(ms + NORM_EPS) * gn_ref[:, hh] * (og * jax.nn.sigmoid(og))
        o_ref[:, hh] = y.astype(o_ref.dtype)


def _hgrn(hz, loglb, log1mlb, gnorm):
    S = hz.shape[0]
    tb = 512 if S % 512 == 0 else S
    nh = HG_HEADS_PER_STEP
    ng = HG_HEADS // nh
    w = nh * HG_DIM
    rm, sm = _hgrn_tables()
    blk = lambda part: pl.BlockSpec((tb, w), lambda h, c, part=part: (c, part * ng + h))
    row = pl.BlockSpec((1, w), lambda h, c: (0, h))
    return pl.pallas_call(
        functools.partial(_hgrn_kernel, tb=tb, nh=nh),
        grid=(ng, S // tb),
        in_specs=[blk(0), blk(1), blk(2), blk(3), row, row, row,
                  pl.BlockSpec((_ROW_MASKS, HG_CHUNK, HG_DIM), lambda h, c: (0, 0, 0)),
                  pl.BlockSpec((_SCORE_MASKS, HG_CHUNK, HG_CHUNK), lambda h, c: (0, 0, 0))],
        out_specs=pl.BlockSpec((tb, w), lambda h, c: (c, h)),
        out_shape=jax.ShapeDtypeStruct((S, HG_WIDTH), BF16),
        scratch_shapes=[pltpu.VMEM((nh, HG_DIM, HG_DIM), F32), pltpu.VMEM((tb, w), F32),
                        pltpu.VMEM((tb, w), F32), pltpu.VMEM((tb, w), F32)],
        compiler_params=_cparams(("arbitrary", "arbitrary")),
        name="hgrn",
    )(hz, hz, hz, hz, loglb, log1mlb, gnorm, jnp.asarray(rm), jnp.asarray(sm))


def _merge_kernel(at_ref, hg_ref, ga_ref, gh_ref, wa_ref, wh_ref, wo_ref, x_ref, gt_ref, gp_ref, o_ref):
    mixed = (ga_ref[...].astype(F32) * _dot(at_ref[...], wa_ref[...])
             + gh_ref[...].astype(F32) * _dot(hg_ref[...], wh_ref[...]))
    y = _dot(mixed.astype(BF16), wo_ref[...])
    ms = jnp.mean(y * y, axis=-1, keepdims=True)
    yn = y * lax.rsqrt(ms + NORM_EPS) * gp_ref[...]
    o_ref[...] = x_ref[...] + gt_ref[...] * yn


def _merge(attn, hg, mg, wa, wh, wo, x, gt, gpost):
    S, D = x.shape
    tm = 256
    const = lambda shape: pl.BlockSpec(shape, lambda m: (0, 0), pipeline_mode=pl.Buffered(1))
    row = pl.BlockSpec((1, D), lambda m: (0, 0))
    return pl.pallas_call(
        _merge_kernel,
        grid=(S // tm,),
        in_specs=[
            pl.BlockSpec((tm, A_WIDTH), lambda m: (m, 0)),
            pl.BlockSpec((tm, HG_WIDTH), lambda m: (m, 0)),
            pl.BlockSpec((tm, D), lambda m: (m, 0)),
            pl.BlockSpec((tm, D), lambda m: (m, 1)),
            const((A_WIDTH, D)), const((HG_WIDTH, D)), const((D, D)),
            pl.BlockSpec((tm, D), lambda m: (m, 0)),
            row, row,
        ],
        out_specs=pl.BlockSpec((tm, D), lambda m: (m, 0)),
        out_shape=jax.ShapeDtypeStruct((S, D), F32),
        compiler_params=_cparams(("arbitrary",)),
        name="merge",
    )(attn, hg, mg, mg, wa, wh, wo, x, gt, gpost.reshape(1, D))


_HALO = 16
_FF_SUB = 256


def _ffn_kernel(x_ref, xh_ref, g_ref, sc_ref, sh_ref, wg_ref, wv_ref, cwg_ref, cwv_ref, cbg_ref, cbv_ref,
                wd_ref, gt_ref, gp_ref, o_ref, h_scr, acc_scr, ug_scr, uv_scr, *, tm):
    m = pl.program_id(0)
    f = pl.program_id(1)

    def norm_mod(x):
        ms = jnp.mean(x * x, axis=-1, keepdims=True)
        y = x * lax.rsqrt(ms + NORM_EPS) * g_ref[...]
        return (y * (1.0 + sc_ref[...]) + sh_ref[...]).astype(BF16)

    @pl.when(f == 0)
    def _():
        h_scr[0:_HALO, :] = jnp.where(m > 0, norm_mod(xh_ref[...]), jnp.zeros((), BF16))
        h_scr[_HALO:_HALO + tm, :] = norm_mod(x_ref[...])
        acc_scr[...] = jnp.zeros_like(acc_scr)

    tf = wg_ref.shape[1]
    subs = [slice(s * _FF_SUB, (s + 1) * _FF_SUB) for s in range(tf // _FF_SUB)]
    h = h_scr[...]
    for cs in subs:
        ug_scr[:, cs] = _dot(h, wg_ref[:, cs])
        uv_scr[:, cs] = _dot(h, wv_ref[:, cs])

    def conv(u_scr, cw_ref, cb_ref, cs):
        out = cb_ref[:, cs] + cw_ref[0:1, cs] * u_scr[pl.ds(_HALO - 2, tm), cs]
        out = out + cw_ref[1:2, cs] * u_scr[pl.ds(_HALO - 1, tm), cs]
        return out + cw_ref[2:3, cs] * u_scr[pl.ds(_HALO, tm), cs]

    acc = acc_scr[...]
    for cs in subs:
        cg = conv(ug_scr, cwg_ref, cbg_ref, cs)
        cv = conv(uv_scr, cwv_ref, cbv_ref, cs)
        act = (cg * jax.nn.sigmoid(cg) * cv).astype(BF16)
        acc = acc + _dot(act, wd_ref[cs, :])
    acc_scr[...] = acc

    @pl.when(f == pl.num_programs(1) - 1)
    def _():
        y = acc_scr[...]
        ms = jnp.mean(y * y, axis=-1, keepdims=True)
        yn = y * lax.rsqrt(ms + NORM_EPS) * gp_ref[...]
        o_ref[...] = x_ref[...] + gt_ref[...] * yn


def _ffn(x, gpre, sc, sh, wup, convw, convb, wdown, gt, gpost):
    S, D = x.shape
    F = wdown.shape[0]
    tm = 512 if S % 512 == 0 else S
    tf = 512
    nf = F // tf
    hb = tm // _HALO
    row = pl.BlockSpec((1, D), lambda m, f: (0, 0))
    return pl.pallas_call(
        functools.partial(_ffn_kernel, tm=tm),
        grid=(S // tm, nf),
        in_specs=[
            pl.BlockSpec((tm, D), lambda m, f: (m, 0)),
            pl.BlockSpec((_HALO, D), lambda m, f: (jnp.maximum(m * hb - 1, 0), 0)),
            row, row, row,
            pl.BlockSpec((D, tf), lambda m, f: (0, f)),
            pl.BlockSpec((D, tf), lambda m, f: (0, nf + f)),
            pl.BlockSpec((CONV_WIDTH, tf), lambda m, f: (0, f)),
            pl.BlockSpec((CONV_WIDTH, tf), lambda m, f: (0, nf + f)),
            pl.BlockSpec((1, tf), lambda m, f: (0, f)),
            pl.BlockSpec((1, tf), lambda m, f: (0, nf + f)),
            pl.BlockSpec((tf, D), lambda m, f: (f, 0)),
            row, row,
        ],
        out_specs=pl.BlockSpec((tm, D), lambda m, f: (m, 0)),
        out_shape=jax.ShapeDtypeStruct((S, D), F32),
        scratch_shapes=[pltpu.VMEM((tm + _HALO, D), BF16), pltpu.VMEM((tm, D), F32),
                        pltpu.VMEM((tm + _HALO, tf), F32), pltpu.VMEM((tm + _HALO, tf), F32)],
        compiler_params=_cparams(("arbitrary", "arbitrary")),
        name="ffn",
    )(x, x, gpre.reshape(1, D), sc, sh, wup, wup, convw, convw, convb.reshape(1, -1), convb.reshape(1, -1),
      wdown, gt, gpost.reshape(1, D))


def _rope_tables(positions):
    inv_freq = 1.0 / (ROPE_THETA ** (jnp.arange(0, HEAD_DIM, 2, dtype=F32) / HEAD_DIM))
    ang = positions.astype(F32)[:, None] * inv_freq
    cos, sin = jnp.cos(ang), jnp.sin(ang)
    return jnp.concatenate([cos, cos], axis=-1), jnp.concatenate([-sin, sin], axis=-1)


def kernel(x, c, positions, w_ada, b_ada, g_pre_mix, w_in, pe_kc, w_kc, pe_vc, w_vc, lb_logits, g_hg_norm,
           w_br_attn, w_br_hgrn, w_out, g_post_mix, g_pre_ffn, w_up, conv_w, conv_b, w_down, g_post_ffn):
    B, S, D = x.shape
    assert B == 1, "kernel is written for one sequence"
    L = w_ada.shape[0]
    xs = x[0]
    cosf, sinf = _rope_tables(positions[0])
    scale = HEAD_DIM ** -0.5 * np.log2(np.e)
    tables = (cosf * scale, sinf * scale, cosf, sinf)
    lb_cum = jnp.cumsum(jax.nn.softmax(lb_logits.astype(F32), axis=0), axis=0)
    lower = lb_cum - lb_cum[0:1]
    log_lb = jnp.log(lower)
    log_1m = jnp.log1p(-lower)
    ada = _ada_all(c, w_ada, b_ada)

    w_in_b = w_in.astype(BF16)
    w_tail = w_in_b[:, :, A_COLS:]
    hg_tiles = 4 * HG_WIDTH // 512
    mg_tiles = 2 * D // 512

    for l in range(L):
        sh1, sc1, gt1, sh2, sc2, gt2 = [ada[l, :, j * D:(j + 1) * D] for j in range(N_ADA)]

        h = _prenorm(xs, g_pre_mix[l], sc1, sh1)
        a = _proj(h, w_in_b, l, 0, A_ROLES, BF16, tn=A_TILE, tables=tables, name="proj_attn")
        hz = _proj(h, w_tail, l, 0, ("id",) * hg_tiles, F32, tn=512, name="proj_hgrn")
        mg = _proj(h, w_tail, l, hg_tiles, ("sg",) * mg_tiles, BF16, tn=512, name="proj_merge_gate")

        cmp_kv = _compress(a, jnp.stack([pe_kc[l], pe_vc[l]]), jnp.stack([w_kc[l], w_vc[l]]))
        attn = _nsa(a, cmp_kv)
        hg = _hgrn(hz, log_lb[l:l + 1], log_1m[l:l + 1], g_hg_norm[l:l + 1])
        xs = _merge(attn, hg, mg, w_br_attn[l].astype(BF16), w_br_hgrn[l].astype(BF16),
                    w_out[l].astype(BF16), xs, gt1, g_post_mix[l])

        xs = _ffn(xs, g_pre_ffn[l], sc2, sh2, w_up[l].astype(BF16), conv_w[l], conv_b[l],
                  w_down[l].astype(BF16), gt2, g_post_ffn[l])
    return xs[None]
```

```python
import functools

import jax
import jax.numpy as jnp
import numpy as np
from jax import lax
from jax.experimental import pallas as pl
from jax.experimental.pallas import tpu as pltpu

F32 = jnp.float32
BF16 = jnp.bfloat16

A_HEADS = 8
A_KV_GROUPS = 2
A_HPG = A_HEADS // A_KV_GROUPS
HEAD_DIM = 128
A_WIDTH = A_HEADS * HEAD_DIM
A_KV_WIDTH = A_KV_GROUPS * HEAD_DIM
CMP_BLOCK = 32
CMP_STRIDE = 16
SEL_BLOCK = 64
SEL_TOPK = 16
WINDOW = 512
ROPE_THETA = 10000.0
HG_HEADS = 8
HG_DIM = 128
HG_WIDTH = HG_HEADS * HG_DIM
HG_CHUNK = 64
CONV_WIDTH = 3
NORM_EPS = 1e-6
N_ADA = 6
A_COLS = A_WIDTH + 6 * A_KV_WIDTH + 3 * A_HEADS

LANES = 128
SEL_SLOTS = LANES
KEY_TILE = 256
NEG_BIG = -1e30
VMEM_LIMIT = 56 * 1024 * 1024


def _cparams(sem):
    return pltpu.CompilerParams(dimension_semantics=sem, vmem_limit_bytes=VMEM_LIMIT)


def _split3(a):
    hi = a.astype(BF16)
    r1 = a - hi.astype(F32)
    mid = r1.astype(BF16)
    lo = (r1 - mid.astype(F32)).astype(BF16)
    return hi, mid, lo


def _dot(a, b):
    return jnp.dot(a, b, preferred_element_type=F32)


def _dot_nt(a, b):
    return lax.dot_general(a, b, (((1,), (1,)), ((), ())), preferred_element_type=F32)


def _dot_tn(a, b):
    return lax.dot_general(a, b, (((0,), (0,)), ((), ())), preferred_element_type=F32)


def _ada_kernel(c_ref, w_ref, b_ref, o_ref):
    c = c_ref[...]
    ca = c * jax.nn.sigmoid(c)
    o_ref[0] = jnp.sum(ca * w_ref[0], axis=0, keepdims=True) + b_ref[0]


def _ada_all(c, w_ada, b_ada):
    L, D, N = w_ada.shape
    tn = 1024
    return pl.pallas_call(
        _ada_kernel,
        grid=(L, N // tn),
        in_specs=[
            pl.BlockSpec((D, 1), lambda l, n: (0, 0)),
            pl.BlockSpec((1, D, tn), lambda l, n: (l, 0, n)),
            pl.BlockSpec((1, 1, tn), lambda l, n: (l, 0, n)),
        ],
        out_specs=pl.BlockSpec((1, 1, tn), lambda l, n: (l, 0, n)),
        out_shape=jax.ShapeDtypeStruct((L, 1, N), F32),
        compiler_params=_cparams(("arbitrary", "arbitrary")),
        name="ada",
    )(c.reshape(D, 1), w_ada, b_ada.reshape(L, 1, N))


def _prenorm_kernel(x_ref, g_ref, sc_ref, sh_ref, o_ref):
    x = x_ref[...]
    ms = jnp.mean(x * x, axis=-1, keepdims=True)
    y = x * lax.rsqrt(ms + NORM_EPS) * g_ref[...]
    o_ref[...] = (y * (1.0 + sc_ref[...]) + sh_ref[...]).astype(o_ref.dtype)


def _prenorm(x, g, sc, sh):
    S, D = x.shape
    tm = 512
    row = pl.BlockSpec((1, D), lambda m: (0, 0))
    return pl.pallas_call(
        _prenorm_kernel,
        grid=(S // tm,),
        in_specs=[pl.BlockSpec((tm, D), lambda m: (m, 0)), row, row, row],
        out_specs=pl.BlockSpec((tm, D), lambda m: (m, 0)),
        out_shape=jax.ShapeDtypeStruct((S, D), BF16),
        compiler_params=_cparams(("arbitrary",)),
        name="prenorm",
    )(x, g.reshape(1, D), sc, sh)


def _rope(t, cosf, sinf):
    return t * cosf + pltpu.roll(t, HEAD_DIM // 2, axis=1) * sinf


def _proj_kernel(h_ref, w_ref, *rest, roles, tn, stationary):
    if stationary:
        o_ref, w_scr = rest[-2:]
        tabs = rest[:-2]
        n = pl.program_id(0)

        @pl.when(pl.program_id(1) == 0)
        def _():
            w_scr[...] = w_ref[0].astype(BF16)
        acc = _dot(h_ref[...], w_scr[...])
    else:
        o_ref = rest[-1]
        tabs = rest[:-1]
        n = pl.program_id(1)
        acc = _dot(h_ref[...], w_ref[0])

    def emit(kind):
        if kind == "id":
            o_ref[...] = acc.astype(o_ref.dtype)
        elif kind == "sg":
            o_ref[...] = jax.nn.sigmoid(acc).astype(o_ref.dtype)
        else:
            cos_ref, sin_ref = tabs[0:2] if kind == "rq" else tabs[-2:]
            cosf, sinf = cos_ref[...], sin_ref[...]
            for s in range(tn // HEAD_DIM):
                sl = slice(s * HEAD_DIM, (s + 1) * HEAD_DIM)
                o_ref[:, sl] = _rope(acc[:, sl], cosf, sinf).astype(o_ref.dtype)

    kinds = sorted(set(roles))
    if len(kinds) == 1:
        emit(kinds[0])
        return
    for kind in kinds:
        hit = functools.reduce(jnp.logical_or, [n == t for t, r in enumerate(roles) if r == kind])
        pl.when(hit)(functools.partial(emit, kind))


def _proj(h, w3, layer, col0, roles, out_dtype, *, tn, tables=(), name="proj"):
    S, D = h.shape
    tm = 1024 if S % 1024 == 0 else S
    nt = len(roles)
    stationary = w3.dtype == F32
    if stationary:
        grid = (nt, S // tm)
        mn = lambda f: (lambda n, m: f(m, n))
        scratch = [pltpu.VMEM((D, tn), BF16)]
    else:
        grid = (S // tm, nt)
        mn = lambda f: f
        scratch = []
    tab = pl.BlockSpec((tm, HEAD_DIM), mn(lambda m, n: (m, 0)))
    return pl.pallas_call(
        functools.partial(_proj_kernel, roles=tuple(roles), tn=tn, stationary=stationary),
        grid=grid,
        in_specs=[pl.BlockSpec((tm, D), mn(lambda m, n: (m, 0))),
                  pl.BlockSpec((1, D, tn), mn(lambda m, n: (layer, 0, col0 + n)))] + [tab] * len(tables),
        out_specs=pl.BlockSpec((tm, tn), mn(lambda m, n: (m, n))),
        out_shape=jax.ShapeDtypeStruct((S, nt * tn), out_dtype),
        scratch_shapes=scratch,
        compiler_params=_cparams(("arbitrary", "arbitrary")),
        name=name,
    )(h, w3, *tables)


_TAIL_TILE = 512


def _tail_kernel(main_ref, next_ref, o_ref, *, shift):
    wide = jnp.concatenate([main_ref[0], next_ref[0]], axis=1)
    width = wide.shape[1]
    o_ref[0] = pltpu.roll(wide, width - shift, axis=1)[:, :_TAIL_TILE].astype(o_ref.dtype)


def _rebase_tail(w_in):
    L, D, N = w_in.shape
    base = (A_COLS // LANES) * LANES
    shift = A_COLS - base
    width = N - A_COLS
    assert width % _TAIL_TILE == 0 and base % _TAIL_TILE == 0 and shift < LANES
    b0 = base // _TAIL_TILE
    per = _TAIL_TILE // LANES
    return pl.pallas_call(
        functools.partial(_tail_kernel, shift=shift),
        grid=(L, width // _TAIL_TILE),
        in_specs=[pl.BlockSpec((1, D, _TAIL_TILE), lambda l, n: (l, 0, b0 + n)),
                  pl.BlockSpec((1, D, LANES), lambda l, n: (l, 0, (b0 + n + 1) * per))],
        out_specs=pl.BlockSpec((1, D, _TAIL_TILE), lambda l, n: (l, 0, n)),
        out_shape=jax.ShapeDtypeStruct((L, D, width), BF16),
        compiler_params=_cparams(("arbitrary", "arbitrary")),
        name="rebase_tail",
    )(w_in, w_in)


A_TILE = 256
A_ROLES = ("rq",) * (A_WIDTH // A_TILE) + ("rk", "id", "rk", "id", "rk", "id", "sg")
_KV_BLOCK0 = A_WIDTH // LANES
BLK_KC, BLK_VC, BLK_KS, BLK_VS, BLK_KW, BLK_VW = [_KV_BLOCK0 + j * A_KV_GROUPS for j in range(6)]
BLK_GATE = _KV_BLOCK0 + 6 * A_KV_GROUPS


def _compress_kernel(t_ref, pe_ref, w_ref, o_ref, t32_ref, *, n_chunk):
    half = CMP_BLOCK // 2
    t32_ref[...] = t_ref[...].astype(F32)
    top, bot = [], []
    for j in range(half):
        a = t32_ref[pl.ds(j, n_chunk, stride=CMP_STRIDE), :]
        top.append(a + pe_ref[0, j:j + 1, :])
        bot.append(a + pe_ref[0, half + j:half + j + 1, :])
    a_top = jnp.concatenate(top, axis=1)
    a_bot = jnp.concatenate(bot, axis=1)
    kw = half * HEAD_DIM
    w = w_ref[0]
    w_top = w[:kw].astype(BF16)
    w_bot = w[kw:].astype(BF16)
    y_top = jnp.zeros((n_chunk, HEAD_DIM), F32)
    y_bot = jnp.zeros((n_chunk, HEAD_DIM), F32)
    for piece in _split3(a_top):
        y_top = y_top + _dot(piece, w_top)
    for piece in _split3(a_bot):
        y_bot = y_bot + _dot(piece, w_bot)
    o_ref[0, 0] = y_top + pltpu.roll(y_bot, n_chunk - 1, axis=0)


def _compress(a, pe2, w2):
    S = a.shape[0]
    n_chunk = S // CMP_STRIDE
    G = A_KV_GROUPS
    return pl.pallas_call(
        functools.partial(_compress_kernel, n_chunk=n_chunk),
        grid=(2, G),
        in_specs=[
            pl.BlockSpec((S, HEAD_DIM), lambda kv, g: (0, BLK_KC + kv * G + g)),
            pl.BlockSpec((1, CMP_BLOCK, HEAD_DIM), lambda kv, g: (kv, 0, 0)),
            pl.BlockSpec((1, CMP_BLOCK * HEAD_DIM, HEAD_DIM), lambda kv, g: (kv, 0, 0)),
        ],
        out_specs=pl.BlockSpec((1, 1, n_chunk, HEAD_DIM), lambda kv, g: (kv, g, 0, 0)),
        out_shape=jax.ShapeDtypeStruct((2, G, n_chunk, HEAD_DIM), F32),
        scratch_shapes=[pltpu.VMEM((S, HEAD_DIM), F32)],
        compiler_params=_cparams(("arbitrary", "arbitrary")),
        name="compress",
    )(a, pe2, w2)


NSA_QB = 256
SEL_BODY = 8
QK_AHEAD = 2


def _nsa_kernel(q_ref, ks_ref, kw_ref, vs_ref, vw_ref, kc_ref, vc_ref, ovt_ref, gate_ref,
                o_ref, kaug_ref, vt_ref, vwt_ref, vct_ref, acc_ref, m_ref, l_ref, *, seq):
    grp = pl.program_id(0)
    i = pl.program_id(1)
    t0 = i * NSA_QB
    n_cmp = seq // CMP_STRIDE
    cols = A_HPG * NSA_QB

    @pl.when(i == 0)
    def _():
        def build(r, c):
            r0 = pl.multiple_of(r * KEY_TILE, KEY_TILE)
            kaug_ref[r, :, 0:HEAD_DIM] = ks_ref[pl.ds(r0, KEY_TILE), :]
            blk = (r0 + lax.broadcasted_iota(jnp.int32, (KEY_TILE, SEL_SLOTS), 0)) // SEL_BLOCK
            slot = lax.broadcasted_iota(jnp.int32, (KEY_TILE, SEL_SLOTS), 1)
            kaug_ref[r, :, HEAD_DIM:HEAD_DIM + SEL_SLOTS] = jnp.where(blk == slot, 1.0, 0.0).astype(BF16)
            vt_ref[r] = vs_ref[pl.ds(r0, KEY_TILE), :].astype(F32).T.astype(BF16)
            vwt_ref[r] = vw_ref[pl.ds(r0, KEY_TILE), :].astype(F32).T.astype(BF16)
            return c
        lax.fori_loop(0, seq // KEY_TILE, build, 0)
        vct_ref[...] = vc_ref[0, 0].T.astype(BF16)

    qt = jnp.concatenate(
        [q_ref[:, h * HEAD_DIM:(h + 1) * HEAD_DIM].astype(F32).T.astype(BF16) for h in range(A_HPG)],
        axis=1)
    tq = t0 + lax.broadcasted_iota(jnp.int32, (1, cols), 1) % NSA_QB

    def softmax0(s):
        m = jnp.max(s, axis=0, keepdims=True)
        p = jnp.exp2(s - m)
        return p, jnp.sum(p, axis=0, keepdims=True)

    s_c = _dot(kc_ref[0, 0].astype(BF16), qt)
    nrow = lax.broadcasted_iota(jnp.int32, (n_cmp, 1), 0)
    vis = (nrow * CMP_STRIDE + (CMP_BLOCK - 1)) <= tq
    p_c, den = softmax0(jnp.where(vis, s_c, NEG_BIG))
    p_c = p_c * jnp.where(tq >= CMP_BLOCK - 1, 1.0 / den, 0.0)
    o_ct = _dot(vct_ref[...], p_c.astype(BF16))

    wlen = WINDOW + NSA_QB
    w0 = pl.multiple_of(jnp.maximum(t0 - WINDOW, 0), KEY_TILE)
    wt = jnp.maximum(i - WINDOW // KEY_TILE, 0)
    s_w = _dot(kw_ref[pl.ds(w0, wlen), :], qt)
    dist = tq - (w0 + lax.broadcasted_iota(jnp.int32, (wlen, 1), 0))
    in_window = dist.astype(jnp.uint32) < WINDOW
    p_w, l_w = softmax0(jnp.where(in_window, s_w, NEG_BIG))
    p_w = p_w.astype(BF16)
    o_wt = _dot(vwt_ref[wt], p_w[0:KEY_TILE])
    for k in range(1, wlen // KEY_TILE):
        o_wt = o_wt + _dot(vwt_ref[wt + k], p_w[k * KEY_TILE:(k + 1) * KEY_TILE])
    o_wt = o_wt * (1.0 / l_w)

    psum = p_c[:, 0:NSA_QB]
    for h in range(1, A_HPG):
        psum = psum + p_c[:, h * NSA_QB:(h + 1) * NSA_QB]
    ovt = ovt_ref[...]
    imp_t = jnp.zeros((SEL_SLOTS, NSA_QB), F32)
    for piece in _split3(psum):
        imp_t = imp_t + _dot(ovt, piece)
    slot_i = lax.broadcasted_iota(jnp.int32, (SEL_SLOTS, NSA_QB), 0)
    tq_l = t0 + lax.broadcasted_iota(jnp.int32, (SEL_SLOTS, NSA_QB), 1)
    cur = tq_l // SEL_BLOCK
    forced = jnp.where(slot_i == 0, 1, 0) + jnp.where(slot_i == cur, 1, 0) + jnp.where(slot_i == cur - 1, 1, 0)
    valid = slot_i * SEL_BLOCK <= tq_l
    key = jnp.where(forced > 0, -NEG_BIG, jnp.where(valid, imp_t, NEG_BIG))
    slot_f = slot_i.astype(F32)
    bias_t = jnp.full((SEL_SLOTS, NSA_QB), NEG_BIG, F32)
    for _ in range(SEL_TOPK):
        mx = jnp.max(key, axis=0, keepdims=True)
        first = jnp.min(jnp.where(key == mx, slot_f, 1e9), axis=0, keepdims=True)
        pick = slot_f == first
        bias_t = jnp.where(pick, 0.0, bias_t)
        key = jnp.where(pick, -3e38, key)

    qt_aug = jnp.concatenate([qt, jnp.concatenate([bias_t.astype(BF16)] * A_HPG, axis=1)], axis=0)
    acc_ref[...] = jnp.zeros_like(acc_ref)
    m_ref[...] = jnp.full(m_ref.shape, 0.5 * NEG_BIG, F32)
    l_ref[...] = jnp.zeros_like(l_ref)

    def sel_body(base, causal):
        n = len(causal)
        m, l = m_ref[...], l_ref[...]
        s = {}
        for k in range(min(QK_AHEAD, n)):
            s[k] = _dot(kaug_ref[base + k], qt_aug)
        for k in range(n):
            if k + QK_AHEAD < n:
                s[k + QK_AHEAD] = _dot(kaug_ref[base + k + QK_AHEAD], qt_aug)
            sk = s.pop(k)
            if causal[k]:
                kpos = (base + k) * KEY_TILE + lax.broadcasted_iota(jnp.int32, (KEY_TILE, 1), 0)
                sk = jnp.where(kpos <= tq, sk, NEG_BIG)
            m_new = jnp.maximum(m, jnp.max(sk, axis=0, keepdims=True))
            alpha = jnp.exp2(m - m_new)
            p = jnp.exp2(sk - m_new)
            l = alpha * l + jnp.sum(p, axis=0, keepdims=True)
            acc_ref[...] = alpha * acc_ref[...] + _dot(vt_ref[base + k], p.astype(BF16))
            m = m_new
        m_ref[...] = m
        l_ref[...] = l

    n_bulk = i // SEL_BODY
    rem = i % SEL_BODY

    def bulk(b, c):
        sel_body(b * SEL_BODY, (False,) * SEL_BODY)
        return c
    lax.fori_loop(0, n_bulk, bulk, 0)

    base = n_bulk * SEL_BODY
    part = SEL_BODY // 2
    while part >= 2:
        pl.when(rem & part != 0)(functools.partial(sel_body, base, (False,) * part))
        base = base + (rem & part)
        part //= 2

    @pl.when(rem % 2 == 1)
    def _():
        sel_body(i - 1, (False, True))

    @pl.when(rem % 2 == 0)
    def _():
        sel_body(i, (True,))

    o_st = acc_ref[...] * (1.0 / l_ref[...])

    gates_t = gate_ref[...].astype(F32).T
    per_group = 3 * A_HPG

    def gate(h, b):
        r = 3 * h + b
        return jnp.where(grp == 0, gates_t[r:r + 1], gates_t[per_group + r:per_group + r + 1])

    for h in range(A_HPG):
        c = slice(h * NSA_QB, (h + 1) * NSA_QB)
        o = gate(h, 0) * o_ct[:, c] + gate(h, 1) * o_st[:, c] + gate(h, 2) * o_wt[:, c]
        o_ref[:, h * HEAD_DIM:(h + 1) * HEAD_DIM] = o.T.astype(o_ref.dtype)


def _overlap_matrix(seq):
    n_chunk = seq // CMP_STRIDE
    cs = np.arange(n_chunk)[:, None] * CMP_STRIDE
    ss = np.arange(SEL_SLOTS)[None, :] * SEL_BLOCK
    ov = np.maximum(np.minimum(cs + CMP_BLOCK, ss + SEL_BLOCK) - np.maximum(cs, ss), 0) / CMP_BLOCK
    n_cmp = (seq - CMP_BLOCK) // CMP_STRIDE + 1
    ov[n_cmp:] = 0.0
    ov[:, seq // SEL_BLOCK:] = 0.0
    return ov.astype(np.float32)


def _nsa(a, cmp_kv):
    S = a.shape[0]
    G = A_KV_GROUPS
    n_tiles = S // KEY_TILE
    assert S // SEL_BLOCK <= SEL_SLOTS and S % KEY_TILE == 0 and S >= WINDOW + NSA_QB
    assert NSA_QB == KEY_TILE and G == 2
    n_chunk = S // CMP_STRIDE
    ovt = jnp.asarray(_overlap_matrix(S).T, BF16)
    col = lambda b: pl.BlockSpec((S, HEAD_DIM), lambda g, i, b=b: (0, b + g))
    return pl.pallas_call(
        functools.partial(_nsa_kernel, seq=S),
        grid=(G, S // NSA_QB),
        in_specs=[
            pl.BlockSpec((NSA_QB, A_HPG * HEAD_DIM), lambda g, i: (i, g)),
            col(BLK_KS), col(BLK_KW), col(BLK_VS), col(BLK_VW),
            pl.BlockSpec((1, 1, n_chunk, HEAD_DIM), lambda g, i: (0, g, 0, 0)),
            pl.BlockSpec((1, 1, n_chunk, HEAD_DIM), lambda g, i: (1, g, 0, 0)),
            pl.BlockSpec((SEL_SLOTS, n_chunk), lambda g, i: (0, 0)),
            pl.BlockSpec((NSA_QB, LANES), lambda g, i: (i, BLK_GATE)),
        ],
        out_specs=pl.BlockSpec((NSA_QB, A_HPG * HEAD_DIM), lambda g, i: (i, g)),
        out_shape=jax.ShapeDtypeStruct((S, A_WIDTH), BF16),
        scratch_shapes=[pltpu.VMEM((n_tiles, KEY_TILE, HEAD_DIM + SEL_SLOTS), BF16),
                        pltpu.VMEM((n_tiles, HEAD_DIM, KEY_TILE), BF16),
                        pltpu.VMEM((n_tiles, HEAD_DIM, KEY_TILE), BF16),
                        pltpu.VMEM((HEAD_DIM, n_chunk), BF16),
                        pltpu.VMEM((HEAD_DIM, A_HPG * NSA_QB), F32),
                        pltpu.VMEM((1, A_HPG * NSA_QB), F32),
                        pltpu.VMEM((1, A_HPG * NSA_QB), F32)],
        compiler_params=_cparams(("arbitrary", "arbitrary")),
        name="nsa",
    )(a, a, a, a, a, cmp_kv, cmp_kv, ovt, a)


_ROW_MASKS = 18
_SCORE_MASKS = 10
HG_HEADS_PER_STEP = 4


def _hgrn_tables():
    t = np.arange(HG_CHUNK)
    p4, i4, i16 = t % 4, (t // 4) % 4, t // 16
    conds = [p4 >= 1, p4 >= 2, p4 >= 3, p4 <= 2, p4 <= 1, p4 == 0,
             i4 >= 1, i4 >= 2, i4 >= 3, i4 <= 2, i4 <= 1, i4 == 0,
             i16 >= 1, i16 >= 2, i16 >= 3, i16 <= 2, i16 <= 1, i16 == 0]
    rm = np.stack([np.broadcast_to(c[:, None], (HG_CHUNK, HG_DIM)) for c in conds]).astype(np.float32)
    tt, ss = t[:, None], t[None, :]
    lag16 = tt // 16 - ss // 16
    lag4 = tt // 4 - ss // 4
    same16, same4 = lag16 == 0, lag4 == 0
    sm = [lag16 == 1, lag16 == 2, lag16 == 3,
          same16 & (lag4 == 1), same16 & (lag4 == 2), same16 & (lag4 == 3),
          same4 & (tt - ss == 0), same4 & (tt - ss == 1), same4 & (tt - ss == 2), same4 & (tt - ss == 3)]
    return rm, np.stack(sm).astype(np.float32)


def _hgrn_kernel(q_ref, f_ref, i_ref, og_ref, loglb_ref, log1m_ref, gn_ref, rm_ref, sm_ref,
                 o_ref, st_ref, g_scr, k_scr, o_scr, *, tb, nh):
    C = HG_CHUNK
    heads = [slice(h * HG_DIM, (h + 1) * HG_DIM) for h in range(nh)]

    @pl.when(pl.program_id(1) == 0)
    def _():
        st_ref[...] = jnp.zeros_like(st_ref)

    x = f_ref[...]
    log_sig = jnp.minimum(x, 0.0) - jnp.log(1.0 + jnp.exp(-jnp.abs(x)))
    a = loglb_ref[...]
    c = log1m_ref[...] + log_sig
    log_f = jnp.maximum(a, c) + jnp.log(1.0 + jnp.exp(-jnp.abs(a - c)))
    g_scr[...] = log_f
    k_scr[...] = 1.0 - jnp.exp(log_f)

    def rmask(n):
        return jnp.concatenate([rm_ref[n]] * nh, axis=1)

    def roll(v, s):
        return pltpu.roll(v, s % C, axis=0)

    def b16(z):
        return z.astype(BF16)

    def chunk(ci, carry):
        r0 = pl.multiple_of(ci * C, C)
        g = g_scr[pl.ds(r0, C), :]
        q = q_ref[pl.ds(r0, C), :]
        k = k_scr[pl.ds(r0, C), :]
        v = i_ref[pl.ds(r0, C), :].astype(BF16)
        pre4 = g + rmask(0) * roll(g, 1) + rmask(1) * roll(g, 2) + rmask(2) * roll(g, 3)
        suf4 = rmask(3) * roll(g, -1) + rmask(4) * roll(g, -2) + rmask(5) * roll(g, -3)
        tot4 = pre4 + suf4
        a1, a2, a3 = roll(tot4, 4), roll(tot4, 8), roll(tot4, 12)
        pre16 = pre4 + rmask(6) * a1 + rmask(7) * a2 + rmask(8) * a3
        suf16 = suf4 + rmask(9) * roll(tot4, -4) + rmask(10) * roll(tot4, -8) + rmask(11) * roll(tot4, -12)
        tot16 = pre16 + suf16
        d1, d2, d3 = roll(tot16, 16), roll(tot16, 32), roll(tot16, 48)
        pre64 = pre16 + rmask(12) * d1 + rmask(13) * d2 + rmask(14) * d3
        suf64 = suf16 + rmask(15) * roll(tot16, -16) + rmask(16) * roll(tot16, -32) + rmask(17) * roll(tot16, -48)
        tot64 = pre64 + suf64
        g1, g2 = roll(g, 1), roll(g, 2)

        q_a = q * jnp.exp(pre16)
        lhs_a = [b16(q_a), b16(q_a * jnp.exp(d1)), b16(q_a * jnp.exp(d1 + d2))]
        k_a = b16(k * jnp.exp(suf16))
        q_b = q * jnp.exp(pre4)
        lhs_b = [b16(q_b), b16(q_b * jnp.exp(a1)), b16(q_b * jnp.exp(a1 + a2))]
        k_b = b16(k * jnp.exp(suf4))
        q_c = q * jnp.exp(g)
        lhs_c = [b16(q), b16(q_c), b16(q_c * jnp.exp(g1)), b16(q_c * jnp.exp(g1 + g2))]
        k_c = b16(k)
        q_s = b16(q * jnp.exp(pre64))
        k_s = b16(k * jnp.exp(suf64))
        decay = jnp.exp(tot64[0:1, :])

        def level(lhs, rhs):
            return [_dot_nt(jnp.concatenate([z[:, hh] for z in lhs], axis=0), rhs[:, hh]) for hh in heads]

        s_a, s_b, s_c = level(lhs_a, k_a), level(lhs_b, k_b), level(lhs_c, k_c)
        o_inter = [_dot_nt(q_s[:, hh], b16(st_ref[h])) for h, hh in enumerate(heads)]
        kv = [_dot_tn(v[:, hh], k_s[:, hh]) for hh in heads]
        for h, hh in enumerate(heads):
            sa, sb, sc = s_a[h], s_b[h], s_c[h]
            scores = (sm_ref[0] * sa[0:C] + sm_ref[1] * sa[C:2 * C] + sm_ref[2] * sa[2 * C:3 * C]
                      + sm_ref[3] * sb[0:C] + sm_ref[4] * sb[C:2 * C] + sm_ref[5] * sb[2 * C:3 * C]
                      + sm_ref[6] * sc[0:C] + sm_ref[7] * sc[C:2 * C] + sm_ref[8] * sc[2 * C:3 * C]
                      + sm_ref[9] * sc[3 * C:4 * C])
            o_scr[pl.ds(r0, C), hh] = _dot(b16(scores), v[:, hh]) + o_inter[h]
            st_ref[h] = st_ref[h] * decay[:, hh] + kv[h]
        return carry

    lax.fori_loop(0, tb // C, chunk, 0)

    for hh in heads:
        o = o_scr[:, hh]
        ms = jnp.mean(o * o, axis=-1, keepdims=True)
        og = og_ref[:, hh]
        y = o * lax.rsqrt(ms + NORM_EPS) * gn_ref[:, hh] * (og * jax.nn.sigmoid(og))
        o_ref[:, hh] = y.astype(o_ref.dtype)


def _hgrn(hz, loglb, log1mlb, gnorm):
    S = hz.shape[0]
    tb = 512 if S % 512 == 0 else S
    nh = HG_HEADS_PER_STEP
    ng = HG_HEADS // nh
    w = nh * HG_DIM
    rm, sm = _hgrn_tables()
    blk = lambda part: pl.BlockSpec((tb, w), lambda h, c, part=part: (c, part * ng + h))
    row = pl.BlockSpec((1, w), lambda h, c: (0, h))
    return pl.pallas_call(
        functools.partial(_hgrn_kernel, tb=tb, nh=nh),
        grid=(ng, S // tb),
        in_specs=[blk(0), blk(1), blk(2), blk(3), row, row, row,
                  pl.BlockSpec((_ROW_MASKS, HG_CHUNK, HG_DIM), lambda h, c: (0, 0, 0)),
                  pl.BlockSpec((_SCORE_MASKS, HG_CHUNK, HG_CHUNK), lambda h, c: (0, 0, 0))],
        out_specs=pl.BlockSpec((tb, w), lambda h, c: (c, h)),
        out_shape=jax.ShapeDtypeStruct((S, HG_WIDTH), BF16),
        scratch_shapes=[pltpu.VMEM((nh, HG_DIM, HG_DIM), F32), pltpu.VMEM((tb, w), F32),
                        pltpu.VMEM((tb, w), F32), pltpu.VMEM((tb, w), F32)],
        compiler_params=_cparams(("arbitrary", "arbitrary")),
        name="hgrn",
    )(hz, hz, hz, hz, loglb, log1mlb, gnorm, jnp.asarray(rm), jnp.asarray(sm))


def _merge_kernel(at_ref, hg_ref, ga_ref, gh_ref, wa_ref, wh_ref, wo_ref, x_ref, gt_ref, gp_ref, o_ref):
    mixed = (ga_ref[...].astype(F32) * _dot(at_ref[...], wa_ref[...])
             + gh_ref[...].astype(F32) * _dot(hg_ref[...], wh_ref[...]))
    y = _dot(mixed.astype(BF16), wo_ref[...])
    ms = jnp.mean(y * y, axis=-1, keepdims=True)
    yn = y * lax.rsqrt(ms + NORM_EPS) * gp_ref[...]
    o_ref[...] = x_ref[...] + gt_ref[...] * yn


def _merge(attn, hg, mg, wa, wh, wo, x, gt, gpost):
    S, D = x.shape
    tm = 256
    const = lambda shape: pl.BlockSpec(shape, lambda m: (0, 0), pipeline_mode=pl.Buffered(1))
    row = pl.BlockSpec((1, D), lambda m: (0, 0))
    return pl.pallas_call(
        _merge_kernel,
        grid=(S // tm,),
        in_specs=[
            pl.BlockSpec((tm, A_WIDTH), lambda m: (m, 0)),
            pl.BlockSpec((tm, HG_WIDTH), lambda m: (m, 0)),
            pl.BlockSpec((tm, D), lambda m: (m, 0)),
            pl.BlockSpec((tm, D), lambda m: (m, 1)),
            const((A_WIDTH, D)), const((HG_WIDTH, D)), const((D, D)),
            pl.BlockSpec((tm, D), lambda m: (m, 0)),
            row, row,
        ],
        out_specs=pl.BlockSpec((tm, D), lambda m: (m, 0)),
        out_shape=jax.ShapeDtypeStruct((S, D), F32),
        compiler_params=_cparams(("arbitrary",)),
        name="merge",
    )(attn, hg, mg, mg, wa, wh, wo, x, gt, gpost.reshape(1, D))


_HALO = 16
_FF_SUB = 256


def _ffn_kernel(x_ref, xh_ref, g_ref, sc_ref, sh_ref, wg_ref, wv_ref, cwg_ref, cwv_ref, cbg_ref, cbv_ref,
                wd_ref, gt_ref, gp_ref, o_ref, h_scr, acc_scr, ug_scr, uv_scr, *, tm):
    m = pl.program_id(0)
    f = pl.program_id(1)

    def norm_mod(x):
        ms = jnp.mean(x * x, axis=-1, keepdims=True)
        y = x * lax.rsqrt(ms + NORM_EPS) * g_ref[...]
        return (y * (1.0 + sc_ref[...]) + sh_ref[...]).astype(BF16)

    @pl.when(f == 0)
    def _():
        h_scr[0:_HALO, :] = jnp.where(m > 0, norm_mod(xh_ref[...]), jnp.zeros((), BF16))
        h_scr[_HALO:_HALO + tm, :] = norm_mod(x_ref[...])
        acc_scr[...] = jnp.zeros_like(acc_scr)

    tf = wg_ref.shape[1]
    subs = [slice(s * _FF_SUB, (s + 1) * _FF_SUB) for s in range(tf // _FF_SUB)]
    h = h_scr[...]
    for cs in subs:
        ug_scr[:, cs] = _dot(h, wg_ref[:, cs])
        uv_scr[:, cs] = _dot(h, wv_ref[:, cs])

    def conv(u_scr, cw_ref, cb_ref, cs):
        out = cb_ref[:, cs] + cw_ref[0:1, cs] * u_scr[pl.ds(_HALO - 2, tm), cs]
        out = out + cw_ref[1:2, cs] * u_scr[pl.ds(_HALO - 1, tm), cs]
        return out + cw_ref[2:3, cs] * u_scr[pl.ds(_HALO, tm), cs]

    acc = acc_scr[...]
    for cs in subs:
        cg = conv(ug_scr, cwg_ref, cbg_ref, cs)
        cv = conv(uv_scr, cwv_ref, cbv_ref, cs)
        act = (cg * jax.nn.sigmoid(cg) * cv).astype(BF16)
        acc = acc + _dot(act, wd_ref[cs, :])
    acc_scr[...] = acc

    @pl.when(f == pl.num_programs(1) - 1)
    def _():
        y = acc_scr[...]
        ms = jnp.mean(y * y, axis=-1, keepdims=True)
        yn = y * lax.rsqrt(ms + NORM_EPS) * gp_ref[...]
        o_ref[...] = x_ref[...] + gt_ref[...] * yn


def _ffn(x, gpre, sc, sh, wup, convw, convb, wdown, gt, gpost):
    S, D = x.shape
    F = wdown.shape[0]
    tm = 512 if S % 512 == 0 else S
    tf = 512
    nf = F // tf
    hb = tm // _HALO
    row = pl.BlockSpec((1, D), lambda m, f: (0, 0))
    return pl.pallas_call(
        functools.partial(_ffn_kernel, tm=tm),
        grid=(S // tm, nf),
        in_specs=[
            pl.BlockSpec((tm, D), lambda m, f: (m, 0)),
            pl.BlockSpec((_HALO, D), lambda m, f: (jnp.maximum(m * hb - 1, 0), 0)),
            row, row, row,
            pl.BlockSpec((D, tf), lambda m, f: (0, f)),
            pl.BlockSpec((D, tf), lambda m, f: (0, nf + f)),
            pl.BlockSpec((CONV_WIDTH, tf), lambda m, f: (0, f)),
            pl.BlockSpec((CONV_WIDTH, tf), lambda m, f: (0, nf + f)),
            pl.BlockSpec((1, tf), lambda m, f: (0, f)),
            pl.BlockSpec((1, tf), lambda m, f: (0, nf + f)),
            pl.BlockSpec((tf, D), lambda m, f: (f, 0)),
            row, row,
        ],
        out_specs=pl.BlockSpec((tm, D), lambda m, f: (m, 0)),
        out_shape=jax.ShapeDtypeStruct((S, D), F32),
        scratch_shapes=[pltpu.VMEM((tm + _HALO, D), BF16), pltpu.VMEM((tm, D), F32),
                        pltpu.VMEM((tm + _HALO, tf), F32), pltpu.VMEM((tm + _HALO, tf), F32)],
        compiler_params=_cparams(("arbitrary", "arbitrary")),
        name="ffn",
    )(x, x, gpre.reshape(1, D), sc, sh, wup, wup, convw, convw, convb.reshape(1, -1), convb.reshape(1, -1),
      wdown, gt, gpost.reshape(1, D))


def _rope_tables(positions):
    inv_freq = 1.0 / (ROPE_THETA ** (jnp.arange(0, HEAD_DIM, 2, dtype=F32) / HEAD_DIM))
    ang = positions.astype(F32)[:, None] * inv_freq
    cos, sin = jnp.cos(ang), jnp.sin(ang)
    return jnp.concatenate([cos, cos], axis=-1), jnp.concatenate([-sin, sin], axis=-1)


def kernel(x, c, positions, w_ada, b_ada, g_pre_mix, w_in, pe_kc, w_kc, pe_vc, w_vc, lb_logits, g_hg_norm,
           w_br_attn, w_br_hgrn, w_out, g_post_mix, g_pre_ffn, w_up, conv_w, conv_b, w_down, g_post_ffn):
    B, S, D = x.shape
    assert B == 1, "kernel is written for one sequence"
    L = w_ada.shape[0]
    xs = x[0]
    cosf, sinf = _rope_tables(positions[0])
    scale = HEAD_DIM ** -0.5 * np.log2(np.e)
    tables = (cosf * scale, sinf * scale, cosf, sinf)
    lb_cum = jnp.cumsum(jax.nn.softmax(lb_logits.astype(F32), axis=0), axis=0)
    lower = lb_cum - lb_cum[0:1]
    log_lb = jnp.log(lower)
    log_1m = jnp.log1p(-lower)
    ada = _ada_all(c, w_ada, b_ada)

    w_tail = _rebase_tail(w_in)
    tail_tn = 1024
    hg_tiles = 4 * HG_WIDTH // tail_tn
    mg_tiles = 2 * D // tail_tn

    for l in range(L):
        sh1, sc1, gt1, sh2, sc2, gt2 = [ada[l, :, j * D:(j + 1) * D] for j in range(N_ADA)]

        h = _prenorm(xs, g_pre_mix[l], sc1, sh1)
        a = _proj(h, w_in, l, 0, A_ROLES, BF16, tn=A_TILE, tables=tables, name="proj_attn")
        hz = _proj(h, w_tail, l, 0, ("id",) * hg_tiles, F32, tn=tail_tn, name="proj_hgrn")
        mg = _proj(h, w_tail, l, hg_tiles, ("sg",) * mg_tiles, BF16, tn=tail_tn, name="proj_merge_gate")

        cmp_kv = _compress(a, jnp.stack([pe_kc[l], pe_vc[l]]), jnp.stack([w_kc[l], w_vc[l]]))
        attn = _nsa(a, cmp_kv)
        hg = _hgrn(hz, log_lb[l:l + 1], log_1m[l:l + 1], g_hg_norm[l:l + 1])
        xs = _merge(attn, hg, mg, w_br_attn[l].astype(BF16), w_br_hgrn[l].astype(BF16),
                    w_out[l].astype(BF16), xs, gt1, g_post_mix[l])

        xs = _ffn(xs, g_pre_ffn[l], sc2, sh2, w_up[l].astype(BF16), conv_w[l], conv_b[l],
                  w_down[l].astype(BF16), gt2, g_post_ffn[l])
    return xs[None]
```

```python
import functools

import jax
import jax.numpy as jnp
import numpy as np
from jax import lax
from jax.experimental import pallas as pl
from jax.experimental.pallas import tpu as pltpu

F32 = jnp.float32
BF16 = jnp.bfloat16

A_HEADS = 8
A_KV_GROUPS = 2
A_HPG = A_HEADS // A_KV_GROUPS
HEAD_DIM = 128
A_WIDTH = A_HEADS * HEAD_DIM
A_KV_WIDTH = A_KV_GROUPS * HEAD_DIM
CMP_BLOCK = 32
CMP_STRIDE = 16
SEL_BLOCK = 64
SEL_TOPK = 16
WINDOW = 512
ROPE_THETA = 10000.0
HG_HEADS = 8
HG_DIM = 128
HG_WIDTH = HG_HEADS * HG_DIM
HG_CHUNK = 64
CONV_WIDTH = 3
NORM_EPS = 1e-6
N_ADA = 6
A_COLS = A_WIDTH + 6 * A_KV_WIDTH + 3 * A_HEADS

LANES = 128
SEL_SLOTS = LANES
KEY_TILE = 256
NEG_BIG = -1e30
VMEM_LIMIT = 56 * 1024 * 1024


def _cparams(sem):
    return pltpu.CompilerParams(dimension_semantics=sem, vmem_limit_bytes=VMEM_LIMIT)


def _split3(a):
    hi = a.astype(BF16)
    r1 = a - hi.astype(F32)
    mid = r1.astype(BF16)
    lo = (r1 - mid.astype(F32)).astype(BF16)
    return hi, mid, lo


def _dot(a, b):
    return jnp.dot(a, b, preferred_element_type=F32)


def _dot_nt(a, b):
    return lax.dot_general(a, b, (((1,), (1,)), ((), ())), preferred_element_type=F32)


def _dot_tn(a, b):
    return lax.dot_general(a, b, (((0,), (0,)), ((), ())), preferred_element_type=F32)


def _ada_kernel(c_ref, w_ref, b_ref, o_ref):
    c = c_ref[...]
    ca = c * jax.nn.sigmoid(c)
    o_ref[0] = jnp.sum(ca * w_ref[0], axis=0, keepdims=True) + b_ref[0]


def _ada_all(c, w_ada, b_ada):
    L, D, N = w_ada.shape
    tn = 1024
    return pl.pallas_call(
        _ada_kernel,
        grid=(L, N // tn),
        in_specs=[
            pl.BlockSpec((D, 1), lambda l, n: (0, 0)),
            pl.BlockSpec((1, D, tn), lambda l, n: (l, 0, n)),
            pl.BlockSpec((1, 1, tn), lambda l, n: (l, 0, n)),
        ],
        out_specs=pl.BlockSpec((1, 1, tn), lambda l, n: (l, 0, n)),
        out_shape=jax.ShapeDtypeStruct((L, 1, N), F32),
        compiler_params=_cparams(("arbitrary", "arbitrary")),
        name="ada",
    )(c.reshape(D, 1), w_ada, b_ada.reshape(L, 1, N))


def _prenorm_kernel(x_ref, g_ref, sc_ref, sh_ref, o_ref):
    x = x_ref[...]
    ms = jnp.mean(x * x, axis=-1, keepdims=True)
    y = x * lax.rsqrt(ms + NORM_EPS) * g_ref[...]
    o_ref[...] = (y * (1.0 + sc_ref[...]) + sh_ref[...]).astype(o_ref.dtype)


def _prenorm(x, g, sc, sh):
    S, D = x.shape
    tm = 512
    row = pl.BlockSpec((1, D), lambda m: (0, 0))
    return pl.pallas_call(
        _prenorm_kernel,
        grid=(S // tm,),
        in_specs=[pl.BlockSpec((tm, D), lambda m: (m, 0)), row, row, row],
        out_specs=pl.BlockSpec((tm, D), lambda m: (m, 0)),
        out_shape=jax.ShapeDtypeStruct((S, D), BF16),
        compiler_params=_cparams(("arbitrary",)),
        name="prenorm",
    )(x, g.reshape(1, D), sc, sh)


def _rope(t, cosf, sinf):
    return t * cosf + pltpu.roll(t, HEAD_DIM // 2, axis=1) * sinf


def _proj_kernel(h_ref, w_ref, *rest, roles, tn):
    o_ref, w_scr = rest[-2:]
    tabs = rest[:-2]
    n = pl.program_id(0)

    @pl.when(pl.program_id(1) == 0)
    def _():
        w_scr[...] = w_ref[0].astype(BF16)

    acc = _dot_nt(h_ref[...], w_scr[...])

    def emit(slabs):
        if len(set(slabs)) == 1 and slabs[0] in ("id", "sg"):
            z = acc if slabs[0] == "id" else jax.nn.sigmoid(acc)
            o_ref[...] = z.astype(o_ref.dtype)
            return
        for s, kind in enumerate(slabs):
            sl = slice(s * LANES, (s + 1) * LANES)
            t = acc[:, sl]
            if kind == "sg":
                t = jax.nn.sigmoid(t)
            elif kind != "id":
                cos_ref, sin_ref = tabs[0:2] if kind == "rq" else tabs[-2:]
                t = _rope(t, cos_ref[...], sin_ref[...])
            o_ref[:, sl] = t.astype(o_ref.dtype)

    kinds = sorted(set(roles))
    if len(kinds) == 1:
        emit(kinds[0])
        return
    for kind in kinds:
        hit = functools.reduce(jnp.logical_or, [n == t for t, r in enumerate(roles) if r == kind])
        pl.when(hit)(functools.partial(emit, kind))


def _proj(h, w_t, layer, row0, roles, out_dtype, *, tn, tables=(), name="proj"):
    S, D = h.shape
    tm = 1024 if S % 1024 == 0 else S
    nt = len(roles)
    tab = pl.BlockSpec((tm, HEAD_DIM), lambda n, m: (m, 0))
    return pl.pallas_call(
        functools.partial(_proj_kernel, roles=tuple(roles), tn=tn),
        grid=(nt, S // tm),
        in_specs=[pl.BlockSpec((tm, D), lambda n, m: (m, 0)),
                  pl.BlockSpec((pl.Element(1), pl.Element(tn), pl.Element(D)),
                               lambda n, m: (layer, pl.multiple_of(row0 + n * tn, 8), 0))]
                 + [tab] * len(tables),
        out_specs=pl.BlockSpec((tm, tn), lambda n, m: (m, n)),
        out_shape=jax.ShapeDtypeStruct((S, nt * tn), out_dtype),
        scratch_shapes=[pltpu.VMEM((tn, D), BF16)],
        compiler_params=_cparams(("arbitrary", "arbitrary")),
        name=name,
    )(h, w_t, *tables)


A_TILE = 512
_KV_TILE_ROLES = ("rk",) * A_KV_GROUPS + ("id",) * A_KV_GROUPS
A_ROLES = ((("rq",) * (A_TILE // LANES),) * (A_WIDTH // A_TILE) + (_KV_TILE_ROLES,) * 3
           + (("sg",) * (A_TILE // LANES),))
_KV_BLOCK0 = A_WIDTH // LANES
BLK_KC, BLK_VC, BLK_KS, BLK_VS, BLK_KW, BLK_VW = [_KV_BLOCK0 + j * A_KV_GROUPS for j in range(6)]
BLK_GATE = _KV_BLOCK0 + 6 * A_KV_GROUPS


def _compress_kernel(t_ref, pe_ref, w_ref, o_ref, t32_ref, *, n_chunk):
    half = CMP_BLOCK // 2
    t32_ref[...] = t_ref[...].astype(F32)
    top, bot = [], []
    for j in range(half):
        a = t32_ref[pl.ds(j, n_chunk, stride=CMP_STRIDE), :]
        top.append(a + pe_ref[0, j:j + 1, :])
        bot.append(a + pe_ref[0, half + j:half + j + 1, :])
    a_top = jnp.concatenate(top, axis=1)
    a_bot = jnp.concatenate(bot, axis=1)
    kw = half * HEAD_DIM
    w = w_ref[0]
    w_top = w[:kw].astype(BF16)
    w_bot = w[kw:].astype(BF16)
    y_top = jnp.zeros((n_chunk, HEAD_DIM), F32)
    y_bot = jnp.zeros((n_chunk, HEAD_DIM), F32)
    for piece in _split3(a_top):
        y_top = y_top + _dot(piece, w_top)
    for piece in _split3(a_bot):
        y_bot = y_bot + _dot(piece, w_bot)
    o_ref[0, 0] = y_top + pltpu.roll(y_bot, n_chunk - 1, axis=0)


def _compress(a, pe2, w2):
    S = a.shape[0]
    n_chunk = S // CMP_STRIDE
    G = A_KV_GROUPS
    return pl.pallas_call(
        functools.partial(_compress_kernel, n_chunk=n_chunk),
        grid=(2, G),
        in_specs=[
            pl.BlockSpec((S, HEAD_DIM), lambda kv, g: (0, BLK_KC + kv * G + g)),
            pl.BlockSpec((1, CMP_BLOCK, HEAD_DIM), lambda kv, g: (kv, 0, 0)),
            pl.BlockSpec((1, CMP_BLOCK * HEAD_DIM, HEAD_DIM), lambda kv, g: (kv, 0, 0)),
        ],
        out_specs=pl.BlockSpec((1, 1, n_chunk, HEAD_DIM), lambda kv, g: (kv, g, 0, 0)),
        out_shape=jax.ShapeDtypeStruct((2, G, n_chunk, HEAD_DIM), F32),
        scratch_shapes=[pltpu.VMEM((S, HEAD_DIM), F32)],
        compiler_params=_cparams(("arbitrary", "arbitrary")),
        name="compress",
    )(a, pe2, w2)


NSA_QB = 256
SEL_BODY = 8
QK_AHEAD = 2


def _nsa_kernel(q_ref, ks_ref, kw_ref, vs_ref, vw_ref, kc_ref, vc_ref, ovt_ref, gate_ref,
                o_ref, kaug_ref, vt_ref, vwt_ref, vct_ref, acc_ref, m_ref, l_ref, *, seq):
    grp = pl.program_id(0)
    i = pl.program_id(1)
    t0 = i * NSA_QB
    n_cmp = seq // CMP_STRIDE
    cols = A_HPG * NSA_QB

    @pl.when(i == 0)
    def _():
        def build(r, c):
            r0 = pl.multiple_of(r * KEY_TILE, KEY_TILE)
            kaug_ref[r, :, 0:HEAD_DIM] = ks_ref[pl.ds(r0, KEY_TILE), :]
            blk = (r0 + lax.broadcasted_iota(jnp.int32, (KEY_TILE, SEL_SLOTS), 0)) // SEL_BLOCK
            slot = lax.broadcasted_iota(jnp.int32, (KEY_TILE, SEL_SLOTS), 1)
            kaug_ref[r, :, HEAD_DIM:HEAD_DIM + SEL_SLOTS] = jnp.where(blk == slot, 1.0, 0.0).astype(BF16)
            vt_ref[r] = vs_ref[pl.ds(r0, KEY_TILE), :].astype(F32).T.astype(BF16)
            vwt_ref[r] = vw_ref[pl.ds(r0, KEY_TILE), :].astype(F32).T.astype(BF16)
            return c
        lax.fori_loop(0, seq // KEY_TILE, build, 0)
        vct_ref[...] = vc_ref[0, 0].T.astype(BF16)

    qt = jnp.concatenate(
        [q_ref[:, h * HEAD_DIM:(h + 1) * HEAD_DIM].astype(F32).T.astype(BF16) for h in range(A_HPG)],
        axis=1)
    tq = t0 + lax.broadcasted_iota(jnp.int32, (1, cols), 1) % NSA_QB

    def softmax0(s):
        m = jnp.max(s, axis=0, keepdims=True)
        p = jnp.exp2(s - m)
        return p, jnp.sum(p, axis=0, keepdims=True)

    s_c = _dot(kc_ref[0, 0].astype(BF16), qt)
    nrow = lax.broadcasted_iota(jnp.int32, (n_cmp, 1), 0)
    vis = (nrow * CMP_STRIDE + (CMP_BLOCK - 1)) <= tq
    p_c, den = softmax0(jnp.where(vis, s_c, NEG_BIG))
    p_c = p_c * jnp.where(tq >= CMP_BLOCK - 1, 1.0 / den, 0.0)
    o_ct = _dot(vct_ref[...], p_c.astype(BF16))

    wlen = WINDOW + NSA_QB
    w0 = pl.multiple_of(jnp.maximum(t0 - WINDOW, 0), KEY_TILE)
    wt = jnp.maximum(i - WINDOW // KEY_TILE, 0)
    s_w = _dot(kw_ref[pl.ds(w0, wlen), :], qt)
    dist = tq - (w0 + lax.broadcasted_iota(jnp.int32, (wlen, 1), 0))
    in_window = dist.astype(jnp.uint32) < WINDOW
    p_w, l_w = softmax0(jnp.where(in_window, s_w, NEG_BIG))
    p_w = p_w.astype(BF16)
    o_wt = _dot(vwt_ref[wt], p_w[0:KEY_TILE])
    for k in range(1, wlen // KEY_TILE):
        o_wt = o_wt + _dot(vwt_ref[wt + k], p_w[k * KEY_TILE:(k + 1) * KEY_TILE])
    o_wt = o_wt * (1.0 / l_w)

    psum = p_c[:, 0:NSA_QB]
    for h in range(1, A_HPG):
        psum = psum + p_c[:, h * NSA_QB:(h + 1) * NSA_QB]
    ovt = ovt_ref[...]
    imp_t = jnp.zeros((SEL_SLOTS, NSA_QB), F32)
    for piece in _split3(psum):
        imp_t = imp_t + _dot(ovt, piece)
    slot_i = lax.broadcasted_iota(jnp.int32, (SEL_SLOTS, NSA_QB), 0)
    tq_l = t0 + lax.broadcasted_iota(jnp.int32, (SEL_SLOTS, NSA_QB), 1)
    cur = tq_l // SEL_BLOCK
    forced = jnp.where(slot_i == 0, 1, 0) + jnp.where(slot_i == cur, 1, 0) + jnp.where(slot_i == cur - 1, 1, 0)
    valid = slot_i * SEL_BLOCK <= tq_l
    key = jnp.where(forced > 0, -NEG_BIG, jnp.where(valid, imp_t, NEG_BIG))
    slot_f = slot_i.astype(F32)
    bias_t = jnp.full((SEL_SLOTS, NSA_QB), NEG_BIG, F32)
    for _ in range(SEL_TOPK):
        mx = jnp.max(key, axis=0, keepdims=True)
        first = jnp.min(jnp.where(key == mx, slot_f, 1e9), axis=0, keepdims=True)
        pick = slot_f == first
        bias_t = jnp.where(pick, 0.0, bias_t)
        key = jnp.where(pick, -3e38, key)

    qt_aug = jnp.concatenate([qt, jnp.concatenate([bias_t.astype(BF16)] * A_HPG, axis=1)], axis=0)
    acc_ref[...] = jnp.zeros_like(acc_ref)
    m_ref[...] = jnp.full(m_ref.shape, 0.5 * NEG_BIG, F32)
    l_ref[...] = jnp.zeros_like(l_ref)

    def sel_body(base, causal):
        n = len(causal)
        m, l = m_ref[...], l_ref[...]
        s = {}
        for k in range(min(QK_AHEAD, n)):
            s[k] = _dot(kaug_ref[base + k], qt_aug)
        for k in range(n):
            if k + QK_AHEAD < n:
                s[k + QK_AHEAD] = _dot(kaug_ref[base + k + QK_AHEAD], qt_aug)
            sk = s.pop(k)
            if causal[k]:
                kpos = (base + k) * KEY_TILE + lax.broadcasted_iota(jnp.int32, (KEY_TILE, 1), 0)
                sk = jnp.where(kpos <= tq, sk, NEG_BIG)
            m_new = jnp.maximum(m, jnp.max(sk, axis=0, keepdims=True))
            alpha = jnp.exp2(m - m_new)
            p = jnp.exp2(sk - m_new)
            l = alpha * l + jnp.sum(p, axis=0, keepdims=True)
            acc_ref[...] = alpha * acc_ref[...] + _dot(vt_ref[base + k], p.astype(BF16))
            m = m_new
        m_ref[...] = m
        l_ref[...] = l

    n_bulk = i // SEL_BODY
    rem = i % SEL_BODY

    def bulk(b, c):
        sel_body(b * SEL_BODY, (False,) * SEL_BODY)
        return c
    lax.fori_loop(0, n_bulk, bulk, 0)

    base = n_bulk * SEL_BODY
    part = SEL_BODY // 2
    while part >= 2:
        pl.when(rem & part != 0)(functools.partial(sel_body, base, (False,) * part))
        base = base + (rem & part)
        part //= 2

    @pl.when(rem % 2 == 1)
    def _():
        sel_body(i - 1, (False, True))

    @pl.when(rem % 2 == 0)
    def _():
        sel_body(i, (True,))

    o_st = acc_ref[...] * (1.0 / l_ref[...])

    gates_t = gate_ref[...].astype(F32).T
    per_group = 3 * A_HPG

    def gate(h, b):
        r = 3 * h + b
        return jnp.where(grp == 0, gates_t[r:r + 1], gates_t[per_group + r:per_group + r + 1])

    for h in range(A_HPG):
        c = slice(h * NSA_QB, (h + 1) * NSA_QB)
        o = gate(h, 0) * o_ct[:, c] + gate(h, 1) * o_st[:, c] + gate(h, 2) * o_wt[:, c]
        o_ref[:, h * HEAD_DIM:(h + 1) * HEAD_DIM] = o.T.astype(o_ref.dtype)


def _overlap_matrix(seq):
    n_chunk = seq // CMP_STRIDE
    cs = np.arange(n_chunk)[:, None] * CMP_STRIDE
    ss = np.arange(SEL_SLOTS)[None, :] * SEL_BLOCK
    ov = np.maximum(np.minimum(cs + CMP_BLOCK, ss + SEL_BLOCK) - np.maximum(cs, ss), 0) / CMP_BLOCK
    n_cmp = (seq - CMP_BLOCK) // CMP_STRIDE + 1
    ov[n_cmp:] = 0.0
    ov[:, seq // SEL_BLOCK:] = 0.0
    return ov.astype(np.float32)


def _nsa(a, cmp_kv):
    S = a.shape[0]
    G = A_KV_GROUPS
    n_tiles = S // KEY_TILE
    assert S // SEL_BLOCK <= SEL_SLOTS and S % KEY_TILE == 0 and S >= WINDOW + NSA_QB
    assert NSA_QB == KEY_TILE and G == 2
    n_chunk = S // CMP_STRIDE
    ovt = jnp.asarray(_overlap_matrix(S).T, BF16)
    col = lambda b: pl.BlockSpec((S, HEAD_DIM), lambda g, i, b=b: (0, b + g))
    return pl.pallas_call(
        functools.partial(_nsa_kernel, seq=S),
        grid=(G, S // NSA_QB),
        in_specs=[
            pl.BlockSpec((NSA_QB, A_HPG * HEAD_DIM), lambda g, i: (i, g)),
            col(BLK_KS), col(BLK_KW), col(BLK_VS), col(BLK_VW),
            pl.BlockSpec((1, 1, n_chunk, HEAD_DIM), lambda g, i: (0, g, 0, 0)),
            pl.BlockSpec((1, 1, n_chunk, HEAD_DIM), lambda g, i: (1, g, 0, 0)),
            pl.BlockSpec((SEL_SLOTS, n_chunk), lambda g, i: (0, 0)),
            pl.BlockSpec((NSA_QB, LANES), lambda g, i: (i, BLK_GATE)),
        ],
        out_specs=pl.BlockSpec((NSA_QB, A_HPG * HEAD_DIM), lambda g, i: (i, g)),
        out_shape=jax.ShapeDtypeStruct((S, A_WIDTH), BF16),
        scratch_shapes=[pltpu.VMEM((n_tiles, KEY_TILE, HEAD_DIM + SEL_SLOTS), BF16),
                        pltpu.VMEM((n_tiles, HEAD_DIM, KEY_TILE), BF16),
                        pltpu.VMEM((n_tiles, HEAD_DIM, KEY_TILE), BF16),
                        pltpu.VMEM((HEAD_DIM, n_chunk), BF16),
                        pltpu.VMEM((HEAD_DIM, A_HPG * NSA_QB), F32),
                        pltpu.VMEM((1, A_HPG * NSA_QB), F32),
                        pltpu.VMEM((1, A_HPG * NSA_QB), F32)],
        compiler_params=_cparams(("arbitrary", "arbitrary")),
        name="nsa",
    )(a, a, a, a, a, cmp_kv, cmp_kv, ovt, a)


_ROW_MASKS = 18
_SCORE_MASKS = 10
HG_HEADS_PER_STEP = 4


def _hgrn_tables():
    t = np.arange(HG_CHUNK)
    p4, i4, i16 = t % 4, (t // 4) % 4, t // 16
    conds = [p4 >= 1, p4 >= 2, p4 >= 3, p4 <= 2, p4 <= 1, p4 == 0,
             i4 >= 1, i4 >= 2, i4 >= 3, i4 <= 2, i4 <= 1, i4 == 0,
             i16 >= 1, i16 >= 2, i16 >= 3, i16 <= 2, i16 <= 1, i16 == 0]
    rm = np.stack([np.broadcast_to(c[:, None], (HG_CHUNK, HG_DIM)) for c in conds]).astype(np.float32)
    tt, ss = t[:, None], t[None, :]
    lag16 = tt // 16 - ss // 16
    lag4 = tt // 4 - ss // 4
    same16, same4 = lag16 == 0, lag4 == 0
    sm = [lag16 == 1, lag16 == 2, lag16 == 3,
          same16 & (lag4 == 1), same16 & (lag4 == 2), same16 & (lag4 == 3),
          same4 & (tt - ss == 0), same4 & (tt - ss == 1), same4 & (tt - ss == 2), same4 & (tt - ss == 3)]
    return rm, np.stack(sm).astype(np.float32)


def _hgrn_kernel(q_ref, f_ref, i_ref, og_ref, loglb_ref, log1m_ref, gn_ref, rm_ref, sm_ref,
                 o_ref, st_ref, g_scr, k_scr, o_scr, *, tb, nh):
    C = HG_CHUNK
    heads = [slice(h * HG_DIM, (h + 1) * HG_DIM) for h in range(nh)]

    @pl.when(pl.program_id(1) == 0)
    def _():
        st_ref[...] = jnp.zeros_like(st_ref)

    x = f_ref[...]
    log_sig = jnp.minimum(x, 0.0) - jnp.log(1.0 + jnp.exp(-jnp.abs(x)))
    a = loglb_ref[...]
    c = log1m_ref[...] + log_sig
    log_f = jnp.maximum(a, c) + jnp.log(1.0 + jnp.exp(-jnp.abs(a - c)))
    g_scr[...] = log_f
    k_scr[...] = 1.0 - jnp.exp(log_f)

    def rmask(n):
        return jnp.concatenate([rm_ref[n]] * nh, axis=1)

    def roll(v, s):
        return pltpu.roll(v, s % C, axis=0)

    def b16(z):
        return z.astype(BF16)

    def chunk(ci, carry):
        r0 = pl.multiple_of(ci * C, C)
        g = g_scr[pl.ds(r0, C), :]
        q = q_ref[pl.ds(r0, C), :]
        k = k_scr[pl.ds(r0, C), :]
        v = i_ref[pl.ds(r0, C), :].astype(BF16)
        pre4 = g + rmask(0) * roll(g, 1) + rmask(1) * roll(g, 2) + rmask(2) * roll(g, 3)
        suf4 = rmask(3) * roll(g, -1) + rmask(4) * roll(g, -2) + rmask(5) * roll(g, -3)
        tot4 = pre4 + suf4
        a1, a2, a3 = roll(tot4, 4), roll(tot4, 8), roll(tot4, 12)
        pre16 = pre4 + rmask(6) * a1 + rmask(7) * a2 + rmask(8) * a3
        suf16 = suf4 + rmask(9) * roll(tot4, -4) + rmask(10) * roll(tot4, -8) + rmask(11) * roll(tot4, -12)
        tot16 = pre16 + suf16
        d1, d2, d3 = roll(tot16, 16), roll(tot16, 32), roll(tot16, 48)
        pre64 = pre16 + rmask(12) * d1 + rmask(13) * d2 + rmask(14) * d3
        suf64 = suf16 + rmask(15) * roll(tot16, -16) + rmask(16) * roll(tot16, -32) + rmask(17) * roll(tot16, -48)
        tot64 = pre64 + suf64
        g1, g2 = roll(g, 1), roll(g, 2)

        q_a = q * jnp.exp(pre16)
        lhs_a = [b16(q_a), b16(q_a * jnp.exp(d1)), b16(q_a * jnp.exp(d1 + d2))]
        k_a = b16(k * jnp.exp(suf16))
        q_b = q * jnp.exp(pre4)
        lhs_b = [b16(q_b), b16(q_b * jnp.exp(a1)), b16(q_b * jnp.exp(a1 + a2))]
        k_b = b16(k * jnp.exp(suf4))
        q_c = q * jnp.exp(g)
        lhs_c = [b16(q), b16(q_c), b16(q_c * jnp.exp(g1)), b16(q_c * jnp.exp(g1 + g2))]
        k_c = b16(k)
        q_s = b16(q * jnp.exp(pre64))
        k_s = b16(k * jnp.exp(suf64))
        decay = jnp.exp(tot64[0:1, :])

        def level(lhs, rhs):
            return [_dot_nt(jnp.concatenate([z[:, hh] for z in lhs], axis=0), rhs[:, hh]) for hh in heads]

        s_a, s_b, s_c = level(lhs_a, k_a), level(lhs_b, k_b), level(lhs_c, k_c)
        o_inter = [_dot_nt(q_s[:, hh], b16(st_ref[h])) for h, hh in enumerate(heads)]
        kv = [_dot_tn(v[:, hh], k_s[:, hh]) for hh in heads]
        for h, hh in enumerate(heads):
            sa, sb, sc = s_a[h], s_b[h], s_c[h]
            scores = (sm_ref[0] * sa[0:C] + sm_ref[1] * sa[C:2 * C] + sm_ref[2] * sa[2 * C:3 * C]
                      + sm_ref[3] * sb[0:C] + sm_ref[4] * sb[C:2 * C] + sm_ref[5] * sb[2 * C:3 * C]
                      + sm_ref[6] * sc[0:C] + sm_ref[7] * sc[C:2 * C] + sm_ref[8] * sc[2 * C:3 * C]
                      + sm_ref[9] * sc[3 * C:4 * C])
            o_scr[pl.ds(r0, C), hh] = _dot(b16(scores), v[:, hh]) + o_inter[h]
            st_ref[h] = st_ref[h] * decay[:, hh] + kv[h]
        return carry

    lax.fori_loop(0, tb // C, chunk, 0)

    for hh in heads:
        o = o_scr[:, hh]
        ms = jnp.mean(o * o, axis=-1, keepdims=True)
        og = og_ref[:, hh]
        y = o * lax.rsqrt(ms + NORM_EPS) * gn_ref[:, hh] * (og * jax.nn.sigmoid(og))
        o_ref[:, hh] = y.astype(o_ref.dtype)


def _hgrn(hz, loglb, log1mlb, gnorm):
    S = hz.shape[0]
    tb = 512 if S % 512 == 0 else S
    nh = HG_HEADS_PER_STEP
    ng = HG_HEADS // nh
    w = nh * HG_DIM
    rm, sm = _hgrn_tables()
    blk = lambda part: pl.BlockSpec((tb, w), lambda h, c, part=part: (c, part * ng + h))
    row = pl.BlockSpec((1, w), lambda h, c: (0, h))
    return pl.pallas_call(
        functools.partial(_hgrn_kernel, tb=tb, nh=nh),
        grid=(ng, S // tb),
        in_specs=[blk(0), blk(1), blk(2), blk(3), row, row, row,
                  pl.BlockSpec((_ROW_MASKS, HG_CHUNK, HG_DIM), lambda h, c: (0, 0, 0)),
                  pl.BlockSpec((_SCORE_MASKS, HG_CHUNK, HG_CHUNK), lambda h, c: (0, 0, 0))],
        out_specs=pl.BlockSpec((tb, w), lambda h, c: (c, h)),
        out_shape=jax.ShapeDtypeStruct((S, HG_WIDTH), BF16),
        scratch_shapes=[pltpu.VMEM((nh, HG_DIM, HG_DIM), F32), pltpu.VMEM((tb, w), F32),
                        pltpu.VMEM((tb, w), F32), pltpu.VMEM((tb, w), F32)],
        compiler_params=_cparams(("arbitrary", "arbitrary")),
        name="hgrn",
    )(hz, hz, hz, hz, loglb, log1mlb, gnorm, jnp.asarray(rm), jnp.asarray(sm))


def _merge_kernel(at_ref, hg_ref, ga_ref, gh_ref, wa_ref, wh_ref, wo_ref, x_ref, gt_ref, gp_ref, o_ref):
    mixed = (ga_ref[...].astype(F32) * _dot(at_ref[...], wa_ref[0])
             + gh_ref[...].astype(F32) * _dot(hg_ref[...], wh_ref[0]))
    y = _dot(mixed.astype(BF16), wo_ref[0])
    ms = jnp.mean(y * y, axis=-1, keepdims=True)
    yn = y * lax.rsqrt(ms + NORM_EPS) * gp_ref[...]
    o_ref[...] = x_ref[...] + gt_ref[...] * yn


def _merge(attn, hg, mg, wa, wh, wo, layer, x, gt, gpost):
    S, D = x.shape
    tm = 256
    const = lambda shape: pl.BlockSpec((1,) + shape, lambda m: (layer, 0, 0), pipeline_mode=pl.Buffered(1))
    row = pl.BlockSpec((1, D), lambda m: (0, 0))
    return pl.pallas_call(
        _merge_kernel,
        grid=(S // tm,),
        in_specs=[
            pl.BlockSpec((tm, A_WIDTH), lambda m: (m, 0)),
            pl.BlockSpec((tm, HG_WIDTH), lambda m: (m, 0)),
            pl.BlockSpec((tm, D), lambda m: (m, 0)),
            pl.BlockSpec((tm, D), lambda m: (m, 1)),
            const((A_WIDTH, D)), const((HG_WIDTH, D)), const((D, D)),
            pl.BlockSpec((tm, D), lambda m: (m, 0)),
            row, row,
        ],
        out_specs=pl.BlockSpec((tm, D), lambda m: (m, 0)),
        out_shape=jax.ShapeDtypeStruct((S, D), F32),
        compiler_params=_cparams(("arbitrary",)),
        name="merge",
    )(attn, hg, mg, mg, wa, wh, wo, x, gt, gpost.reshape(1, D))


_HALO = 16
_FF_SUB = 256


def _ffn_kernel(x_ref, xh_ref, g_ref, sc_ref, sh_ref, wg_ref, wv_ref, cwg_ref, cwv_ref, cbg_ref, cbv_ref,
                wd_ref, gt_ref, gp_ref, o_ref, h_scr, acc_scr, ug_scr, uv_scr, *, tm):
    m = pl.program_id(0)
    f = pl.program_id(1)

    def norm_mod(x):
        ms = jnp.mean(x * x, axis=-1, keepdims=True)
        y = x * lax.rsqrt(ms + NORM_EPS) * g_ref[...]
        return (y * (1.0 + sc_ref[...]) + sh_ref[...]).astype(BF16)

    @pl.when(f == 0)
    def _():
        h_scr[0:_HALO, :] = jnp.where(m > 0, norm_mod(xh_ref[...]), jnp.zeros((), BF16))
        h_scr[_HALO:_HALO + tm, :] = norm_mod(x_ref[...])
        acc_scr[...] = jnp.zeros_like(acc_scr)

    tf = wg_ref.shape[2]
    subs = [slice(s * _FF_SUB, (s + 1) * _FF_SUB) for s in range(tf // _FF_SUB)]
    h = h_scr[...]
    for cs in subs:
        ug_scr[:, cs] = _dot(h, wg_ref[0, :, cs])
        uv_scr[:, cs] = _dot(h, wv_ref[0, :, cs])

    def conv(u_scr, cw_ref, cb_ref, cs):
        out = cb_ref[:, cs] + cw_ref[0:1, cs] * u_scr[pl.ds(_HALO - 2, tm), cs]
        out = out + cw_ref[1:2, cs] * u_scr[pl.ds(_HALO - 1, tm), cs]
        return out + cw_ref[2:3, cs] * u_scr[pl.ds(_HALO, tm), cs]

    acc = acc_scr[...]
    for cs in subs:
        cg = conv(ug_scr, cwg_ref, cbg_ref, cs)
        cv = conv(uv_scr, cwv_ref, cbv_ref, cs)
        act = (cg * jax.nn.sigmoid(cg) * cv).astype(BF16)
        acc = acc + _dot(act, wd_ref[0, cs, :])
    acc_scr[...] = acc

    @pl.when(f == pl.num_programs(1) - 1)
    def _():
        y = acc_scr[...]
        ms = jnp.mean(y * y, axis=-1, keepdims=True)
        yn = y * lax.rsqrt(ms + NORM_EPS) * gp_ref[...]
        o_ref[...] = x_ref[...] + gt_ref[...] * yn


def _ffn(x, gpre, sc, sh, wup, convw, convb, wdown, layer, gt, gpost):
    S, D = x.shape
    F = wdown.shape[1]
    tm = 512 if S % 512 == 0 else S
    tf = 512
    nf = F // tf
    hb = tm // _HALO
    row = pl.BlockSpec((1, D), lambda m, f: (0, 0))
    return pl.pallas_call(
        functools.partial(_ffn_kernel, tm=tm),
        grid=(S // tm, nf),
        in_specs=[
            pl.BlockSpec((tm, D), lambda m, f: (m, 0)),
            pl.BlockSpec((_HALO, D), lambda m, f: (jnp.maximum(m * hb - 1, 0), 0)),
            row, row, row,
            pl.BlockSpec((1, D, tf), lambda m, f: (layer, 0, f)),
            pl.BlockSpec((1, D, tf), lambda m, f: (layer, 0, nf + f)),
            pl.BlockSpec((CONV_WIDTH, tf), lambda m, f: (0, f)),
            pl.BlockSpec((CONV_WIDTH, tf), lambda m, f: (0, nf + f)),
            pl.BlockSpec((1, tf), lambda m, f: (0, f)),
            pl.BlockSpec((1, tf), lambda m, f: (0, nf + f)),
            pl.BlockSpec((1, tf, D), lambda m, f: (layer, f, 0)),
            row, row,
        ],
        out_specs=pl.BlockSpec((tm, D), lambda m, f: (m, 0)),
        out_shape=jax.ShapeDtypeStruct((S, D), F32),
        scratch_shapes=[pltpu.VMEM((tm + _HALO, D), BF16), pltpu.VMEM((tm, D), F32),
                        pltpu.VMEM((tm + _HALO, tf), F32), pltpu.VMEM((tm + _HALO, tf), F32)],
        compiler_params=_cparams(("arbitrary", "arbitrary")),
        name="ffn",
    )(x, x, gpre.reshape(1, D), sc, sh, wup, wup, convw, convw, convb.reshape(1, -1), convb.reshape(1, -1),
      wdown, gt, gpost.reshape(1, D))


def _rope_tables(positions):
    inv_freq = 1.0 / (ROPE_THETA ** (jnp.arange(0, HEAD_DIM, 2, dtype=F32) / HEAD_DIM))
    ang = positions.astype(F32)[:, None] * inv_freq
    cos, sin = jnp.cos(ang), jnp.sin(ang)
    return jnp.concatenate([cos, cos], axis=-1), jnp.concatenate([-sin, sin], axis=-1)


def kernel(x, c, positions, w_ada, b_ada, g_pre_mix, w_in, pe_kc, w_kc, pe_vc, w_vc, lb_logits, g_hg_norm,
           w_br_attn, w_br_hgrn, w_out, g_post_mix, g_pre_ffn, w_up, conv_w, conv_b, w_down, g_post_ffn):
    B, S, D = x.shape
    assert B == 1, "kernel is written for one sequence"
    L = w_ada.shape[0]
    xs = x[0]
    cosf, sinf = _rope_tables(positions[0])
    scale = HEAD_DIM ** -0.5 * np.log2(np.e)
    tables = (cosf * scale, sinf * scale, cosf, sinf)
    lb_cum = jnp.cumsum(jax.nn.softmax(lb_logits.astype(F32), axis=0), axis=0)
    lower = lb_cum - lb_cum[0:1]
    log_lb = jnp.log(lower)
    log_1m = jnp.log1p(-lower)
    ada = _ada_all(c, w_ada, b_ada)

    w_in_t = jnp.swapaxes(w_in, 1, 2)
    tail_tn = 1024
    hg_roles = (("id",) * (tail_tn // LANES),) * (4 * HG_WIDTH // tail_tn)
    mg_roles = (("sg",) * (tail_tn // LANES),) * (2 * D // tail_tn)

    wa_b, wh_b, wo_b = w_br_attn.astype(BF16), w_br_hgrn.astype(BF16), w_out.astype(BF16)
    wup_b, wdown_b = w_up.astype(BF16), w_down.astype(BF16)

    for l in range(L):
        sh1, sc1, gt1, sh2, sc2, gt2 = [ada[l, :, j * D:(j + 1) * D] for j in range(N_ADA)]

        h = _prenorm(xs, g_pre_mix[l], sc1, sh1)
        a = _proj(h, w_in_t, l, 0, A_ROLES, BF16, tn=A_TILE, tables=tables, name="proj_attn")
        hz = _proj(h, w_in_t, l, A_COLS, hg_roles, F32, tn=tail_tn, name="proj_hgrn")
        mg = _proj(h, w_in_t, l, A_COLS + 4 * HG_WIDTH, mg_roles, BF16, tn=tail_tn, name="proj_merge_gate")

        cmp_kv = _compress(a, jnp.stack([pe_kc[l], pe_vc[l]]), jnp.stack([w_kc[l], w_vc[l]]))
        attn = _nsa(a, cmp_kv)
        hg = _hgrn(hz, log_lb[l:l + 1], log_1m[l:l + 1], g_hg_norm[l:l + 1])
        xs = _merge(attn, hg, mg, wa_b, wh_b, wo_b, l, xs, gt1, g_post_mix[l])

        xs = _ffn(xs, g_pre_ffn[l], sc2, sh2, wup_b, conv_w[l], conv_b[l], wdown_b, l, gt2, g_post_ffn[l])
    return xs[None]
```

```python
import functools

import jax
import jax.numpy as jnp
import numpy as np
from jax import lax
from jax.experimental import pallas as pl
from jax.experimental.pallas import tpu as pltpu

F32 = jnp.float32
BF16 = jnp.bfloat16

A_HEADS = 8
A_KV_GROUPS = 2
A_HPG = A_HEADS // A_KV_GROUPS
HEAD_DIM = 128
A_WIDTH = A_HEADS * HEAD_DIM
A_KV_WIDTH = A_KV_GROUPS * HEAD_DIM
CMP_BLOCK = 32
CMP_STRIDE = 16
SEL_BLOCK = 64
SEL_TOPK = 16
WINDOW = 512
ROPE_THETA = 10000.0
HG_HEADS = 8
HG_DIM = 128
HG_WIDTH = HG_HEADS * HG_DIM
HG_CHUNK = 64
CONV_WIDTH = 3
NORM_EPS = 1e-6
N_ADA = 6
A_COLS = A_WIDTH + 6 * A_KV_WIDTH + 3 * A_HEADS

LANES = 128
SEL_SLOTS = LANES
KEY_TILE = 256
NEG_BIG = -1e30
VMEM_LIMIT = 56 * 1024 * 1024


def _cparams(sem):
    return pltpu.CompilerParams(dimension_semantics=sem, vmem_limit_bytes=VMEM_LIMIT)


def _split3(a):
    hi = a.astype(BF16)
    r1 = a - hi.astype(F32)
    mid = r1.astype(BF16)
    lo = (r1 - mid.astype(F32)).astype(BF16)
    return hi, mid, lo


def _dot(a, b):
    return jnp.dot(a, b, preferred_element_type=F32)


def _dot_nt(a, b):
    return lax.dot_general(a, b, (((1,), (1,)), ((), ())), preferred_element_type=F32)


def _dot_tn(a, b):
    return lax.dot_general(a, b, (((0,), (0,)), ((), ())), preferred_element_type=F32)


def _ada_kernel(c_ref, w_ref, b_ref, o_ref):
    c = c_ref[...]
    ca = c * jax.nn.sigmoid(c)
    o_ref[0] = jnp.sum(ca * w_ref[0], axis=0, keepdims=True) + b_ref[0]


def _ada_all(c, w_ada, b_ada):
    L, D, N = w_ada.shape
    tn = 1024
    return pl.pallas_call(
        _ada_kernel,
        grid=(L, N // tn),
        in_specs=[
            pl.BlockSpec((D, 1), lambda l, n: (0, 0)),
            pl.BlockSpec((1, D, tn), lambda l, n: (l, 0, n)),
            pl.BlockSpec((1, 1, tn), lambda l, n: (l, 0, n)),
        ],
        out_specs=pl.BlockSpec((1, 1, tn), lambda l, n: (l, 0, n)),
        out_shape=jax.ShapeDtypeStruct((L, 1, N), F32),
        compiler_params=_cparams(("arbitrary", "arbitrary")),
        name="ada",
    )(c.reshape(D, 1), w_ada, b_ada.reshape(L, 1, N))


def _prenorm_kernel(x_ref, g_ref, sc_ref, sh_ref, o_ref):
    x = x_ref[...]
    ms = jnp.mean(x * x, axis=-1, keepdims=True)
    y = x * lax.rsqrt(ms + NORM_EPS) * g_ref[...]
    o_ref[...] = (y * (1.0 + sc_ref[...]) + sh_ref[...]).astype(o_ref.dtype)


def _prenorm(x, g, sc, sh):
    S, D = x.shape
    tm = 512
    row = pl.BlockSpec((1, D), lambda m: (0, 0))
    return pl.pallas_call(
        _prenorm_kernel,
        grid=(S // tm,),
        in_specs=[pl.BlockSpec((tm, D), lambda m: (m, 0)), row, row, row],
        out_specs=pl.BlockSpec((tm, D), lambda m: (m, 0)),
        out_shape=jax.ShapeDtypeStruct((S, D), BF16),
        compiler_params=_cparams(("arbitrary",)),
        name="prenorm",
    )(x, g.reshape(1, D), sc, sh)


def _rope(t, cosf, sinf):
    return t * cosf + pltpu.roll(t, HEAD_DIM // 2, axis=1) * sinf


def _proj_kernel(h_ref, w_ref, *rest, roles, tn):
    o_ref, w_scr = rest[-2:]
    tabs = rest[:-2]
    n = pl.program_id(0)

    @pl.when(pl.program_id(1) == 0)
    def _():
        w_scr[...] = w_ref[0].astype(BF16)

    acc = _dot_nt(h_ref[...], w_scr[...])

    def emit(slabs):
        if len(set(slabs)) == 1 and slabs[0] in ("id", "sg"):
            z = acc if slabs[0] == "id" else jax.nn.sigmoid(acc)
            o_ref[...] = z.astype(o_ref.dtype)
            return
        for s, kind in enumerate(slabs):
            sl = slice(s * LANES, (s + 1) * LANES)
            t = acc[:, sl]
            if kind == "sg":
                t = jax.nn.sigmoid(t)
            elif kind != "id":
                cos_ref, sin_ref = tabs[0:2] if kind == "rq" else tabs[-2:]
                t = _rope(t, cos_ref[...], sin_ref[...])
            o_ref[:, sl] = t.astype(o_ref.dtype)

    kinds = sorted(set(roles))
    if len(kinds) == 1:
        emit(kinds[0])
        return
    for kind in kinds:
        hit = functools.reduce(jnp.logical_or, [n == t for t, r in enumerate(roles) if r == kind])
        pl.when(hit)(functools.partial(emit, kind))


def _proj(h, w_t, layer, row0, roles, out_dtype, *, tn, tables=(), name="proj"):
    S, D = h.shape
    tm = 1024 if S % 1024 == 0 else S
    nt = len(roles)
    tab = pl.BlockSpec((tm, HEAD_DIM), lambda n, m: (m, 0))
    return pl.pallas_call(
        functools.partial(_proj_kernel, roles=tuple(roles), tn=tn),
        grid=(nt, S // tm),
        in_specs=[pl.BlockSpec((tm, D), lambda n, m: (m, 0)),
                  pl.BlockSpec((pl.Element(1), pl.Element(tn), pl.Element(D)),
                               lambda n, m: (layer, pl.multiple_of(row0 + n * tn, 8), 0))]
                 + [tab] * len(tables),
        out_specs=pl.BlockSpec((tm, tn), lambda n, m: (m, n)),
        out_shape=jax.ShapeDtypeStruct((S, nt * tn), out_dtype),
        scratch_shapes=[pltpu.VMEM((tn, D), BF16)],
        compiler_params=_cparams(("arbitrary", "arbitrary")),
        name=name,
    )(h, w_t, *tables)


A_TILE = 512
_KV_TILE_ROLES = ("rk",) * A_KV_GROUPS + ("id",) * A_KV_GROUPS
A_ROLES = ((("rq",) * (A_TILE // LANES),) * (A_WIDTH // A_TILE) + (_KV_TILE_ROLES,) * 3
           + (("sg",) * (A_TILE // LANES),))
_KV_BLOCK0 = A_WIDTH // LANES
BLK_KC, BLK_VC, BLK_KS, BLK_VS, BLK_KW, BLK_VW = [_KV_BLOCK0 + j * A_KV_GROUPS for j in range(6)]
BLK_GATE = _KV_BLOCK0 + 6 * A_KV_GROUPS


def _compress_kernel(t_ref, pe_ref, w_ref, o_ref, t32_ref, *, n_chunk):
    half = CMP_BLOCK // 2
    t32_ref[...] = t_ref[...].astype(F32)
    top, bot = [], []
    for j in range(half):
        a = t32_ref[pl.ds(j, n_chunk, stride=CMP_STRIDE), :]
        top.append(a + pe_ref[0, j:j + 1, :])
        bot.append(a + pe_ref[0, half + j:half + j + 1, :])
    a_top = jnp.concatenate(top, axis=1)
    a_bot = jnp.concatenate(bot, axis=1)
    kw = half * HEAD_DIM
    w = w_ref[0]
    w_top = w[:kw].astype(BF16)
    w_bot = w[kw:].astype(BF16)
    y_top = jnp.zeros((n_chunk, HEAD_DIM), F32)
    y_bot = jnp.zeros((n_chunk, HEAD_DIM), F32)
    for piece in _split3(a_top):
        y_top = y_top + _dot(piece, w_top)
    for piece in _split3(a_bot):
        y_bot = y_bot + _dot(piece, w_bot)
    o_ref[0, 0] = y_top + pltpu.roll(y_bot, n_chunk - 1, axis=0)


def _compress(a, pe2, w2):
    S = a.shape[0]
    n_chunk = S // CMP_STRIDE
    G = A_KV_GROUPS
    return pl.pallas_call(
        functools.partial(_compress_kernel, n_chunk=n_chunk),
        grid=(2, G),
        in_specs=[
            pl.BlockSpec((S, HEAD_DIM), lambda kv, g: (0, BLK_KC + kv * G + g)),
            pl.BlockSpec((1, CMP_BLOCK, HEAD_DIM), lambda kv, g: (kv, 0, 0)),
            pl.BlockSpec((1, CMP_BLOCK * HEAD_DIM, HEAD_DIM), lambda kv, g: (kv, 0, 0)),
        ],
        out_specs=pl.BlockSpec((1, 1, n_chunk, HEAD_DIM), lambda kv, g: (kv, g, 0, 0)),
        out_shape=jax.ShapeDtypeStruct((2, G, n_chunk, HEAD_DIM), F32),
        scratch_shapes=[pltpu.VMEM((S, HEAD_DIM), F32)],
        compiler_params=_cparams(("arbitrary", "arbitrary")),
        name="compress",
    )(a, pe2, w2)


NSA_QB = 256
SEL_BODY = 8
SEL_GROUP = 4


def _nsa_kernel(q_ref, ks_ref, kw_ref, vs_ref, vw_ref, kc_ref, vc_ref, ovt_ref, gate_ref,
                o_ref, kaug_ref, vt_ref, vwt_ref, vct_ref, acc_ref, m_ref, l_ref, *, seq):
    grp = pl.program_id(0)
    i = pl.program_id(1)
    t0 = i * NSA_QB
    n_cmp = seq // CMP_STRIDE
    cols = A_HPG * NSA_QB

    @pl.when(i == 0)
    def _():
        def build(r, c):
            r0 = pl.multiple_of(r * KEY_TILE, KEY_TILE)
            kaug_ref[r, :, 0:HEAD_DIM] = ks_ref[pl.ds(r0, KEY_TILE), :]
            blk = (r0 + lax.broadcasted_iota(jnp.int32, (KEY_TILE, SEL_SLOTS), 0)) // SEL_BLOCK
            slot = lax.broadcasted_iota(jnp.int32, (KEY_TILE, SEL_SLOTS), 1)
            kaug_ref[r, :, HEAD_DIM:HEAD_DIM + SEL_SLOTS] = jnp.where(blk == slot, 1.0, 0.0).astype(BF16)
            vt_ref[r] = vs_ref[pl.ds(r0, KEY_TILE), :].astype(F32).T.astype(BF16)
            vwt_ref[r] = vw_ref[pl.ds(r0, KEY_TILE), :].astype(F32).T.astype(BF16)
            return c
        lax.fori_loop(0, seq // KEY_TILE, build, 0)
        vct_ref[...] = vc_ref[0, 0].T.astype(BF16)

    qt = jnp.concatenate(
        [q_ref[:, h * HEAD_DIM:(h + 1) * HEAD_DIM].astype(F32).T.astype(BF16) for h in range(A_HPG)],
        axis=1)
    tq = t0 + lax.broadcasted_iota(jnp.int32, (1, cols), 1) % NSA_QB

    def softmax0(s):
        m = jnp.max(s, axis=0, keepdims=True)
        p = jnp.exp2(s - m)
        return p, jnp.sum(p, axis=0, keepdims=True)

    s_c = _dot(kc_ref[0, 0].astype(BF16), qt)
    nrow = lax.broadcasted_iota(jnp.int32, (n_cmp, 1), 0)
    vis = (nrow * CMP_STRIDE + (CMP_BLOCK - 1)) <= tq
    p_c, den = softmax0(jnp.where(vis, s_c, NEG_BIG))
    p_c = p_c * jnp.where(tq >= CMP_BLOCK - 1, 1.0 / den, 0.0)
    o_ct = _dot(vct_ref[...], p_c.astype(BF16))

    psum = p_c[:, 0:NSA_QB]
    for h in range(1, A_HPG):
        psum = psum + p_c[:, h * NSA_QB:(h + 1) * NSA_QB]
    ovt = ovt_ref[...]
    imp_t = jnp.zeros((SEL_SLOTS, NSA_QB), F32)
    for piece in _split3(psum):
        imp_t = imp_t + _dot(ovt, piece)
    slot_i = lax.broadcasted_iota(jnp.int32, (SEL_SLOTS, NSA_QB), 0)
    tq_l = t0 + lax.broadcasted_iota(jnp.int32, (SEL_SLOTS, NSA_QB), 1)
    cur = tq_l // SEL_BLOCK
    forced = jnp.where(slot_i == 0, 1, 0) + jnp.where(slot_i == cur, 1, 0) + jnp.where(slot_i == cur - 1, 1, 0)
    valid = slot_i * SEL_BLOCK <= tq_l
    key = jnp.where(forced > 0, -NEG_BIG, jnp.where(valid, imp_t, NEG_BIG))
    slot_f = slot_i.astype(F32)
    bias_t = jnp.full((SEL_SLOTS, NSA_QB), NEG_BIG, F32)
    for _ in range(SEL_TOPK):
        mx = jnp.max(key, axis=0, keepdims=True)
        first = jnp.min(jnp.where(key == mx, slot_f, 1e9), axis=0, keepdims=True)
        pick = slot_f == first
        bias_t = jnp.where(pick, 0.0, bias_t)
        key = jnp.where(pick, -3e38, key)

    wlen = WINDOW + NSA_QB
    w0 = pl.multiple_of(jnp.maximum(t0 - WINDOW, 0), KEY_TILE)
    wt = jnp.maximum(i - WINDOW // KEY_TILE, 0)
    s_w = _dot(kw_ref[pl.ds(w0, wlen), :], qt)
    dist = tq - (w0 + lax.broadcasted_iota(jnp.int32, (wlen, 1), 0))
    in_window = dist.astype(jnp.uint32) < WINDOW
    p_w, l_w = softmax0(jnp.where(in_window, s_w, NEG_BIG))
    p_w = p_w.astype(BF16)
    o_wt = _dot(vwt_ref[wt], p_w[0:KEY_TILE])
    for k in range(1, wlen // KEY_TILE):
        o_wt = o_wt + _dot(vwt_ref[wt + k], p_w[k * KEY_TILE:(k + 1) * KEY_TILE])
    o_wt = o_wt * (1.0 / l_w)

    qt_aug = jnp.concatenate([qt, jnp.concatenate([bias_t.astype(BF16)] * A_HPG, axis=1)], axis=0)
    acc_ref[...] = jnp.zeros_like(acc_ref)
    m_ref[...] = jnp.full(m_ref.shape, 0.5 * NEG_BIG, F32)
    l_ref[...] = jnp.zeros_like(l_ref)

    def sel_body(base, causal):
        n = len(causal)
        m, l = m_ref[...], l_ref[...]
        groups = [list(range(g0, min(g0 + SEL_GROUP, n))) for g0 in range(0, n, SEL_GROUP)]

        def scores(ks):
            out = {}
            for k in ks:
                sk = _dot(kaug_ref[base + k], qt_aug)
                if causal[k]:
                    kpos = (base + k) * KEY_TILE + lax.broadcasted_iota(jnp.int32, (KEY_TILE, 1), 0)
                    sk = jnp.where(kpos <= tq, sk, NEG_BIG)
                out[k] = sk
            return out

        s = scores(groups[0])
        for gi, ks in enumerate(groups):
            if gi + 1 < len(groups):
                s.update(scores(groups[gi + 1]))
            m_new = m
            for k in ks:
                m_new = jnp.maximum(m_new, jnp.max(s[k], axis=0, keepdims=True))
            alpha = jnp.exp2(m - m_new)
            l = alpha * l
            pv = None
            for k in ks:
                p = jnp.exp2(s.pop(k) - m_new)
                l = l + jnp.sum(p, axis=0, keepdims=True)
                d = _dot(vt_ref[base + k], p.astype(BF16))
                pv = d if pv is None else pv + d
            acc_ref[...] = alpha * acc_ref[...] + pv
            m = m_new
        m_ref[...] = m
        l_ref[...] = l

    n_bulk = i // SEL_BODY
    rem = i % SEL_BODY

    def bulk(b, c):
        sel_body(b * SEL_BODY, (False,) * SEL_BODY)
        return c
    lax.fori_loop(0, n_bulk, bulk, 0)

    base = n_bulk * SEL_BODY
    part = SEL_BODY // 2
    while part >= 2:
        pl.when(rem & part != 0)(functools.partial(sel_body, base, (False,) * part))
        base = base + (rem & part)
        part //= 2

    @pl.when(rem % 2 == 1)
    def _():
        sel_body(i - 1, (False, True))

    @pl.when(rem % 2 == 0)
    def _():
        sel_body(i, (True,))

    o_st = acc_ref[...] * (1.0 / l_ref[...])

    gates_t = gate_ref[...].astype(F32).T
    per_group = 3 * A_HPG

    def gate(h, b):
        r = 3 * h + b
        return jnp.where(grp == 0, gates_t[r:r + 1], gates_t[per_group + r:per_group + r + 1])

    for h in range(A_HPG):
        c = slice(h * NSA_QB, (h + 1) * NSA_QB)
        o = gate(h, 0) * o_ct[:, c] + gate(h, 1) * o_st[:, c] + gate(h, 2) * o_wt[:, c]
        o_ref[:, h * HEAD_DIM:(h + 1) * HEAD_DIM] = o.T.astype(o_ref.dtype)


def _overlap_matrix(seq):
    n_chunk = seq // CMP_STRIDE
    cs = np.arange(n_chunk)[:, None] * CMP_STRIDE
    ss = np.arange(SEL_SLOTS)[None, :] * SEL_BLOCK
    ov = np.maximum(np.minimum(cs + CMP_BLOCK, ss + SEL_BLOCK) - np.maximum(cs, ss), 0) / CMP_BLOCK
    n_cmp = (seq - CMP_BLOCK) // CMP_STRIDE + 1
    ov[n_cmp:] = 0.0
    ov[:, seq // SEL_BLOCK:] = 0.0
    return ov.astype(np.float32)


def _nsa(a, cmp_kv):
    S = a.shape[0]
    G = A_KV_GROUPS
    n_tiles = S // KEY_TILE
    assert S // SEL_BLOCK <= SEL_SLOTS and S % KEY_TILE == 0 and S >= WINDOW + NSA_QB
    assert NSA_QB == KEY_TILE and G == 2
    n_chunk = S // CMP_STRIDE
    ovt = jnp.asarray(_overlap_matrix(S).T, BF16)
    col = lambda b: pl.BlockSpec((S, HEAD_DIM), lambda g, i, b=b: (0, b + g))
    return pl.pallas_call(
        functools.partial(_nsa_kernel, seq=S),
        grid=(G, S // NSA_QB),
        in_specs=[
            pl.BlockSpec((NSA_QB, A_HPG * HEAD_DIM), lambda g, i: (i, g)),
            col(BLK_KS), col(BLK_KW), col(BLK_VS), col(BLK_VW),
            pl.BlockSpec((1, 1, n_chunk, HEAD_DIM), lambda g, i: (0, g, 0, 0)),
            pl.BlockSpec((1, 1, n_chunk, HEAD_DIM), lambda g, i: (1, g, 0, 0)),
            pl.BlockSpec((SEL_SLOTS, n_chunk), lambda g, i: (0, 0)),
            pl.BlockSpec((NSA_QB, LANES), lambda g, i: (i, BLK_GATE)),
        ],
        out_specs=pl.BlockSpec((NSA_QB, A_HPG * HEAD_DIM), lambda g, i: (i, g)),
        out_shape=jax.ShapeDtypeStruct((S, A_WIDTH), BF16),
        scratch_shapes=[pltpu.VMEM((n_tiles, KEY_TILE, HEAD_DIM + SEL_SLOTS), BF16),
                        pltpu.VMEM((n_tiles, HEAD_DIM, KEY_TILE), BF16),
                        pltpu.VMEM((n_tiles, HEAD_DIM, KEY_TILE), BF16),
                        pltpu.VMEM((HEAD_DIM, n_chunk), BF16),
                        pltpu.VMEM((HEAD_DIM, A_HPG * NSA_QB), F32),
                        pltpu.VMEM((1, A_HPG * NSA_QB), F32),
                        pltpu.VMEM((1, A_HPG * NSA_QB), F32)],
        compiler_params=_cparams(("arbitrary", "arbitrary")),
        name="nsa",
    )(a, a, a, a, a, cmp_kv, cmp_kv, ovt, a)


_ROW_MASKS = 18
_SCORE_MASKS = 10
HG_HEADS_PER_STEP = 4


def _hgrn_tables():
    t = np.arange(HG_CHUNK)
    p4, i4, i16 = t % 4, (t // 4) % 4, t // 16
    conds = [p4 >= 1, p4 >= 2, p4 >= 3, p4 <= 2, p4 <= 1, p4 == 0,
             i4 >= 1, i4 >= 2, i4 >= 3, i4 <= 2, i4 <= 1, i4 == 0,
             i16 >= 1, i16 >= 2, i16 >= 3, i16 <= 2, i16 <= 1, i16 == 0]
    rm = np.stack([np.broadcast_to(c[:, None], (HG_CHUNK, HG_DIM)) for c in conds]).astype(np.float32)
    tt, ss = t[:, None], t[None, :]
    lag16 = tt // 16 - ss // 16
    lag4 = tt // 4 - ss // 4
    same16, same4 = lag16 == 0, lag4 == 0
    sm = [lag16 == 1, lag16 == 2, lag16 == 3,
          same16 & (lag4 == 1), same16 & (lag4 == 2), same16 & (lag4 == 3),
          same4 & (tt - ss == 0), same4 & (tt - ss == 1), same4 & (tt - ss == 2), same4 & (tt - ss == 3)]
    return rm, np.stack(sm).astype(np.float32)


def _hgrn_kernel(q_ref, f_ref, i_ref, og_ref, loglb_ref, log1m_ref, gn_ref, rm_ref, sm_ref,
                 o_ref, st_ref, g_scr, k_scr, o_scr, *, tb, nh):
    C = HG_CHUNK
    heads = [slice(h * HG_DIM, (h + 1) * HG_DIM) for h in range(nh)]

    @pl.when(pl.program_id(1) == 0)
    def _():
        st_ref[...] = jnp.zeros_like(st_ref)

    x = f_ref[...]
    log_sig = jnp.minimum(x, 0.0) - jnp.log(1.0 + jnp.exp(-jnp.abs(x)))
    a = loglb_ref[...]
    c = log1m_ref[...] + log_sig
    log_f = jnp.maximum(a, c) + jnp.log(1.0 + jnp.exp(-jnp.abs(a - c)))
    g_scr[...] = log_f
    k_scr[...] = 1.0 - jnp.exp(log_f)

    def rmask(n):
        return jnp.concatenate([rm_ref[n]] * nh, axis=1)

    def roll(v, s):
        return pltpu.roll(v, s % C, axis=0)

    def b16(z):
        return z.astype(BF16)

    def chunk(ci, carry):
        r0 = pl.multiple_of(ci * C, C)
        g = g_scr[pl.ds(r0, C), :]
        q = q_ref[pl.ds(r0, C), :]
        k = k_scr[pl.ds(r0, C), :]
        v = i_ref[pl.ds(r0, C), :].astype(BF16)
        pre4 = g + rmask(0) * roll(g, 1) + rmask(1) * roll(g, 2) + rmask(2) * roll(g, 3)
        suf4 = rmask(3) * roll(g, -1) + rmask(4) * roll(g, -2) + rmask(5) * roll(g, -3)
        tot4 = pre4 + suf4
        a1, a2, a3 = roll(tot4, 4), roll(tot4, 8), roll(tot4, 12)
        pre16 = pre4 + rmask(6) * a1 + rmask(7) * a2 + rmask(8) * a3
        suf16 = suf4 + rmask(9) * roll(tot4, -4) + rmask(10) * roll(tot4, -8) + rmask(11) * roll(tot4, -12)
        tot16 = pre16 + suf16
        d1, d2, d3 = roll(tot16, 16), roll(tot16, 32), roll(tot16, 48)
        pre64 = pre16 + rmask(12) * d1 + rmask(13) * d2 + rmask(14) * d3
        suf64 = suf16 + rmask(15) * roll(tot16, -16) + rmask(16) * roll(tot16, -32) + rmask(17) * roll(tot16, -48)
        tot64 = pre64 + suf64
        g1, g2 = roll(g, 1), roll(g, 2)

        q_a = q * jnp.exp(pre16)
        lhs_a = [b16(q_a), b16(q_a * jnp.exp(d1)), b16(q_a * jnp.exp(d1 + d2))]
        k_a = b16(k * jnp.exp(suf16))
        q_b = q * jnp.exp(pre4)
        lhs_b = [b16(q_b), b16(q_b * jnp.exp(a1)), b16(q_b * jnp.exp(a1 + a2))]
        k_b = b16(k * jnp.exp(suf4))
        q_c = q * jnp.exp(g)
        lhs_c = [b16(q), b16(q_c), b16(q_c * jnp.exp(g1)), b16(q_c * jnp.exp(g1 + g2))]
        k_c = b16(k)
        q_s = b16(q * jnp.exp(pre64))
        k_s = b16(k * jnp.exp(suf64))
        decay = jnp.exp(tot64[0:1, :])

        def level(lhs, rhs):
            return [_dot_nt(jnp.concatenate([z[:, hh] for z in lhs], axis=0), rhs[:, hh]) for hh in heads]

        s_a, s_b, s_c = level(lhs_a, k_a), level(lhs_b, k_b), level(lhs_c, k_c)
        o_inter = [_dot_nt(q_s[:, hh], b16(st_ref[h])) for h, hh in enumerate(heads)]
        kv = [_dot_tn(v[:, hh], k_s[:, hh]) for hh in heads]
        for h, hh in enumerate(heads):
            sa, sb, sc = s_a[h], s_b[h], s_c[h]
            scores = (sm_ref[0] * sa[0:C] + sm_ref[1] * sa[C:2 * C] + sm_ref[2] * sa[2 * C:3 * C]
                      + sm_ref[3] * sb[0:C] + sm_ref[4] * sb[C:2 * C] + sm_ref[5] * sb[2 * C:3 * C]
                      + sm_ref[6] * sc[0:C] + sm_ref[7] * sc[C:2 * C] + sm_ref[8] * sc[2 * C:3 * C]
                      + sm_ref[9] * sc[3 * C:4 * C])
            o_scr[pl.ds(r0, C), hh] = _dot(b16(scores), v[:, hh]) + o_inter[h]
            st_ref[h] = st_ref[h] * decay[:, hh] + kv[h]
        return carry

    lax.fori_loop(0, tb // C, chunk, 0)

    for hh in heads:
        o = o_scr[:, hh]
        ms = jnp.mean(o * o, axis=-1, keepdims=True)
        og = og_ref[:, hh]
        y = o * lax.rsqrt(ms + NORM_EPS) * gn_ref[:, hh] * (og * jax.nn.sigmoid(og))
        o_ref[:, hh] = y.astype(o_ref.dtype)


def _hgrn(hz, loglb, log1mlb, gnorm):
    S = hz.shape[0]
    tb = 512 if S % 512 == 0 else S
    nh = HG_HEADS_PER_STEP
    ng = HG_HEADS // nh
    w = nh * HG_DIM
    rm, sm = _hgrn_tables()
    blk = lambda part: pl.BlockSpec((tb, w), lambda h, c, part=part: (c, part * ng + h))
    row = pl.BlockSpec((1, w), lambda h, c: (0, h))
    return pl.pallas_call(
        functools.partial(_hgrn_kernel, tb=tb, nh=nh),
        grid=(ng, S // tb),
        in_specs=[blk(0), blk(1), blk(2), blk(3), row, row, row,
                  pl.BlockSpec((_ROW_MASKS, HG_CHUNK, HG_DIM), lambda h, c: (0, 0, 0)),
                  pl.BlockSpec((_SCORE_MASKS, HG_CHUNK, HG_CHUNK), lambda h, c: (0, 0, 0))],
        out_specs=pl.BlockSpec((tb, w), lambda h, c: (c, h)),
        out_shape=jax.ShapeDtypeStruct((S, HG_WIDTH), BF16),
        scratch_shapes=[pltpu.VMEM((nh, HG_DIM, HG_DIM), F32), pltpu.VMEM((tb, w), F32),
                        pltpu.VMEM((tb, w), F32), pltpu.VMEM((tb, w), F32)],
        compiler_params=_cparams(("arbitrary", "arbitrary")),
        name="hgrn",
    )(hz, hz, hz, hz, loglb, log1mlb, gnorm, jnp.asarray(rm), jnp.asarray(sm))


def _merge_kernel(at_ref, hg_ref, ga_ref, gh_ref, wa_ref, wh_ref, wo_ref, x_ref, gt_ref, gp_ref, o_ref):
    mixed = (ga_ref[...].astype(F32) * _dot(at_ref[...], wa_ref[0])
             + gh_ref[...].astype(F32) * _dot(hg_ref[...], wh_ref[0]))
    y = _dot(mixed.astype(BF16), wo_ref[0])
    ms = jnp.mean(y * y, axis=-1, keepdims=True)
    yn = y * lax.rsqrt(ms + NORM_EPS) * gp_ref[...]
    o_ref[...] = x_ref[...] + gt_ref[...] * yn


def _merge(attn, hg, mg, wa, wh, wo, layer, x, gt, gpost):
    S, D = x.shape
    tm = 256
    const = lambda shape: pl.BlockSpec((1,) + shape, lambda m: (layer, 0, 0), pipeline_mode=pl.Buffered(1))
    row = pl.BlockSpec((1, D), lambda m: (0, 0))
    return pl.pallas_call(
        _merge_kernel,
        grid=(S // tm,),
        in_specs=[
            pl.BlockSpec((tm, A_WIDTH), lambda m: (m, 0)),
            pl.BlockSpec((tm, HG_WIDTH), lambda m: (m, 0)),
            pl.BlockSpec((tm, D), lambda m: (m, 0)),
            pl.BlockSpec((tm, D), lambda m: (m, 1)),
            const((A_WIDTH, D)), const((HG_WIDTH, D)), const((D, D)),
            pl.BlockSpec((tm, D), lambda m: (m, 0)),
            row, row,
        ],
        out_specs=pl.BlockSpec((tm, D), lambda m: (m, 0)),
        out_shape=jax.ShapeDtypeStruct((S, D), F32),
        compiler_params=_cparams(("arbitrary",)),
        name="merge",
    )(attn, hg, mg, mg, wa, wh, wo, x, gt, gpost.reshape(1, D))


_HALO = 16
_FF_SUB = 256


def _ffn_kernel(x_ref, xh_ref, g_ref, sc_ref, sh_ref, wg_ref, wv_ref, cwg_ref, cwv_ref, cbg_ref, cbv_ref,
                wd_ref, gt_ref, gp_ref, o_ref, h_scr, ug_scr, uv_scr, *, tm):
    m = pl.program_id(0)
    f = pl.program_id(1)

    def norm_mod(x):
        ms = jnp.mean(x * x, axis=-1, keepdims=True)
        y = x * lax.rsqrt(ms + NORM_EPS) * g_ref[...]
        return (y * (1.0 + sc_ref[...]) + sh_ref[...]).astype(BF16)

    @pl.when(f == 0)
    def _():
        h_scr[0:_HALO, :] = jnp.where(m > 0, norm_mod(xh_ref[...]), jnp.zeros((), BF16))
        h_scr[_HALO:_HALO + tm, :] = norm_mod(x_ref[...])
        o_ref[...] = jnp.zeros_like(o_ref)

    tf = wg_ref.shape[2]
    subs = [slice(s * _FF_SUB, (s + 1) * _FF_SUB) for s in range(tf // _FF_SUB)]
    h = h_scr[...]
    for cs in subs:
        ug_scr[:, cs] = _dot(h, wg_ref[0, :, cs])
        uv_scr[:, cs] = _dot(h, wv_ref[0, :, cs])

    def conv(u_scr, cw_ref, cb_ref, cs):
        out = cb_ref[:, cs] + cw_ref[0:1, cs] * u_scr[pl.ds(_HALO - 2, tm), cs]
        out = out + cw_ref[1:2, cs] * u_scr[pl.ds(_HALO - 1, tm), cs]
        return out + cw_ref[2:3, cs] * u_scr[pl.ds(_HALO, tm), cs]

    acc = o_ref[...]
    for cs in subs:
        cg = conv(ug_scr, cwg_ref, cbg_ref, cs)
        cv = conv(uv_scr, cwv_ref, cbv_ref, cs)
        act = (cg * jax.nn.sigmoid(cg) * cv).astype(BF16)
        acc = acc + _dot(act, wd_ref[0, cs, :])
    o_ref[...] = acc

    @pl.when(f == pl.num_programs(1) - 1)
    def _():
        y = o_ref[...]
        ms = jnp.mean(y * y, axis=-1, keepdims=True)
        yn = y * lax.rsqrt(ms + NORM_EPS) * gp_ref[...]
        o_ref[...] = x_ref[...] + gt_ref[...] * yn


def _ffn(x, gpre, sc, sh, wup, convw, convb, wdown, layer, gt, gpost):
    S, D = x.shape
    F = wdown.shape[1]
    tm = 512 if S % 512 == 0 else S
    tf = 512
    nf = F // tf
    hb = tm // _HALO
    row = pl.BlockSpec((1, D), lambda m, f: (0, 0))
    return pl.pallas_call(
        functools.partial(_ffn_kernel, tm=tm),
        grid=(S // tm, nf),
        in_specs=[
            pl.BlockSpec((tm, D), lambda m, f: (m, 0)),
            pl.BlockSpec((_HALO, D), lambda m, f: (jnp.maximum(m * hb - 1, 0), 0)),
            row, row, row,
            pl.BlockSpec((1, D, tf), lambda m, f: (layer, 0, f)),
            pl.BlockSpec((1, D, tf), lambda m, f: (layer, 0, nf + f)),
            pl.BlockSpec((CONV_WIDTH, tf), lambda m, f: (0, f)),
            pl.BlockSpec((CONV_WIDTH, tf), lambda m, f: (0, nf + f)),
            pl.BlockSpec((1, tf), lambda m, f: (0, f)),
            pl.BlockSpec((1, tf), lambda m, f: (0, nf + f)),
            pl.BlockSpec((1, tf, D), lambda m, f: (layer, f, 0)),
            row, row,
        ],
        out_specs=pl.BlockSpec((tm, D), lambda m, f: (m, 0)),
        out_shape=jax.ShapeDtypeStruct((S, D), F32),
        scratch_shapes=[pltpu.VMEM((tm + _HALO, D), BF16),
                        pltpu.VMEM((tm + _HALO, tf), F32), pltpu.VMEM((tm + _HALO, tf), F32)],
        compiler_params=_cparams(("arbitrary", "arbitrary")),
        name="ffn",
    )(x, x, gpre.reshape(1, D), sc, sh, wup, wup, convw, convw, convb.reshape(1, -1), convb.reshape(1, -1),
      wdown, gt, gpost.reshape(1, D))


def _rope_tables(positions):
    inv_freq = 1.0 / (ROPE_THETA ** (jnp.arange(0, HEAD_DIM, 2, dtype=F32) / HEAD_DIM))
    ang = positions.astype(F32)[:, None] * inv_freq
    cos, sin = jnp.cos(ang), jnp.sin(ang)
    return jnp.concatenate([cos, cos], axis=-1), jnp.concatenate([-sin, sin], axis=-1)


def kernel(x, c, positions, w_ada, b_ada, g_pre_mix, w_in, pe_kc, w_kc, pe_vc, w_vc, lb_logits, g_hg_norm,
           w_br_attn, w_br_hgrn, w_out, g_post_mix, g_pre_ffn, w_up, conv_w, conv_b, w_down, g_post_ffn):
    B, S, D = x.shape
    assert B == 1, "kernel is written for one sequence"
    L = w_ada.shape[0]
    xs = x[0]
    cosf, sinf = _rope_tables(positions[0])
    scale = HEAD_DIM ** -0.5 * np.log2(np.e)
    tables = (cosf * scale, sinf * scale, cosf, sinf)
    lb_cum = jnp.cumsum(jax.nn.softmax(lb_logits.astype(F32), axis=0), axis=0)
    lower = lb_cum - lb_cum[0:1]
    log_lb = jnp.log(lower)
    log_1m = jnp.log1p(-lower)
    ada = _ada_all(c, w_ada, b_ada)

    w_in_t = jnp.swapaxes(w_in, 1, 2)
    tail_tn = 1024
    hg_roles = (("id",) * (tail_tn // LANES),) * (4 * HG_WIDTH // tail_tn)
    mg_roles = (("sg",) * (tail_tn // LANES),) * (2 * D // tail_tn)

    wa_b, wh_b, wo_b = w_br_attn.astype(BF16), w_br_hgrn.astype(BF16), w_out.astype(BF16)
    wup_b, wdown_b = w_up.astype(BF16), w_down.astype(BF16)

    for l in range(L):
        sh1, sc1, gt1, sh2, sc2, gt2 = [ada[l, :, j * D:(j + 1) * D] for j in range(N_ADA)]

        h = _prenorm(xs, g_pre_mix[l], sc1, sh1)
        a = _proj(h, w_in_t, l, 0, A_ROLES, BF16, tn=A_TILE, tables=tables, name="proj_attn")
        hz = _proj(h, w_in_t, l, A_COLS, hg_roles, F32, tn=tail_tn, name="proj_hgrn")
        mg = _proj(h, w_in_t, l, A_COLS + 4 * HG_WIDTH, mg_roles, BF16, tn=tail_tn, name="proj_merge_gate")

        cmp_kv = _compress(a, jnp.stack([pe_kc[l], pe_vc[l]]), jnp.stack([w_kc[l], w_vc[l]]))
        attn = _nsa(a, cmp_kv)
        hg = _hgrn(hz, log_lb[l:l + 1], log_1m[l:l + 1], g_hg_norm[l:l + 1])
        xs = _merge(attn, hg, mg, wa_b, wh_b, wo_b, l, xs, gt1, g_post_mix[l])

        xs = _ffn(xs, g_pre_ffn[l], sc2, sh2, wup_b, conv_w[l], conv_b[l], wdown_b, l, gt2, g_post_ffn[l])
    return xs[None]
```

```python
import functools

import jax
import jax.numpy as jnp
import numpy as np
from jax import lax
from jax.experimental import pallas as pl
from jax.experimental.pallas import tpu as pltpu

F32 = jnp.float32
BF16 = jnp.bfloat16

A_HEADS = 8
A_KV_GROUPS = 2
A_HPG = A_HEADS // A_KV_GROUPS
HEAD_DIM = 128
A_WIDTH = A_HEADS * HEAD_DIM
A_KV_WIDTH = A_KV_GROUPS * HEAD_DIM
CMP_BLOCK = 32
CMP_STRIDE = 16
SEL_BLOCK = 64
SEL_TOPK = 16
WINDOW = 512
ROPE_THETA = 10000.0
HG_HEADS = 8
HG_DIM = 128
HG_WIDTH = HG_HEADS * HG_DIM
HG_CHUNK = 64
CONV_WIDTH = 3
NORM_EPS = 1e-6
N_ADA = 6
A_COLS = A_WIDTH + 6 * A_KV_WIDTH + 3 * A_HEADS

LANES = 128
SEL_SLOTS = LANES
KEY_TILE = 256
NEG_BIG = -1e30
VMEM_LIMIT = 56 * 1024 * 1024


def _cparams(sem):
    return pltpu.CompilerParams(dimension_semantics=sem, vmem_limit_bytes=VMEM_LIMIT)


def _split3(a):
    hi = a.astype(BF16)
    r1 = a - hi.astype(F32)
    mid = r1.astype(BF16)
    lo = (r1 - mid.astype(F32)).astype(BF16)
    return hi, mid, lo


def _dot(a, b):
    return jnp.dot(a, b, preferred_element_type=F32)


def _dot_nt(a, b):
    return lax.dot_general(a, b, (((1,), (1,)), ((), ())), preferred_element_type=F32)


def _dot_tn(a, b):
    return lax.dot_general(a, b, (((0,), (0,)), ((), ())), preferred_element_type=F32)


def _ada_kernel(c_ref, w_ref, b_ref, o_ref):
    c = c_ref[...]
    ca = c * jax.nn.sigmoid(c)
    o_ref[0] = jnp.sum(ca * w_ref[0], axis=0, keepdims=True) + b_ref[0]


def _ada_all(c, w_ada, b_ada):
    L, D, N = w_ada.shape
    tn = 1024
    return pl.pallas_call(
        _ada_kernel,
        grid=(L, N // tn),
        in_specs=[
            pl.BlockSpec((D, 1), lambda l, n: (0, 0)),
            pl.BlockSpec((1, D, tn), lambda l, n: (l, 0, n)),
            pl.BlockSpec((1, 1, tn), lambda l, n: (l, 0, n)),
        ],
        out_specs=pl.BlockSpec((1, 1, tn), lambda l, n: (l, 0, n)),
        out_shape=jax.ShapeDtypeStruct((L, 1, N), F32),
        compiler_params=_cparams(("arbitrary", "arbitrary")),
        name="ada",
    )(c.reshape(D, 1), w_ada, b_ada.reshape(L, 1, N))


def _prenorm_kernel(x_ref, g_ref, sc_ref, sh_ref, o_ref):
    x = x_ref[...]
    ms = jnp.mean(x * x, axis=-1, keepdims=True)
    y = x * lax.rsqrt(ms + NORM_EPS) * g_ref[...]
    o_ref[...] = (y * (1.0 + sc_ref[...]) + sh_ref[...]).astype(o_ref.dtype)


def _prenorm(x, g, sc, sh):
    S, D = x.shape
    tm = 512
    row = pl.BlockSpec((1, D), lambda m: (0, 0))
    return pl.pallas_call(
        _prenorm_kernel,
        grid=(S // tm,),
        in_specs=[pl.BlockSpec((tm, D), lambda m: (m, 0)), row, row, row],
        out_specs=pl.BlockSpec((tm, D), lambda m: (m, 0)),
        out_shape=jax.ShapeDtypeStruct((S, D), BF16),
        compiler_params=_cparams(("arbitrary",)),
        name="prenorm",
    )(x, g.reshape(1, D), sc, sh)


def _rope(t, cosf, sinf):
    return t * cosf + pltpu.roll(t, HEAD_DIM // 2, axis=1) * sinf


_PROJ_PART = 256


def _proj_kernel(h_ref, w_ref, *rest, roles, tn):
    o_ref, w_scr = rest[-2:]
    tabs = rest[:-2]
    n = pl.program_id(0)

    @pl.when(pl.program_id(1) == 0)
    def _():
        w_scr[...] = w_ref[0].astype(BF16)

    def emit(slabs):
        h = h_ref[...]
        per = _PROJ_PART // LANES
        parts = [_dot_nt(h, w_scr[c0:c0 + _PROJ_PART, :]) for c0 in range(0, tn, _PROJ_PART)]
        for pi, acc in enumerate(parts):
            for s in range(per):
                kind = slabs[pi * per + s]
                t = acc[:, s * LANES:(s + 1) * LANES]
                if kind == "sg":
                    t = jax.nn.sigmoid(t)
                elif kind != "id":
                    cos_ref, sin_ref = tabs[0:2] if kind == "rq" else tabs[-2:]
                    t = _rope(t, cos_ref[...], sin_ref[...])
                c = (pi * per + s) * LANES
                o_ref[:, c:c + LANES] = t.astype(o_ref.dtype)

    kinds = sorted(set(roles))
    if len(kinds) == 1:
        emit(kinds[0])
        return
    for kind in kinds:
        hit = functools.reduce(jnp.logical_or, [n == t for t, r in enumerate(roles) if r == kind])
        pl.when(hit)(functools.partial(emit, kind))


def _proj(h, w_t, layer, row0, roles, out_dtype, *, tn, tables=(), name="proj"):
    S, D = h.shape
    tm = 1024 if S % 1024 == 0 else S
    nt = len(roles)
    tab = pl.BlockSpec((tm, HEAD_DIM), lambda n, m: (m, 0))
    return pl.pallas_call(
        functools.partial(_proj_kernel, roles=tuple(roles), tn=tn),
        grid=(nt, S // tm),
        in_specs=[pl.BlockSpec((tm, D), lambda n, m: (m, 0)),
                  pl.BlockSpec((pl.Element(1), pl.Element(tn), pl.Element(D)),
                               lambda n, m: (layer, pl.multiple_of(row0 + n * tn, 8), 0))]
                 + [tab] * len(tables),
        out_specs=pl.BlockSpec((tm, tn), lambda n, m: (m, n)),
        out_shape=jax.ShapeDtypeStruct((S, nt * tn), out_dtype),
        scratch_shapes=[pltpu.VMEM((tn, D), BF16)],
        compiler_params=_cparams(("arbitrary", "arbitrary")),
        name=name,
    )(h, w_t, *tables)


A_TILE = 512
_KV_TILE_ROLES = ("rk",) * A_KV_GROUPS + ("id",) * A_KV_GROUPS
A_ROLES = ((("rq",) * (A_TILE // LANES),) * (A_WIDTH // A_TILE) + (_KV_TILE_ROLES,) * 3
           + (("sg",) * (A_TILE // LANES),))
_KV_BLOCK0 = A_WIDTH // LANES
BLK_KC, BLK_VC, BLK_KS, BLK_VS, BLK_KW, BLK_VW = [_KV_BLOCK0 + j * A_KV_GROUPS for j in range(6)]
BLK_GATE = _KV_BLOCK0 + 6 * A_KV_GROUPS


def _compress_kernel(t_ref, pe_ref, w_ref, o_ref, t32_ref, *, n_chunk):
    half = CMP_BLOCK // 2
    t32_ref[...] = t_ref[...].astype(F32)
    top, bot = [], []
    for j in range(half):
        a = t32_ref[pl.ds(j, n_chunk, stride=CMP_STRIDE), :]
        top.append(a + pe_ref[0, j:j + 1, :])
        bot.append(a + pe_ref[0, half + j:half + j + 1, :])
    a_top = jnp.concatenate(top, axis=1)
    a_bot = jnp.concatenate(bot, axis=1)
    kw = half * HEAD_DIM
    w = w_ref[0]
    w_top = w[:kw].astype(BF16)
    w_bot = w[kw:].astype(BF16)
    y_top = jnp.zeros((n_chunk, HEAD_DIM), F32)
    y_bot = jnp.zeros((n_chunk, HEAD_DIM), F32)
    for piece in _split3(a_top):
        y_top = y_top + _dot(piece, w_top)
    for piece in _split3(a_bot):
        y_bot = y_bot + _dot(piece, w_bot)
    o_ref[0, 0] = y_top + pltpu.roll(y_bot, n_chunk - 1, axis=0)


def _compress(a, pe2, w2):
    S = a.shape[0]
    n_chunk = S // CMP_STRIDE
    G = A_KV_GROUPS
    return pl.pallas_call(
        functools.partial(_compress_kernel, n_chunk=n_chunk),
        grid=(2, G),
        in_specs=[
            pl.BlockSpec((S, HEAD_DIM), lambda kv, g: (0, BLK_KC + kv * G + g)),
            pl.BlockSpec((1, CMP_BLOCK, HEAD_DIM), lambda kv, g: (kv, 0, 0)),
            pl.BlockSpec((1, CMP_BLOCK * HEAD_DIM, HEAD_DIM), lambda kv, g: (kv, 0, 0)),
        ],
        out_specs=pl.BlockSpec((1, 1, n_chunk, HEAD_DIM), lambda kv, g: (kv, g, 0, 0)),
        out_shape=jax.ShapeDtypeStruct((2, G, n_chunk, HEAD_DIM), F32),
        scratch_shapes=[pltpu.VMEM((S, HEAD_DIM), F32)],
        compiler_params=_cparams(("arbitrary", "arbitrary")),
        name="compress",
    )(a, pe2, w2)


NSA_QB = 256
SEL_BODY = 8
SEL_GROUP = 4


def _nsa_kernel(q_ref, ks_ref, kw_ref, vs_ref, vw_ref, kc_ref, vc_ref, ovt_ref, gate_ref,
                o_ref, kaug_ref, vt_ref, vwt_ref, vct_ref, acc_ref, m_ref, l_ref, *, seq):
    grp = pl.program_id(0)
    i = pl.program_id(1)
    t0 = i * NSA_QB
    n_cmp = seq // CMP_STRIDE
    cols = A_HPG * NSA_QB

    @pl.when(i == 0)
    def _():
        def build(r, c):
            r0 = pl.multiple_of(r * KEY_TILE, KEY_TILE)
            kaug_ref[r, :, 0:HEAD_DIM] = ks_ref[pl.ds(r0, KEY_TILE), :]
            blk = (r0 + lax.broadcasted_iota(jnp.int32, (KEY_TILE, SEL_SLOTS), 0)) // SEL_BLOCK
            slot = lax.broadcasted_iota(jnp.int32, (KEY_TILE, SEL_SLOTS), 1)
            kaug_ref[r, :, HEAD_DIM:HEAD_DIM + SEL_SLOTS] = jnp.where(blk == slot, 1.0, 0.0).astype(BF16)
            vt_ref[r] = vs_ref[pl.ds(r0, KEY_TILE), :].astype(F32).T.astype(BF16)
            vwt_ref[r] = vw_ref[pl.ds(r0, KEY_TILE), :].astype(F32).T.astype(BF16)
            return c
        lax.fori_loop(0, seq // KEY_TILE, build, 0)
        vct_ref[...] = vc_ref[0, 0].T.astype(BF16)

    qt = jnp.concatenate(
        [q_ref[:, h * HEAD_DIM:(h + 1) * HEAD_DIM].astype(F32).T.astype(BF16) for h in range(A_HPG)],
        axis=1)
    tq = t0 + lax.broadcasted_iota(jnp.int32, (1, cols), 1) % NSA_QB

    def softmax0(s):
        m = jnp.max(s, axis=0, keepdims=True)
        p = jnp.exp2(s - m)
        return p, jnp.sum(p, axis=0, keepdims=True)

    s_c = _dot(kc_ref[0, 0].astype(BF16), qt)
    nrow = lax.broadcasted_iota(jnp.int32, (n_cmp, 1), 0)
    vis = (nrow * CMP_STRIDE + (CMP_BLOCK - 1)) <= tq
    p_c, den = softmax0(jnp.where(vis, s_c, NEG_BIG))
    p_c = p_c * jnp.where(tq >= CMP_BLOCK - 1, 1.0 / den, 0.0)
    o_ct = _dot(vct_ref[...], p_c.astype(BF16))

    psum = p_c[:, 0:NSA_QB]
    for h in range(1, A_HPG):
        psum = psum + p_c[:, h * NSA_QB:(h + 1) * NSA_QB]
    ovt = ovt_ref[...]
    imp_t = jnp.zeros((SEL_SLOTS, NSA_QB), F32)
    for piece in _split3(psum):
        imp_t = imp_t + _dot(ovt, piece)
    slot_i = lax.broadcasted_iota(jnp.int32, (SEL_SLOTS, NSA_QB), 0)
    tq_l = t0 + lax.broadcasted_iota(jnp.int32, (SEL_SLOTS, NSA_QB), 1)
    cur = tq_l // SEL_BLOCK
    forced = jnp.where(slot_i == 0, 1, 0) + jnp.where(slot_i == cur, 1, 0) + jnp.where(slot_i == cur - 1, 1, 0)
    valid = slot_i * SEL_BLOCK <= tq_l
    key = jnp.where(forced > 0, -NEG_BIG, jnp.where(valid, imp_t, NEG_BIG))
    slot_f = slot_i.astype(F32)
    bias_t = jnp.full((SEL_SLOTS, NSA_QB), NEG_BIG, F32)
    for _ in range(SEL_TOPK):
        mx = jnp.max(key, axis=0, keepdims=True)
        first = jnp.min(jnp.where(key == mx, slot_f, 1e9), axis=0, keepdims=True)
        pick = slot_f == first
        bias_t = jnp.where(pick, 0.0, bias_t)
        key = jnp.where(pick, -3e38, key)

    wlen = WINDOW + NSA_QB
    w0 = pl.multiple_of(jnp.maximum(t0 - WINDOW, 0), KEY_TILE)
    wt = jnp.maximum(i - WINDOW // KEY_TILE, 0)
    s_w = _dot(kw_ref[pl.ds(w0, wlen), :], qt)
    dist = tq - (w0 + lax.broadcasted_iota(jnp.int32, (wlen, 1), 0))
    in_window = dist.astype(jnp.uint32) < WINDOW
    p_w, l_w = softmax0(jnp.where(in_window, s_w, NEG_BIG))
    p_w = p_w.astype(BF16)
    o_wt = _dot(vwt_ref[wt], p_w[0:KEY_TILE])
    for k in range(1, wlen // KEY_TILE):
        o_wt = o_wt + _dot(vwt_ref[wt + k], p_w[k * KEY_TILE:(k + 1) * KEY_TILE])
    o_wt = o_wt * (1.0 / l_w)

    qt_aug = jnp.concatenate([qt, jnp.concatenate([bias_t.astype(BF16)] * A_HPG, axis=1)], axis=0)
    acc_ref[...] = jnp.zeros_like(acc_ref)
    m_ref[...] = jnp.full(m_ref.shape, 0.5 * NEG_BIG, F32)
    l_ref[...] = jnp.zeros_like(l_ref)

    def sel_body(base, causal):
        n = len(causal)
        m, l = m_ref[...], l_ref[...]
        groups = [list(range(g0, min(g0 + SEL_GROUP, n))) for g0 in range(0, n, SEL_GROUP)]

        def scores(ks):
            out = {}
            for k in ks:
                sk = _dot(kaug_ref[base + k], qt_aug)
                if causal[k]:
                    kpos = (base + k) * KEY_TILE + lax.broadcasted_iota(jnp.int32, (KEY_TILE, 1), 0)
                    sk = jnp.where(kpos <= tq, sk, NEG_BIG)
                out[k] = sk
            return out

        s = scores(groups[0])
        for gi, ks in enumerate(groups):
            if gi + 1 < len(groups):
                s.update(scores(groups[gi + 1]))
            m_new = m
            for k in ks:
                m_new = jnp.maximum(m_new, jnp.max(s[k], axis=0, keepdims=True))
            alpha = jnp.exp2(m - m_new)
            l = alpha * l
            pv = None
            for k in ks:
                p = jnp.exp2(s.pop(k) - m_new)
                l = l + jnp.sum(p, axis=0, keepdims=True)
                d = _dot(vt_ref[base + k], p.astype(BF16))
                pv = d if pv is None else pv + d
            acc_ref[...] = alpha * acc_ref[...] + pv
            m = m_new
        m_ref[...] = m
        l_ref[...] = l

    n_bulk = i // SEL_BODY
    rem = i % SEL_BODY

    def bulk(b, c):
        sel_body(b * SEL_BODY, (False,) * SEL_BODY)
        return c
    lax.fori_loop(0, n_bulk, bulk, 0)

    base = n_bulk * SEL_BODY
    part = SEL_BODY // 2
    while part >= 2:
        pl.when(rem & part != 0)(functools.partial(sel_body, base, (False,) * part))
        base = base + (rem & part)
        part //= 2

    @pl.when(rem % 2 == 1)
    def _():
        sel_body(i - 1, (False, True))

    @pl.when(rem % 2 == 0)
    def _():
        sel_body(i, (True,))

    o_st = acc_ref[...] * (1.0 / l_ref[...])

    gates_t = gate_ref[...].astype(F32).T
    per_group = 3 * A_HPG

    def gate(h, b):
        r = 3 * h + b
        return jnp.where(grp == 0, gates_t[r:r + 1], gates_t[per_group + r:per_group + r + 1])

    for h in range(A_HPG):
        c = slice(h * NSA_QB, (h + 1) * NSA_QB)
        o = gate(h, 0) * o_ct[:, c] + gate(h, 1) * o_st[:, c] + gate(h, 2) * o_wt[:, c]
        o_ref[:, h * HEAD_DIM:(h + 1) * HEAD_DIM] = o.T.astype(o_ref.dtype)


def _overlap_matrix(seq):
    n_chunk = seq // CMP_STRIDE
    cs = np.arange(n_chunk)[:, None] * CMP_STRIDE
    ss = np.arange(SEL_SLOTS)[None, :] * SEL_BLOCK
    ov = np.maximum(np.minimum(cs + CMP_BLOCK, ss + SEL_BLOCK) - np.maximum(cs, ss), 0) / CMP_BLOCK
    n_cmp = (seq - CMP_BLOCK) // CMP_STRIDE + 1
    ov[n_cmp:] = 0.0
    ov[:, seq // SEL_BLOCK:] = 0.0
    return ov.astype(np.float32)


def _nsa(a, cmp_kv):
    S = a.shape[0]
    G = A_KV_GROUPS
    n_tiles = S // KEY_TILE
    assert S // SEL_BLOCK <= SEL_SLOTS and S % KEY_TILE == 0 and S >= WINDOW + NSA_QB
    assert NSA_QB == KEY_TILE and G == 2
    n_chunk = S // CMP_STRIDE
    ovt = jnp.asarray(_overlap_matrix(S).T, BF16)
    col = lambda b: pl.BlockSpec((S, HEAD_DIM), lambda g, i, b=b: (0, b + g))
    return pl.pallas_call(
        functools.partial(_nsa_kernel, seq=S),
        grid=(G, S // NSA_QB),
        in_specs=[
            pl.BlockSpec((NSA_QB, A_HPG * HEAD_DIM), lambda g, i: (i, g)),
            col(BLK_KS), col(BLK_KW), col(BLK_VS), col(BLK_VW),
            pl.BlockSpec((1, 1, n_chunk, HEAD_DIM), lambda g, i: (0, g, 0, 0)),
            pl.BlockSpec((1, 1, n_chunk, HEAD_DIM), lambda g, i: (1, g, 0, 0)),
            pl.BlockSpec((SEL_SLOTS, n_chunk), lambda g, i: (0, 0)),
            pl.BlockSpec((NSA_QB, LANES), lambda g, i: (i, BLK_GATE)),
        ],
        out_specs=pl.BlockSpec((NSA_QB, A_HPG * HEAD_DIM), lambda g, i: (i, g)),
        out_shape=jax.ShapeDtypeStruct((S, A_WIDTH), BF16),
        scratch_shapes=[pltpu.VMEM((n_tiles, KEY_TILE, HEAD_DIM + SEL_SLOTS), BF16),
                        pltpu.VMEM((n_tiles, HEAD_DIM, KEY_TILE), BF16),
                        pltpu.VMEM((n_tiles, HEAD_DIM, KEY_TILE), BF16),
                        pltpu.VMEM((HEAD_DIM, n_chunk), BF16),
                        pltpu.VMEM((HEAD_DIM, A_HPG * NSA_QB), F32),
                        pltpu.VMEM((1, A_HPG * NSA_QB), F32),
                        pltpu.VMEM((1, A_HPG * NSA_QB), F32)],
        compiler_params=_cparams(("arbitrary", "arbitrary")),
        name="nsa",
    )(a, a, a, a, a, cmp_kv, cmp_kv, ovt, a)


_ROW_MASKS = 18
_SCORE_MASKS = 10
HG_HEADS_PER_STEP = 4


def _hgrn_tables():
    t = np.arange(HG_CHUNK)
    p4, i4, i16 = t % 4, (t // 4) % 4, t // 16
    conds = [p4 >= 1, p4 >= 2, p4 >= 3, p4 <= 2, p4 <= 1, p4 == 0,
             i4 >= 1, i4 >= 2, i4 >= 3, i4 <= 2, i4 <= 1, i4 == 0,
             i16 >= 1, i16 >= 2, i16 >= 3, i16 <= 2, i16 <= 1, i16 == 0]
    rm = np.stack([np.broadcast_to(c[:, None], (HG_CHUNK, HG_DIM)) for c in conds]).astype(np.float32)
    tt, ss = t[:, None], t[None, :]
    lag16 = tt // 16 - ss // 16
    lag4 = tt // 4 - ss // 4
    same16, same4 = lag16 == 0, lag4 == 0
    sm = [lag16 == 1, lag16 == 2, lag16 == 3,
          same16 & (lag4 == 1), same16 & (lag4 == 2), same16 & (lag4 == 3),
          same4 & (tt - ss == 0), same4 & (tt - ss == 1), same4 & (tt - ss == 2), same4 & (tt - ss == 3)]
    return rm, np.stack(sm).astype(np.float32)


def _hgrn_kernel(q_ref, f_ref, i_ref, og_ref, loglb_ref, log1m_ref, gn_ref, rm_ref, sm_ref,
                 o_ref, st_ref, g_scr, k_scr, o_scr, *, tb, nh):
    C = HG_CHUNK
    heads = [slice(h * HG_DIM, (h + 1) * HG_DIM) for h in range(nh)]

    @pl.when(pl.program_id(1) == 0)
    def _():
        st_ref[...] = jnp.zeros_like(st_ref)

    x = f_ref[...]
    log_sig = jnp.minimum(x, 0.0) - jnp.log(1.0 + jnp.exp(-jnp.abs(x)))
    a = loglb_ref[...]
    c = log1m_ref[...] + log_sig
    log_f = jnp.maximum(a, c) + jnp.log(1.0 + jnp.exp(-jnp.abs(a - c)))
    g_scr[...] = log_f
    k_scr[...] = 1.0 - jnp.exp(log_f)

    def rmask(n):
        return jnp.concatenate([rm_ref[n]] * nh, axis=1)

    def roll(v, s):
        return pltpu.roll(v, s % C, axis=0)

    def b16(z):
        return z.astype(BF16)

    def chunk(ci, carry):
        r0 = pl.multiple_of(ci * C, C)
        g = g_scr[pl.ds(r0, C), :]
        q = q_ref[pl.ds(r0, C), :]
        k = k_scr[pl.ds(r0, C), :]
        v = i_ref[pl.ds(r0, C), :].astype(BF16)
        pre4 = g + rmask(0) * roll(g, 1) + rmask(1) * roll(g, 2) + rmask(2) * roll(g, 3)
        suf4 = rmask(3) * roll(g, -1) + rmask(4) * roll(g, -2) + rmask(5) * roll(g, -3)
        tot4 = pre4 + suf4
        a1, a2, a3 = roll(tot4, 4), roll(tot4, 8), roll(tot4, 12)
        pre16 = pre4 + rmask(6) * a1 + rmask(7) * a2 + rmask(8) * a3
        suf16 = suf4 + rmask(9) * roll(tot4, -4) + rmask(10) * roll(tot4, -8) + rmask(11) * roll(tot4, -12)
        tot16 = pre16 + suf16
        d1, d2, d3 = roll(tot16, 16), roll(tot16, 32), roll(tot16, 48)
        pre64 = pre16 + rmask(12) * d1 + rmask(13) * d2 + rmask(14) * d3
        suf64 = suf16 + rmask(15) * roll(tot16, -16) + rmask(16) * roll(tot16, -32) + rmask(17) * roll(tot16, -48)
        tot64 = pre64 + suf64
        g1, g2 = roll(g, 1), roll(g, 2)

        q_a = q * jnp.exp(pre16)
        lhs_a = [b16(q_a), b16(q_a * jnp.exp(d1)), b16(q_a * jnp.exp(d1 + d2))]
        k_a = b16(k * jnp.exp(suf16))
        q_b = q * jnp.exp(pre4)
        lhs_b = [b16(q_b), b16(q_b * jnp.exp(a1)), b16(q_b * jnp.exp(a1 + a2))]
        k_b = b16(k * jnp.exp(suf4))
        q_c = q * jnp.exp(g)
        lhs_c = [b16(q), b16(q_c), b16(q_c * jnp.exp(g1)), b16(q_c * jnp.exp(g1 + g2))]
        k_c = b16(k)
        q_s = b16(q * jnp.exp(pre64))
        k_s = b16(k * jnp.exp(suf64))
        decay = jnp.exp(tot64[0:1, :])

        def level(lhs, rhs):
            return [_dot_nt(jnp.concatenate([z[:, hh] for z in lhs], axis=0), rhs[:, hh]) for hh in heads]

        s_a, s_b, s_c = level(lhs_a, k_a), level(lhs_b, k_b), level(lhs_c, k_c)
        o_inter = [_dot_nt(q_s[:, hh], b16(st_ref[h])) for h, hh in enumerate(heads)]
        kv = [_dot_tn(v[:, hh], k_s[:, hh]) for hh in heads]
        for h, hh in enumerate(heads):
            sa, sb, sc = s_a[h], s_b[h], s_c[h]
            scores = (sm_ref[0] * sa[0:C] + sm_ref[1] * sa[C:2 * C] + sm_ref[2] * sa[2 * C:3 * C]
                      + sm_ref[3] * sb[0:C] + sm_ref[4] * sb[C:2 * C] + sm_ref[5] * sb[2 * C:3 * C]
                      + sm_ref[6] * sc[0:C] + sm_ref[7] * sc[C:2 * C] + sm_ref[8] * sc[2 * C:3 * C]
                      + sm_ref[9] * sc[3 * C:4 * C])
            o_scr[pl.ds(r0, C), hh] = _dot(b16(scores), v[:, hh]) + o_inter[h]
            st_ref[h] = st_ref[h] * decay[:, hh] + kv[h]
        return carry

    lax.fori_loop(0, tb // C, chunk, 0)

    for hh in heads:
        o = o_scr[:, hh]
        ms = jnp.mean(o * o, axis=-1, keepdims=True)
        og = og_ref[:, hh]
        y = o * lax.rsqrt(ms + NORM_EPS) * gn_ref[:, hh] * (og * jax.nn.sigmoid(og))
        o_ref[:, hh] = y.astype(o_ref.dtype)


def _hgrn(hz, loglb, log1mlb, gnorm):
    S = hz.shape[0]
    tb = 512 if S % 512 == 0 else S
    nh = HG_HEADS_PER_STEP
    ng = HG_HEADS // nh
    w = nh * HG_DIM
    rm, sm = _hgrn_tables()
    blk = lambda part: pl.BlockSpec((tb, w), lambda h, c, part=part: (c, part * ng + h))
    row = pl.BlockSpec((1, w), lambda h, c: (0, h))
    return pl.pallas_call(
        functools.partial(_hgrn_kernel, tb=tb, nh=nh),
        grid=(ng, S // tb),
        in_specs=[blk(0), blk(1), blk(2), blk(3), row, row, row,
                  pl.BlockSpec((_ROW_MASKS, HG_CHUNK, HG_DIM), lambda h, c: (0, 0, 0)),
                  pl.BlockSpec((_SCORE_MASKS, HG_CHUNK, HG_CHUNK), lambda h, c: (0, 0, 0))],
        out_specs=pl.BlockSpec((tb, w), lambda h, c: (c, h)),
        out_shape=jax.ShapeDtypeStruct((S, HG_WIDTH), BF16),
        scratch_shapes=[pltpu.VMEM((nh, HG_DIM, HG_DIM), F32), pltpu.VMEM((tb, w), F32),
                        pltpu.VMEM((tb, w), F32), pltpu.VMEM((tb, w), F32)],
        compiler_params=_cparams(("arbitrary", "arbitrary")),
        name="hgrn",
    )(hz, hz, hz, hz, loglb, log1mlb, gnorm, jnp.asarray(rm), jnp.asarray(sm))


_MERGE_PART = 512


def _merge_kernel(at_ref, hg_ref, ga_ref, gh_ref, wa_ref, wh_ref, wo_ref, x_ref, gt_ref, gp_ref, o_ref):
    at, hg = at_ref[...], hg_ref[...]
    d = wo_ref.shape[1]
    parts = [slice(c0, c0 + _MERGE_PART) for c0 in range(0, d, _MERGE_PART)]
    branch = [(_dot(at, wa_ref[0, :, cs]), _dot(hg, wh_ref[0, :, cs])) for cs in parts]
    y = None
    for cs, (ma, mh) in zip(parts, branch):
        mixed = ga_ref[:, cs].astype(F32) * ma + gh_ref[:, cs].astype(F32) * mh
        yp = _dot(mixed.astype(BF16), wo_ref[0, cs, :])
        y = yp if y is None else y + yp
    ms = jnp.mean(y * y, axis=-1, keepdims=True)
    yn = y * lax.rsqrt(ms + NORM_EPS) * gp_ref[...]
    o_ref[...] = x_ref[...] + gt_ref[...] * yn


def _merge(attn, hg, mg, wa, wh, wo, layer, x, gt, gpost):
    S, D = x.shape
    tm = 256
    const = lambda shape: pl.BlockSpec((1,) + shape, lambda m: (layer, 0, 0), pipeline_mode=pl.Buffered(1))
    row = pl.BlockSpec((1, D), lambda m: (0, 0))
    return pl.pallas_call(
        _merge_kernel,
        grid=(S // tm,),
        in_specs=[
            pl.BlockSpec((tm, A_WIDTH), lambda m: (m, 0)),
            pl.BlockSpec((tm, HG_WIDTH), lambda m: (m, 0)),
            pl.BlockSpec((tm, D), lambda m: (m, 0)),
            pl.BlockSpec((tm, D), lambda m: (m, 1)),
            const((A_WIDTH, D)), const((HG_WIDTH, D)), const((D, D)),
            pl.BlockSpec((tm, D), lambda m: (m, 0)),
            row, row,
        ],
        out_specs=pl.BlockSpec((tm, D), lambda m: (m, 0)),
        out_shape=jax.ShapeDtypeStruct((S, D), F32),
        compiler_params=_cparams(("arbitrary",)),
        name="merge",
    )(attn, hg, mg, mg, wa, wh, wo, x, gt, gpost.reshape(1, D))


_HALO = 16
_FF_SUB = 256


def _ffn_kernel(x_ref, xh_ref, g_ref, sc_ref, sh_ref, wg_ref, wv_ref, cwg_ref, cwv_ref, cbg_ref, cbv_ref,
                wd_ref, gt_ref, gp_ref, o_ref, h_scr, ug_scr, uv_scr, *, tm):
    m = pl.program_id(0)
    f = pl.program_id(1)

    def norm_mod(x):
        ms = jnp.mean(x * x, axis=-1, keepdims=True)
        y = x * lax.rsqrt(ms + NORM_EPS) * g_ref[...]
        return (y * (1.0 + sc_ref[...]) + sh_ref[...]).astype(BF16)

    @pl.when(f == 0)
    def _():
        h_scr[0:_HALO, :] = jnp.where(m > 0, norm_mod(xh_ref[...]), jnp.zeros((), BF16))
        h_scr[_HALO:_HALO + tm, :] = norm_mod(x_ref[...])
        o_ref[...] = jnp.zeros_like(o_ref)

    tf = wg_ref.shape[2]
    subs = [slice(s * _FF_SUB, (s + 1) * _FF_SUB) for s in range(tf // _FF_SUB)]
    h = h_scr[...]
    ups = [(_dot(h, wg_ref[0, :, cs]), _dot(h, wv_ref[0, :, cs])) for cs in subs]

    def conv(u_scr, cw_ref, cb_ref, cs):
        out = cb_ref[:, cs] + cw_ref[0:1, cs] * u_scr[pl.ds(_HALO - 2, tm), cs]
        out = out + cw_ref[1:2, cs] * u_scr[pl.ds(_HALO - 1, tm), cs]
        return out + cw_ref[2:3, cs] * u_scr[pl.ds(_HALO, tm), cs]

    acc = o_ref[...]
    for cs, (ug, uv) in zip(subs, ups):
        ug_scr[:, cs] = ug
        uv_scr[:, cs] = uv
        cg = conv(ug_scr, cwg_ref, cbg_ref, cs)
        cv = conv(uv_scr, cwv_ref, cbv_ref, cs)
        act = (cg * jax.nn.sigmoid(cg) * cv).astype(BF16)
        acc = acc + _dot(act, wd_ref[0, cs, :])
    o_ref[...] = acc

    @pl.when(f == pl.num_programs(1) - 1)
    def _():
        y = o_ref[...]
        ms = jnp.mean(y * y, axis=-1, keepdims=True)
        yn = y * lax.rsqrt(ms + NORM_EPS) * gp_ref[...]
        o_ref[...] = x_ref[...] + gt_ref[...] * yn


def _ffn(x, gpre, sc, sh, wup, convw, convb, wdown, layer, gt, gpost):
    S, D = x.shape
    F = wdown.shape[1]
    tm = 512 if S % 512 == 0 else S
    tf = 512
    nf = F // tf
    hb = tm // _HALO
    row = pl.BlockSpec((1, D), lambda m, f: (0, 0))
    return pl.pallas_call(
        functools.partial(_ffn_kernel, tm=tm),
        grid=(S // tm, nf),
        in_specs=[
            pl.BlockSpec((tm, D), lambda m, f: (m, 0)),
            pl.BlockSpec((_HALO, D), lambda m, f: (jnp.maximum(m * hb - 1, 0), 0)),
            row, row, row,
            pl.BlockSpec((1, D, tf), lambda m, f: (layer, 0, f)),
            pl.BlockSpec((1, D, tf), lambda m, f: (layer, 0, nf + f)),
            pl.BlockSpec((CONV_WIDTH, tf), lambda m, f: (0, f)),
            pl.BlockSpec((CONV_WIDTH, tf), lambda m, f: (0, nf + f)),
            pl.BlockSpec((1, tf), lambda m, f: (0, f)),
            pl.BlockSpec((1, tf), lambda m, f: (0, nf + f)),
            pl.BlockSpec((1, tf, D), lambda m, f: (layer, f, 0)),
            row, row,
        ],
        out_specs=pl.BlockSpec((tm, D), lambda m, f: (m, 0)),
        out_shape=jax.ShapeDtypeStruct((S, D), F32),
        scratch_shapes=[pltpu.VMEM((tm + _HALO, D), BF16),
                        pltpu.VMEM((tm + _HALO, tf), F32), pltpu.VMEM((tm + _HALO, tf), F32)],
        compiler_params=_cparams(("arbitrary", "arbitrary")),
        name="ffn",
    )(x, x, gpre.reshape(1, D), sc, sh, wup, wup, convw, convw, convb.reshape(1, -1), convb.reshape(1, -1),
      wdown, gt, gpost.reshape(1, D))


def _rope_tables(positions):
    inv_freq = 1.0 / (ROPE_THETA ** (jnp.arange(0, HEAD_DIM, 2, dtype=F32) / HEAD_DIM))
    ang = positions.astype(F32)[:, None] * inv_freq
    cos, sin = jnp.cos(ang), jnp.sin(ang)
    return jnp.concatenate([cos, cos], axis=-1), jnp.concatenate([-sin, sin], axis=-1)


def kernel(x, c, positions, w_ada, b_ada, g_pre_mix, w_in, pe_kc, w_kc, pe_vc, w_vc, lb_logits, g_hg_norm,
           w_br_attn, w_br_hgrn, w_out, g_post_mix, g_pre_ffn, w_up, conv_w, conv_b, w_down, g_post_ffn):
    B, S, D = x.shape
    assert B == 1, "kernel is written for one sequence"
    L = w_ada.shape[0]
    xs = x[0]
    cosf, sinf = _rope_tables(positions[0])
    scale = HEAD_DIM ** -0.5 * np.log2(np.e)
    tables = (cosf * scale, sinf * scale, cosf, sinf)
    lb_cum = jnp.cumsum(jax.nn.softmax(lb_logits.astype(F32), axis=0), axis=0)
    lower = lb_cum - lb_cum[0:1]
    log_lb = jnp.log(lower)
    log_1m = jnp.log1p(-lower)
    ada = _ada_all(c, w_ada, b_ada)

    w_in_t = jnp.swapaxes(w_in, 1, 2)
    tail_tn = 1024
    hg_roles = (("id",) * (tail_tn // LANES),) * (4 * HG_WIDTH // tail_tn)
    mg_roles = (("sg",) * (tail_tn // LANES),) * (2 * D // tail_tn)

    wa_b, wh_b, wo_b = w_br_attn.astype(BF16), w_br_hgrn.astype(BF16), w_out.astype(BF16)
    wup_b, wdown_b = w_up.astype(BF16), w_down.astype(BF16)

    for l in range(L):
        sh1, sc1, gt1, sh2, sc2, gt2 = [ada[l, :, j * D:(j + 1) * D] for j in range(N_ADA)]

        h = _prenorm(xs, g_pre_mix[l], sc1, sh1)
        a = _proj(h, w_in_t, l, 0, A_ROLES, BF16, tn=A_TILE, tables=tables, name="proj_attn")
        hz = _proj(h, w_in_t, l, A_COLS, hg_roles, F32, tn=tail_tn, name="proj_hgrn")
        mg = _proj(h, w_in_t, l, A_COLS + 4 * HG_WIDTH, mg_roles, BF16, tn=tail_tn, name="proj_merge_gate")

        cmp_kv = _compress(a, jnp.stack([pe_kc[l], pe_vc[l]]), jnp.stack([w_kc[l], w_vc[l]]))
        attn = _nsa(a, cmp_kv)
        hg = _hgrn(hz, log_lb[l:l + 1], log_1m[l:l + 1], g_hg_norm[l:l + 1])
        xs = _merge(attn, hg, mg, wa_b, wh_b, wo_b, l, xs, gt1, g_post_mix[l])

        xs = _ffn(xs, g_pre_ffn[l], sc2, sh2, wup_b, conv_w[l], conv_b[l], wdown_b, l, gt2, g_post_ffn[l])
    return xs[None]
```

```python
import functools

import jax
import jax.numpy as jnp
import numpy as np
from jax import lax
from jax.experimental import pallas as pl
from jax.experimental.pallas import tpu as pltpu

F32 = jnp.float32
BF16 = jnp.bfloat16

A_HEADS = 8
A_KV_GROUPS = 2
A_HPG = A_HEADS // A_KV_GROUPS
HEAD_DIM = 128
A_WIDTH = A_HEADS * HEAD_DIM
A_KV_WIDTH = A_KV_GROUPS * HEAD_DIM
CMP_BLOCK = 32
CMP_STRIDE = 16
SEL_BLOCK = 64
SEL_TOPK = 16
WINDOW = 512
ROPE_THETA = 10000.0
HG_HEADS = 8
HG_DIM = 128
HG_WIDTH = HG_HEADS * HG_DIM
HG_CHUNK = 64
CONV_WIDTH = 3
NORM_EPS = 1e-6
N_ADA = 6
A_COLS = A_WIDTH + 6 * A_KV_WIDTH + 3 * A_HEADS

LANES = 128
SEL_SLOTS = LANES
KEY_TILE = 256
NEG_BIG = -1e30
VMEM_LIMIT = 56 * 1024 * 1024


def _cparams(sem):
    return pltpu.CompilerParams(dimension_semantics=sem, vmem_limit_bytes=VMEM_LIMIT)


def _split3(a):
    hi = a.astype(BF16)
    r1 = a - hi.astype(F32)
    mid = r1.astype(BF16)
    lo = (r1 - mid.astype(F32)).astype(BF16)
    return hi, mid, lo


def _dot(a, b):
    return jnp.dot(a, b, preferred_element_type=F32)


def _dot_nt(a, b):
    return lax.dot_general(a, b, (((1,), (1,)), ((), ())), preferred_element_type=F32)


def _dot_tn(a, b):
    return lax.dot_general(a, b, (((0,), (0,)), ((), ())), preferred_element_type=F32)


def _ada_kernel(c_ref, w_ref, b_ref, o_ref):
    c = c_ref[...]
    ca = c * jax.nn.sigmoid(c)
    o_ref[0] = jnp.sum(ca * w_ref[0], axis=0, keepdims=True) + b_ref[0]


def _ada_all(c, w_ada, b_ada):
    L, D, N = w_ada.shape
    tn = 1024
    return pl.pallas_call(
        _ada_kernel,
        grid=(L, N // tn),
        in_specs=[
            pl.BlockSpec((D, 1), lambda l, n: (0, 0)),
            pl.BlockSpec((1, D, tn), lambda l, n: (l, 0, n)),
            pl.BlockSpec((1, 1, tn), lambda l, n: (l, 0, n)),
        ],
        out_specs=pl.BlockSpec((1, 1, tn), lambda l, n: (l, 0, n)),
        out_shape=jax.ShapeDtypeStruct((L, 1, N), F32),
        compiler_params=_cparams(("arbitrary", "arbitrary")),
        name="ada",
    )(c.reshape(D, 1), w_ada, b_ada.reshape(L, 1, N))


def _prenorm_kernel(x_ref, g_ref, sc_ref, sh_ref, o_ref):
    x = x_ref[...]
    ms = jnp.mean(x * x, axis=-1, keepdims=True)
    y = x * lax.rsqrt(ms + NORM_EPS) * g_ref[...]
    o_ref[...] = (y * (1.0 + sc_ref[...]) + sh_ref[...]).astype(o_ref.dtype)


def _prenorm(x, g, sc, sh):
    S, D = x.shape
    tm = 512
    row = pl.BlockSpec((1, D), lambda m: (0, 0))
    return pl.pallas_call(
        _prenorm_kernel,
        grid=(S // tm,),
        in_specs=[pl.BlockSpec((tm, D), lambda m: (m, 0)), row, row, row],
        out_specs=pl.BlockSpec((tm, D), lambda m: (m, 0)),
        out_shape=jax.ShapeDtypeStruct((S, D), BF16),
        compiler_params=_cparams(("arbitrary",)),
        name="prenorm",
    )(x, g.reshape(1, D), sc, sh)


def _rope(t, cosf, sinf):
    return t * cosf + pltpu.roll(t, HEAD_DIM // 2, axis=1) * sinf


_PROJ_PART = 256


def _proj_kernel(h_ref, w_ref, *rest, roles, tn):
    o_ref, w_scr = rest[-2:]
    tabs = rest[:-2]
    n = pl.program_id(0)

    @pl.when(pl.program_id(1) == 0)
    def _():
        w_scr[...] = w_ref[0].astype(BF16)

    def emit(slabs):
        h = h_ref[...]
        per = _PROJ_PART // LANES
        parts = [_dot_nt(h, w_scr[c0:c0 + _PROJ_PART, :]) for c0 in range(0, tn, _PROJ_PART)]
        for pi, acc in enumerate(parts):
            for s in range(per):
                kind = slabs[pi * per + s]
                t = acc[:, s * LANES:(s + 1) * LANES]
                if kind == "sg":
                    t = jax.nn.sigmoid(t)
                elif kind != "id":
                    cos_ref, sin_ref = tabs[0:2] if kind == "rq" else tabs[-2:]
                    t = _rope(t, cos_ref[...], sin_ref[...])
                c = (pi * per + s) * LANES
                o_ref[:, c:c + LANES] = t.astype(o_ref.dtype)

    kinds = sorted(set(roles))
    if len(kinds) == 1:
        emit(kinds[0])
        return
    for kind in kinds:
        hit = functools.reduce(jnp.logical_or, [n == t for t, r in enumerate(roles) if r == kind])
        pl.when(hit)(functools.partial(emit, kind))


def _proj(h, w_t, layer, row0, roles, out_dtype, *, tn, tables=(), name="proj"):
    S, D = h.shape
    tm = 1024 if S % 1024 == 0 else S
    nt = len(roles)
    tab = pl.BlockSpec((tm, HEAD_DIM), lambda n, m: (m, 0))
    return pl.pallas_call(
        functools.partial(_proj_kernel, roles=tuple(roles), tn=tn),
        grid=(nt, S // tm),
        in_specs=[pl.BlockSpec((tm, D), lambda n, m: (m, 0)),
                  pl.BlockSpec((pl.Element(1), pl.Element(tn), pl.Element(D)),
                               lambda n, m: (layer, pl.multiple_of(row0 + n * tn, 8), 0))]
                 + [tab] * len(tables),
        out_specs=pl.BlockSpec((tm, tn), lambda n, m: (m, n)),
        out_shape=jax.ShapeDtypeStruct((S, nt * tn), out_dtype),
        scratch_shapes=[pltpu.VMEM((tn, D), BF16)],
        compiler_params=_cparams(("arbitrary", "arbitrary")),
        name=name,
    )(h, w_t, *tables)


A_TILE = 512
_KV_TILE_ROLES = ("rk",) * A_KV_GROUPS + ("id",) * A_KV_GROUPS
A_ROLES = ((("rq",) * (A_TILE // LANES),) * (A_WIDTH // A_TILE) + (_KV_TILE_ROLES,) * 3
           + (("sg",) * (A_TILE // LANES),))
_KV_BLOCK0 = A_WIDTH // LANES
BLK_KC, BLK_VC, BLK_KS, BLK_VS, BLK_KW, BLK_VW = [_KV_BLOCK0 + j * A_KV_GROUPS for j in range(6)]
BLK_GATE = _KV_BLOCK0 + 6 * A_KV_GROUPS


def _compress_kernel(t_ref, pe_ref, w_ref, o_ref, t32_ref, *, n_chunk):
    half = CMP_BLOCK // 2
    t32_ref[...] = t_ref[...].astype(F32)
    top, bot = [], []
    for j in range(half):
        a = t32_ref[pl.ds(j, n_chunk, stride=CMP_STRIDE), :]
        top.append(a + pe_ref[0, j:j + 1, :])
        bot.append(a + pe_ref[0, half + j:half + j + 1, :])
    a_top = jnp.concatenate(top, axis=1)
    a_bot = jnp.concatenate(bot, axis=1)
    kw = half * HEAD_DIM
    w = w_ref[0]
    w_top = w[:kw].astype(BF16)
    w_bot = w[kw:].astype(BF16)
    y_top = jnp.zeros((n_chunk, HEAD_DIM), F32)
    y_bot = jnp.zeros((n_chunk, HEAD_DIM), F32)
    for piece in _split3(a_top):
        y_top = y_top + _dot(piece, w_top)
    for piece in _split3(a_bot):
        y_bot = y_bot + _dot(piece, w_bot)
    o_ref[0, 0] = y_top + pltpu.roll(y_bot, n_chunk - 1, axis=0)


def _compress(a, pe2, w2):
    S = a.shape[0]
    n_chunk = S // CMP_STRIDE
    G = A_KV_GROUPS
    return pl.pallas_call(
        functools.partial(_compress_kernel, n_chunk=n_chunk),
        grid=(2, G),
        in_specs=[
            pl.BlockSpec((S, HEAD_DIM), lambda kv, g: (0, BLK_KC + kv * G + g)),
            pl.BlockSpec((1, CMP_BLOCK, HEAD_DIM), lambda kv, g: (kv, 0, 0)),
            pl.BlockSpec((1, CMP_BLOCK * HEAD_DIM, HEAD_DIM), lambda kv, g: (kv, 0, 0)),
        ],
        out_specs=pl.BlockSpec((1, 1, n_chunk, HEAD_DIM), lambda kv, g: (kv, g, 0, 0)),
        out_shape=jax.ShapeDtypeStruct((2, G, n_chunk, HEAD_DIM), F32),
        scratch_shapes=[pltpu.VMEM((S, HEAD_DIM), F32)],
        compiler_params=_cparams(("arbitrary", "arbitrary")),
        name="compress",
    )(a, pe2, w2)


NSA_QB = 256
SEL_BODY = 8
SEL_GROUP = 4


def _nsa_kernel(q_ref, ks_ref, kw_ref, vs_ref, vw_ref, kc_ref, vc_ref, ovt_ref, gate_ref,
                o_ref, kaug_ref, vt_ref, vwt_ref, vct_ref, acc_ref, m_ref, l_ref, *, seq):
    grp = pl.program_id(0)
    i = pl.program_id(1)
    t0 = i * NSA_QB
    n_cmp = seq // CMP_STRIDE
    cols = A_HPG * NSA_QB

    @pl.when(i == 0)
    def _():
        def build(r, c):
            r0 = pl.multiple_of(r * KEY_TILE, KEY_TILE)
            kaug_ref[r, :, 0:HEAD_DIM] = ks_ref[pl.ds(r0, KEY_TILE), :]
            blk = (r0 + lax.broadcasted_iota(jnp.int32, (KEY_TILE, SEL_SLOTS), 0)) // SEL_BLOCK
            slot = lax.broadcasted_iota(jnp.int32, (KEY_TILE, SEL_SLOTS), 1)
            kaug_ref[r, :, HEAD_DIM:HEAD_DIM + SEL_SLOTS] = jnp.where(blk == slot, 1.0, 0.0).astype(BF16)
            vt_ref[r] = vs_ref[pl.ds(r0, KEY_TILE), :].astype(F32).T.astype(BF16)
            vwt_ref[r] = vw_ref[pl.ds(r0, KEY_TILE), :].astype(F32).T.astype(BF16)
            return c
        lax.fori_loop(0, seq // KEY_TILE, build, 0)
        vct_ref[...] = vc_ref[0, 0].T.astype(BF16)

    qt = jnp.concatenate(
        [q_ref[:, h * HEAD_DIM:(h + 1) * HEAD_DIM].astype(F32).T.astype(BF16) for h in range(A_HPG)],
        axis=1)
    tq = t0 + lax.broadcasted_iota(jnp.int32, (1, cols), 1) % NSA_QB

    def softmax0(s):
        m = jnp.max(s, axis=0, keepdims=True)
        p = jnp.exp2(s - m)
        return p, jnp.sum(p, axis=0, keepdims=True)

    s_c = _dot(kc_ref[0, 0].astype(BF16), qt)
    nrow = lax.broadcasted_iota(jnp.int32, (n_cmp, 1), 0)
    vis = (nrow * CMP_STRIDE + (CMP_BLOCK - 1)) <= tq
    p_c, den = softmax0(jnp.where(vis, s_c, NEG_BIG))
    p_c = p_c * jnp.where(tq >= CMP_BLOCK - 1, 1.0 / den, 0.0)
    o_ct = _dot(vct_ref[...], p_c.astype(BF16))

    psum = p_c[:, 0:NSA_QB]
    for h in range(1, A_HPG):
        psum = psum + p_c[:, h * NSA_QB:(h + 1) * NSA_QB]
    ovt = ovt_ref[...]
    imp_t = jnp.zeros((SEL_SLOTS, NSA_QB), F32)
    for piece in _split3(psum):
        imp_t = imp_t + _dot(ovt, piece)
    slot_i = lax.broadcasted_iota(jnp.int32, (SEL_SLOTS, NSA_QB), 0)
    tq_l = t0 + lax.broadcasted_iota(jnp.int32, (SEL_SLOTS, NSA_QB), 1)
    cur = tq_l // SEL_BLOCK
    forced = jnp.where(slot_i == 0, 1, 0) + jnp.where(slot_i == cur, 1, 0) + jnp.where(slot_i == cur - 1, 1, 0)
    valid = slot_i * SEL_BLOCK <= tq_l
    key = jnp.where(forced > 0, -NEG_BIG, jnp.where(valid, imp_t, NEG_BIG))
    slot_f = slot_i.astype(F32)
    bias_t = jnp.full((SEL_SLOTS, NSA_QB), NEG_BIG, F32)
    for _ in range(SEL_TOPK):
        mx = jnp.max(key, axis=0, keepdims=True)
        first = jnp.min(jnp.where(key == mx, slot_f, 1e9), axis=0, keepdims=True)
        pick = slot_f == first
        bias_t = jnp.where(pick, 0.0, bias_t)
        key = jnp.where(pick, -3e38, key)

    wlen = WINDOW + NSA_QB
    w0 = pl.multiple_of(jnp.maximum(t0 - WINDOW, 0), KEY_TILE)
    wt = jnp.maximum(i - WINDOW // KEY_TILE, 0)
    s_w = _dot(kw_ref[pl.ds(w0, wlen), :], qt)
    dist = tq - (w0 + lax.broadcasted_iota(jnp.int32, (wlen, 1), 0))
    in_window = dist.astype(jnp.uint32) < WINDOW
    p_w, l_w = softmax0(jnp.where(in_window, s_w, NEG_BIG))
    p_w = p_w.astype(BF16)
    o_wt = _dot(vwt_ref[wt], p_w[0:KEY_TILE])
    for k in range(1, wlen // KEY_TILE):
        o_wt = o_wt + _dot(vwt_ref[wt + k], p_w[k * KEY_TILE:(k + 1) * KEY_TILE])
    o_wt = o_wt * (1.0 / l_w)

    qt_aug = jnp.concatenate([qt, jnp.concatenate([bias_t.astype(BF16)] * A_HPG, axis=1)], axis=0)
    acc_ref[...] = jnp.zeros_like(acc_ref)
    m_ref[...] = jnp.full(m_ref.shape, 0.5 * NEG_BIG, F32)
    l_ref[...] = jnp.zeros_like(l_ref)

    def sel_body(base, causal):
        n = len(causal)
        m, l = m_ref[...], l_ref[...]
        groups = [list(range(g0, min(g0 + SEL_GROUP, n))) for g0 in range(0, n, SEL_GROUP)]

        def scores(ks):
            out = {}
            for k in ks:
                sk = _dot(kaug_ref[base + k], qt_aug)
                if causal[k]:
                    kpos = (base + k) * KEY_TILE + lax.broadcasted_iota(jnp.int32, (KEY_TILE, 1), 0)
                    sk = jnp.where(kpos <= tq, sk, NEG_BIG)
                out[k] = sk
            return out

        s = scores(groups[0])
        for gi, ks in enumerate(groups):
            if gi + 1 < len(groups):
                s.update(scores(groups[gi + 1]))
            m_new = m
            for k in ks:
                m_new = jnp.maximum(m_new, jnp.max(s[k], axis=0, keepdims=True))
            alpha = jnp.exp2(m - m_new)
            l = alpha * l
            pv = None
            for k in ks:
                p = jnp.exp2(s.pop(k) - m_new)
                l = l + jnp.sum(p, axis=0, keepdims=True)
                d = _dot(vt_ref[base + k], p.astype(BF16))
                pv = d if pv is None else pv + d
            acc_ref[...] = alpha * acc_ref[...] + pv
            m = m_new
        m_ref[...] = m
        l_ref[...] = l

    n_bulk = i // SEL_BODY
    rem = i % SEL_BODY

    def bulk(b, c):
        sel_body(b * SEL_BODY, (False,) * SEL_BODY)
        return c
    lax.fori_loop(0, n_bulk, bulk, 0)

    base = n_bulk * SEL_BODY
    part = SEL_BODY // 2
    while part >= 2:
        pl.when(rem & part != 0)(functools.partial(sel_body, base, (False,) * part))
        base = base + (rem & part)
        part //= 2

    @pl.when(rem % 2 == 1)
    def _():
        sel_body(i - 1, (False, True))

    @pl.when(rem % 2 == 0)
    def _():
        sel_body(i, (True,))

    o_st = acc_ref[...] * (1.0 / l_ref[...])

    gates_t = gate_ref[...].astype(F32).T
    per_group = 3 * A_HPG

    def gate(h, b):
        r = 3 * h + b
        return jnp.where(grp == 0, gates_t[r:r + 1], gates_t[per_group + r:per_group + r + 1])

    for h in range(A_HPG):
        c = slice(h * NSA_QB, (h + 1) * NSA_QB)
        o = gate(h, 0) * o_ct[:, c] + gate(h, 1) * o_st[:, c] + gate(h, 2) * o_wt[:, c]
        o_ref[:, h * HEAD_DIM:(h + 1) * HEAD_DIM] = o.T.astype(o_ref.dtype)


def _overlap_matrix(seq):
    n_chunk = seq // CMP_STRIDE
    cs = np.arange(n_chunk)[:, None] * CMP_STRIDE
    ss = np.arange(SEL_SLOTS)[None, :] * SEL_BLOCK
    ov = np.maximum(np.minimum(cs + CMP_BLOCK, ss + SEL_BLOCK) - np.maximum(cs, ss), 0) / CMP_BLOCK
    n_cmp = (seq - CMP_BLOCK) // CMP_STRIDE + 1
    ov[n_cmp:] = 0.0
    ov[:, seq // SEL_BLOCK:] = 0.0
    return ov.astype(np.float32)


def _nsa(a, cmp_kv):
    S = a.shape[0]
    G = A_KV_GROUPS
    n_tiles = S // KEY_TILE
    assert S // SEL_BLOCK <= SEL_SLOTS and S % KEY_TILE == 0 and S >= WINDOW + NSA_QB
    assert NSA_QB == KEY_TILE and G == 2
    n_chunk = S // CMP_STRIDE
    ovt = jnp.asarray(_overlap_matrix(S).T, BF16)
    col = lambda b: pl.BlockSpec((S, HEAD_DIM), lambda g, i, b=b: (0, b + g))
    return pl.pallas_call(
        functools.partial(_nsa_kernel, seq=S),
        grid=(G, S // NSA_QB),
        in_specs=[
            pl.BlockSpec((NSA_QB, A_HPG * HEAD_DIM), lambda g, i: (i, g)),
            col(BLK_KS), col(BLK_KW), col(BLK_VS), col(BLK_VW),
            pl.BlockSpec((1, 1, n_chunk, HEAD_DIM), lambda g, i: (0, g, 0, 0)),
            pl.BlockSpec((1, 1, n_chunk, HEAD_DIM), lambda g, i: (1, g, 0, 0)),
            pl.BlockSpec((SEL_SLOTS, n_chunk), lambda g, i: (0, 0)),
            pl.BlockSpec((NSA_QB, LANES), lambda g, i: (i, BLK_GATE)),
        ],
        out_specs=pl.BlockSpec((NSA_QB, A_HPG * HEAD_DIM), lambda g, i: (i, g)),
        out_shape=jax.ShapeDtypeStruct((S, A_WIDTH), BF16),
        scratch_shapes=[pltpu.VMEM((n_tiles, KEY_TILE, HEAD_DIM + SEL_SLOTS), BF16),
                        pltpu.VMEM((n_tiles, HEAD_DIM, KEY_TILE), BF16),
                        pltpu.VMEM((n_tiles, HEAD_DIM, KEY_TILE), BF16),
                        pltpu.VMEM((HEAD_DIM, n_chunk), BF16),
                        pltpu.VMEM((HEAD_DIM, A_HPG * NSA_QB), F32),
                        pltpu.VMEM((1, A_HPG * NSA_QB), F32),
                        pltpu.VMEM((1, A_HPG * NSA_QB), F32)],
        compiler_params=_cparams(("arbitrary", "arbitrary")),
        name="nsa",
    )(a, a, a, a, a, cmp_kv, cmp_kv, ovt, a)


_ROW_MASKS = 18
_SCORE_MASKS = 10
HG_HEADS_PER_STEP = 8


def _hgrn_tables():
    t = np.arange(HG_CHUNK)
    p4, i4, i16 = t % 4, (t // 4) % 4, t // 16
    conds = [p4 >= 1, p4 >= 2, p4 >= 3, p4 <= 2, p4 <= 1, p4 == 0,
             i4 >= 1, i4 >= 2, i4 >= 3, i4 <= 2, i4 <= 1, i4 == 0,
             i16 >= 1, i16 >= 2, i16 >= 3, i16 <= 2, i16 <= 1, i16 == 0]
    rm = np.stack([np.broadcast_to(c[:, None], (HG_CHUNK, HG_DIM)) for c in conds]).astype(np.float32)
    tt, ss = t[:, None], t[None, :]
    lag16 = tt // 16 - ss // 16
    lag4 = tt // 4 - ss // 4
    same16, same4 = lag16 == 0, lag4 == 0
    sm = [lag16 == 1, lag16 == 2, lag16 == 3,
          same16 & (lag4 == 1), same16 & (lag4 == 2), same16 & (lag4 == 3),
          same4 & (tt - ss == 0), same4 & (tt - ss == 1), same4 & (tt - ss == 2), same4 & (tt - ss == 3)]
    return rm, np.stack(sm).astype(np.float32)


def _hgrn_kernel(q_ref, f_ref, i_ref, og_ref, loglb_ref, log1m_ref, gn_ref, rm_ref, sm_ref,
                 o_ref, st_ref, g_scr, k_scr, o_scr, *, tb, nh):
    C = HG_CHUNK
    heads = [slice(h * HG_DIM, (h + 1) * HG_DIM) for h in range(nh)]

    @pl.when(pl.program_id(1) == 0)
    def _():
        st_ref[...] = jnp.zeros_like(st_ref)

    x = f_ref[...]
    log_sig = jnp.minimum(x, 0.0) - jnp.log(1.0 + jnp.exp(-jnp.abs(x)))
    a = loglb_ref[...]
    c = log1m_ref[...] + log_sig
    log_f = jnp.maximum(a, c) + jnp.log(1.0 + jnp.exp(-jnp.abs(a - c)))
    g_scr[...] = log_f
    k_scr[...] = 1.0 - jnp.exp(log_f)

    def rmask(n):
        return jnp.concatenate([rm_ref[n]] * nh, axis=1)

    def roll(v, s):
        return pltpu.roll(v, s % C, axis=0)

    def b16(z):
        return z.astype(BF16)

    def chunk(ci, carry):
        r0 = pl.multiple_of(ci * C, C)
        g = g_scr[pl.ds(r0, C), :]
        q = q_ref[pl.ds(r0, C), :]
        k = k_scr[pl.ds(r0, C), :]
        v = i_ref[pl.ds(r0, C), :].astype(BF16)
        pre4 = g + rmask(0) * roll(g, 1) + rmask(1) * roll(g, 2) + rmask(2) * roll(g, 3)
        suf4 = rmask(3) * roll(g, -1) + rmask(4) * roll(g, -2) + rmask(5) * roll(g, -3)
        tot4 = pre4 + suf4
        a1, a2, a3 = roll(tot4, 4), roll(tot4, 8), roll(tot4, 12)
        pre16 = pre4 + rmask(6) * a1 + rmask(7) * a2 + rmask(8) * a3
        suf16 = suf4 + rmask(9) * roll(tot4, -4) + rmask(10) * roll(tot4, -8) + rmask(11) * roll(tot4, -12)
        tot16 = pre16 + suf16
        d1, d2, d3 = roll(tot16, 16), roll(tot16, 32), roll(tot16, 48)
        pre64 = pre16 + rmask(12) * d1 + rmask(13) * d2 + rmask(14) * d3
        suf64 = suf16 + rmask(15) * roll(tot16, -16) + rmask(16) * roll(tot16, -32) + rmask(17) * roll(tot16, -48)
        tot64 = pre64 + suf64
        g1, g2 = roll(g, 1), roll(g, 2)

        q_a = q * jnp.exp(pre16)
        lhs_a = [b16(q_a), b16(q_a * jnp.exp(d1)), b16(q_a * jnp.exp(d1 + d2))]
        k_a = b16(k * jnp.exp(suf16))
        q_b = q * jnp.exp(pre4)
        lhs_b = [b16(q_b), b16(q_b * jnp.exp(a1)), b16(q_b * jnp.exp(a1 + a2))]
        k_b = b16(k * jnp.exp(suf4))
        q_c = q * jnp.exp(g)
        lhs_c = [b16(q), b16(q_c), b16(q_c * jnp.exp(g1)), b16(q_c * jnp.exp(g1 + g2))]
        k_c = b16(k)
        q_s = b16(q * jnp.exp(pre64))
        k_s = b16(k * jnp.exp(suf64))
        decay = jnp.exp(tot64[0:1, :])

        def level(lhs, rhs):
            return [_dot_nt(jnp.concatenate([z[:, hh] for z in lhs], axis=0), rhs[:, hh]) for hh in heads]

        s_a, s_b, s_c = level(lhs_a, k_a), level(lhs_b, k_b), level(lhs_c, k_c)
        o_inter = [_dot_nt(q_s[:, hh], b16(st_ref[h])) for h, hh in enumerate(heads)]
        kv = [_dot_tn(v[:, hh], k_s[:, hh]) for hh in heads]
        for h, hh in enumerate(heads):
            sa, sb, sc = s_a[h], s_b[h], s_c[h]
            scores = (sm_ref[0] * sa[0:C] + sm_ref[1] * sa[C:2 * C] + sm_ref[2] * sa[2 * C:3 * C]
                      + sm_ref[3] * sb[0:C] + sm_ref[4] * sb[C:2 * C] + sm_ref[5] * sb[2 * C:3 * C]
                      + sm_ref[6] * sc[0:C] + sm_ref[7] * sc[C:2 * C] + sm_ref[8] * sc[2 * C:3 * C]
                      + sm_ref[9] * sc[3 * C:4 * C])
            o_scr[pl.ds(r0, C), hh] = _dot(b16(scores), v[:, hh]) + o_inter[h]
            st_ref[h] = st_ref[h] * decay[:, hh] + kv[h]
        return carry

    lax.fori_loop(0, tb // C, chunk, 0)

    for hh in heads:
        o = o_scr[:, hh]
        ms = jnp.mean(o * o, axis=-1, keepdims=True)
        og = og_ref[:, hh]
        y = o * lax.rsqrt(ms + NORM_EPS) * gn_ref[:, hh] * (og * jax.nn.sigmoid(og))
        o_ref[:, hh] = y.astype(o_ref.dtype)


def _hgrn(hz, loglb, log1mlb, gnorm):
    S = hz.shape[0]
    tb = 512 if S % 512 == 0 else S
    nh = HG_HEADS_PER_STEP
    ng = HG_HEADS // nh
    w = nh * HG_DIM
    rm, sm = _hgrn_tables()
    blk = lambda part: pl.BlockSpec((tb, w), lambda h, c, part=part: (c, part * ng + h))
    row = pl.BlockSpec((1, w), lambda h, c: (0, h))
    return pl.pallas_call(
        functools.partial(_hgrn_kernel, tb=tb, nh=nh),
        grid=(ng, S // tb),
        in_specs=[blk(0), blk(1), blk(2), blk(3), row, row, row,
                  pl.BlockSpec((_ROW_MASKS, HG_CHUNK, HG_DIM), lambda h, c: (0, 0, 0)),
                  pl.BlockSpec((_SCORE_MASKS, HG_CHUNK, HG_CHUNK), lambda h, c: (0, 0, 0))],
        out_specs=pl.BlockSpec((tb, w), lambda h, c: (c, h)),
        out_shape=jax.ShapeDtypeStruct((S, HG_WIDTH), BF16),
        scratch_shapes=[pltpu.VMEM((nh, HG_DIM, HG_DIM), F32), pltpu.VMEM((tb, w), F32),
                        pltpu.VMEM((tb, w), F32), pltpu.VMEM((tb, w), F32)],
        compiler_params=_cparams(("arbitrary", "arbitrary")),
        name="hgrn",
    )(hz, hz, hz, hz, loglb, log1mlb, gnorm, jnp.asarray(rm), jnp.asarray(sm))


_MERGE_PART = 512


def _merge_kernel(at_ref, hg_ref, ga_ref, gh_ref, wa_ref, wh_ref, wo_ref, x_ref, gt_ref, gp_ref, o_ref):
    at, hg = at_ref[...], hg_ref[...]
    d = wo_ref.shape[1]
    parts = [slice(c0, c0 + _MERGE_PART) for c0 in range(0, d, _MERGE_PART)]
    branch = [(_dot(at, wa_ref[0, :, cs]), _dot(hg, wh_ref[0, :, cs])) for cs in parts]
    y = None
    for cs, (ma, mh) in zip(parts, branch):
        mixed = ga_ref[:, cs].astype(F32) * ma + gh_ref[:, cs].astype(F32) * mh
        yp = _dot(mixed.astype(BF16), wo_ref[0, cs, :])
        y = yp if y is None else y + yp
    ms = jnp.mean(y * y, axis=-1, keepdims=True)
    yn = y * lax.rsqrt(ms + NORM_EPS) * gp_ref[...]
    o_ref[...] = x_ref[...] + gt_ref[...] * yn


def _merge(attn, hg, mg, wa, wh, wo, layer, x, gt, gpost):
    S, D = x.shape
    tm = 256
    const = lambda shape: pl.BlockSpec((1,) + shape, lambda m: (layer, 0, 0), pipeline_mode=pl.Buffered(1))
    row = pl.BlockSpec((1, D), lambda m: (0, 0))
    return pl.pallas_call(
        _merge_kernel,
        grid=(S // tm,),
        in_specs=[
            pl.BlockSpec((tm, A_WIDTH), lambda m: (m, 0)),
            pl.BlockSpec((tm, HG_WIDTH), lambda m: (m, 0)),
            pl.BlockSpec((tm, D), lambda m: (m, 0)),
            pl.BlockSpec((tm, D), lambda m: (m, 1)),
            const((A_WIDTH, D)), const((HG_WIDTH, D)), const((D, D)),
            pl.BlockSpec((tm, D), lambda m: (m, 0)),
            row, row,
        ],
        out_specs=pl.BlockSpec((tm, D), lambda m: (m, 0)),
        out_shape=jax.ShapeDtypeStruct((S, D), F32),
        compiler_params=_cparams(("arbitrary",)),
        name="merge",
    )(attn, hg, mg, mg, wa, wh, wo, x, gt, gpost.reshape(1, D))


_HALO = 16
_FF_SUB = 256


def _ffn_kernel(x_ref, xh_ref, g_ref, sc_ref, sh_ref, wg_ref, wv_ref, cwg_ref, cwv_ref, cbg_ref, cbv_ref,
                wd_ref, gt_ref, gp_ref, o_ref, h_scr, ug_scr, uv_scr, *, tm):
    m = pl.program_id(0)
    f = pl.program_id(1)

    def norm_mod(x):
        ms = jnp.mean(x * x, axis=-1, keepdims=True)
        y = x * lax.rsqrt(ms + NORM_EPS) * g_ref[...]
        return (y * (1.0 + sc_ref[...]) + sh_ref[...]).astype(BF16)

    @pl.when(f == 0)
    def _():
        h_scr[0:_HALO, :] = jnp.where(m > 0, norm_mod(xh_ref[...]), jnp.zeros((), BF16))
        h_scr[_HALO:_HALO + tm, :] = norm_mod(x_ref[...])
        o_ref[...] = jnp.zeros_like(o_ref)

    tf = wg_ref.shape[2]
    subs = [slice(s * _FF_SUB, (s + 1) * _FF_SUB) for s in range(tf // _FF_SUB)]
    h = h_scr[...]
    ups = [(_dot(h, wg_ref[0, :, cs]), _dot(h, wv_ref[0, :, cs])) for cs in subs]

    def conv(u_scr, cw_ref, cb_ref, cs):
        out = cb_ref[:, cs] + cw_ref[0:1, cs] * u_scr[pl.ds(_HALO - 2, tm), cs]
        out = out + cw_ref[1:2, cs] * u_scr[pl.ds(_HALO - 1, tm), cs]
        return out + cw_ref[2:3, cs] * u_scr[pl.ds(_HALO, tm), cs]

    acc = o_ref[...]
    for cs, (ug, uv) in zip(subs, ups):
        ug_scr[:, cs] = ug
        uv_scr[:, cs] = uv
        cg = conv(ug_scr, cwg_ref, cbg_ref, cs)
        cv = conv(uv_scr, cwv_ref, cbv_ref, cs)
        act = (cg * jax.nn.sigmoid(cg) * cv).astype(BF16)
        acc = acc + _dot(act, wd_ref[0, cs, :])
    o_ref[...] = acc

    @pl.when(f == pl.num_programs(1) - 1)
    def _():
        y = o_ref[...]
        ms = jnp.mean(y * y, axis=-1, keepdims=True)
        yn = y * lax.rsqrt(ms + NORM_EPS) * gp_ref[...]
        o_ref[...] = x_ref[...] + gt_ref[...] * yn


def _ffn(x, gpre, sc, sh, wup, convw, convb, wdown, layer, gt, gpost):
    S, D = x.shape
    F = wdown.shape[1]
    tm = 512 if S % 512 == 0 else S
    tf = 512
    nf = F // tf
    hb = tm // _HALO
    row = pl.BlockSpec((1, D), lambda m, f: (0, 0))
    return pl.pallas_call(
        functools.partial(_ffn_kernel, tm=tm),
        grid=(S // tm, nf),
        in_specs=[
            pl.BlockSpec((tm, D), lambda m, f: (m, 0)),
            pl.BlockSpec((_HALO, D), lambda m, f: (jnp.maximum(m * hb - 1, 0), 0)),
            row, row, row,
            pl.BlockSpec((1, D, tf), lambda m, f: (layer, 0, f)),
            pl.BlockSpec((1, D, tf), lambda m, f: (layer, 0, nf + f)),
            pl.BlockSpec((CONV_WIDTH, tf), lambda m, f: (0, f)),
            pl.BlockSpec((CONV_WIDTH, tf), lambda m, f: (0, nf + f)),
            pl.BlockSpec((1, tf), lambda m, f: (0, f)),
            pl.BlockSpec((1, tf), lambda m, f: (0, nf + f)),
            pl.BlockSpec((1, tf, D), lambda m, f: (layer, f, 0)),
            row, row,
        ],
        out_specs=pl.BlockSpec((tm, D), lambda m, f: (m, 0)),
        out_shape=jax.ShapeDtypeStruct((S, D), F32),
        scratch_shapes=[pltpu.VMEM((tm + _HALO, D), BF16),
                        pltpu.VMEM((tm + _HALO, tf), F32), pltpu.VMEM((tm + _HALO, tf), F32)],
        compiler_params=_cparams(("arbitrary", "arbitrary")),
        name="ffn",
    )(x, x, gpre.reshape(1, D), sc, sh, wup, wup, convw, convw, convb.reshape(1, -1), convb.reshape(1, -1),
      wdown, gt, gpost.reshape(1, D))


def _rope_tables(positions):
    inv_freq = 1.0 / (ROPE_THETA ** (jnp.arange(0, HEAD_DIM, 2, dtype=F32) / HEAD_DIM))
    ang = positions.astype(F32)[:, None] * inv_freq
    cos, sin = jnp.cos(ang), jnp.sin(ang)
    return jnp.concatenate([cos, cos], axis=-1), jnp.concatenate([-sin, sin], axis=-1)


def kernel(x, c, positions, w_ada, b_ada, g_pre_mix, w_in, pe_kc, w_kc, pe_vc, w_vc, lb_logits, g_hg_norm,
           w_br_attn, w_br_hgrn, w_out, g_post_mix, g_pre_ffn, w_up, conv_w, conv_b, w_down, g_post_ffn):
    B, S, D = x.shape
    assert B == 1, "kernel is written for one sequence"
    L = w_ada.shape[0]
    xs = x[0]
    cosf, sinf = _rope_tables(positions[0])
    scale = HEAD_DIM ** -0.5 * np.log2(np.e)
    tables = (cosf * scale, sinf * scale, cosf, sinf)
    lb_cum = jnp.cumsum(jax.nn.softmax(lb_logits.astype(F32), axis=0), axis=0)
    lower = lb_cum - lb_cum[0:1]
    log_lb = jnp.log(lower)
    log_1m = jnp.log1p(-lower)
    ada = _ada_all(c, w_ada, b_ada)

    w_in_t = jnp.swapaxes(w_in, 1, 2)
    tail_tn = 1024
    hg_roles = (("id",) * (tail_tn // LANES),) * (4 * HG_WIDTH // tail_tn)
    mg_roles = (("sg",) * (tail_tn // LANES),) * (2 * D // tail_tn)

    wa_b, wh_b, wo_b = w_br_attn.astype(BF16), w_br_hgrn.astype(BF16), w_out.astype(BF16)
    wup_b, wdown_b = w_up.astype(BF16), w_down.astype(BF16)

    for l in range(L):
        sh1, sc1, gt1, sh2, sc2, gt2 = [ada[l, :, j * D:(j + 1) * D] for j in range(N_ADA)]

        h = _prenorm(xs, g_pre_mix[l], sc1, sh1)
        a = _proj(h, w_in_t, l, 0, A_ROLES, BF16, tn=A_TILE, tables=tables, name="proj_attn")
        hz = _proj(h, w_in_t, l, A_COLS, hg_roles, F32, tn=tail_tn, name="proj_hgrn")
        mg = _proj(h, w_in_t, l, A_COLS + 4 * HG_WIDTH, mg_roles, BF16, tn=tail_tn, name="proj_merge_gate")

        cmp_kv = _compress(a, jnp.stack([pe_kc[l], pe_vc[l]]), jnp.stack([w_kc[l], w_vc[l]]))
        attn = _nsa(a, cmp_kv)
        hg = _hgrn(hz, log_lb[l:l + 1], log_1m[l:l + 1], g_hg_norm[l:l + 1])
        xs = _merge(attn, hg, mg, wa_b, wh_b, wo_b, l, xs, gt1, g_post_mix[l])

        xs = _ffn(xs, g_pre_ffn[l], sc2, sh2, wup_b, conv_w[l], conv_b[l], wdown_b, l, gt2, g_post_ffn[l])
    return xs[None]
```

```python
import functools

import jax
import jax.numpy as jnp
import numpy as np
from jax import lax
from jax.experimental import pallas as pl
from jax.experimental.pallas import tpu as pltpu

F32 = jnp.float32
BF16 = jnp.bfloat16

A_HEADS = 8
A_KV_GROUPS = 2
A_HPG = A_HEADS // A_KV_GROUPS
HEAD_DIM = 128
A_WIDTH = A_HEADS * HEAD_DIM
A_KV_WIDTH = A_KV_GROUPS * HEAD_DIM
CMP_BLOCK = 32
CMP_STRIDE = 16
SEL_BLOCK = 64
SEL_TOPK = 16
WINDOW = 512
ROPE_THETA = 10000.0
HG_HEADS = 8
HG_DIM = 128
HG_WIDTH = HG_HEADS * HG_DIM
HG_CHUNK = 64
CONV_WIDTH = 3
NORM_EPS = 1e-6
N_ADA = 6
A_COLS = A_WIDTH + 6 * A_KV_WIDTH + 3 * A_HEADS

LANES = 128
SEL_SLOTS = LANES
KEY_TILE = 256
NEG_BIG = -1e30
VMEM_LIMIT = 56 * 1024 * 1024


def _cparams(sem):
    return pltpu.CompilerParams(dimension_semantics=sem, vmem_limit_bytes=VMEM_LIMIT)


def _split3(a):
    hi = a.astype(BF16)
    r1 = a - hi.astype(F32)
    mid = r1.astype(BF16)
    lo = (r1 - mid.astype(F32)).astype(BF16)
    return hi, mid, lo


def _dot(a, b):
    return jnp.dot(a, b, preferred_element_type=F32)


def _dot_nt(a, b):
    return lax.dot_general(a, b, (((1,), (1,)), ((), ())), preferred_element_type=F32)


def _dot_tn(a, b):
    return lax.dot_general(a, b, (((0,), (0,)), ((), ())), preferred_element_type=F32)


def _ada_kernel(c_ref, w_ref, b_ref, o_ref):
    c = c_ref[...]
    ca = c * jax.nn.sigmoid(c)
    o_ref[0] = jnp.sum(ca * w_ref[0], axis=0, keepdims=True) + b_ref[0]


def _ada_all(c, w_ada, b_ada):
    L, D, N = w_ada.shape
    tn = 1024
    return pl.pallas_call(
        _ada_kernel,
        grid=(L, N // tn),
        in_specs=[
            pl.BlockSpec((D, 1), lambda l, n: (0, 0)),
            pl.BlockSpec((1, D, tn), lambda l, n: (l, 0, n)),
            pl.BlockSpec((1, 1, tn), lambda l, n: (l, 0, n)),
        ],
        out_specs=pl.BlockSpec((1, 1, tn), lambda l, n: (l, 0, n)),
        out_shape=jax.ShapeDtypeStruct((L, 1, N), F32),
        compiler_params=_cparams(("arbitrary", "arbitrary")),
        name="ada",
    )(c.reshape(D, 1), w_ada, b_ada.reshape(L, 1, N))


def _prenorm_kernel(x_ref, g_ref, sc_ref, sh_ref, o_ref):
    x = x_ref[...]
    ms = jnp.mean(x * x, axis=-1, keepdims=True)
    y = x * lax.rsqrt(ms + NORM_EPS) * g_ref[...]
    o_ref[...] = (y * (1.0 + sc_ref[...]) + sh_ref[...]).astype(o_ref.dtype)


def _prenorm(x, g, sc, sh):
    S, D = x.shape
    tm = 512
    row = pl.BlockSpec((1, D), lambda m: (0, 0))
    return pl.pallas_call(
        _prenorm_kernel,
        grid=(S // tm,),
        in_specs=[pl.BlockSpec((tm, D), lambda m: (m, 0)), row, row, row],
        out_specs=pl.BlockSpec((tm, D), lambda m: (m, 0)),
        out_shape=jax.ShapeDtypeStruct((S, D), BF16),
        compiler_params=_cparams(("arbitrary",)),
        name="prenorm",
    )(x, g.reshape(1, D), sc, sh)


def _rope(t, cosf, sinf):
    return t * cosf + pltpu.roll(t, HEAD_DIM // 2, axis=1) * sinf


_PROJ_PART = 256


def _proj_kernel(h_ref, w_ref, *rest, roles, tn):
    o_ref, w_scr = rest[-2:]
    tabs = rest[:-2]
    n = pl.program_id(0)

    @pl.when(pl.program_id(1) == 0)
    def _():
        w_scr[...] = w_ref[0].astype(BF16)

    def emit(slabs):
        h = h_ref[...]
        per = _PROJ_PART // LANES
        parts = [_dot_nt(h, w_scr[c0:c0 + _PROJ_PART, :]) for c0 in range(0, tn, _PROJ_PART)]
        for pi, acc in enumerate(parts):
            for s in range(per):
                kind = slabs[pi * per + s]
                t = acc[:, s * LANES:(s + 1) * LANES]
                if kind == "sg":
                    t = jax.nn.sigmoid(t)
                elif kind != "id":
                    cos_ref, sin_ref = tabs[0:2] if kind == "rq" else tabs[-2:]
                    t = _rope(t, cos_ref[...], sin_ref[...])
                c = (pi * per + s) * LANES
                o_ref[:, c:c + LANES] = t.astype(o_ref.dtype)

    kinds = sorted(set(roles))
    if len(kinds) == 1:
        emit(kinds[0])
        return
    for kind in kinds:
        hit = functools.reduce(jnp.logical_or, [n == t for t, r in enumerate(roles) if r == kind])
        pl.when(hit)(functools.partial(emit, kind))


def _proj(h, w_t, layer, row0, roles, out_dtype, *, tn, tables=(), name="proj"):
    S, D = h.shape
    tm = 1024 if S % 1024 == 0 else S
    nt = len(roles)
    tab = pl.BlockSpec((tm, HEAD_DIM), lambda n, m: (m, 0))
    return pl.pallas_call(
        functools.partial(_proj_kernel, roles=tuple(roles), tn=tn),
        grid=(nt, S // tm),
        in_specs=[pl.BlockSpec((tm, D), lambda n, m: (m, 0)),
                  pl.BlockSpec((pl.Element(1), pl.Element(tn), pl.Element(D)),
                               lambda n, m: (layer, pl.multiple_of(row0 + n * tn, 8), 0))]
                 + [tab] * len(tables),
        out_specs=pl.BlockSpec((tm, tn), lambda n, m: (m, n)),
        out_shape=jax.ShapeDtypeStruct((S, nt * tn), out_dtype),
        scratch_shapes=[pltpu.VMEM((tn, D), BF16)],
        compiler_params=_cparams(("arbitrary", "arbitrary")),
        name=name,
    )(h, w_t, *tables)


A_TILE = 512
_KV_TILE_ROLES = ("rk",) * A_KV_GROUPS + ("id",) * A_KV_GROUPS
A_ROLES = ((("rq",) * (A_TILE // LANES),) * (A_WIDTH // A_TILE) + (_KV_TILE_ROLES,) * 3
           + (("sg",) * (A_TILE // LANES),))
_KV_BLOCK0 = A_WIDTH // LANES
BLK_KC, BLK_VC, BLK_KS, BLK_VS, BLK_KW, BLK_VW = [_KV_BLOCK0 + j * A_KV_GROUPS for j in range(6)]
BLK_GATE = _KV_BLOCK0 + 6 * A_KV_GROUPS


def _compress_kernel(t_ref, pe_ref, w_ref, o_ref, t32_ref, *, n_chunk):
    half = CMP_BLOCK // 2
    t32_ref[...] = t_ref[...].astype(F32)
    top, bot = [], []
    for j in range(half):
        a = t32_ref[pl.ds(j, n_chunk, stride=CMP_STRIDE), :]
        top.append(a + pe_ref[0, j:j + 1, :])
        bot.append(a + pe_ref[0, half + j:half + j + 1, :])
    a_top = jnp.concatenate(top, axis=1)
    a_bot = jnp.concatenate(bot, axis=1)
    kw = half * HEAD_DIM
    w = w_ref[0]
    w_top = w[:kw].astype(BF16)
    w_bot = w[kw:].astype(BF16)
    y_top = jnp.zeros((n_chunk, HEAD_DIM), F32)
    y_bot = jnp.zeros((n_chunk, HEAD_DIM), F32)
    for piece in _split3(a_top):
        y_top = y_top + _dot(piece, w_top)
    for piece in _split3(a_bot):
        y_bot = y_bot + _dot(piece, w_bot)
    o_ref[0, 0] = y_top + pltpu.roll(y_bot, n_chunk - 1, axis=0)


def _compress(a, pe2, w2):
    S = a.shape[0]
    n_chunk = S // CMP_STRIDE
    G = A_KV_GROUPS
    return pl.pallas_call(
        functools.partial(_compress_kernel, n_chunk=n_chunk),
        grid=(2, G),
        in_specs=[
            pl.BlockSpec((S, HEAD_DIM), lambda kv, g: (0, BLK_KC + kv * G + g)),
            pl.BlockSpec((1, CMP_BLOCK, HEAD_DIM), lambda kv, g: (kv, 0, 0)),
            pl.BlockSpec((1, CMP_BLOCK * HEAD_DIM, HEAD_DIM), lambda kv, g: (kv, 0, 0)),
        ],
        out_specs=pl.BlockSpec((1, 1, n_chunk, HEAD_DIM), lambda kv, g: (kv, g, 0, 0)),
        out_shape=jax.ShapeDtypeStruct((2, G, n_chunk, HEAD_DIM), F32),
        scratch_shapes=[pltpu.VMEM((S, HEAD_DIM), F32)],
        compiler_params=_cparams(("arbitrary", "arbitrary")),
        name="compress",
    )(a, pe2, w2)


NSA_QB = 256
SEL_BODY = 8
SEL_GROUP = 4
_N_FORCED = 3


def _nsa_kernel(q_ref, ks_ref, kw_ref, vs_ref, vw_ref, kc_ref, vc_ref, ovt_ref, gate_ref,
                o_ref, kaug_ref, vt_ref, vwt_ref, vct_ref, acc_ref, m_ref, l_ref, *, seq):
    grp = pl.program_id(0)
    i = pl.program_id(1)
    t0 = i * NSA_QB
    n_cmp = seq // CMP_STRIDE
    cols = A_HPG * NSA_QB

    @pl.when(i == 0)
    def _():
        def build(r, c):
            r0 = pl.multiple_of(r * KEY_TILE, KEY_TILE)
            kaug_ref[r, :, 0:HEAD_DIM] = ks_ref[pl.ds(r0, KEY_TILE), :]
            blk = (r0 + lax.broadcasted_iota(jnp.int32, (KEY_TILE, SEL_SLOTS), 0)) // SEL_BLOCK
            slot = lax.broadcasted_iota(jnp.int32, (KEY_TILE, SEL_SLOTS), 1)
            kaug_ref[r, :, HEAD_DIM:HEAD_DIM + SEL_SLOTS] = jnp.where(blk == slot, 1.0, 0.0).astype(BF16)
            vt_ref[r] = vs_ref[pl.ds(r0, KEY_TILE), :].astype(F32).T.astype(BF16)
            vwt_ref[r] = vw_ref[pl.ds(r0, KEY_TILE), :].astype(F32).T.astype(BF16)
            return c
        lax.fori_loop(0, seq // KEY_TILE, build, 0)
        vct_ref[...] = vc_ref[0, 0].T.astype(BF16)

    qt = jnp.concatenate(
        [q_ref[:, h * HEAD_DIM:(h + 1) * HEAD_DIM].astype(F32).T.astype(BF16) for h in range(A_HPG)],
        axis=1)
    tq = t0 + lax.broadcasted_iota(jnp.int32, (1, cols), 1) % NSA_QB

    def softmax0(s):
        m = jnp.max(s, axis=0, keepdims=True)
        p = jnp.exp2(s - m)
        return p, jnp.sum(p, axis=0, keepdims=True)

    s_c = _dot(kc_ref[0, 0].astype(BF16), qt)
    nrow = lax.broadcasted_iota(jnp.int32, (n_cmp, 1), 0)
    vis = (nrow * CMP_STRIDE + (CMP_BLOCK - 1)) <= tq
    p_c, den = softmax0(jnp.where(vis, s_c, NEG_BIG))
    p_c = p_c * jnp.where(tq >= CMP_BLOCK - 1, 1.0 / den, 0.0)
    o_ct = _dot(vct_ref[...], p_c.astype(BF16))

    psum = p_c[:, 0:NSA_QB]
    for h in range(1, A_HPG):
        psum = psum + p_c[:, h * NSA_QB:(h + 1) * NSA_QB]
    ovt = ovt_ref[...]
    imp_t = jnp.zeros((SEL_SLOTS, NSA_QB), F32)
    for piece in _split3(psum):
        imp_t = imp_t + _dot(ovt, piece)
    slot_i = lax.broadcasted_iota(jnp.int32, (SEL_SLOTS, NSA_QB), 0)
    tq_l = t0 + lax.broadcasted_iota(jnp.int32, (SEL_SLOTS, NSA_QB), 1)
    cur = tq_l // SEL_BLOCK
    forced = jnp.where(slot_i == 0, 1, 0) + jnp.where(slot_i == cur, 1, 0) + jnp.where(slot_i == cur - 1, 1, 0)
    valid = slot_i * SEL_BLOCK <= tq_l
    key = jnp.where(forced > 0, -3e38, jnp.where(valid, imp_t, NEG_BIG))
    slot_f = slot_i.astype(F32)
    bias_t = jnp.where(forced > 0, 0.0, NEG_BIG)
    for _ in range(SEL_TOPK - _N_FORCED):
        mx = jnp.max(key, axis=0, keepdims=True)
        first = jnp.min(jnp.where(key == mx, slot_f, 1e9), axis=0, keepdims=True)
        pick = slot_f == first
        bias_t = jnp.where(pick, 0.0, bias_t)
        key = jnp.where(pick, -3e38, key)

    wlen = WINDOW + NSA_QB
    w0 = pl.multiple_of(jnp.maximum(t0 - WINDOW, 0), KEY_TILE)
    wt = jnp.maximum(i - WINDOW // KEY_TILE, 0)
    s_w = _dot(kw_ref[pl.ds(w0, wlen), :], qt)
    dist = tq - (w0 + lax.broadcasted_iota(jnp.int32, (wlen, 1), 0))
    in_window = dist.astype(jnp.uint32) < WINDOW
    p_w, l_w = softmax0(jnp.where(in_window, s_w, NEG_BIG))
    p_w = p_w.astype(BF16)
    o_wt = _dot(vwt_ref[wt], p_w[0:KEY_TILE])
    for k in range(1, wlen // KEY_TILE):
        o_wt = o_wt + _dot(vwt_ref[wt + k], p_w[k * KEY_TILE:(k + 1) * KEY_TILE])
    o_wt = o_wt * (1.0 / l_w)

    qt_aug = jnp.concatenate([qt, jnp.concatenate([bias_t.astype(BF16)] * A_HPG, axis=1)], axis=0)
    acc_ref[...] = jnp.zeros_like(acc_ref)
    m_ref[...] = jnp.full(m_ref.shape, 0.5 * NEG_BIG, F32)
    l_ref[...] = jnp.zeros_like(l_ref)

    def sel_body(base, causal):
        n = len(causal)
        m, l = m_ref[...], l_ref[...]
        gsz = min(SEL_GROUP, max(n // 2, 1))
        groups = [list(range(g0, min(g0 + gsz, n))) for g0 in range(0, n, gsz)]

        def scores(ks):
            out = {}
            for k in ks:
                sk = _dot(kaug_ref[base + k], qt_aug)
                if causal[k]:
                    kpos = (base + k) * KEY_TILE + lax.broadcasted_iota(jnp.int32, (KEY_TILE, 1), 0)
                    sk = jnp.where(kpos <= tq, sk, NEG_BIG)
                out[k] = sk
            return out

        s = scores(groups[0])
        for gi, ks in enumerate(groups):
            if gi + 1 < len(groups):
                s.update(scores(groups[gi + 1]))
            m_new = m
            for k in ks:
                m_new = jnp.maximum(m_new, jnp.max(s[k], axis=0, keepdims=True))
            alpha = jnp.exp2(m - m_new)
            l = alpha * l
            pv = None
            for k in ks:
                p = jnp.exp2(s.pop(k) - m_new)
                l = l + jnp.sum(p, axis=0, keepdims=True)
                d = _dot(vt_ref[base + k], p.astype(BF16))
                pv = d if pv is None else pv + d
            acc_ref[...] = alpha * acc_ref[...] + pv
            m = m_new
        m_ref[...] = m
        l_ref[...] = l

    n_bulk = i // SEL_BODY
    rem = i % SEL_BODY

    def bulk(b, c):
        sel_body(b * SEL_BODY, (False,) * SEL_BODY)
        return c
    lax.fori_loop(0, n_bulk, bulk, 0)

    base = n_bulk * SEL_BODY
    part = SEL_BODY // 2
    while part >= 2:
        pl.when(rem & part != 0)(functools.partial(sel_body, base, (False,) * part))
        base = base + (rem & part)
        part //= 2

    @pl.when(rem % 2 == 1)
    def _():
        sel_body(i - 1, (False, True))

    @pl.when(rem % 2 == 0)
    def _():
        sel_body(i, (True,))

    o_st = acc_ref[...] * (1.0 / l_ref[...])

    gates_t = gate_ref[...].astype(F32).T
    per_group = 3 * A_HPG

    def gate(h, b):
        r = 3 * h + b
        return jnp.where(grp == 0, gates_t[r:r + 1], gates_t[per_group + r:per_group + r + 1])

    for h in range(A_HPG):
        c = slice(h * NSA_QB, (h + 1) * NSA_QB)
        o = gate(h, 0) * o_ct[:, c] + gate(h, 1) * o_st[:, c] + gate(h, 2) * o_wt[:, c]
        o_ref[:, h * HEAD_DIM:(h + 1) * HEAD_DIM] = o.T.astype(o_ref.dtype)


def _overlap_matrix(seq):
    n_chunk = seq // CMP_STRIDE
    cs = np.arange(n_chunk)[:, None] * CMP_STRIDE
    ss = np.arange(SEL_SLOTS)[None, :] * SEL_BLOCK
    ov = np.maximum(np.minimum(cs + CMP_BLOCK, ss + SEL_BLOCK) - np.maximum(cs, ss), 0) / CMP_BLOCK
    n_cmp = (seq - CMP_BLOCK) // CMP_STRIDE + 1
    ov[n_cmp:] = 0.0
    ov[:, seq // SEL_BLOCK:] = 0.0
    return ov.astype(np.float32)


def _nsa(a, cmp_kv):
    S = a.shape[0]
    G = A_KV_GROUPS
    n_tiles = S // KEY_TILE
    assert S // SEL_BLOCK <= SEL_SLOTS and S % KEY_TILE == 0 and S >= WINDOW + NSA_QB
    assert NSA_QB == KEY_TILE and G == 2
    n_chunk = S // CMP_STRIDE
    ovt = jnp.asarray(_overlap_matrix(S).T, BF16)
    col = lambda b: pl.BlockSpec((S, HEAD_DIM), lambda g, i, b=b: (0, b + g))
    return pl.pallas_call(
        functools.partial(_nsa_kernel, seq=S),
        grid=(G, S // NSA_QB),
        in_specs=[
            pl.BlockSpec((NSA_QB, A_HPG * HEAD_DIM), lambda g, i: (i, g)),
            col(BLK_KS), col(BLK_KW), col(BLK_VS), col(BLK_VW),
            pl.BlockSpec((1, 1, n_chunk, HEAD_DIM), lambda g, i: (0, g, 0, 0)),
            pl.BlockSpec((1, 1, n_chunk, HEAD_DIM), lambda g, i: (1, g, 0, 0)),
            pl.BlockSpec((SEL_SLOTS, n_chunk), lambda g, i: (0, 0)),
            pl.BlockSpec((NSA_QB, LANES), lambda g, i: (i, BLK_GATE)),
        ],
        out_specs=pl.BlockSpec((NSA_QB, A_HPG * HEAD_DIM), lambda g, i: (i, g)),
        out_shape=jax.ShapeDtypeStruct((S, A_WIDTH), BF16),
        scratch_shapes=[pltpu.VMEM((n_tiles, KEY_TILE, HEAD_DIM + SEL_SLOTS), BF16),
                        pltpu.VMEM((n_tiles, HEAD_DIM, KEY_TILE), BF16),
                        pltpu.VMEM((n_tiles, HEAD_DIM, KEY_TILE), BF16),
                        pltpu.VMEM((HEAD_DIM, n_chunk), BF16),
                        pltpu.VMEM((HEAD_DIM, A_HPG * NSA_QB), F32),
                        pltpu.VMEM((1, A_HPG * NSA_QB), F32),
                        pltpu.VMEM((1, A_HPG * NSA_QB), F32)],
        compiler_params=_cparams(("arbitrary", "arbitrary")),
        name="nsa",
    )(a, a, a, a, a, cmp_kv, cmp_kv, ovt, a)


_ROW_MASKS = 18
_SCORE_MASKS = 10
HG_HEADS_PER_STEP = 8


def _hgrn_tables():
    t = np.arange(HG_CHUNK)
    p4, i4, i16 = t % 4, (t // 4) % 4, t // 16
    conds = [p4 >= 1, p4 >= 2, p4 >= 3, p4 <= 2, p4 <= 1, p4 == 0,
             i4 >= 1, i4 >= 2, i4 >= 3, i4 <= 2, i4 <= 1, i4 == 0,
             i16 >= 1, i16 >= 2, i16 >= 3, i16 <= 2, i16 <= 1, i16 == 0]
    rm = np.stack([np.broadcast_to(c[:, None], (HG_CHUNK, HG_DIM)) for c in conds]).astype(np.float32)
    tt, ss = t[:, None], t[None, :]
    lag16 = tt // 16 - ss // 16
    lag4 = tt // 4 - ss // 4
    same16, same4 = lag16 == 0, lag4 == 0
    sm = [lag16 == 1, lag16 == 2, lag16 == 3,
          same16 & (lag4 == 1), same16 & (lag4 == 2), same16 & (lag4 == 3),
          same4 & (tt - ss == 0), same4 & (tt - ss == 1), same4 & (tt - ss == 2), same4 & (tt - ss == 3)]
    return rm, np.stack(sm).astype(np.float32)


def _hgrn_kernel(q_ref, f_ref, i_ref, og_ref, loglb_ref, log1m_ref, gn_ref, rm_ref, sm_ref,
                 o_ref, st_ref, g_scr, k_scr, o_scr, *, tb, nh):
    C = HG_CHUNK
    heads = [slice(h * HG_DIM, (h + 1) * HG_DIM) for h in range(nh)]

    @pl.when(pl.program_id(1) == 0)
    def _():
        st_ref[...] = jnp.zeros_like(st_ref)

    x = f_ref[...]
    log_sig = jnp.minimum(x, 0.0) - jnp.log(1.0 + jnp.exp(-jnp.abs(x)))
    a = loglb_ref[...]
    c = log1m_ref[...] + log_sig
    log_f = jnp.maximum(a, c) + jnp.log(1.0 + jnp.exp(-jnp.abs(a - c)))
    g_scr[...] = log_f
    k_scr[...] = 1.0 - jnp.exp(log_f)

    def rmask(n):
        return jnp.concatenate([rm_ref[n]] * nh, axis=1)

    def roll(v, s):
        return pltpu.roll(v, s % C, axis=0)

    def b16(z):
        return z.astype(BF16)

    def chunk(ci, carry):
        r0 = pl.multiple_of(ci * C, C)
        g = g_scr[pl.ds(r0, C), :]
        q = q_ref[pl.ds(r0, C), :]
        k = k_scr[pl.ds(r0, C), :]
        v = i_ref[pl.ds(r0, C), :].astype(BF16)
        pre4 = g + rmask(0) * roll(g, 1) + rmask(1) * roll(g, 2) + rmask(2) * roll(g, 3)
        suf4 = rmask(3) * roll(g, -1) + rmask(4) * roll(g, -2) + rmask(5) * roll(g, -3)
        tot4 = pre4 + suf4
        a1, a2, a3 = roll(tot4, 4), roll(tot4, 8), roll(tot4, 12)
        pre16 = pre4 + rmask(6) * a1 + rmask(7) * a2 + rmask(8) * a3
        suf16 = suf4 + rmask(9) * roll(tot4, -4) + rmask(10) * roll(tot4, -8) + rmask(11) * roll(tot4, -12)
        tot16 = pre16 + suf16
        d1, d2, d3 = roll(tot16, 16), roll(tot16, 32), roll(tot16, 48)
        pre64 = pre16 + rmask(12) * d1 + rmask(13) * d2 + rmask(14) * d3
        suf64 = suf16 + rmask(15) * roll(tot16, -16) + rmask(16) * roll(tot16, -32) + rmask(17) * roll(tot16, -48)
        tot64 = pre64 + suf64
        g1, g2 = roll(g, 1), roll(g, 2)

        q_a = q * jnp.exp(pre16)
        lhs_a = [b16(q_a), b16(q_a * jnp.exp(d1)), b16(q_a * jnp.exp(d1 + d2))]
        k_a = b16(k * jnp.exp(suf16))
        q_b = q * jnp.exp(pre4)
        lhs_b = [b16(q_b), b16(q_b * jnp.exp(a1)), b16(q_b * jnp.exp(a1 + a2))]
        k_b = b16(k * jnp.exp(suf4))
        q_c = q * jnp.exp(g)
        lhs_c = [b16(q), b16(q_c), b16(q_c * jnp.exp(g1)), b16(q_c * jnp.exp(g1 + g2))]
        k_c = b16(k)
        q_s = b16(q * jnp.exp(pre64))
        k_s = b16(k * jnp.exp(suf64))
        decay = jnp.exp(tot64[0:1, :])

        def level(lhs, rhs):
            return [_dot_nt(jnp.concatenate([z[:, hh] for z in lhs], axis=0), rhs[:, hh]) for hh in heads]

        s_a, s_b, s_c = level(lhs_a, k_a), level(lhs_b, k_b), level(lhs_c, k_c)
        o_inter = [_dot_nt(q_s[:, hh], b16(st_ref[h])) for h, hh in enumerate(heads)]
        kv = [_dot_tn(v[:, hh], k_s[:, hh]) for hh in heads]
        for h, hh in enumerate(heads):
            sa, sb, sc = s_a[h], s_b[h], s_c[h]
            scores = (sm_ref[0] * sa[0:C] + sm_ref[1] * sa[C:2 * C] + sm_ref[2] * sa[2 * C:3 * C]
                      + sm_ref[3] * sb[0:C] + sm_ref[4] * sb[C:2 * C] + sm_ref[5] * sb[2 * C:3 * C]
                      + sm_ref[6] * sc[0:C] + sm_ref[7] * sc[C:2 * C] + sm_ref[8] * sc[2 * C:3 * C]
                      + sm_ref[9] * sc[3 * C:4 * C])
            o_scr[pl.ds(r0, C), hh] = _dot(b16(scores), v[:, hh]) + o_inter[h]
            st_ref[h] = st_ref[h] * decay[:, hh] + kv[h]
        return carry

    lax.fori_loop(0, tb // C, chunk, 0)

    for hh in heads:
        o = o_scr[:, hh]
        ms = jnp.mean(o * o, axis=-1, keepdims=True)
        og = og_ref[:, hh]
        y = o * lax.rsqrt(ms + NORM_EPS) * gn_ref[:, hh] * (og * jax.nn.sigmoid(og))
        o_ref[:, hh] = y.astype(o_ref.dtype)


def _hgrn(hz, loglb, log1mlb, gnorm):
    S = hz.shape[0]
    tb = 512 if S % 512 == 0 else S
    nh = HG_HEADS_PER_STEP
    ng = HG_HEADS // nh
    w = nh * HG_DIM
    rm, sm = _hgrn_tables()
    blk = lambda part: pl.BlockSpec((tb, w), lambda h, c, part=part: (c, part * ng + h))
    row = pl.BlockSpec((1, w), lambda h, c: (0, h))
    return pl.pallas_call(
        functools.partial(_hgrn_kernel, tb=tb, nh=nh),
        grid=(ng, S // tb),
        in_specs=[blk(0), blk(1), blk(2), blk(3), row, row, row,
                  pl.BlockSpec((_ROW_MASKS, HG_CHUNK, HG_DIM), lambda h, c: (0, 0, 0)),
                  pl.BlockSpec((_SCORE_MASKS, HG_CHUNK, HG_CHUNK), lambda h, c: (0, 0, 0))],
        out_specs=pl.BlockSpec((tb, w), lambda h, c: (c, h)),
        out_shape=jax.ShapeDtypeStruct((S, HG_WIDTH), BF16),
        scratch_shapes=[pltpu.VMEM((nh, HG_DIM, HG_DIM), F32), pltpu.VMEM((tb, w), F32),
                        pltpu.VMEM((tb, w), F32), pltpu.VMEM((tb, w), F32)],
        compiler_params=_cparams(("arbitrary", "arbitrary")),
        name="hgrn",
    )(hz, hz, hz, hz, loglb, log1mlb, gnorm, jnp.asarray(rm), jnp.asarray(sm))


_MERGE_PART = 512


def _merge_kernel(at_ref, hg_ref, ga_ref, gh_ref, wa_ref, wh_ref, wo_ref, x_ref, gt_ref, gp_ref, o_ref):
    at, hg = at_ref[...], hg_ref[...]
    d = wo_ref.shape[1]
    parts = [slice(c0, c0 + _MERGE_PART) for c0 in range(0, d, _MERGE_PART)]
    branch = [(_dot(at, wa_ref[0, :, cs]), _dot(hg, wh_ref[0, :, cs])) for cs in parts]
    y = None
    for cs, (ma, mh) in zip(parts, branch):
        mixed = ga_ref[:, cs].astype(F32) * ma + gh_ref[:, cs].astype(F32) * mh
        yp = _dot(mixed.astype(BF16), wo_ref[0, cs, :])
        y = yp if y is None else y + yp
    ms = jnp.mean(y * y, axis=-1, keepdims=True)
    yn = y * lax.rsqrt(ms + NORM_EPS) * gp_ref[...]
    o_ref[...] = x_ref[...] + gt_ref[...] * yn


def _merge(attn, hg, mg, wa, wh, wo, layer, x, gt, gpost):
    S, D = x.shape
    tm = 256
    const = lambda shape: pl.BlockSpec((1,) + shape, lambda m: (layer, 0, 0), pipeline_mode=pl.Buffered(1))
    row = pl.BlockSpec((1, D), lambda m: (0, 0))
    return pl.pallas_call(
        _merge_kernel,
        grid=(S // tm,),
        in_specs=[
            pl.BlockSpec((tm, A_WIDTH), lambda m: (m, 0)),
            pl.BlockSpec((tm, HG_WIDTH), lambda m: (m, 0)),
            pl.BlockSpec((tm, D), lambda m: (m, 0)),
            pl.BlockSpec((tm, D), lambda m: (m, 1)),
            const((A_WIDTH, D)), const((HG_WIDTH, D)), const((D, D)),
            pl.BlockSpec((tm, D), lambda m: (m, 0)),
            row, row,
        ],
        out_specs=pl.BlockSpec((tm, D), lambda m: (m, 0)),
        out_shape=jax.ShapeDtypeStruct((S, D), F32),
        compiler_params=_cparams(("arbitrary",)),
        name="merge",
    )(attn, hg, mg, mg, wa, wh, wo, x, gt, gpost.reshape(1, D))


_HALO = 16
_FF_SUB = 256


def _ffn_kernel(x_ref, xh_ref, g_ref, sc_ref, sh_ref, wg_ref, wv_ref, cwg_ref, cwv_ref, cbg_ref, cbv_ref,
                wd_ref, gt_ref, gp_ref, o_ref, h_scr, ug_scr, uv_scr, *, tm):
    m = pl.program_id(0)
    f = pl.program_id(1)

    def norm_mod(x):
        ms = jnp.mean(x * x, axis=-1, keepdims=True)
        y = x * lax.rsqrt(ms + NORM_EPS) * g_ref[...]
        return (y * (1.0 + sc_ref[...]) + sh_ref[...]).astype(BF16)

    @pl.when(f == 0)
    def _():
        h_scr[0:_HALO, :] = jnp.where(m > 0, norm_mod(xh_ref[...]), jnp.zeros((), BF16))
        h_scr[_HALO:_HALO + tm, :] = norm_mod(x_ref[...])
        o_ref[...] = jnp.zeros_like(o_ref)

    tf = wg_ref.shape[2]
    subs = [slice(s * _FF_SUB, (s + 1) * _FF_SUB) for s in range(tf // _FF_SUB)]
    h = h_scr[...]
    ups = [(_dot(h, wg_ref[0, :, cs]), _dot(h, wv_ref[0, :, cs])) for cs in subs]

    def conv(u_scr, cw_ref, cb_ref, cs):
        out = cb_ref[:, cs] + cw_ref[0:1, cs] * u_scr[pl.ds(_HALO - 2, tm), cs]
        out = out + cw_ref[1:2, cs] * u_scr[pl.ds(_HALO - 1, tm), cs]
        return out + cw_ref[2:3, cs] * u_scr[pl.ds(_HALO, tm), cs]

    acc = o_ref[...]
    for cs, (ug, uv) in zip(subs, ups):
        ug_scr[:, cs] = ug
        uv_scr[:, cs] = uv
        cg = conv(ug_scr, cwg_ref, cbg_ref, cs)
        cv = conv(uv_scr, cwv_ref, cbv_ref, cs)
        act = (cg * jax.nn.sigmoid(cg) * cv).astype(BF16)
        acc = acc + _dot(act, wd_ref[0, cs, :])
    o_ref[...] = acc

    @pl.when(f == pl.num_programs(1) - 1)
    def _():
        y = o_ref[...]
        ms = jnp.mean(y * y, axis=-1, keepdims=True)
        yn = y * lax.rsqrt(ms + NORM_EPS) * gp_ref[...]
        o_ref[...] = x_ref[...] + gt_ref[...] * yn


def _ffn(x, gpre, sc, sh, wup, convw, convb, wdown, layer, gt, gpost):
    S, D = x.shape
    F = wdown.shape[1]
    tm = 512 if S % 512 == 0 else S
    tf = 512
    nf = F // tf
    hb = tm // _HALO
    row = pl.BlockSpec((1, D), lambda m, f: (0, 0))
    return pl.pallas_call(
        functools.partial(_ffn_kernel, tm=tm),
        grid=(S // tm, nf),
        in_specs=[
            pl.BlockSpec((tm, D), lambda m, f: (m, 0)),
            pl.BlockSpec((_HALO, D), lambda m, f: (jnp.maximum(m * hb - 1, 0), 0)),
            row, row, row,
            pl.BlockSpec((1, D, tf), lambda m, f: (layer, 0, f)),
            pl.BlockSpec((1, D, tf), lambda m, f: (layer, 0, nf + f)),
            pl.BlockSpec((CONV_WIDTH, tf), lambda m, f: (0, f)),
            pl.BlockSpec((CONV_WIDTH, tf), lambda m, f: (0, nf + f)),
            pl.BlockSpec((1, tf), lambda m, f: (0, f)),
            pl.BlockSpec((1, tf), lambda m, f: (0, nf + f)),
            pl.BlockSpec((1, tf, D), lambda m, f: (layer, f, 0)),
            row, row,
        ],
        out_specs=pl.BlockSpec((tm, D), lambda m, f: (m, 0)),
        out_shape=jax.ShapeDtypeStruct((S, D), F32),
        scratch_shapes=[pltpu.VMEM((tm + _HALO, D), BF16),
                        pltpu.VMEM((tm + _HALO, tf), F32), pltpu.VMEM((tm + _HALO, tf), F32)],
        compiler_params=_cparams(("arbitrary", "arbitrary")),
        name="ffn",
    )(x, x, gpre.reshape(1, D), sc, sh, wup, wup, convw, convw, convb.reshape(1, -1), convb.reshape(1, -1),
      wdown, gt, gpost.reshape(1, D))


def _rope_tables(positions):
    inv_freq = 1.0 / (ROPE_THETA ** (jnp.arange(0, HEAD_DIM, 2, dtype=F32) / HEAD_DIM))
    ang = positions.astype(F32)[:, None] * inv_freq
    cos, sin = jnp.cos(ang), jnp.sin(ang)
    return jnp.concatenate([cos, cos], axis=-1), jnp.concatenate([-sin, sin], axis=-1)


def kernel(x, c, positions, w_ada, b_ada, g_pre_mix, w_in, pe_kc, w_kc, pe_vc, w_vc, lb_logits, g_hg_norm,
           w_br_attn, w_br_hgrn, w_out, g_post_mix, g_pre_ffn, w_up, conv_w, conv_b, w_down, g_post_ffn):
    B, S, D = x.shape
    assert B == 1, "kernel is written for one sequence"
    L = w_ada.shape[0]
    xs = x[0]
    cosf, sinf = _rope_tables(positions[0])
    scale = HEAD_DIM ** -0.5 * np.log2(np.e)
    tables = (cosf * scale, sinf * scale, cosf, sinf)
    lb_cum = jnp.cumsum(jax.nn.softmax(lb_logits.astype(F32), axis=0), axis=0)
    lower = lb_cum - lb_cum[0:1]
    log_lb = jnp.log(lower)
    log_1m = jnp.log1p(-lower)
    ada = _ada_all(c, w_ada, b_ada)

    w_in_t = jnp.swapaxes(w_in, 1, 2)
    tail_tn = 1024
    hg_roles = (("id",) * (tail_tn // LANES),) * (4 * HG_WIDTH // tail_tn)
    mg_roles = (("sg",) * (tail_tn // LANES),) * (2 * D // tail_tn)

    wa_b, wh_b, wo_b = w_br_attn.astype(BF16), w_br_hgrn.astype(BF16), w_out.astype(BF16)
    wup_b, wdown_b = w_up.astype(BF16), w_down.astype(BF16)

    for l in range(L):
        sh1, sc1, gt1, sh2, sc2, gt2 = [ada[l, :, j * D:(j + 1) * D] for j in range(N_ADA)]

        h = _prenorm(xs, g_pre_mix[l], sc1, sh1)
        a = _proj(h, w_in_t, l, 0, A_ROLES, BF16, tn=A_TILE, tables=tables, name="proj_attn")
        hz = _proj(h, w_in_t, l, A_COLS, hg_roles, F32, tn=tail_tn, name="proj_hgrn")
        mg = _proj(h, w_in_t, l, A_COLS + 4 * HG_WIDTH, mg_roles, BF16, tn=tail_tn, name="proj_merge_gate")

        cmp_kv = _compress(a, jnp.stack([pe_kc[l], pe_vc[l]]), jnp.stack([w_kc[l], w_vc[l]]))
        attn = _nsa(a, cmp_kv)
        hg = _hgrn(hz, log_lb[l:l + 1], log_1m[l:l + 1], g_hg_norm[l:l + 1])
        xs = _merge(attn, hg, mg, wa_b, wh_b, wo_b, l, xs, gt1, g_post_mix[l])

        xs = _ffn(xs, g_pre_ffn[l], sc2, sh2, wup_b, conv_w[l], conv_b[l], wdown_b, l, gt2, g_post_ffn[l])
    return xs[None]
```

```python
import functools

import jax
import jax.numpy as jnp
import numpy as np
from jax import lax
from jax.experimental import pallas as pl
from jax.experimental.pallas import tpu as pltpu

F32 = jnp.float32
BF16 = jnp.bfloat16

A_HEADS = 8
A_KV_GROUPS = 2
A_HPG = A_HEADS // A_KV_GROUPS
HEAD_DIM = 128
A_WIDTH = A_HEADS * HEAD_DIM
A_KV_WIDTH = A_KV_GROUPS * HEAD_DIM
CMP_BLOCK = 32
CMP_STRIDE = 16
SEL_BLOCK = 64
SEL_TOPK = 16
WINDOW = 512
ROPE_THETA = 10000.0
HG_HEADS = 8
HG_DIM = 128
HG_WIDTH = HG_HEADS * HG_DIM
HG_CHUNK = 64
CONV_WIDTH = 3
NORM_EPS = 1e-6
N_ADA = 6
A_COLS = A_WIDTH + 6 * A_KV_WIDTH + 3 * A_HEADS

LANES = 128
SEL_SLOTS = LANES
KEY_TILE = 256
NEG_BIG = -1e30
VMEM_LIMIT = 56 * 1024 * 1024


def _cparams(sem):
    return pltpu.CompilerParams(dimension_semantics=sem, vmem_limit_bytes=VMEM_LIMIT)


def _split3(a):
    hi = a.astype(BF16)
    r1 = a - hi.astype(F32)
    mid = r1.astype(BF16)
    lo = (r1 - mid.astype(F32)).astype(BF16)
    return hi, mid, lo


def _dot(a, b):
    return jnp.dot(a, b, preferred_element_type=F32)


def _dot_nt(a, b):
    return lax.dot_general(a, b, (((1,), (1,)), ((), ())), preferred_element_type=F32)


def _dot_tn(a, b):
    return lax.dot_general(a, b, (((0,), (0,)), ((), ())), preferred_element_type=F32)


def _ada_kernel(c_ref, w_ref, b_ref, o_ref):
    c = c_ref[...]
    ca = c * jax.nn.sigmoid(c)
    o_ref[0] = jnp.sum(ca * w_ref[0], axis=0, keepdims=True) + b_ref[0]


def _ada_all(c, w_ada, b_ada):
    L, D, N = w_ada.shape
    tn = 1024
    return pl.pallas_call(
        _ada_kernel,
        grid=(L, N // tn),
        in_specs=[
            pl.BlockSpec((D, 1), lambda l, n: (0, 0)),
            pl.BlockSpec((1, D, tn), lambda l, n: (l, 0, n)),
            pl.BlockSpec((1, 1, tn), lambda l, n: (l, 0, n)),
        ],
        out_specs=pl.BlockSpec((1, 1, tn), lambda l, n: (l, 0, n)),
        out_shape=jax.ShapeDtypeStruct((L, 1, N), F32),
        compiler_params=_cparams(("arbitrary", "arbitrary")),
        name="ada",
    )(c.reshape(D, 1), w_ada, b_ada.reshape(L, 1, N))


def _prenorm_kernel(x_ref, g_ref, sc_ref, sh_ref, o_ref):
    x = x_ref[...]
    ms = jnp.mean(x * x, axis=-1, keepdims=True)
    y = x * lax.rsqrt(ms + NORM_EPS) * g_ref[...]
    o_ref[...] = (y * (1.0 + sc_ref[...]) + sh_ref[...]).astype(o_ref.dtype)


def _prenorm(x, g, sc, sh):
    S, D = x.shape
    tm = 512
    row = pl.BlockSpec((1, D), lambda m: (0, 0))
    return pl.pallas_call(
        _prenorm_kernel,
        grid=(S // tm,),
        in_specs=[pl.BlockSpec((tm, D), lambda m: (m, 0)), row, row, row],
        out_specs=pl.BlockSpec((tm, D), lambda m: (m, 0)),
        out_shape=jax.ShapeDtypeStruct((S, D), BF16),
        compiler_params=_cparams(("arbitrary",)),
        name="prenorm",
    )(x, g.reshape(1, D), sc, sh)


def _rope(t, cosf, sinf):
    return t * cosf + pltpu.roll(t, HEAD_DIM // 2, axis=1) * sinf


_PROJ_PART = 256


def _proj_kernel(h_ref, w_ref, *rest, roles, tn):
    o_ref, w_scr = rest[-2:]
    tabs = rest[:-2]
    n = pl.program_id(0)

    @pl.when(pl.program_id(1) == 0)
    def _():
        w_scr[...] = w_ref[0].astype(BF16)

    def emit(slabs):
        h = h_ref[...]
        per = _PROJ_PART // LANES
        parts = [_dot_nt(h, w_scr[c0:c0 + _PROJ_PART, :]) for c0 in range(0, tn, _PROJ_PART)]
        for pi, acc in enumerate(parts):
            for s in range(per):
                kind = slabs[pi * per + s]
                t = acc[:, s * LANES:(s + 1) * LANES]
                if kind == "sg":
                    t = jax.nn.sigmoid(t)
                elif kind != "id":
                    cos_ref, sin_ref = tabs[0:2] if kind == "rq" else tabs[-2:]
                    t = _rope(t, cos_ref[...], sin_ref[...])
                c = (pi * per + s) * LANES
                o_ref[:, c:c + LANES] = t.astype(o_ref.dtype)

    kinds = sorted(set(roles))
    if len(kinds) == 1:
        emit(kinds[0])
        return
    for kind in kinds:
        hit = functools.reduce(jnp.logical_or, [n == t for t, r in enumerate(roles) if r == kind])
        pl.when(hit)(functools.partial(emit, kind))


def _proj(h, w_t, layer, row0, roles, out_dtype, *, tn, tables=(), name="proj"):
    S, D = h.shape
    tm = 1024 if S % 1024 == 0 else S
    nt = len(roles)
    tab = pl.BlockSpec((tm, HEAD_DIM), lambda n, m: (m, 0))
    return pl.pallas_call(
        functools.partial(_proj_kernel, roles=tuple(roles), tn=tn),
        grid=(nt, S // tm),
        in_specs=[pl.BlockSpec((tm, D), lambda n, m: (m, 0)),
                  pl.BlockSpec((pl.Element(1), pl.Element(tn), pl.Element(D)),
                               lambda n, m: (layer, pl.multiple_of(row0 + n * tn, 8), 0))]
                 + [tab] * len(tables),
        out_specs=pl.BlockSpec((tm, tn), lambda n, m: (m, n)),
        out_shape=jax.ShapeDtypeStruct((S, nt * tn), out_dtype),
        scratch_shapes=[pltpu.VMEM((tn, D), BF16)],
        compiler_params=_cparams(("arbitrary", "arbitrary")),
        name=name,
    )(h, w_t, *tables)


A_TILE = 512
_KV_TILE_ROLES = ("rk",) * A_KV_GROUPS + ("id",) * A_KV_GROUPS
A_ROLES = ((("rq",) * (A_TILE // LANES),) * (A_WIDTH // A_TILE) + (_KV_TILE_ROLES,) * 3
           + (("sg",) * (A_TILE // LANES),))
_KV_BLOCK0 = A_WIDTH // LANES
BLK_KC, BLK_VC, BLK_KS, BLK_VS, BLK_KW, BLK_VW = [_KV_BLOCK0 + j * A_KV_GROUPS for j in range(6)]
BLK_GATE = _KV_BLOCK0 + 6 * A_KV_GROUPS


def _compress_kernel(t_ref, pe_ref, w_ref, o_ref, t32_ref, *, n_chunk):
    half = CMP_BLOCK // 2
    t32_ref[...] = t_ref[...].astype(F32)
    top, bot = [], []
    for j in range(half):
        a = t32_ref[pl.ds(j, n_chunk, stride=CMP_STRIDE), :]
        top.append(a + pe_ref[0, j:j + 1, :])
        bot.append(a + pe_ref[0, half + j:half + j + 1, :])
    a_top = jnp.concatenate(top, axis=1)
    a_bot = jnp.concatenate(bot, axis=1)
    kw = half * HEAD_DIM
    w = w_ref[0]
    w_top = w[:kw].astype(BF16)
    w_bot = w[kw:].astype(BF16)
    y_top = jnp.zeros((n_chunk, HEAD_DIM), F32)
    y_bot = jnp.zeros((n_chunk, HEAD_DIM), F32)
    for piece in _split3(a_top):
        y_top = y_top + _dot(piece, w_top)
    for piece in _split3(a_bot):
        y_bot = y_bot + _dot(piece, w_bot)
    o_ref[0, 0] = y_top + pltpu.roll(y_bot, n_chunk - 1, axis=0)


def _compress(a, pe2, w2):
    S = a.shape[0]
    n_chunk = S // CMP_STRIDE
    G = A_KV_GROUPS
    return pl.pallas_call(
        functools.partial(_compress_kernel, n_chunk=n_chunk),
        grid=(2, G),
        in_specs=[
            pl.BlockSpec((S, HEAD_DIM), lambda kv, g: (0, BLK_KC + kv * G + g)),
            pl.BlockSpec((1, CMP_BLOCK, HEAD_DIM), lambda kv, g: (kv, 0, 0)),
            pl.BlockSpec((1, CMP_BLOCK * HEAD_DIM, HEAD_DIM), lambda kv, g: (kv, 0, 0)),
        ],
        out_specs=pl.BlockSpec((1, 1, n_chunk, HEAD_DIM), lambda kv, g: (kv, g, 0, 0)),
        out_shape=jax.ShapeDtypeStruct((2, G, n_chunk, HEAD_DIM), F32),
        scratch_shapes=[pltpu.VMEM((S, HEAD_DIM), F32)],
        compiler_params=_cparams(("arbitrary", "arbitrary")),
        name="compress",
    )(a, pe2, w2)


NSA_QB = 256
SEL_BODY = 8
SEL_GROUP = 4
_N_FORCED = 3


def _nsa_kernel(q_ref, ks_ref, kw_ref, vs_ref, vw_ref, kc_ref, vc_ref, ovt_ref, gate_ref,
                o_ref, kaug_ref, vt_ref, vwt_ref, vct_ref, acc_ref, m_ref, l_ref, *, seq):
    grp = pl.program_id(0)
    i = pl.program_id(1)
    t0 = i * NSA_QB
    n_cmp = seq // CMP_STRIDE
    cols = A_HPG * NSA_QB

    @pl.when(i == 0)
    def _():
        def build(r, c):
            r0 = pl.multiple_of(r * KEY_TILE, KEY_TILE)
            kaug_ref[r, :, 0:HEAD_DIM] = ks_ref[pl.ds(r0, KEY_TILE), :]
            blk = (r0 + lax.broadcasted_iota(jnp.int32, (KEY_TILE, SEL_SLOTS), 0)) // SEL_BLOCK
            slot = lax.broadcasted_iota(jnp.int32, (KEY_TILE, SEL_SLOTS), 1)
            kaug_ref[r, :, HEAD_DIM:HEAD_DIM + SEL_SLOTS] = jnp.where(blk == slot, 1.0, 0.0).astype(BF16)
            vt_ref[r] = vs_ref[pl.ds(r0, KEY_TILE), :].astype(F32).T.astype(BF16)
            vwt_ref[r] = vw_ref[pl.ds(r0, KEY_TILE), :].astype(F32).T.astype(BF16)
            return c
        lax.fori_loop(0, seq // KEY_TILE, build, 0)
        vct_ref[...] = vc_ref[0, 0].T.astype(BF16)

    qt = jnp.concatenate(
        [q_ref[:, h * HEAD_DIM:(h + 1) * HEAD_DIM].astype(F32).T.astype(BF16) for h in range(A_HPG)],
        axis=1)
    tq = t0 + lax.broadcasted_iota(jnp.int32, (1, cols), 1) % NSA_QB

    def softmax0(s):
        m = jnp.max(s, axis=0, keepdims=True)
        p = jnp.exp2(s - m)
        return p, jnp.sum(p, axis=0, keepdims=True)

    s_c = _dot(kc_ref[0, 0].astype(BF16), qt)
    nrow = lax.broadcasted_iota(jnp.int32, (n_cmp, 1), 0)
    vis = (nrow * CMP_STRIDE + (CMP_BLOCK - 1)) <= tq
    p_c, den = softmax0(jnp.where(vis, s_c, NEG_BIG))
    p_c = p_c * jnp.where(tq >= CMP_BLOCK - 1, 1.0 / den, 0.0)
    o_ct = _dot(vct_ref[...], p_c.astype(BF16))

    psum = p_c[:, 0:NSA_QB]
    for h in range(1, A_HPG):
        psum = psum + p_c[:, h * NSA_QB:(h + 1) * NSA_QB]
    ovt = ovt_ref[...]
    imp_t = jnp.zeros((SEL_SLOTS, NSA_QB), F32)
    for piece in _split3(psum):
        imp_t = imp_t + _dot(ovt, piece)
    slot_i = lax.broadcasted_iota(jnp.int32, (SEL_SLOTS, NSA_QB), 0)
    tq_l = t0 + lax.broadcasted_iota(jnp.int32, (SEL_SLOTS, NSA_QB), 1)
    cur = tq_l // SEL_BLOCK
    forced = jnp.where(slot_i == 0, 1, 0) + jnp.where(slot_i == cur, 1, 0) + jnp.where(slot_i == cur - 1, 1, 0)
    valid = slot_i * SEL_BLOCK <= tq_l
    key = jnp.where(forced > 0, -3e38, jnp.where(valid, imp_t, NEG_BIG))
    slot_f = slot_i.astype(F32)
    bias_t = jnp.where(forced > 0, 0.0, NEG_BIG)
    for _ in range(SEL_TOPK - _N_FORCED):
        mx = jnp.max(key, axis=0, keepdims=True)
        first = jnp.min(jnp.where(key == mx, slot_f, 1e9), axis=0, keepdims=True)
        pick = slot_f == first
        bias_t = jnp.where(pick, 0.0, bias_t)
        key = jnp.where(pick, -3e38, key)

    wlen = WINDOW + NSA_QB
    w0 = pl.multiple_of(jnp.maximum(t0 - WINDOW, 0), KEY_TILE)
    wt = jnp.maximum(i - WINDOW // KEY_TILE, 0)
    s_w = _dot(kw_ref[pl.ds(w0, wlen), :], qt)
    dist = tq - (w0 + lax.broadcasted_iota(jnp.int32, (wlen, 1), 0))
    in_window = dist.astype(jnp.uint32) < WINDOW
    p_w, l_w = softmax0(jnp.where(in_window, s_w, NEG_BIG))
    p_w = p_w.astype(BF16)
    o_wt = _dot(vwt_ref[wt], p_w[0:KEY_TILE])
    for k in range(1, wlen // KEY_TILE):
        o_wt = o_wt + _dot(vwt_ref[wt + k], p_w[k * KEY_TILE:(k + 1) * KEY_TILE])
    o_wt = o_wt * (1.0 / l_w)

    qt_aug = jnp.concatenate([qt, jnp.concatenate([bias_t.astype(BF16)] * A_HPG, axis=1)], axis=0)
    acc_ref[...] = jnp.zeros_like(acc_ref)
    m_ref[...] = jnp.full(m_ref.shape, 0.5 * NEG_BIG, F32)
    l_ref[...] = jnp.zeros_like(l_ref)

    def sel_body(base, causal):
        n = len(causal)
        m, l = m_ref[...], l_ref[...]
        gsz = min(SEL_GROUP, max(n // 2, 1))
        groups = [list(range(g0, min(g0 + gsz, n))) for g0 in range(0, n, gsz)]

        def scores(ks):
            out = {}
            for k in ks:
                sk = _dot(kaug_ref[base + k], qt_aug)
                if causal[k]:
                    kpos = (base + k) * KEY_TILE + lax.broadcasted_iota(jnp.int32, (KEY_TILE, 1), 0)
                    sk = jnp.where(kpos <= tq, sk, NEG_BIG)
                out[k] = sk
            return out

        s = scores(groups[0])
        for gi, ks in enumerate(groups):
            if gi + 1 < len(groups):
                s.update(scores(groups[gi + 1]))
            m_new = m
            for k in ks:
                m_new = jnp.maximum(m_new, jnp.max(s[k], axis=0, keepdims=True))
            alpha = jnp.exp2(m - m_new)
            l = alpha * l
            pv = None
            for k in ks:
                p = jnp.exp2(s.pop(k) - m_new)
                l = l + jnp.sum(p, axis=0, keepdims=True)
                d = _dot(vt_ref[base + k], p.astype(BF16))
                pv = d if pv is None else pv + d
            acc_ref[...] = alpha * acc_ref[...] + pv
            m = m_new
        m_ref[...] = m
        l_ref[...] = l

    n_bulk = i // SEL_BODY
    rem = i % SEL_BODY

    def bulk(b, c):
        sel_body(b * SEL_BODY, (False,) * SEL_BODY)
        return c
    lax.fori_loop(0, n_bulk, bulk, 0)

    base = n_bulk * SEL_BODY
    part = SEL_BODY // 2
    while part >= 2:
        pl.when(rem & part != 0)(functools.partial(sel_body, base, (False,) * part))
        base = base + (rem & part)
        part //= 2

    @pl.when(rem % 2 == 1)
    def _():
        sel_body(i - 1, (False, True))

    @pl.when(rem % 2 == 0)
    def _():
        sel_body(i, (True,))

    o_st = acc_ref[...] * (1.0 / l_ref[...])

    gates_t = gate_ref[...].astype(F32).T
    per_group = 3 * A_HPG

    def gate(h, b):
        r = 3 * h + b
        return jnp.where(grp == 0, gates_t[r:r + 1], gates_t[per_group + r:per_group + r + 1])

    for h in range(A_HPG):
        c = slice(h * NSA_QB, (h + 1) * NSA_QB)
        o = gate(h, 0) * o_ct[:, c] + gate(h, 1) * o_st[:, c] + gate(h, 2) * o_wt[:, c]
        o_ref[:, h * HEAD_DIM:(h + 1) * HEAD_DIM] = o.T.astype(o_ref.dtype)


def _overlap_matrix(seq):
    n_chunk = seq // CMP_STRIDE
    cs = np.arange(n_chunk)[:, None] * CMP_STRIDE
    ss = np.arange(SEL_SLOTS)[None, :] * SEL_BLOCK
    ov = np.maximum(np.minimum(cs + CMP_BLOCK, ss + SEL_BLOCK) - np.maximum(cs, ss), 0) / CMP_BLOCK
    n_cmp = (seq - CMP_BLOCK) // CMP_STRIDE + 1
    ov[n_cmp:] = 0.0
    ov[:, seq // SEL_BLOCK:] = 0.0
    return ov.astype(np.float32)


def _nsa(a, cmp_kv):
    S = a.shape[0]
    G = A_KV_GROUPS
    n_tiles = S // KEY_TILE
    assert S // SEL_BLOCK <= SEL_SLOTS and S % KEY_TILE == 0 and S >= WINDOW + NSA_QB
    assert NSA_QB == KEY_TILE and G == 2
    n_chunk = S // CMP_STRIDE
    ovt = jnp.asarray(_overlap_matrix(S).T, BF16)
    col = lambda b: pl.BlockSpec((S, HEAD_DIM), lambda g, i, b=b: (0, b + g))
    return pl.pallas_call(
        functools.partial(_nsa_kernel, seq=S),
        grid=(G, S // NSA_QB),
        in_specs=[
            pl.BlockSpec((NSA_QB, A_HPG * HEAD_DIM), lambda g, i: (i, g)),
            col(BLK_KS), col(BLK_KW), col(BLK_VS), col(BLK_VW),
            pl.BlockSpec((1, 1, n_chunk, HEAD_DIM), lambda g, i: (0, g, 0, 0)),
            pl.BlockSpec((1, 1, n_chunk, HEAD_DIM), lambda g, i: (1, g, 0, 0)),
            pl.BlockSpec((SEL_SLOTS, n_chunk), lambda g, i: (0, 0)),
            pl.BlockSpec((NSA_QB, LANES), lambda g, i: (i, BLK_GATE)),
        ],
        out_specs=pl.BlockSpec((NSA_QB, A_HPG * HEAD_DIM), lambda g, i: (i, g)),
        out_shape=jax.ShapeDtypeStruct((S, A_WIDTH), BF16),
        scratch_shapes=[pltpu.VMEM((n_tiles, KEY_TILE, HEAD_DIM + SEL_SLOTS), BF16),
                        pltpu.VMEM((n_tiles, HEAD_DIM, KEY_TILE), BF16),
                        pltpu.VMEM((n_tiles, HEAD_DIM, KEY_TILE), BF16),
                        pltpu.VMEM((HEAD_DIM, n_chunk), BF16),
                        pltpu.VMEM((HEAD_DIM, A_HPG * NSA_QB), F32),
                        pltpu.VMEM((1, A_HPG * NSA_QB), F32),
                        pltpu.VMEM((1, A_HPG * NSA_QB), F32)],
        compiler_params=_cparams(("arbitrary", "arbitrary")),
        name="nsa",
    )(a, a, a, a, a, cmp_kv, cmp_kv, ovt, a)


_ROW_MASKS = 18
_SCORE_MASKS = 10
HG_HEADS_PER_STEP = 8


def _hgrn_tables():
    t = np.arange(HG_CHUNK)
    p4, i4, i16 = t % 4, (t // 4) % 4, t // 16
    conds = [p4 >= 1, p4 >= 2, p4 >= 3, p4 <= 2, p4 <= 1, p4 == 0,
             i4 >= 1, i4 >= 2, i4 >= 3, i4 <= 2, i4 <= 1, i4 == 0,
             i16 >= 1, i16 >= 2, i16 >= 3, i16 <= 2, i16 <= 1, i16 == 0]
    rm = np.stack([np.broadcast_to(c[:, None], (HG_CHUNK, HG_DIM)) for c in conds]).astype(np.float32)
    tt, ss = t[:, None], t[None, :]
    lag16 = tt // 16 - ss // 16
    lag4 = tt // 4 - ss // 4
    same16, same4 = lag16 == 0, lag4 == 0
    sm = [lag16 == 1, lag16 == 2, lag16 == 3,
          same16 & (lag4 == 1), same16 & (lag4 == 2), same16 & (lag4 == 3),
          same4 & (tt - ss == 0), same4 & (tt - ss == 1), same4 & (tt - ss == 2), same4 & (tt - ss == 3)]
    return rm, np.stack(sm).astype(np.float32)


def _hgrn_kernel(q_ref, f_ref, i_ref, og_ref, loglb_ref, log1m_ref, gn_ref, rm_ref, sm_ref,
                 o_ref, st_ref, g_scr, k_scr, o_scr, *, tb, nh):
    C = HG_CHUNK
    heads = [slice(h * HG_DIM, (h + 1) * HG_DIM) for h in range(nh)]

    @pl.when(pl.program_id(1) == 0)
    def _():
        st_ref[...] = jnp.zeros_like(st_ref)

    x = f_ref[...]
    log_sig = jnp.minimum(x, 0.0) - jnp.log(1.0 + jnp.exp(-jnp.abs(x)))
    a = loglb_ref[...]
    c = log1m_ref[...] + log_sig
    log_f = jnp.maximum(a, c) + jnp.log(1.0 + jnp.exp(-jnp.abs(a - c)))
    g_scr[...] = log_f
    k_scr[...] = 1.0 - jnp.exp(log_f)

    def rmask(n):
        return jnp.concatenate([rm_ref[n]] * nh, axis=1)

    def roll(v, s):
        return pltpu.roll(v, s % C, axis=0)

    def b16(z):
        return z.astype(BF16)

    def chunk(ci, carry):
        r0 = pl.multiple_of(ci * C, C)
        g = g_scr[pl.ds(r0, C), :]
        q = q_ref[pl.ds(r0, C), :]
        k = k_scr[pl.ds(r0, C), :]
        v = i_ref[pl.ds(r0, C), :].astype(BF16)
        pre4 = g + rmask(0) * roll(g, 1) + rmask(1) * roll(g, 2) + rmask(2) * roll(g, 3)
        suf4 = rmask(3) * roll(g, -1) + rmask(4) * roll(g, -2) + rmask(5) * roll(g, -3)
        tot4 = pre4 + suf4
        a1, a2, a3 = roll(tot4, 4), roll(tot4, 8), roll(tot4, 12)
        pre16 = pre4 + rmask(6) * a1 + rmask(7) * a2 + rmask(8) * a3
        suf16 = suf4 + rmask(9) * roll(tot4, -4) + rmask(10) * roll(tot4, -8) + rmask(11) * roll(tot4, -12)
        tot16 = pre16 + suf16
        d1, d2, d3 = roll(tot16, 16), roll(tot16, 32), roll(tot16, 48)
        pre64 = pre16 + rmask(12) * d1 + rmask(13) * d2 + rmask(14) * d3
        suf64 = suf16 + rmask(15) * roll(tot16, -16) + rmask(16) * roll(tot16, -32) + rmask(17) * roll(tot16, -48)
        tot64 = pre64 + suf64
        g1, g2 = roll(g, 1), roll(g, 2)

        q_a = q * jnp.exp(pre16)
        lhs_a = [b16(q_a), b16(q_a * jnp.exp(d1)), b16(q_a * jnp.exp(d1 + d2))]
        k_a = b16(k * jnp.exp(suf16))
        q_b = q * jnp.exp(pre4)
        lhs_b = [b16(q_b), b16(q_b * jnp.exp(a1)), b16(q_b * jnp.exp(a1 + a2))]
        k_b = b16(k * jnp.exp(suf4))
        q_c = q * jnp.exp(g)
        lhs_c = [b16(q), b16(q_c), b16(q_c * jnp.exp(g1)), b16(q_c * jnp.exp(g1 + g2))]
        k_c = b16(k)
        q_s = b16(q * jnp.exp(pre64))
        k_s = b16(k * jnp.exp(suf64))
        decay = jnp.exp(tot64[0:1, :])

        def level(lhs, rhs):
            return [_dot_nt(jnp.concatenate([z[:, hh] for z in lhs], axis=0), rhs[:, hh]) for hh in heads]

        s_a, s_b, s_c = level(lhs_a, k_a), level(lhs_b, k_b), level(lhs_c, k_c)
        o_inter = [_dot_nt(q_s[:, hh], b16(st_ref[h])) for h, hh in enumerate(heads)]
        kv = [_dot_tn(v[:, hh], k_s[:, hh]) for hh in heads]
        for h, hh in enumerate(heads):
            sa, sb, sc = s_a[h], s_b[h], s_c[h]
            scores = (sm_ref[0] * sa[0:C] + sm_ref[1] * sa[C:2 * C] + sm_ref[2] * sa[2 * C:3 * C]
                      + sm_ref[3] * sb[0:C] + sm_ref[4] * sb[C:2 * C] + sm_ref[5] * sb[2 * C:3 * C]
                      + sm_ref[6] * sc[0:C] + sm_ref[7] * sc[C:2 * C] + sm_ref[8] * sc[2 * C:3 * C]
                      + sm_ref[9] * sc[3 * C:4 * C])
            o_scr[pl.ds(r0, C), hh] = _dot(b16(scores), v[:, hh]) + o_inter[h]
            st_ref[h] = st_ref[h] * decay[:, hh] + kv[h]
        return carry

    lax.fori_loop(0, tb // C, chunk, 0)

    for hh in heads:
        o = o_scr[:, hh]
        ms = jnp.mean(o * o, axis=-1, keepdims=True)
        og = og_ref[:, hh]
        y = o * lax.rsqrt(ms + NORM_EPS) * gn_ref[:, hh] * (og * jax.nn.sigmoid(og))
        o_ref[:, hh] = y.astype(o_ref.dtype)


def _hgrn(hz, loglb, log1mlb, gnorm):
    S = hz.shape[0]
    tb = 512 if S % 512 == 0 else S
    nh = HG_HEADS_PER_STEP
    ng = HG_HEADS // nh
    w = nh * HG_DIM
    rm, sm = _hgrn_tables()
    blk = lambda part: pl.BlockSpec((tb, w), lambda h, c, part=part: (c, part * ng + h))
    row = pl.BlockSpec((1, w), lambda h, c: (0, h))
    return pl.pallas_call(
        functools.partial(_hgrn_kernel, tb=tb, nh=nh),
        grid=(ng, S // tb),
        in_specs=[blk(0), blk(1), blk(2), blk(3), row, row, row,
                  pl.BlockSpec((_ROW_MASKS, HG_CHUNK, HG_DIM), lambda h, c: (0, 0, 0)),
                  pl.BlockSpec((_SCORE_MASKS, HG_CHUNK, HG_CHUNK), lambda h, c: (0, 0, 0))],
        out_specs=pl.BlockSpec((tb, w), lambda h, c: (c, h)),
        out_shape=jax.ShapeDtypeStruct((S, HG_WIDTH), BF16),
        scratch_shapes=[pltpu.VMEM((nh, HG_DIM, HG_DIM), F32), pltpu.VMEM((tb, w), F32),
                        pltpu.VMEM((tb, w), F32), pltpu.VMEM((tb, w), F32)],
        compiler_params=_cparams(("arbitrary", "arbitrary")),
        name="hgrn",
    )(hz, hz, hz, hz, loglb, log1mlb, gnorm, jnp.asarray(rm), jnp.asarray(sm))


_MERGE_PART = 512


def _merge_kernel(at_ref, hg_ref, ga_ref, gh_ref, wa_ref, wh_ref, wo_ref, x_ref, gt_ref, gp_ref, o_ref):
    at, hg = at_ref[...], hg_ref[...]
    d = wo_ref.shape[1]
    parts = [slice(c0, c0 + _MERGE_PART) for c0 in range(0, d, _MERGE_PART)]
    branch = [(_dot(at, wa_ref[0, :, cs]), _dot(hg, wh_ref[0, :, cs])) for cs in parts]
    y = None
    for cs, (ma, mh) in zip(parts, branch):
        mixed = ga_ref[:, cs].astype(F32) * ma + gh_ref[:, cs].astype(F32) * mh
        yp = _dot(mixed.astype(BF16), wo_ref[0, cs, :])
        y = yp if y is None else y + yp
    ms = jnp.mean(y * y, axis=-1, keepdims=True)
    yn = y * lax.rsqrt(ms + NORM_EPS) * gp_ref[...]
    o_ref[...] = x_ref[...] + gt_ref[...] * yn


def _merge(attn, hg, mg, wa, wh, wo, layer, x, gt, gpost):
    S, D = x.shape
    tm = 256
    const = lambda shape: pl.BlockSpec((1,) + shape, lambda m: (layer, 0, 0), pipeline_mode=pl.Buffered(1))
    row = pl.BlockSpec((1, D), lambda m: (0, 0))
    return pl.pallas_call(
        _merge_kernel,
        grid=(S // tm,),
        in_specs=[
            pl.BlockSpec((tm, A_WIDTH), lambda m: (m, 0)),
            pl.BlockSpec((tm, HG_WIDTH), lambda m: (m, 0)),
            pl.BlockSpec((tm, D), lambda m: (m, 0)),
            pl.BlockSpec((tm, D), lambda m: (m, 1)),
            const((A_WIDTH, D)), const((HG_WIDTH, D)), const((D, D)),
            pl.BlockSpec((tm, D), lambda m: (m, 0)),
            row, row,
        ],
        out_specs=pl.BlockSpec((tm, D), lambda m: (m, 0)),
        out_shape=jax.ShapeDtypeStruct((S, D), F32),
        compiler_params=_cparams(("arbitrary",)),
        name="merge",
    )(attn, hg, mg, mg, wa, wh, wo, x, gt, gpost.reshape(1, D))


_HALO = 16
_FF_SUB = 256


def _norm_mod(x, g_ref, sc_ref, sh_ref):
    ms = jnp.mean(x * x, axis=-1, keepdims=True)
    y = x * lax.rsqrt(ms + NORM_EPS) * g_ref[...]
    return (y * (1.0 + sc_ref[...]) + sh_ref[...]).astype(BF16)


def _ffn_kernel(x_ref, xh_ref, g_ref, sc_ref, sh_ref, wg_ref, wv_ref, cwg_ref, cwv_ref, cbg_ref, cbv_ref,
                wd_ref, gt_ref, gp_ref, *rest, tm, emit_next):
    if emit_next:
        gn_ref, scn_ref, shn_ref, o_ref, hn_ref, h_scr, ug_scr, uv_scr = rest
    else:
        o_ref, h_scr, ug_scr, uv_scr = rest
    m = pl.program_id(0)
    f = pl.program_id(1)
    norm_mod = functools.partial(_norm_mod, g_ref=g_ref, sc_ref=sc_ref, sh_ref=sh_ref)

    @pl.when(f == 0)
    def _():
        h_scr[0:_HALO, :] = jnp.where(m > 0, norm_mod(xh_ref[...]), jnp.zeros((), BF16))
        h_scr[_HALO:_HALO + tm, :] = norm_mod(x_ref[...])
        o_ref[...] = jnp.zeros_like(o_ref)

    tf = wg_ref.shape[2]
    subs = [slice(s * _FF_SUB, (s + 1) * _FF_SUB) for s in range(tf // _FF_SUB)]
    h = h_scr[...]
    ups = [(_dot(h, wg_ref[0, :, cs]), _dot(h, wv_ref[0, :, cs])) for cs in subs]

    def conv(u_scr, cw_ref, cb_ref, cs):
        out = cb_ref[:, cs] + cw_ref[0:1, cs] * u_scr[pl.ds(_HALO - 2, tm), cs]
        out = out + cw_ref[1:2, cs] * u_scr[pl.ds(_HALO - 1, tm), cs]
        return out + cw_ref[2:3, cs] * u_scr[pl.ds(_HALO, tm), cs]

    acc = o_ref[...]
    for cs, (ug, uv) in zip(subs, ups):
        ug_scr[:, cs] = ug
        uv_scr[:, cs] = uv
        cg = conv(ug_scr, cwg_ref, cbg_ref, cs)
        cv = conv(uv_scr, cwv_ref, cbv_ref, cs)
        act = (cg * jax.nn.sigmoid(cg) * cv).astype(BF16)
        acc = acc + _dot(act, wd_ref[0, cs, :])
    o_ref[...] = acc

    @pl.when(f == pl.num_programs(1) - 1)
    def _():
        y = o_ref[...]
        ms = jnp.mean(y * y, axis=-1, keepdims=True)
        yn = y * lax.rsqrt(ms + NORM_EPS) * gp_ref[...]
        x_new = x_ref[...] + gt_ref[...] * yn
        o_ref[...] = x_new
        if emit_next:
            hn_ref[...] = _norm_mod(x_new, gn_ref, scn_ref, shn_ref)


def _ffn(x, gpre, sc, sh, wup, convw, convb, wdown, layer, gt, gpost, nxt=None):
    S, D = x.shape
    F = wdown.shape[1]
    tm = 512 if S % 512 == 0 else S
    tf = 512
    nf = F // tf
    hb = tm // _HALO
    row = pl.BlockSpec((1, D), lambda m, f: (0, 0))
    tile = pl.BlockSpec((tm, D), lambda m, f: (m, 0))
    extra = () if nxt is None else (nxt[0].reshape(1, D), nxt[1], nxt[2])
    return pl.pallas_call(
        functools.partial(_ffn_kernel, tm=tm, emit_next=nxt is not None),
        grid=(S // tm, nf),
        in_specs=[
            pl.BlockSpec((tm, D), lambda m, f: (m, 0)),
            pl.BlockSpec((_HALO, D), lambda m, f: (jnp.maximum(m * hb - 1, 0), 0)),
            row, row, row,
            pl.BlockSpec((1, D, tf), lambda m, f: (layer, 0, f)),
            pl.BlockSpec((1, D, tf), lambda m, f: (layer, 0, nf + f)),
            pl.BlockSpec((CONV_WIDTH, tf), lambda m, f: (0, f)),
            pl.BlockSpec((CONV_WIDTH, tf), lambda m, f: (0, nf + f)),
            pl.BlockSpec((1, tf), lambda m, f: (0, f)),
            pl.BlockSpec((1, tf), lambda m, f: (0, nf + f)),
            pl.BlockSpec((1, tf, D), lambda m, f: (layer, f, 0)),
            row, row,
        ] + ([row] * 3 if nxt is not None else []),
        out_specs=[tile, tile] if nxt is not None else tile,
        out_shape=([jax.ShapeDtypeStruct((S, D), F32), jax.ShapeDtypeStruct((S, D), BF16)]
                   if nxt is not None else jax.ShapeDtypeStruct((S, D), F32)),
        scratch_shapes=[pltpu.VMEM((tm + _HALO, D), BF16),
                        pltpu.VMEM((tm + _HALO, tf), F32), pltpu.VMEM((tm + _HALO, tf), F32)],
        compiler_params=_cparams(("arbitrary", "arbitrary")),
        name="ffn",
    )(x, x, gpre.reshape(1, D), sc, sh, wup, wup, convw, convw, convb.reshape(1, -1), convb.reshape(1, -1),
      wdown, gt, gpost.reshape(1, D), *extra)


def _rope_tables(positions):
    inv_freq = 1.0 / (ROPE_THETA ** (jnp.arange(0, HEAD_DIM, 2, dtype=F32) / HEAD_DIM))
    ang = positions.astype(F32)[:, None] * inv_freq
    cos, sin = jnp.cos(ang), jnp.sin(ang)
    return jnp.concatenate([cos, cos], axis=-1), jnp.concatenate([-sin, sin], axis=-1)


def kernel(x, c, positions, w_ada, b_ada, g_pre_mix, w_in, pe_kc, w_kc, pe_vc, w_vc, lb_logits, g_hg_norm,
           w_br_attn, w_br_hgrn, w_out, g_post_mix, g_pre_ffn, w_up, conv_w, conv_b, w_down, g_post_ffn):
    B, S, D = x.shape
    assert B == 1, "kernel is written for one sequence"
    L = w_ada.shape[0]
    xs = x[0]
    cosf, sinf = _rope_tables(positions[0])
    scale = HEAD_DIM ** -0.5 * np.log2(np.e)
    tables = (cosf * scale, sinf * scale, cosf, sinf)
    lb_cum = jnp.cumsum(jax.nn.softmax(lb_logits.astype(F32), axis=0), axis=0)
    lower = lb_cum - lb_cum[0:1]
    log_lb = jnp.log(lower)
    log_1m = jnp.log1p(-lower)
    ada = _ada_all(c, w_ada, b_ada)

    w_in_t = jnp.swapaxes(w_in, 1, 2)
    tail_tn = 1024
    hg_roles = (("id",) * (tail_tn // LANES),) * (4 * HG_WIDTH // tail_tn)
    mg_roles = (("sg",) * (tail_tn // LANES),) * (2 * D // tail_tn)

    wa_b, wh_b, wo_b = w_br_attn.astype(BF16), w_br_hgrn.astype(BF16), w_out.astype(BF16)
    wup_b, wdown_b = w_up.astype(BF16), w_down.astype(BF16)

    mods = [[ada[l, :, j * D:(j + 1) * D] for j in range(N_ADA)] for l in range(L)]
    h = _prenorm(xs, g_pre_mix[0], mods[0][1], mods[0][0])
    for l in range(L):
        sh1, sc1, gt1, sh2, sc2, gt2 = mods[l]

        a = _proj(h, w_in_t, l, 0, A_ROLES, BF16, tn=A_TILE, tables=tables, name="proj_attn")
        hz = _proj(h, w_in_t, l, A_COLS, hg_roles, F32, tn=tail_tn, name="proj_hgrn")
        mg = _proj(h, w_in_t, l, A_COLS + 4 * HG_WIDTH, mg_roles, BF16, tn=tail_tn, name="proj_merge_gate")

        cmp_kv = _compress(a, jnp.stack([pe_kc[l], pe_vc[l]]), jnp.stack([w_kc[l], w_vc[l]]))
        attn = _nsa(a, cmp_kv)
        hg = _hgrn(hz, log_lb[l:l + 1], log_1m[l:l + 1], g_hg_norm[l:l + 1])
        xs = _merge(attn, hg, mg, wa_b, wh_b, wo_b, l, xs, gt1, g_post_mix[l])

        nxt = (g_pre_mix[l + 1], mods[l + 1][1], mods[l + 1][0]) if l + 1 < L else None
        out = _ffn(xs, g_pre_ffn[l], sc2, sh2, wup_b, conv_w[l], conv_b[l], wdown_b, l, gt2, g_post_ffn[l], nxt)
        xs, h = out if nxt is not None else (out, None)
    return xs[None]
```

```python
import functools

import jax
import jax.numpy as jnp
import numpy as np
from jax import lax
from jax.experimental import pallas as pl
from jax.experimental.pallas import tpu as pltpu

F32 = jnp.float32
BF16 = jnp.bfloat16

A_HEADS = 8
A_KV_GROUPS = 2
A_HPG = A_HEADS // A_KV_GROUPS
HEAD_DIM = 128
A_WIDTH = A_HEADS * HEAD_DIM
A_KV_WIDTH = A_KV_GROUPS * HEAD_DIM
CMP_BLOCK = 32
CMP_STRIDE = 16
SEL_BLOCK = 64
SEL_TOPK = 16
WINDOW = 512
ROPE_THETA = 10000.0
HG_HEADS = 8
HG_DIM = 128
HG_WIDTH = HG_HEADS * HG_DIM
HG_CHUNK = 64
CONV_WIDTH = 3
NORM_EPS = 1e-6
N_ADA = 6
A_COLS = A_WIDTH + 6 * A_KV_WIDTH + 3 * A_HEADS

LANES = 128
SEL_SLOTS = LANES
KEY_TILE = 256
NEG_BIG = -1e30
VMEM_LIMIT = 56 * 1024 * 1024


def _cparams(sem):
    return pltpu.CompilerParams(dimension_semantics=sem, vmem_limit_bytes=VMEM_LIMIT)


def _split3(a):
    hi = a.astype(BF16)
    r1 = a - hi.astype(F32)
    mid = r1.astype(BF16)
    lo = (r1 - mid.astype(F32)).astype(BF16)
    return hi, mid, lo


def _dot(a, b):
    return jnp.dot(a, b, preferred_element_type=F32)


def _dot_nt(a, b):
    return lax.dot_general(a, b, (((1,), (1,)), ((), ())), preferred_element_type=F32)


def _dot_tn(a, b):
    return lax.dot_general(a, b, (((0,), (0,)), ((), ())), preferred_element_type=F32)


def _ada_kernel(c_ref, w_ref, b_ref, o_ref):
    c = c_ref[...]
    ca = c * jax.nn.sigmoid(c)
    o_ref[0] = jnp.sum(ca * w_ref[0], axis=0, keepdims=True) + b_ref[0]


def _ada_all(c, w_ada, b_ada):
    L, D, N = w_ada.shape
    tn = 1024
    return pl.pallas_call(
        _ada_kernel,
        grid=(L, N // tn),
        in_specs=[
            pl.BlockSpec((D, 1), lambda l, n: (0, 0)),
            pl.BlockSpec((1, D, tn), lambda l, n: (l, 0, n)),
            pl.BlockSpec((1, 1, tn), lambda l, n: (l, 0, n)),
        ],
        out_specs=pl.BlockSpec((1, 1, tn), lambda l, n: (l, 0, n)),
        out_shape=jax.ShapeDtypeStruct((L, 1, N), F32),
        compiler_params=_cparams(("arbitrary", "arbitrary")),
        name="ada",
    )(c.reshape(D, 1), w_ada, b_ada.reshape(L, 1, N))


def _prenorm_kernel(x_ref, g_ref, sc_ref, sh_ref, o_ref):
    x = x_ref[...]
    ms = jnp.mean(x * x, axis=-1, keepdims=True)
    y = x * lax.rsqrt(ms + NORM_EPS) * g_ref[...]
    o_ref[...] = (y * (1.0 + sc_ref[...]) + sh_ref[...]).astype(o_ref.dtype)


def _prenorm(x, g, sc, sh):
    S, D = x.shape
    tm = 512
    row = pl.BlockSpec((1, D), lambda m: (0, 0))
    return pl.pallas_call(
        _prenorm_kernel,
        grid=(S // tm,),
        in_specs=[pl.BlockSpec((tm, D), lambda m: (m, 0)), row, row, row],
        out_specs=pl.BlockSpec((tm, D), lambda m: (m, 0)),
        out_shape=jax.ShapeDtypeStruct((S, D), BF16),
        compiler_params=_cparams(("arbitrary",)),
        name="prenorm",
    )(x, g.reshape(1, D), sc, sh)


def _rope(t, cosf, sinf):
    return t * cosf + pltpu.roll(t, HEAD_DIM // 2, axis=1) * sinf


_PROJ_PART = 256


def _proj_kernel(h_ref, w_ref, *rest, roles, tn):
    o_ref, w_scr = rest[-2:]
    tabs = rest[:-2]
    n = pl.program_id(0)

    @pl.when(pl.program_id(1) == 0)
    def _():
        w_scr[...] = w_ref[0].astype(BF16)

    def emit(slabs):
        h = h_ref[...]
        per = _PROJ_PART // LANES
        parts = [_dot_nt(h, w_scr[c0:c0 + _PROJ_PART, :]) for c0 in range(0, tn, _PROJ_PART)]
        for pi, acc in enumerate(parts):
            for s in range(per):
                kind = slabs[pi * per + s]
                t = acc[:, s * LANES:(s + 1) * LANES]
                if kind == "sg":
                    t = jax.nn.sigmoid(t)
                elif kind != "id":
                    cos_ref, sin_ref = tabs[0:2] if kind == "rq" else tabs[-2:]
                    t = _rope(t, cos_ref[...], sin_ref[...])
                c = (pi * per + s) * LANES
                o_ref[:, c:c + LANES] = t.astype(o_ref.dtype)

    kinds = sorted(set(roles))
    if len(kinds) == 1:
        emit(kinds[0])
        return
    for kind in kinds:
        hit = functools.reduce(jnp.logical_or, [n == t for t, r in enumerate(roles) if r == kind])
        pl.when(hit)(functools.partial(emit, kind))


def _proj(h, w_t, layer, row0, roles, out_dtype, *, tn, tables=(), name="proj"):
    S, D = h.shape
    tm = 1024 if S % 1024 == 0 else S
    nt = len(roles)
    tab = pl.BlockSpec((tm, HEAD_DIM), lambda n, m: (m, 0))
    return pl.pallas_call(
        functools.partial(_proj_kernel, roles=tuple(roles), tn=tn),
        grid=(nt, S // tm),
        in_specs=[pl.BlockSpec((tm, D), lambda n, m: (m, 0)),
                  pl.BlockSpec((pl.Element(1), pl.Element(tn), pl.Element(D)),
                               lambda n, m: (layer, pl.multiple_of(row0 + n * tn, 8), 0))]
                 + [tab] * len(tables),
        out_specs=pl.BlockSpec((tm, tn), lambda n, m: (m, n)),
        out_shape=jax.ShapeDtypeStruct((S, nt * tn), out_dtype),
        scratch_shapes=[pltpu.VMEM((tn, D), BF16)],
        compiler_params=_cparams(("arbitrary", "arbitrary")),
        name=name,
    )(h, w_t, *tables)


A_TILE = 512
_KV_TILE_ROLES = ("rk",) * A_KV_GROUPS + ("id",) * A_KV_GROUPS
A_ROLES = ((("rq",) * (A_TILE // LANES),) * (A_WIDTH // A_TILE) + (_KV_TILE_ROLES,) * 3
           + (("sg",) * (A_TILE // LANES),))
_KV_BLOCK0 = A_WIDTH // LANES
BLK_KC, BLK_VC, BLK_KS, BLK_VS, BLK_KW, BLK_VW = [_KV_BLOCK0 + j * A_KV_GROUPS for j in range(6)]
BLK_GATE = _KV_BLOCK0 + 6 * A_KV_GROUPS


def _compress_kernel(t_ref, pe_ref, w_ref, o_ref, t32_ref, *, n_chunk):
    half = CMP_BLOCK // 2
    t32_ref[...] = t_ref[...].astype(F32)
    a = jnp.concatenate([t32_ref[pl.ds(j, n_chunk, stride=CMP_STRIDE), :] for j in range(half)],
                        axis=1).astype(BF16)
    kw = half * HEAD_DIM
    w = w_ref[0].astype(BF16)
    y_top = _dot(a, w[:kw])
    y_bot = _dot(a, w[kw:])
    pe = jnp.concatenate([pe_ref[0, j:j + 1, :] for j in range(CMP_BLOCK)], axis=1)
    pe_term = jnp.zeros((1, HEAD_DIM), F32)
    for piece in _split3(pe):
        pe_term = pe_term + _dot(piece, w)
    o_ref[0, 0] = y_top + pltpu.roll(y_bot, n_chunk - 1, axis=0) + pe_term


def _compress(a, pe2, w2):
    S = a.shape[0]
    n_chunk = S // CMP_STRIDE
    G = A_KV_GROUPS
    return pl.pallas_call(
        functools.partial(_compress_kernel, n_chunk=n_chunk),
        grid=(2, G),
        in_specs=[
            pl.BlockSpec((S, HEAD_DIM), lambda kv, g: (0, BLK_KC + kv * G + g)),
            pl.BlockSpec((1, CMP_BLOCK, HEAD_DIM), lambda kv, g: (kv, 0, 0)),
            pl.BlockSpec((1, CMP_BLOCK * HEAD_DIM, HEAD_DIM), lambda kv, g: (kv, 0, 0)),
        ],
        out_specs=pl.BlockSpec((1, 1, n_chunk, HEAD_DIM), lambda kv, g: (kv, g, 0, 0)),
        out_shape=jax.ShapeDtypeStruct((2, G, n_chunk, HEAD_DIM), F32),
        scratch_shapes=[pltpu.VMEM((S, HEAD_DIM), F32)],
        compiler_params=_cparams(("arbitrary", "arbitrary")),
        name="compress",
    )(a, pe2, w2)


NSA_QB = 256
SEL_BODY = 8
SEL_GROUP = 4
_N_FORCED = 3


def _nsa_kernel(q_ref, ks_ref, kw_ref, vs_ref, vw_ref, kc_ref, vc_ref, ovt_ref, gate_ref,
                o_ref, kaug_ref, vt_ref, vwt_ref, vct_ref, acc_ref, m_ref, l_ref, *, seq):
    grp = pl.program_id(0)
    i = pl.program_id(1)
    t0 = i * NSA_QB
    n_cmp = seq // CMP_STRIDE
    cols = A_HPG * NSA_QB

    @pl.when(i == 0)
    def _():
        def build(r, c):
            r0 = pl.multiple_of(r * KEY_TILE, KEY_TILE)
            kaug_ref[r, :, 0:HEAD_DIM] = ks_ref[pl.ds(r0, KEY_TILE), :]
            blk = (r0 + lax.broadcasted_iota(jnp.int32, (KEY_TILE, SEL_SLOTS), 0)) // SEL_BLOCK
            slot = lax.broadcasted_iota(jnp.int32, (KEY_TILE, SEL_SLOTS), 1)
            kaug_ref[r, :, HEAD_DIM:HEAD_DIM + SEL_SLOTS] = jnp.where(blk == slot, 1.0, 0.0).astype(BF16)
            vt_ref[r] = vs_ref[pl.ds(r0, KEY_TILE), :].astype(F32).T.astype(BF16)
            vwt_ref[r] = vw_ref[pl.ds(r0, KEY_TILE), :].astype(F32).T.astype(BF16)
            return c
        lax.fori_loop(0, seq // KEY_TILE, build, 0)
        vct_ref[...] = vc_ref[0, 0].T.astype(BF16)

    qt = jnp.concatenate(
        [q_ref[:, h * HEAD_DIM:(h + 1) * HEAD_DIM].astype(F32).T.astype(BF16) for h in range(A_HPG)],
        axis=1)
    tq = t0 + lax.broadcasted_iota(jnp.int32, (1, cols), 1) % NSA_QB

    def softmax0(s):
        m = jnp.max(s, axis=0, keepdims=True)
        p = jnp.exp2(s - m)
        return p, jnp.sum(p, axis=0, keepdims=True)

    s_c = _dot(kc_ref[0, 0].astype(BF16), qt)
    nrow = lax.broadcasted_iota(jnp.int32, (n_cmp, 1), 0)
    vis = (nrow * CMP_STRIDE + (CMP_BLOCK - 1)) <= tq
    p_c, den = softmax0(jnp.where(vis, s_c, NEG_BIG))
    p_c = p_c * jnp.where(tq >= CMP_BLOCK - 1, 1.0 / den, 0.0)
    o_ct = _dot(vct_ref[...], p_c.astype(BF16))

    psum = p_c[:, 0:NSA_QB]
    for h in range(1, A_HPG):
        psum = psum + p_c[:, h * NSA_QB:(h + 1) * NSA_QB]
    ovt = ovt_ref[...]
    imp_t = jnp.zeros((SEL_SLOTS, NSA_QB), F32)
    for piece in _split3(psum):
        imp_t = imp_t + _dot(ovt, piece)
    slot_i = lax.broadcasted_iota(jnp.int32, (SEL_SLOTS, NSA_QB), 0)
    tq_l = t0 + lax.broadcasted_iota(jnp.int32, (SEL_SLOTS, NSA_QB), 1)
    cur = tq_l // SEL_BLOCK
    forced = jnp.where(slot_i == 0, 1, 0) + jnp.where(slot_i == cur, 1, 0) + jnp.where(slot_i == cur - 1, 1, 0)
    valid = slot_i * SEL_BLOCK <= tq_l
    key = jnp.where(forced > 0, -3e38, jnp.where(valid, imp_t, NEG_BIG))
    slot_f = slot_i.astype(F32)
    bias_t = jnp.where(forced > 0, 0.0, NEG_BIG)
    for _ in range(SEL_TOPK - _N_FORCED):
        mx = jnp.max(key, axis=0, keepdims=True)
        first = jnp.min(jnp.where(key == mx, slot_f, 1e9), axis=0, keepdims=True)
        pick = slot_f == first
        bias_t = jnp.where(pick, 0.0, bias_t)
        key = jnp.where(pick, -3e38, key)

    wlen = WINDOW + NSA_QB
    w0 = pl.multiple_of(jnp.maximum(t0 - WINDOW, 0), KEY_TILE)
    wt = jnp.maximum(i - WINDOW // KEY_TILE, 0)
    s_w = _dot(kw_ref[pl.ds(w0, wlen), :], qt)
    dist = tq - (w0 + lax.broadcasted_iota(jnp.int32, (wlen, 1), 0))
    in_window = dist.astype(jnp.uint32) < WINDOW
    p_w, l_w = softmax0(jnp.where(in_window, s_w, NEG_BIG))
    p_w = p_w.astype(BF16)
    o_wt = _dot(vwt_ref[wt], p_w[0:KEY_TILE])
    for k in range(1, wlen // KEY_TILE):
        o_wt = o_wt + _dot(vwt_ref[wt + k], p_w[k * KEY_TILE:(k + 1) * KEY_TILE])
    o_wt = o_wt * (1.0 / l_w)

    qt_aug = jnp.concatenate([qt, jnp.concatenate([bias_t.astype(BF16)] * A_HPG, axis=1)], axis=0)
    acc_ref[...] = jnp.zeros_like(acc_ref)
    m_ref[...] = jnp.full(m_ref.shape, 0.5 * NEG_BIG, F32)
    l_ref[...] = jnp.zeros_like(l_ref)

    def sel_body(base, causal):
        n = len(causal)
        m, l = m_ref[...], l_ref[...]
        gsz = min(SEL_GROUP, max(n // 2, 1))
        groups = [list(range(g0, min(g0 + gsz, n))) for g0 in range(0, n, gsz)]

        def scores(ks):
            out = {}
            for k in ks:
                sk = _dot(kaug_ref[base + k], qt_aug)
                if causal[k]:
                    kpos = (base + k) * KEY_TILE + lax.broadcasted_iota(jnp.int32, (KEY_TILE, 1), 0)
                    sk = jnp.where(kpos <= tq, sk, NEG_BIG)
                out[k] = sk
            return out

        s = scores(groups[0])
        for gi, ks in enumerate(groups):
            if gi + 1 < len(groups):
                s.update(scores(groups[gi + 1]))
            m_new = m
            for k in ks:
                m_new = jnp.maximum(m_new, jnp.max(s[k], axis=0, keepdims=True))
            alpha = jnp.exp2(m - m_new)
            l = alpha * l
            pv = None
            for k in ks:
                p = jnp.exp2(s.pop(k) - m_new)
                l = l + jnp.sum(p, axis=0, keepdims=True)
                d = _dot(vt_ref[base + k], p.astype(BF16))
                pv = d if pv is None else pv + d
            acc_ref[...] = alpha * acc_ref[...] + pv
            m = m_new
        m_ref[...] = m
        l_ref[...] = l

    n_bulk = i // SEL_BODY
    rem = i % SEL_BODY

    def bulk(b, c):
        sel_body(b * SEL_BODY, (False,) * SEL_BODY)
        return c
    lax.fori_loop(0, n_bulk, bulk, 0)

    base = n_bulk * SEL_BODY
    part = SEL_BODY // 2
    while part >= 2:
        pl.when(rem & part != 0)(functools.partial(sel_body, base, (False,) * part))
        base = base + (rem & part)
        part //= 2

    @pl.when(rem % 2 == 1)
    def _():
        sel_body(i - 1, (False, True))

    @pl.when(rem % 2 == 0)
    def _():
        sel_body(i, (True,))

    o_st = acc_ref[...] * (1.0 / l_ref[...])

    gates_t = gate_ref[...].astype(F32).T
    per_group = 3 * A_HPG

    def gate(h, b):
        r = 3 * h + b
        return jnp.where(grp == 0, gates_t[r:r + 1], gates_t[per_group + r:per_group + r + 1])

    for h in range(A_HPG):
        c = slice(h * NSA_QB, (h + 1) * NSA_QB)
        o = gate(h, 0) * o_ct[:, c] + gate(h, 1) * o_st[:, c] + gate(h, 2) * o_wt[:, c]
        o_ref[:, h * HEAD_DIM:(h + 1) * HEAD_DIM] = o.T.astype(o_ref.dtype)


def _overlap_matrix(seq):
    n_chunk = seq // CMP_STRIDE
    cs = np.arange(n_chunk)[:, None] * CMP_STRIDE
    ss = np.arange(SEL_SLOTS)[None, :] * SEL_BLOCK
    ov = np.maximum(np.minimum(cs + CMP_BLOCK, ss + SEL_BLOCK) - np.maximum(cs, ss), 0) / CMP_BLOCK
    n_cmp = (seq - CMP_BLOCK) // CMP_STRIDE + 1
    ov[n_cmp:] = 0.0
    ov[:, seq // SEL_BLOCK:] = 0.0
    return ov.astype(np.float32)


def _nsa(a, cmp_kv):
    S = a.shape[0]
    G = A_KV_GROUPS
    n_tiles = S // KEY_TILE
    assert S // SEL_BLOCK <= SEL_SLOTS and S % KEY_TILE == 0 and S >= WINDOW + NSA_QB
    assert NSA_QB == KEY_TILE and G == 2
    n_chunk = S // CMP_STRIDE
    ovt = jnp.asarray(_overlap_matrix(S).T, BF16)
    col = lambda b: pl.BlockSpec((S, HEAD_DIM), lambda g, i, b=b: (0, b + g))
    return pl.pallas_call(
        functools.partial(_nsa_kernel, seq=S),
        grid=(G, S // NSA_QB),
        in_specs=[
            pl.BlockSpec((NSA_QB, A_HPG * HEAD_DIM), lambda g, i: (i, g)),
            col(BLK_KS), col(BLK_KW), col(BLK_VS), col(BLK_VW),
            pl.BlockSpec((1, 1, n_chunk, HEAD_DIM), lambda g, i: (0, g, 0, 0)),
            pl.BlockSpec((1, 1, n_chunk, HEAD_DIM), lambda g, i: (1, g, 0, 0)),
            pl.BlockSpec((SEL_SLOTS, n_chunk), lambda g, i: (0, 0)),
            pl.BlockSpec((NSA_QB, LANES), lambda g, i: (i, BLK_GATE)),
        ],
        out_specs=pl.BlockSpec((NSA_QB, A_HPG * HEAD_DIM), lambda g, i: (i, g)),
        out_shape=jax.ShapeDtypeStruct((S, A_WIDTH), BF16),
        scratch_shapes=[pltpu.VMEM((n_tiles, KEY_TILE, HEAD_DIM + SEL_SLOTS), BF16),
                        pltpu.VMEM((n_tiles, HEAD_DIM, KEY_TILE), BF16),
                        pltpu.VMEM((n_tiles, HEAD_DIM, KEY_TILE), BF16),
                        pltpu.VMEM((HEAD_DIM, n_chunk), BF16),
                        pltpu.VMEM((HEAD_DIM, A_HPG * NSA_QB), F32),
                        pltpu.VMEM((1, A_HPG * NSA_QB), F32),
                        pltpu.VMEM((1, A_HPG * NSA_QB), F32)],
        compiler_params=_cparams(("arbitrary", "arbitrary")),
        name="nsa",
    )(a, a, a, a, a, cmp_kv, cmp_kv, ovt, a)


_ROW_MASKS = 18
_SCORE_MASKS = 10
HG_HEADS_PER_STEP = 8


def _hgrn_tables():
    t = np.arange(HG_CHUNK)
    p4, i4, i16 = t % 4, (t // 4) % 4, t // 16
    conds = [p4 >= 1, p4 >= 2, p4 >= 3, p4 <= 2, p4 <= 1, p4 == 0,
             i4 >= 1, i4 >= 2, i4 >= 3, i4 <= 2, i4 <= 1, i4 == 0,
             i16 >= 1, i16 >= 2, i16 >= 3, i16 <= 2, i16 <= 1, i16 == 0]
    rm = np.stack([np.broadcast_to(c[:, None], (HG_CHUNK, HG_DIM)) for c in conds]).astype(np.float32)
    tt, ss = t[:, None], t[None, :]
    lag16 = tt // 16 - ss // 16
    lag4 = tt // 4 - ss // 4
    same16, same4 = lag16 == 0, lag4 == 0
    sm = [lag16 == 1, lag16 == 2, lag16 == 3,
          same16 & (lag4 == 1), same16 & (lag4 == 2), same16 & (lag4 == 3),
          same4 & (tt - ss == 0), same4 & (tt - ss == 1), same4 & (tt - ss == 2), same4 & (tt - ss == 3)]
    return rm, np.stack(sm).astype(np.float32)


def _hgrn_kernel(q_ref, f_ref, i_ref, og_ref, loglb_ref, log1m_ref, gn_ref, rm_ref, sm_ref,
                 o_ref, st_ref, g_scr, k_scr, o_scr, *, tb, nh):
    C = HG_CHUNK
    heads = [slice(h * HG_DIM, (h + 1) * HG_DIM) for h in range(nh)]

    @pl.when(pl.program_id(1) == 0)
    def _():
        st_ref[...] = jnp.zeros_like(st_ref)

    x = f_ref[...]
    log_sig = jnp.minimum(x, 0.0) - jnp.log(1.0 + jnp.exp(-jnp.abs(x)))
    a = loglb_ref[...]
    c = log1m_ref[...] + log_sig
    log_f = jnp.maximum(a, c) + jnp.log(1.0 + jnp.exp(-jnp.abs(a - c)))
    g_scr[...] = log_f
    k_scr[...] = 1.0 - jnp.exp(log_f)

    def rmask(n):
        return jnp.concatenate([rm_ref[n]] * nh, axis=1)

    def roll(v, s):
        return pltpu.roll(v, s % C, axis=0)

    def b16(z):
        return z.astype(BF16)

    def chunk(ci, carry):
        r0 = pl.multiple_of(ci * C, C)
        g = g_scr[pl.ds(r0, C), :]
        q = q_ref[pl.ds(r0, C), :]
        k = k_scr[pl.ds(r0, C), :]
        v = i_ref[pl.ds(r0, C), :].astype(BF16)
        pre4 = g + rmask(0) * roll(g, 1) + rmask(1) * roll(g, 2) + rmask(2) * roll(g, 3)
        suf4 = rmask(3) * roll(g, -1) + rmask(4) * roll(g, -2) + rmask(5) * roll(g, -3)
        tot4 = pre4 + suf4
        a1, a2, a3 = roll(tot4, 4), roll(tot4, 8), roll(tot4, 12)
        pre16 = pre4 + rmask(6) * a1 + rmask(7) * a2 + rmask(8) * a3
        suf16 = suf4 + rmask(9) * roll(tot4, -4) + rmask(10) * roll(tot4, -8) + rmask(11) * roll(tot4, -12)
        tot16 = pre16 + suf16
        d1, d2, d3 = roll(tot16, 16), roll(tot16, 32), roll(tot16, 48)
        pre64 = pre16 + rmask(12) * d1 + rmask(13) * d2 + rmask(14) * d3
        suf64 = suf16 + rmask(15) * roll(tot16, -16) + rmask(16) * roll(tot16, -32) + rmask(17) * roll(tot16, -48)
        tot64 = pre64 + suf64
        g1, g2 = roll(g, 1), roll(g, 2)

        q_a = q * jnp.exp(pre16)
        lhs_a = [b16(q_a), b16(q_a * jnp.exp(d1)), b16(q_a * jnp.exp(d1 + d2))]
        k_a = b16(k * jnp.exp(suf16))
        q_b = q * jnp.exp(pre4)
        lhs_b = [b16(q_b), b16(q_b * jnp.exp(a1)), b16(q_b * jnp.exp(a1 + a2))]
        k_b = b16(k * jnp.exp(suf4))
        q_c = q * jnp.exp(g)
        lhs_c = [b16(q), b16(q_c), b16(q_c * jnp.exp(g1)), b16(q_c * jnp.exp(g1 + g2))]
        k_c = b16(k)
        q_s = b16(q * jnp.exp(pre64))
        k_s = b16(k * jnp.exp(suf64))
        decay = jnp.exp(tot64[0:1, :])

        def level(lhs, rhs):
            return [_dot_nt(jnp.concatenate([z[:, hh] for z in lhs], axis=0), rhs[:, hh]) for hh in heads]

        s_a, s_b, s_c = level(lhs_a, k_a), level(lhs_b, k_b), level(lhs_c, k_c)
        o_inter = [_dot_nt(q_s[:, hh], b16(st_ref[h])) for h, hh in enumerate(heads)]
        kv = [_dot_tn(v[:, hh], k_s[:, hh]) for hh in heads]
        for h, hh in enumerate(heads):
            sa, sb, sc = s_a[h], s_b[h], s_c[h]
            scores = (sm_ref[0] * sa[0:C] + sm_ref[1] * sa[C:2 * C] + sm_ref[2] * sa[2 * C:3 * C]
                      + sm_ref[3] * sb[0:C] + sm_ref[4] * sb[C:2 * C] + sm_ref[5] * sb[2 * C:3 * C]
                      + sm_ref[6] * sc[0:C] + sm_ref[7] * sc[C:2 * C] + sm_ref[8] * sc[2 * C:3 * C]
                      + sm_ref[9] * sc[3 * C:4 * C])
            o_scr[pl.ds(r0, C), hh] = _dot(b16(scores), v[:, hh]) + o_inter[h]
            st_ref[h] = st_ref[h] * decay[:, hh] + kv[h]
        return carry

    lax.fori_loop(0, tb // C, chunk, 0)

    for hh in heads:
        o = o_scr[:, hh]
        ms = jnp.mean(o * o, axis=-1, keepdims=True)
        og = og_ref[:, hh]
        y = o * lax.rsqrt(ms + NORM_EPS) * gn_ref[:, hh] * (og * jax.nn.sigmoid(og))
        o_ref[:, hh] = y.astype(o_ref.dtype)


def _hgrn(hz, loglb, log1mlb, gnorm):
    S = hz.shape[0]
    tb = 512 if S % 512 == 0 else S
    nh = HG_HEADS_PER_STEP
    ng = HG_HEADS // nh
    w = nh * HG_DIM
    rm, sm = _hgrn_tables()
    blk = lambda part: pl.BlockSpec((tb, w), lambda h, c, part=part: (c, part * ng + h))
    row = pl.BlockSpec((1, w), lambda h, c: (0, h))
    return pl.pallas_call(
        functools.partial(_hgrn_kernel, tb=tb, nh=nh),
        grid=(ng, S // tb),
        in_specs=[blk(0), blk(1), blk(2), blk(3), row, row, row,
                  pl.BlockSpec((_ROW_MASKS, HG_CHUNK, HG_DIM), lambda h, c: (0, 0, 0)),
                  pl.BlockSpec((_SCORE_MASKS, HG_CHUNK, HG_CHUNK), lambda h, c: (0, 0, 0))],
        out_specs=pl.BlockSpec((tb, w), lambda h, c: (c, h)),
        out_shape=jax.ShapeDtypeStruct((S, HG_WIDTH), BF16),
        scratch_shapes=[pltpu.VMEM((nh, HG_DIM, HG_DIM), F32), pltpu.VMEM((tb, w), F32),
                        pltpu.VMEM((tb, w), F32), pltpu.VMEM((tb, w), F32)],
        compiler_params=_cparams(("arbitrary", "arbitrary")),
        name="hgrn",
    )(hz, hz, hz, hz, loglb, log1mlb, gnorm, jnp.asarray(rm), jnp.asarray(sm))


_MERGE_PART = 512


def _merge_kernel(at_ref, hg_ref, ga_ref, gh_ref, wa_ref, wh_ref, wo_ref, x_ref, gt_ref, gp_ref, o_ref):
    at, hg = at_ref[...], hg_ref[...]
    d = wo_ref.shape[1]
    parts = [slice(c0, c0 + _MERGE_PART) for c0 in range(0, d, _MERGE_PART)]
    branch = [(_dot(at, wa_ref[0, :, cs]), _dot(hg, wh_ref[0, :, cs])) for cs in parts]
    y = None
    for cs, (ma, mh) in zip(parts, branch):
        mixed = ga_ref[:, cs].astype(F32) * ma + gh_ref[:, cs].astype(F32) * mh
        yp = _dot(mixed.astype(BF16), wo_ref[0, cs, :])
        y = yp if y is None else y + yp
    ms = jnp.mean(y * y, axis=-1, keepdims=True)
    yn = y * lax.rsqrt(ms + NORM_EPS) * gp_ref[...]
    o_ref[...] = x_ref[...] + gt_ref[...] * yn


def _merge(attn, hg, mg, wa, wh, wo, layer, x, gt, gpost):
    S, D = x.shape
    tm = 256
    const = lambda shape: pl.BlockSpec((1,) + shape, lambda m: (layer, 0, 0), pipeline_mode=pl.Buffered(1))
    row = pl.BlockSpec((1, D), lambda m: (0, 0))
    return pl.pallas_call(
        _merge_kernel,
        grid=(S // tm,),
        in_specs=[
            pl.BlockSpec((tm, A_WIDTH), lambda m: (m, 0)),
            pl.BlockSpec((tm, HG_WIDTH), lambda m: (m, 0)),
            pl.BlockSpec((tm, D), lambda m: (m, 0)),
            pl.BlockSpec((tm, D), lambda m: (m, 1)),
            const((A_WIDTH, D)), const((HG_WIDTH, D)), const((D, D)),
            pl.BlockSpec((tm, D), lambda m: (m, 0)),
            row, row,
        ],
        out_specs=pl.BlockSpec((tm, D), lambda m: (m, 0)),
        out_shape=jax.ShapeDtypeStruct((S, D), F32),
        compiler_params=_cparams(("arbitrary",)),
        name="merge",
    )(attn, hg, mg, mg, wa, wh, wo, x, gt, gpost.reshape(1, D))


_HALO = 16
_FF_SUB = 256


def _norm_mod(x, g_ref, sc_ref, sh_ref):
    ms = jnp.mean(x * x, axis=-1, keepdims=True)
    y = x * lax.rsqrt(ms + NORM_EPS) * g_ref[...]
    return (y * (1.0 + sc_ref[...]) + sh_ref[...]).astype(BF16)


def _ffn_kernel(x_ref, xh_ref, g_ref, sc_ref, sh_ref, wg_ref, wv_ref, cwg_ref, cwv_ref, cbg_ref, cbv_ref,
                wd_ref, gt_ref, gp_ref, *rest, tm, emit_next):
    if emit_next:
        gn_ref, scn_ref, shn_ref, o_ref, hn_ref, h_scr, ug_scr, uv_scr = rest
    else:
        o_ref, h_scr, ug_scr, uv_scr = rest
    m = pl.program_id(0)
    f = pl.program_id(1)
    norm_mod = functools.partial(_norm_mod, g_ref=g_ref, sc_ref=sc_ref, sh_ref=sh_ref)

    @pl.when(f == 0)
    def _():
        h_scr[0:_HALO, :] = jnp.where(m > 0, norm_mod(xh_ref[...]), jnp.zeros((), BF16))
        h_scr[_HALO:_HALO + tm, :] = norm_mod(x_ref[...])

    tf = wg_ref.shape[2]
    subs = [slice(s * _FF_SUB, (s + 1) * _FF_SUB) for s in range(tf // _FF_SUB)]
    h = h_scr[...]
    ups = [(_dot(h, wg_ref[0, :, cs]), _dot(h, wv_ref[0, :, cs])) for cs in subs]

    def conv(u_scr, cw_ref, cb_ref, cs):
        out = cb_ref[:, cs] + cw_ref[0:1, cs] * u_scr[pl.ds(_HALO - 2, tm), cs]
        out = out + cw_ref[1:2, cs] * u_scr[pl.ds(_HALO - 1, tm), cs]
        return out + cw_ref[2:3, cs] * u_scr[pl.ds(_HALO, tm), cs]

    acc = jnp.where(f > 0, o_ref[...], 0.0)
    for cs, (ug, uv) in zip(subs, ups):
        ug_scr[:, cs] = ug
        uv_scr[:, cs] = uv
        cg = conv(ug_scr, cwg_ref, cbg_ref, cs)
        cv = conv(uv_scr, cwv_ref, cbv_ref, cs)
        act = (cg * jax.nn.sigmoid(cg) * cv).astype(BF16)
        acc = acc + _dot(act, wd_ref[0, cs, :])
    o_ref[...] = acc

    @pl.when(f == pl.num_programs(1) - 1)
    def _():
        y = o_ref[...]
        ms = jnp.mean(y * y, axis=-1, keepdims=True)
        yn = y * lax.rsqrt(ms + NORM_EPS) * gp_ref[...]
        x_new = x_ref[...] + gt_ref[...] * yn
        o_ref[...] = x_new
        if emit_next:
            hn_ref[...] = _norm_mod(x_new, gn_ref, scn_ref, shn_ref)


def _ffn(x, gpre, sc, sh, wup, convw, convb, wdown, layer, gt, gpost, nxt=None):
    S, D = x.shape
    F = wdown.shape[1]
    tm = 512 if S % 512 == 0 else S
    tf = 512
    nf = F // tf
    hb = tm // _HALO
    row = pl.BlockSpec((1, D), lambda m, f: (0, 0))
    tile = pl.BlockSpec((tm, D), lambda m, f: (m, 0))
    extra = () if nxt is None else (nxt[0].reshape(1, D), nxt[1], nxt[2])
    return pl.pallas_call(
        functools.partial(_ffn_kernel, tm=tm, emit_next=nxt is not None),
        grid=(S // tm, nf),
        in_specs=[
            pl.BlockSpec((tm, D), lambda m, f: (m, 0)),
            pl.BlockSpec((_HALO, D), lambda m, f: (jnp.maximum(m * hb - 1, 0), 0)),
            row, row, row,
            pl.BlockSpec((1, D, tf), lambda m, f: (layer, 0, f)),
            pl.BlockSpec((1, D, tf), lambda m, f: (layer, 0, nf + f)),
            pl.BlockSpec((CONV_WIDTH, tf), lambda m, f: (0, f)),
            pl.BlockSpec((CONV_WIDTH, tf), lambda m, f: (0, nf + f)),
            pl.BlockSpec((1, tf), lambda m, f: (0, f)),
            pl.BlockSpec((1, tf), lambda m, f: (0, nf + f)),
            pl.BlockSpec((1, tf, D), lambda m, f: (layer, f, 0)),
            row, row,
        ] + ([row] * 3 if nxt is not None else []),
        out_specs=[tile, tile] if nxt is not None else tile,
        out_shape=([jax.ShapeDtypeStruct((S, D), F32), jax.ShapeDtypeStruct((S, D), BF16)]
                   if nxt is not None else jax.ShapeDtypeStruct((S, D), F32)),
        scratch_shapes=[pltpu.VMEM((tm + _HALO, D), BF16),
                        pltpu.VMEM((tm + _HALO, tf), F32), pltpu.VMEM((tm + _HALO, tf), F32)],
        compiler_params=_cparams(("arbitrary", "arbitrary")),
        name="ffn",
    )(x, x, gpre.reshape(1, D), sc, sh, wup, wup, convw, convw, convb.reshape(1, -1), convb.reshape(1, -1),
      wdown, gt, gpost.reshape(1, D), *extra)


def _rope_tables(positions):
    inv_freq = 1.0 / (ROPE_THETA ** (jnp.arange(0, HEAD_DIM, 2, dtype=F32) / HEAD_DIM))
    ang = positions.astype(F32)[:, None] * inv_freq
    cos, sin = jnp.cos(ang), jnp.sin(ang)
    return jnp.concatenate([cos, cos], axis=-1), jnp.concatenate([-sin, sin], axis=-1)


def kernel(x, c, positions, w_ada, b_ada, g_pre_mix, w_in, pe_kc, w_kc, pe_vc, w_vc, lb_logits, g_hg_norm,
           w_br_attn, w_br_hgrn, w_out, g_post_mix, g_pre_ffn, w_up, conv_w, conv_b, w_down, g_post_ffn):
    B, S, D = x.shape
    assert B == 1, "kernel is written for one sequence"
    L = w_ada.shape[0]
    xs = x[0]
    cosf, sinf = _rope_tables(positions[0])
    scale = HEAD_DIM ** -0.5 * np.log2(np.e)
    tables = (cosf * scale, sinf * scale, cosf, sinf)
    lb_cum = jnp.cumsum(jax.nn.softmax(lb_logits.astype(F32), axis=0), axis=0)
    lower = lb_cum - lb_cum[0:1]
    log_lb = jnp.log(lower)
    log_1m = jnp.log1p(-lower)
    ada = _ada_all(c, w_ada, b_ada)

    w_in_t = jnp.swapaxes(w_in, 1, 2)
    tail_tn = 1024
    hg_roles = (("id",) * (tail_tn // LANES),) * (4 * HG_WIDTH // tail_tn)
    mg_roles = (("sg",) * (tail_tn // LANES),) * (2 * D // tail_tn)

    wa_b, wh_b, wo_b = w_br_attn.astype(BF16), w_br_hgrn.astype(BF16), w_out.astype(BF16)
    wup_b, wdown_b = w_up.astype(BF16), w_down.astype(BF16)

    mods = [[ada[l, :, j * D:(j + 1) * D] for j in range(N_ADA)] for l in range(L)]
    h = _prenorm(xs, g_pre_mix[0], mods[0][1], mods[0][0])
    for l in range(L):
        sh1, sc1, gt1, sh2, sc2, gt2 = mods[l]

        a = _proj(h, w_in_t, l, 0, A_ROLES, BF16, tn=A_TILE, tables=tables, name="proj_attn")
        hz = _proj(h, w_in_t, l, A_COLS, hg_roles, F32, tn=tail_tn, name="proj_hgrn")
        mg = _proj(h, w_in_t, l, A_COLS + 4 * HG_WIDTH, mg_roles, BF16, tn=tail_tn, name="proj_merge_gate")

        cmp_kv = _compress(a, jnp.stack([pe_kc[l], pe_vc[l]]), jnp.stack([w_kc[l], w_vc[l]]))
        attn = _nsa(a, cmp_kv)
        hg = _hgrn(hz, log_lb[l:l + 1], log_1m[l:l + 1], g_hg_norm[l:l + 1])
        xs = _merge(attn, hg, mg, wa_b, wh_b, wo_b, l, xs, gt1, g_post_mix[l])

        nxt = (g_pre_mix[l + 1], mods[l + 1][1], mods[l + 1][0]) if l + 1 < L else None
        out = _ffn(xs, g_pre_ffn[l], sc2, sh2, wup_b, conv_w[l], conv_b[l], wdown_b, l, gt2, g_post_ffn[l], nxt)
        xs, h = out if nxt is not None else (out, None)
    return xs[None]
```

```python
import functools

import jax
import jax.numpy as jnp
import numpy as np
from jax import lax
from jax.experimental import pallas as pl
from jax.experimental.pallas import tpu as pltpu

F32 = jnp.float32
BF16 = jnp.bfloat16

A_HEADS = 8
A_KV_GROUPS = 2
A_HPG = A_HEADS // A_KV_GROUPS
HEAD_DIM = 128
A_WIDTH = A_HEADS * HEAD_DIM
A_KV_WIDTH = A_KV_GROUPS * HEAD_DIM
CMP_BLOCK = 32
CMP_STRIDE = 16
SEL_BLOCK = 64
SEL_TOPK = 16
WINDOW = 512
ROPE_THETA = 10000.0
HG_HEADS = 8
HG_DIM = 128
HG_WIDTH = HG_HEADS * HG_DIM
HG_CHUNK = 64
CONV_WIDTH = 3
NORM_EPS = 1e-6
N_ADA = 6
A_COLS = A_WIDTH + 6 * A_KV_WIDTH + 3 * A_HEADS

LANES = 128
SEL_SLOTS = LANES
KEY_TILE = 256
NEG_BIG = -1e30
VMEM_LIMIT = 56 * 1024 * 1024


def _cparams(sem):
    return pltpu.CompilerParams(dimension_semantics=sem, vmem_limit_bytes=VMEM_LIMIT)


def _split3(a):
    hi = a.astype(BF16)
    r1 = a - hi.astype(F32)
    mid = r1.astype(BF16)
    lo = (r1 - mid.astype(F32)).astype(BF16)
    return hi, mid, lo


def _dot(a, b):
    return jnp.dot(a, b, preferred_element_type=F32)


def _dot_nt(a, b):
    return lax.dot_general(a, b, (((1,), (1,)), ((), ())), preferred_element_type=F32)


def _dot_tn(a, b):
    return lax.dot_general(a, b, (((0,), (0,)), ((), ())), preferred_element_type=F32)


def _ada_kernel(c_ref, w_ref, b_ref, o_ref):
    c = c_ref[...]
    ca = c * jax.nn.sigmoid(c)
    o_ref[0] = jnp.sum(ca * w_ref[0], axis=0, keepdims=True) + b_ref[0]


def _ada_all(c, w_ada, b_ada):
    L, D, N = w_ada.shape
    tn = 1024
    return pl.pallas_call(
        _ada_kernel,
        grid=(L, N // tn),
        in_specs=[
            pl.BlockSpec((D, 1), lambda l, n: (0, 0)),
            pl.BlockSpec((1, D, tn), lambda l, n: (l, 0, n)),
            pl.BlockSpec((1, 1, tn), lambda l, n: (l, 0, n)),
        ],
        out_specs=pl.BlockSpec((1, 1, tn), lambda l, n: (l, 0, n)),
        out_shape=jax.ShapeDtypeStruct((L, 1, N), F32),
        compiler_params=_cparams(("arbitrary", "arbitrary")),
        name="ada",
    )(c.reshape(D, 1), w_ada, b_ada.reshape(L, 1, N))


def _prenorm_kernel(x_ref, g_ref, sc_ref, sh_ref, o_ref):
    x = x_ref[...]
    ms = jnp.mean(x * x, axis=-1, keepdims=True)
    y = x * lax.rsqrt(ms + NORM_EPS) * g_ref[...]
    o_ref[...] = (y * (1.0 + sc_ref[...]) + sh_ref[...]).astype(o_ref.dtype)


def _prenorm(x, g, sc, sh):
    S, D = x.shape
    tm = 512
    row = pl.BlockSpec((1, D), lambda m: (0, 0))
    return pl.pallas_call(
        _prenorm_kernel,
        grid=(S // tm,),
        in_specs=[pl.BlockSpec((tm, D), lambda m: (m, 0)), row, row, row],
        out_specs=pl.BlockSpec((tm, D), lambda m: (m, 0)),
        out_shape=jax.ShapeDtypeStruct((S, D), BF16),
        compiler_params=_cparams(("arbitrary",)),
        name="prenorm",
    )(x, g.reshape(1, D), sc, sh)


def _rope(t, cosf, sinf):
    return t * cosf + pltpu.roll(t, HEAD_DIM // 2, axis=1) * sinf


_PROJ_PART = 256


def _proj_kernel(h_ref, w_ref, *rest, roles, tn):
    o_ref, w_scr = rest[-2:]
    tabs = rest[:-2]
    n = pl.program_id(0)

    @pl.when(pl.program_id(1) == 0)
    def _():
        w_scr[...] = w_ref[0].astype(BF16)

    def emit(slabs):
        h = h_ref[...]
        per = _PROJ_PART // LANES
        parts = [_dot_nt(h, w_scr[c0:c0 + _PROJ_PART, :]) for c0 in range(0, tn, _PROJ_PART)]
        for pi, acc in enumerate(parts):
            for s in range(per):
                kind = slabs[pi * per + s]
                t = acc[:, s * LANES:(s + 1) * LANES]
                if kind == "sg":
                    t = jax.nn.sigmoid(t)
                elif kind != "id":
                    cos_ref, sin_ref = tabs[0:2] if kind == "rq" else tabs[-2:]
                    t = _rope(t, cos_ref[...], sin_ref[...])
                c = (pi * per + s) * LANES
                o_ref[:, c:c + LANES] = t.astype(o_ref.dtype)

    kinds = sorted(set(roles))
    if len(kinds) == 1:
        emit(kinds[0])
        return
    for kind in kinds:
        hit = functools.reduce(jnp.logical_or, [n == t for t, r in enumerate(roles) if r == kind])
        pl.when(hit)(functools.partial(emit, kind))


def _proj(h, w_t, layer, row0, roles, out_dtype, *, tn, tables=(), name="proj"):
    S, D = h.shape
    tm = 1024 if S % 1024 == 0 else S
    nt = len(roles)
    tab = pl.BlockSpec((tm, HEAD_DIM), lambda n, m: (m, 0))
    return pl.pallas_call(
        functools.partial(_proj_kernel, roles=tuple(roles), tn=tn),
        grid=(nt, S // tm),
        in_specs=[pl.BlockSpec((tm, D), lambda n, m: (m, 0)),
                  pl.BlockSpec((pl.Element(1), pl.Element(tn), pl.Element(D)),
                               lambda n, m: (layer, pl.multiple_of(row0 + n * tn, 8), 0))]
                 + [tab] * len(tables),
        out_specs=pl.BlockSpec((tm, tn), lambda n, m: (m, n)),
        out_shape=jax.ShapeDtypeStruct((S, nt * tn), out_dtype),
        scratch_shapes=[pltpu.VMEM((tn, D), BF16)],
        compiler_params=_cparams(("arbitrary", "arbitrary")),
        name=name,
    )(h, w_t, *tables)


A_TILE = 1024
_A_SLABS = (("rq",) * (A_WIDTH // LANES)
            + (("rk",) * A_KV_GROUPS + ("id",) * A_KV_GROUPS) * 3
            + ("sg",) * (-(A_WIDTH // LANES + 6 * A_KV_GROUPS) % (A_TILE // LANES)))
A_ROLES = tuple(_A_SLABS[i:i + A_TILE // LANES] for i in range(0, len(_A_SLABS), A_TILE // LANES))
_KV_BLOCK0 = A_WIDTH // LANES
BLK_KC, BLK_VC, BLK_KS, BLK_VS, BLK_KW, BLK_VW = [_KV_BLOCK0 + j * A_KV_GROUPS for j in range(6)]
BLK_GATE = _KV_BLOCK0 + 6 * A_KV_GROUPS


def _compress_kernel(t_ref, pe_ref, w_ref, o_ref, t32_ref, *, n_chunk):
    half = CMP_BLOCK // 2
    t32_ref[...] = t_ref[...].astype(F32)
    a = jnp.concatenate([t32_ref[pl.ds(j, n_chunk, stride=CMP_STRIDE), :] for j in range(half)],
                        axis=1).astype(BF16)
    kw = half * HEAD_DIM
    w = w_ref[0].astype(BF16)
    y_top = _dot(a, w[:kw])
    y_bot = _dot(a, w[kw:])
    pe = jnp.concatenate([pe_ref[0, j:j + 1, :] for j in range(CMP_BLOCK)], axis=1)
    pe_term = jnp.zeros((1, HEAD_DIM), F32)
    for piece in _split3(pe):
        pe_term = pe_term + _dot(piece, w)
    o_ref[0, 0] = y_top + pltpu.roll(y_bot, n_chunk - 1, axis=0) + pe_term


def _compress(a, pe2, w2):
    S = a.shape[0]
    n_chunk = S // CMP_STRIDE
    G = A_KV_GROUPS
    return pl.pallas_call(
        functools.partial(_compress_kernel, n_chunk=n_chunk),
        grid=(2, G),
        in_specs=[
            pl.BlockSpec((S, HEAD_DIM), lambda kv, g: (0, BLK_KC + kv * G + g)),
            pl.BlockSpec((1, CMP_BLOCK, HEAD_DIM), lambda kv, g: (kv, 0, 0)),
            pl.BlockSpec((1, CMP_BLOCK * HEAD_DIM, HEAD_DIM), lambda kv, g: (kv, 0, 0)),
        ],
        out_specs=pl.BlockSpec((1, 1, n_chunk, HEAD_DIM), lambda kv, g: (kv, g, 0, 0)),
        out_shape=jax.ShapeDtypeStruct((2, G, n_chunk, HEAD_DIM), F32),
        scratch_shapes=[pltpu.VMEM((S, HEAD_DIM), F32)],
        compiler_params=_cparams(("arbitrary", "arbitrary")),
        name="compress",
    )(a, pe2, w2)


NSA_QB = 256
SEL_BODY = 8
SEL_GROUP = 4
_N_FORCED = 3


def _nsa_kernel(q_ref, ks_ref, kw_ref, vs_ref, vw_ref, kc_ref, vc_ref, ovt_ref, gate_ref,
                o_ref, kaug_ref, vt_ref, vwt_ref, vct_ref, acc_ref, m_ref, l_ref, *, seq):
    grp = pl.program_id(0)
    i = pl.program_id(1)
    t0 = i * NSA_QB
    n_cmp = seq // CMP_STRIDE
    cols = A_HPG * NSA_QB

    @pl.when(i == 0)
    def _():
        def build(r, c):
            r0 = pl.multiple_of(r * KEY_TILE, KEY_TILE)
            kaug_ref[r, :, 0:HEAD_DIM] = ks_ref[pl.ds(r0, KEY_TILE), :]
            blk = (r0 + lax.broadcasted_iota(jnp.int32, (KEY_TILE, SEL_SLOTS), 0)) // SEL_BLOCK
            slot = lax.broadcasted_iota(jnp.int32, (KEY_TILE, SEL_SLOTS), 1)
            kaug_ref[r, :, HEAD_DIM:HEAD_DIM + SEL_SLOTS] = jnp.where(blk == slot, 1.0, 0.0).astype(BF16)
            vt_ref[r] = vs_ref[pl.ds(r0, KEY_TILE), :].astype(F32).T.astype(BF16)
            vwt_ref[r] = vw_ref[pl.ds(r0, KEY_TILE), :].astype(F32).T.astype(BF16)
            return c
        lax.fori_loop(0, seq // KEY_TILE, build, 0)
        vct_ref[...] = vc_ref[0, 0].T.astype(BF16)

    qt = jnp.concatenate(
        [q_ref[:, h * HEAD_DIM:(h + 1) * HEAD_DIM].astype(F32).T.astype(BF16) for h in range(A_HPG)],
        axis=1)
    tq = t0 + lax.broadcasted_iota(jnp.int32, (1, cols), 1) % NSA_QB

    def softmax0(s):
        m = jnp.max(s, axis=0, keepdims=True)
        p = jnp.exp2(s - m)
        return p, jnp.sum(p, axis=0, keepdims=True)

    s_c = _dot(kc_ref[0, 0].astype(BF16), qt)
    nrow = lax.broadcasted_iota(jnp.int32, (n_cmp, 1), 0)
    vis = (nrow * CMP_STRIDE + (CMP_BLOCK - 1)) <= tq
    p_c, den = softmax0(jnp.where(vis, s_c, NEG_BIG))
    p_c = p_c * jnp.where(tq >= CMP_BLOCK - 1, 1.0 / den, 0.0)
    o_ct = _dot(vct_ref[...], p_c.astype(BF16))

    psum = p_c[:, 0:NSA_QB]
    for h in range(1, A_HPG):
        psum = psum + p_c[:, h * NSA_QB:(h + 1) * NSA_QB]
    ovt = ovt_ref[...]
    imp_t = jnp.zeros((SEL_SLOTS, NSA_QB), F32)
    for piece in _split3(psum):
        imp_t = imp_t + _dot(ovt, piece)
    slot_i = lax.broadcasted_iota(jnp.int32, (SEL_SLOTS, NSA_QB), 0)
    tq_l = t0 + lax.broadcasted_iota(jnp.int32, (SEL_SLOTS, NSA_QB), 1)
    cur = tq_l // SEL_BLOCK
    forced = jnp.where(slot_i == 0, 1, 0) + jnp.where(slot_i == cur, 1, 0) + jnp.where(slot_i == cur - 1, 1, 0)
    valid = slot_i * SEL_BLOCK <= tq_l
    key = jnp.where(forced > 0, -3e38, jnp.where(valid, imp_t, NEG_BIG))
    slot_f = slot_i.astype(F32)
    bias_t = jnp.where(forced > 0, 0.0, NEG_BIG)
    for _ in range(SEL_TOPK - _N_FORCED):
        mx = jnp.max(key, axis=0, keepdims=True)
        first = jnp.min(jnp.where(key == mx, slot_f, 1e9), axis=0, keepdims=True)
        pick = slot_f == first
        bias_t = jnp.where(pick, 0.0, bias_t)
        key = jnp.where(pick, -3e38, key)

    wlen = WINDOW + NSA_QB
    w0 = pl.multiple_of(jnp.maximum(t0 - WINDOW, 0), KEY_TILE)
    wt = jnp.maximum(i - WINDOW // KEY_TILE, 0)
    s_w = _dot(kw_ref[pl.ds(w0, wlen), :], qt)
    dist = tq - (w0 + lax.broadcasted_iota(jnp.int32, (wlen, 1), 0))
    in_window = dist.astype(jnp.uint32) < WINDOW
    p_w, l_w = softmax0(jnp.where(in_window, s_w, NEG_BIG))
    p_w = p_w.astype(BF16)
    o_wt = _dot(vwt_ref[wt], p_w[0:KEY_TILE])
    for k in range(1, wlen // KEY_TILE):
        o_wt = o_wt + _dot(vwt_ref[wt + k], p_w[k * KEY_TILE:(k + 1) * KEY_TILE])
    o_wt = o_wt * (1.0 / l_w)

    qt_aug = jnp.concatenate([qt, jnp.concatenate([bias_t.astype(BF16)] * A_HPG, axis=1)], axis=0)
    acc_ref[...] = jnp.zeros_like(acc_ref)
    m_ref[...] = jnp.full(m_ref.shape, 0.5 * NEG_BIG, F32)
    l_ref[...] = jnp.zeros_like(l_ref)

    def sel_body(base, causal):
        n = len(causal)
        m, l = m_ref[...], l_ref[...]
        gsz = min(SEL_GROUP, max(n // 2, 1))
        groups = [list(range(g0, min(g0 + gsz, n))) for g0 in range(0, n, gsz)]

        def scores(ks):
            out = {}
            for k in ks:
                sk = _dot(kaug_ref[base + k], qt_aug)
                if causal[k]:
                    kpos = (base + k) * KEY_TILE + lax.broadcasted_iota(jnp.int32, (KEY_TILE, 1), 0)
                    sk = jnp.where(kpos <= tq, sk, NEG_BIG)
                out[k] = sk
            return out

        s = scores(groups[0])
        for gi, ks in enumerate(groups):
            if gi + 1 < len(groups):
                s.update(scores(groups[gi + 1]))
            m_new = m
            for k in ks:
                m_new = jnp.maximum(m_new, jnp.max(s[k], axis=0, keepdims=True))
            alpha = jnp.exp2(m - m_new)
            l = alpha * l
            pv = None
            for k in ks:
                p = jnp.exp2(s.pop(k) - m_new)
                l = l + jnp.sum(p, axis=0, keepdims=True)
                d = _dot(vt_ref[base + k], p.astype(BF16))
                pv = d if pv is None else pv + d
            acc_ref[...] = alpha * acc_ref[...] + pv
            m = m_new
        m_ref[...] = m
        l_ref[...] = l

    n_bulk = i // SEL_BODY
    rem = i % SEL_BODY

    def bulk(b, c):
        sel_body(b * SEL_BODY, (False,) * SEL_BODY)
        return c
    lax.fori_loop(0, n_bulk, bulk, 0)

    base = n_bulk * SEL_BODY
    part = SEL_BODY // 2
    while part >= 2:
        pl.when(rem & part != 0)(functools.partial(sel_body, base, (False,) * part))
        base = base + (rem & part)
        part //= 2

    @pl.when(rem % 2 == 1)
    def _():
        sel_body(i - 1, (False, True))

    @pl.when(rem % 2 == 0)
    def _():
        sel_body(i, (True,))

    o_st = acc_ref[...] * (1.0 / l_ref[...])

    gates_t = gate_ref[...].astype(F32).T
    per_group = 3 * A_HPG

    def gate(h, b):
        r = 3 * h + b
        return jnp.where(grp == 0, gates_t[r:r + 1], gates_t[per_group + r:per_group + r + 1])

    for h in range(A_HPG):
        c = slice(h * NSA_QB, (h + 1) * NSA_QB)
        o = gate(h, 0) * o_ct[:, c] + gate(h, 1) * o_st[:, c] + gate(h, 2) * o_wt[:, c]
        o_ref[:, h * HEAD_DIM:(h + 1) * HEAD_DIM] = o.T.astype(o_ref.dtype)


def _overlap_matrix(seq):
    n_chunk = seq // CMP_STRIDE
    cs = np.arange(n_chunk)[:, None] * CMP_STRIDE
    ss = np.arange(SEL_SLOTS)[None, :] * SEL_BLOCK
    ov = np.maximum(np.minimum(cs + CMP_BLOCK, ss + SEL_BLOCK) - np.maximum(cs, ss), 0) / CMP_BLOCK
    n_cmp = (seq - CMP_BLOCK) // CMP_STRIDE + 1
    ov[n_cmp:] = 0.0
    ov[:, seq // SEL_BLOCK:] = 0.0
    return ov.astype(np.float32)


def _nsa(a, cmp_kv):
    S = a.shape[0]
    G = A_KV_GROUPS
    n_tiles = S // KEY_TILE
    assert S // SEL_BLOCK <= SEL_SLOTS and S % KEY_TILE == 0 and S >= WINDOW + NSA_QB
    assert NSA_QB == KEY_TILE and G == 2
    n_chunk = S // CMP_STRIDE
    ovt = jnp.asarray(_overlap_matrix(S).T, BF16)
    col = lambda b: pl.BlockSpec((S, HEAD_DIM), lambda g, i, b=b: (0, b + g))
    return pl.pallas_call(
        functools.partial(_nsa_kernel, seq=S),
        grid=(G, S // NSA_QB),
        in_specs=[
            pl.BlockSpec((NSA_QB, A_HPG * HEAD_DIM), lambda g, i: (i, g)),
            col(BLK_KS), col(BLK_KW), col(BLK_VS), col(BLK_VW),
            pl.BlockSpec((1, 1, n_chunk, HEAD_DIM), lambda g, i: (0, g, 0, 0)),
            pl.BlockSpec((1, 1, n_chunk, HEAD_DIM), lambda g, i: (1, g, 0, 0)),
            pl.BlockSpec((SEL_SLOTS, n_chunk), lambda g, i: (0, 0)),
            pl.BlockSpec((NSA_QB, LANES), lambda g, i: (i, BLK_GATE)),
        ],
        out_specs=pl.BlockSpec((NSA_QB, A_HPG * HEAD_DIM), lambda g, i: (i, g)),
        out_shape=jax.ShapeDtypeStruct((S, A_WIDTH), BF16),
        scratch_shapes=[pltpu.VMEM((n_tiles, KEY_TILE, HEAD_DIM + SEL_SLOTS), BF16),
                        pltpu.VMEM((n_tiles, HEAD_DIM, KEY_TILE), BF16),
                        pltpu.VMEM((n_tiles, HEAD_DIM, KEY_TILE), BF16),
                        pltpu.VMEM((HEAD_DIM, n_chunk), BF16),
                        pltpu.VMEM((HEAD_DIM, A_HPG * NSA_QB), F32),
                        pltpu.VMEM((1, A_HPG * NSA_QB), F32),
                        pltpu.VMEM((1, A_HPG * NSA_QB), F32)],
        compiler_params=_cparams(("arbitrary", "arbitrary")),
        name="nsa",
    )(a, a, a, a, a, cmp_kv, cmp_kv, ovt, a)


_ROW_MASKS = 18
_SCORE_MASKS = 10
HG_HEADS_PER_STEP = 8


def _hgrn_tables():
    t = np.arange(HG_CHUNK)
    p4, i4, i16 = t % 4, (t // 4) % 4, t // 16
    conds = [p4 >= 1, p4 >= 2, p4 >= 3, p4 <= 2, p4 <= 1, p4 == 0,
             i4 >= 1, i4 >= 2, i4 >= 3, i4 <= 2, i4 <= 1, i4 == 0,
             i16 >= 1, i16 >= 2, i16 >= 3, i16 <= 2, i16 <= 1, i16 == 0]
    rm = np.stack([np.broadcast_to(c[:, None], (HG_CHUNK, HG_DIM)) for c in conds]).astype(np.float32)
    tt, ss = t[:, None], t[None, :]
    lag16 = tt // 16 - ss // 16
    lag4 = tt // 4 - ss // 4
    same16, same4 = lag16 == 0, lag4 == 0
    sm = [lag16 == 1, lag16 == 2, lag16 == 3,
          same16 & (lag4 == 1), same16 & (lag4 == 2), same16 & (lag4 == 3),
          same4 & (tt - ss == 0), same4 & (tt - ss == 1), same4 & (tt - ss == 2), same4 & (tt - ss == 3)]
    return rm, np.stack(sm).astype(np.float32)


def _hgrn_kernel(q_ref, f_ref, i_ref, og_ref, loglb_ref, log1m_ref, gn_ref, rm_ref, sm_ref,
                 o_ref, st_ref, g_scr, k_scr, o_scr, *, tb, nh):
    C = HG_CHUNK
    heads = [slice(h * HG_DIM, (h + 1) * HG_DIM) for h in range(nh)]

    @pl.when(pl.program_id(1) == 0)
    def _():
        st_ref[...] = jnp.zeros_like(st_ref)

    x = f_ref[...]
    log_sig = jnp.minimum(x, 0.0) - jnp.log(1.0 + jnp.exp(-jnp.abs(x)))
    a = loglb_ref[...]
    c = log1m_ref[...] + log_sig
    log_f = jnp.maximum(a, c) + jnp.log(1.0 + jnp.exp(-jnp.abs(a - c)))
    g_scr[...] = log_f
    k_scr[...] = 1.0 - jnp.exp(log_f)

    def rmask(n):
        return jnp.concatenate([rm_ref[n]] * nh, axis=1)

    def roll(v, s):
        return pltpu.roll(v, s % C, axis=0)

    def b16(z):
        return z.astype(BF16)

    def chunk(ci, carry):
        r0 = pl.multiple_of(ci * C, C)
        g = g_scr[pl.ds(r0, C), :]
        q = q_ref[pl.ds(r0, C), :]
        k = k_scr[pl.ds(r0, C), :]
        v = i_ref[pl.ds(r0, C), :].astype(BF16)
        pre4 = g + rmask(0) * roll(g, 1) + rmask(1) * roll(g, 2) + rmask(2) * roll(g, 3)
        suf4 = rmask(3) * roll(g, -1) + rmask(4) * roll(g, -2) + rmask(5) * roll(g, -3)
        tot4 = pre4 + suf4
        a1, a2, a3 = roll(tot4, 4), roll(tot4, 8), roll(tot4, 12)
        pre16 = pre4 + rmask(6) * a1 + rmask(7) * a2 + rmask(8) * a3
        suf16 = suf4 + rmask(9) * roll(tot4, -4) + rmask(10) * roll(tot4, -8) + rmask(11) * roll(tot4, -12)
        tot16 = pre16 + suf16
        d1, d2, d3 = roll(tot16, 16), roll(tot16, 32), roll(tot16, 48)
        pre64 = pre16 + rmask(12) * d1 + rmask(13) * d2 + rmask(14) * d3
        suf64 = suf16 + rmask(15) * roll(tot16, -16) + rmask(16) * roll(tot16, -32) + rmask(17) * roll(tot16, -48)
        tot64 = pre64 + suf64
        g1, g2 = roll(g, 1), roll(g, 2)

        q_a = q * jnp.exp(pre16)
        lhs_a = [b16(q_a), b16(q_a * jnp.exp(d1)), b16(q_a * jnp.exp(d1 + d2))]
        k_a = b16(k * jnp.exp(suf16))
        q_b = q * jnp.exp(pre4)
        lhs_b = [b16(q_b), b16(q_b * jnp.exp(a1)), b16(q_b * jnp.exp(a1 + a2))]
        k_b = b16(k * jnp.exp(suf4))
        q_c = q * jnp.exp(g)
        lhs_c = [b16(q), b16(q_c), b16(q_c * jnp.exp(g1)), b16(q_c * jnp.exp(g1 + g2))]
        k_c = b16(k)
        q_s = b16(q * jnp.exp(pre64))
        k_s = b16(k * jnp.exp(suf64))
        decay = jnp.exp(tot64[0:1, :])

        def level(lhs, rhs):
            return [_dot_nt(jnp.concatenate([z[:, hh] for z in lhs], axis=0), rhs[:, hh]) for hh in heads]

        s_a, s_b, s_c = level(lhs_a, k_a), level(lhs_b, k_b), level(lhs_c, k_c)
        o_inter = [_dot_nt(q_s[:, hh], b16(st_ref[h])) for h, hh in enumerate(heads)]
        kv = [_dot_tn(v[:, hh], k_s[:, hh]) for hh in heads]
        for h, hh in enumerate(heads):
            sa, sb, sc = s_a[h], s_b[h], s_c[h]
            scores = (sm_ref[0] * sa[0:C] + sm_ref[1] * sa[C:2 * C] + sm_ref[2] * sa[2 * C:3 * C]
                      + sm_ref[3] * sb[0:C] + sm_ref[4] * sb[C:2 * C] + sm_ref[5] * sb[2 * C:3 * C]
                      + sm_ref[6] * sc[0:C] + sm_ref[7] * sc[C:2 * C] + sm_ref[8] * sc[2 * C:3 * C]
                      + sm_ref[9] * sc[3 * C:4 * C])
            o_scr[pl.ds(r0, C), hh] = _dot(b16(scores), v[:, hh]) + o_inter[h]
            st_ref[h] = st_ref[h] * decay[:, hh] + kv[h]
        return carry

    lax.fori_loop(0, tb // C, chunk, 0)

    for hh in heads:
        o = o_scr[:, hh]
        ms = jnp.mean(o * o, axis=-1, keepdims=True)
        og = og_ref[:, hh]
        y = o * lax.rsqrt(ms + NORM_EPS) * gn_ref[:, hh] * (og * jax.nn.sigmoid(og))
        o_ref[:, hh] = y.astype(o_ref.dtype)


def _hgrn(hz, loglb, log1mlb, gnorm):
    S = hz.shape[0]
    tb = 512 if S % 512 == 0 else S
    nh = HG_HEADS_PER_STEP
    ng = HG_HEADS // nh
    w = nh * HG_DIM
    rm, sm = _hgrn_tables()
    blk = lambda part: pl.BlockSpec((tb, w), lambda h, c, part=part: (c, part * ng + h))
    row = pl.BlockSpec((1, w), lambda h, c: (0, h))
    return pl.pallas_call(
        functools.partial(_hgrn_kernel, tb=tb, nh=nh),
        grid=(ng, S // tb),
        in_specs=[blk(0), blk(1), blk(2), blk(3), row, row, row,
                  pl.BlockSpec((_ROW_MASKS, HG_CHUNK, HG_DIM), lambda h, c: (0, 0, 0)),
                  pl.BlockSpec((_SCORE_MASKS, HG_CHUNK, HG_CHUNK), lambda h, c: (0, 0, 0))],
        out_specs=pl.BlockSpec((tb, w), lambda h, c: (c, h)),
        out_shape=jax.ShapeDtypeStruct((S, HG_WIDTH), BF16),
        scratch_shapes=[pltpu.VMEM((nh, HG_DIM, HG_DIM), F32), pltpu.VMEM((tb, w), F32),
                        pltpu.VMEM((tb, w), F32), pltpu.VMEM((tb, w), F32)],
        compiler_params=_cparams(("arbitrary", "arbitrary")),
        name="hgrn",
    )(hz, hz, hz, hz, loglb, log1mlb, gnorm, jnp.asarray(rm), jnp.asarray(sm))


_MERGE_PART = 512


def _merge_kernel(at_ref, hg_ref, ga_ref, gh_ref, wa_ref, wh_ref, wo_ref, x_ref, gt_ref, gp_ref, o_ref):
    at, hg = at_ref[...], hg_ref[...]
    d = wo_ref.shape[1]
    parts = [slice(c0, c0 + _MERGE_PART) for c0 in range(0, d, _MERGE_PART)]
    branch = [(_dot(at, wa_ref[0, :, cs]), _dot(hg, wh_ref[0, :, cs])) for cs in parts]
    y = None
    for cs, (ma, mh) in zip(parts, branch):
        mixed = ga_ref[:, cs].astype(F32) * ma + gh_ref[:, cs].astype(F32) * mh
        yp = _dot(mixed.astype(BF16), wo_ref[0, cs, :])
        y = yp if y is None else y + yp
    ms = jnp.mean(y * y, axis=-1, keepdims=True)
    yn = y * lax.rsqrt(ms + NORM_EPS) * gp_ref[...]
    o_ref[...] = x_ref[...] + gt_ref[...] * yn


def _merge(attn, hg, mg, wa, wh, wo, layer, x, gt, gpost):
    S, D = x.shape
    tm = 256
    const = lambda shape: pl.BlockSpec((1,) + shape, lambda m: (layer, 0, 0), pipeline_mode=pl.Buffered(1))
    row = pl.BlockSpec((1, D), lambda m: (0, 0))
    return pl.pallas_call(
        _merge_kernel,
        grid=(S // tm,),
        in_specs=[
            pl.BlockSpec((tm, A_WIDTH), lambda m: (m, 0)),
            pl.BlockSpec((tm, HG_WIDTH), lambda m: (m, 0)),
            pl.BlockSpec((tm, D), lambda m: (m, 0)),
            pl.BlockSpec((tm, D), lambda m: (m, 1)),
            const((A_WIDTH, D)), const((HG_WIDTH, D)), const((D, D)),
            pl.BlockSpec((tm, D), lambda m: (m, 0)),
            row, row,
        ],
        out_specs=pl.BlockSpec((tm, D), lambda m: (m, 0)),
        out_shape=jax.ShapeDtypeStruct((S, D), F32),
        compiler_params=_cparams(("arbitrary",)),
        name="merge",
    )(attn, hg, mg, mg, wa, wh, wo, x, gt, gpost.reshape(1, D))


_HALO = 16
_FF_SUB = 256


def _norm_mod(x, g_ref, sc_ref, sh_ref):
    ms = jnp.mean(x * x, axis=-1, keepdims=True)
    y = x * lax.rsqrt(ms + NORM_EPS) * g_ref[...]
    return (y * (1.0 + sc_ref[...]) + sh_ref[...]).astype(BF16)


def _ffn_kernel(x_ref, xh_ref, g_ref, sc_ref, sh_ref, wg_ref, wv_ref, cwg_ref, cwv_ref, cbg_ref, cbv_ref,
                wd_ref, gt_ref, gp_ref, *rest, tm, emit_next):
    if emit_next:
        gn_ref, scn_ref, shn_ref, o_ref, hn_ref, h_scr, ug_scr, uv_scr = rest
    else:
        o_ref, h_scr, ug_scr, uv_scr = rest
    m = pl.program_id(0)
    f = pl.program_id(1)
    norm_mod = functools.partial(_norm_mod, g_ref=g_ref, sc_ref=sc_ref, sh_ref=sh_ref)

    @pl.when(f == 0)
    def _():
        h_scr[0:_HALO, :] = jnp.where(m > 0, norm_mod(xh_ref[...]), jnp.zeros((), BF16))
        h_scr[_HALO:_HALO + tm, :] = norm_mod(x_ref[...])

    tf = wg_ref.shape[2]
    subs = [slice(s * _FF_SUB, (s + 1) * _FF_SUB) for s in range(tf // _FF_SUB)]
    h = h_scr[...]
    ups = [(_dot(h, wg_ref[0, :, cs]), _dot(h, wv_ref[0, :, cs])) for cs in subs]

    def conv(u_scr, cw_ref, cb_ref, cs):
        out = cb_ref[:, cs] + cw_ref[0:1, cs] * u_scr[pl.ds(_HALO - 2, tm), cs]
        out = out + cw_ref[1:2, cs] * u_scr[pl.ds(_HALO - 1, tm), cs]
        return out + cw_ref[2:3, cs] * u_scr[pl.ds(_HALO, tm), cs]

    acc = jnp.where(f > 0, o_ref[...], 0.0)
    for cs, (ug, uv) in zip(subs, ups):
        ug_scr[:, cs] = ug
        uv_scr[:, cs] = uv
        cg = conv(ug_scr, cwg_ref, cbg_ref, cs)
        cv = conv(uv_scr, cwv_ref, cbv_ref, cs)
        act = (cg * jax.nn.sigmoid(cg) * cv).astype(BF16)
        acc = acc + _dot(act, wd_ref[0, cs, :])
    o_ref[...] = acc

    @pl.when(f == pl.num_programs(1) - 1)
    def _():
        y = o_ref[...]
        ms = jnp.mean(y * y, axis=-1, keepdims=True)
        yn = y * lax.rsqrt(ms + NORM_EPS) * gp_ref[...]
        x_new = x_ref[...] + gt_ref[...] * yn
        o_ref[...] = x_new
        if emit_next:
            hn_ref[...] = _norm_mod(x_new, gn_ref, scn_ref, shn_ref)


def _ffn(x, gpre, sc, sh, wup, convw, convb, wdown, layer, gt, gpost, nxt=None):
    S, D = x.shape
    F = wdown.shape[1]
    tm = 512 if S % 512 == 0 else S
    tf = 512
    nf = F // tf
    hb = tm // _HALO
    row = pl.BlockSpec((1, D), lambda m, f: (0, 0))
    tile = pl.BlockSpec((tm, D), lambda m, f: (m, 0))
    extra = () if nxt is None else (nxt[0].reshape(1, D), nxt[1], nxt[2])
    return pl.pallas_call(
        functools.partial(_ffn_kernel, tm=tm, emit_next=nxt is not None),
        grid=(S // tm, nf),
        in_specs=[
            pl.BlockSpec((tm, D), lambda m, f: (m, 0)),
            pl.BlockSpec((_HALO, D), lambda m, f: (jnp.maximum(m * hb - 1, 0), 0)),
            row, row, row,
            pl.BlockSpec((1, D, tf), lambda m, f: (layer, 0, f)),
            pl.BlockSpec((1, D, tf), lambda m, f: (layer, 0, nf + f)),
            pl.BlockSpec((CONV_WIDTH, tf), lambda m, f: (0, f)),
            pl.BlockSpec((CONV_WIDTH, tf), lambda m, f: (0, nf + f)),
            pl.BlockSpec((1, tf), lambda m, f: (0, f)),
            pl.BlockSpec((1, tf), lambda m, f: (0, nf + f)),
            pl.BlockSpec((1, tf, D), lambda m, f: (layer, f, 0)),
            row, row,
        ] + ([row] * 3 if nxt is not None else []),
        out_specs=[tile, tile] if nxt is not None else tile,
        out_shape=([jax.ShapeDtypeStruct((S, D), F32), jax.ShapeDtypeStruct((S, D), BF16)]
                   if nxt is not None else jax.ShapeDtypeStruct((S, D), F32)),
        scratch_shapes=[pltpu.VMEM((tm + _HALO, D), BF16),
                        pltpu.VMEM((tm + _HALO, tf), F32), pltpu.VMEM((tm + _HALO, tf), F32)],
        compiler_params=_cparams(("arbitrary", "arbitrary")),
        name="ffn",
    )(x, x, gpre.reshape(1, D), sc, sh, wup, wup, convw, convw, convb.reshape(1, -1), convb.reshape(1, -1),
      wdown, gt, gpost.reshape(1, D), *extra)


def _rope_tables(positions):
    inv_freq = 1.0 / (ROPE_THETA ** (jnp.arange(0, HEAD_DIM, 2, dtype=F32) / HEAD_DIM))
    ang = positions.astype(F32)[:, None] * inv_freq
    cos, sin = jnp.cos(ang), jnp.sin(ang)
    return jnp.concatenate([cos, cos], axis=-1), jnp.concatenate([-sin, sin], axis=-1)


def kernel(x, c, positions, w_ada, b_ada, g_pre_mix, w_in, pe_kc, w_kc, pe_vc, w_vc, lb_logits, g_hg_norm,
           w_br_attn, w_br_hgrn, w_out, g_post_mix, g_pre_ffn, w_up, conv_w, conv_b, w_down, g_post_ffn):
    B, S, D = x.shape
    assert B == 1, "kernel is written for one sequence"
    L = w_ada.shape[0]
    xs = x[0]
    cosf, sinf = _rope_tables(positions[0])
    scale = HEAD_DIM ** -0.5 * np.log2(np.e)
    tables = (cosf * scale, sinf * scale, cosf, sinf)
    lb_cum = jnp.cumsum(jax.nn.softmax(lb_logits.astype(F32), axis=0), axis=0)
    lower = lb_cum - lb_cum[0:1]
    log_lb = jnp.log(lower)
    log_1m = jnp.log1p(-lower)
    ada = _ada_all(c, w_ada, b_ada)

    w_in_t = jnp.swapaxes(w_in, 1, 2)
    tail_tn = 1024
    hg_roles = (("id",) * (tail_tn // LANES),) * (4 * HG_WIDTH // tail_tn)
    mg_roles = (("sg",) * (tail_tn // LANES),) * (2 * D // tail_tn)

    wa_b, wh_b, wo_b = w_br_attn.astype(BF16), w_br_hgrn.astype(BF16), w_out.astype(BF16)
    wup_b, wdown_b = w_up.astype(BF16), w_down.astype(BF16)

    mods = [[ada[l, :, j * D:(j + 1) * D] for j in range(N_ADA)] for l in range(L)]
    h = _prenorm(xs, g_pre_mix[0], mods[0][1], mods[0][0])
    for l in range(L):
        sh1, sc1, gt1, sh2, sc2, gt2 = mods[l]

        a = _proj(h, w_in_t, l, 0, A_ROLES, BF16, tn=A_TILE, tables=tables, name="proj_attn")
        hz = _proj(h, w_in_t, l, A_COLS, hg_roles, F32, tn=tail_tn, name="proj_hgrn")
        mg = _proj(h, w_in_t, l, A_COLS + 4 * HG_WIDTH, mg_roles, BF16, tn=tail_tn, name="proj_merge_gate")

        cmp_kv = _compress(a, jnp.stack([pe_kc[l], pe_vc[l]]), jnp.stack([w_kc[l], w_vc[l]]))
        attn = _nsa(a, cmp_kv)
        hg = _hgrn(hz, log_lb[l:l + 1], log_1m[l:l + 1], g_hg_norm[l:l + 1])
        xs = _merge(attn, hg, mg, wa_b, wh_b, wo_b, l, xs, gt1, g_post_mix[l])

        nxt = (g_pre_mix[l + 1], mods[l + 1][1], mods[l + 1][0]) if l + 1 < L else None
        out = _ffn(xs, g_pre_ffn[l], sc2, sh2, wup_b, conv_w[l], conv_b[l], wdown_b, l, gt2, g_post_ffn[l], nxt)
        xs, h = out if nxt is not None else (out, None)
    return xs[None]
```

```python
import functools

import jax
import jax.numpy as jnp
import numpy as np
from jax import lax
from jax.experimental import pallas as pl
from jax.experimental.pallas import tpu as pltpu

F32 = jnp.float32
BF16 = jnp.bfloat16

A_HEADS = 8
A_KV_GROUPS = 2
A_HPG = A_HEADS // A_KV_GROUPS
HEAD_DIM = 128
A_WIDTH = A_HEADS * HEAD_DIM
A_KV_WIDTH = A_KV_GROUPS * HEAD_DIM
CMP_BLOCK = 32
CMP_STRIDE = 16
SEL_BLOCK = 64
SEL_TOPK = 16
WINDOW = 512
ROPE_THETA = 10000.0
HG_HEADS = 8
HG_DIM = 128
HG_WIDTH = HG_HEADS * HG_DIM
HG_CHUNK = 64
CONV_WIDTH = 3
NORM_EPS = 1e-6
N_ADA = 6
A_COLS = A_WIDTH + 6 * A_KV_WIDTH + 3 * A_HEADS

LANES = 128
SUBLANES = 8
SEL_SLOTS = LANES
KEY_TILE = 256
NEG_BIG = -1e30
VMEM_LIMIT = 56 * 1024 * 1024

ADA_COLS = 1024
NORM_ROWS = 512
PROJ_ROWS = 1024
TAIL_TILE = 1024
HG_ROWS = 512
MERGE_ROWS = 256
FFN_ROWS = 512
FFN_COLS = 512


def _cparams(sem):
    return pltpu.CompilerParams(dimension_semantics=sem, vmem_limit_bytes=VMEM_LIMIT)


def _split3(a):
    hi = a.astype(BF16)
    r1 = a - hi.astype(F32)
    mid = r1.astype(BF16)
    lo = (r1 - mid.astype(F32)).astype(BF16)
    return hi, mid, lo


def _dot(a, b):
    return jnp.dot(a, b, preferred_element_type=F32)


def _dot_nt(a, b):
    return lax.dot_general(a, b, (((1,), (1,)), ((), ())), preferred_element_type=F32)


def _dot_tn(a, b):
    return lax.dot_general(a, b, (((0,), (0,)), ((), ())), preferred_element_type=F32)


def _ada_kernel(c_ref, w_ref, b_ref, o_ref):
    c = c_ref[...]
    ca = c * jax.nn.sigmoid(c)
    o_ref[0] = jnp.sum(ca * w_ref[0], axis=0, keepdims=True) + b_ref[0]


def _ada_all(c, w_ada, b_ada):
    L, D, N = w_ada.shape
    tn = ADA_COLS
    return pl.pallas_call(
        _ada_kernel,
        grid=(L, N // tn),
        in_specs=[
            pl.BlockSpec((D, 1), lambda l, n: (0, 0)),
            pl.BlockSpec((1, D, tn), lambda l, n: (l, 0, n)),
            pl.BlockSpec((1, 1, tn), lambda l, n: (l, 0, n)),
        ],
        out_specs=pl.BlockSpec((1, 1, tn), lambda l, n: (l, 0, n)),
        out_shape=jax.ShapeDtypeStruct((L, 1, N), F32),
        compiler_params=_cparams(("arbitrary", "arbitrary")),
        name="ada",
    )(c.reshape(D, 1), w_ada, b_ada.reshape(L, 1, N))


def _prenorm_kernel(x_ref, g_ref, sc_ref, sh_ref, o_ref):
    x = x_ref[...]
    ms = jnp.mean(x * x, axis=-1, keepdims=True)
    y = x * lax.rsqrt(ms + NORM_EPS) * g_ref[...]
    o_ref[...] = (y * (1.0 + sc_ref[...]) + sh_ref[...]).astype(o_ref.dtype)


def _prenorm(x, g, sc, sh):
    S, D = x.shape
    tm = NORM_ROWS
    row = pl.BlockSpec((1, D), lambda m: (0, 0))
    return pl.pallas_call(
        _prenorm_kernel,
        grid=(S // tm,),
        in_specs=[pl.BlockSpec((tm, D), lambda m: (m, 0)), row, row, row],
        out_specs=pl.BlockSpec((tm, D), lambda m: (m, 0)),
        out_shape=jax.ShapeDtypeStruct((S, D), BF16),
        compiler_params=_cparams(("arbitrary",)),
        name="prenorm",
    )(x, g.reshape(1, D), sc, sh)


def _rope(t, cosf, sinf):
    return t * cosf + pltpu.roll(t, HEAD_DIM // 2, axis=1) * sinf


_PROJ_PART = 256


def _proj_kernel(h_ref, w_ref, *rest, roles, tn):
    o_ref, w_scr = rest[-2:]
    tabs = rest[:-2]
    n = pl.program_id(0)

    @pl.when(pl.program_id(1) == 0)
    def _():
        w_scr[...] = w_ref[0].astype(BF16)

    def emit(slabs):
        h = h_ref[...]
        per = _PROJ_PART // LANES
        parts = [_dot_nt(h, w_scr[c0:c0 + _PROJ_PART, :]) for c0 in range(0, tn, _PROJ_PART)]
        for pi, acc in enumerate(parts):
            for s in range(per):
                kind = slabs[pi * per + s]
                t = acc[:, s * LANES:(s + 1) * LANES]
                if kind == "sg":
                    t = jax.nn.sigmoid(t)
                elif kind != "id":
                    cos_ref, sin_ref = tabs[0:2] if kind == "rq" else tabs[-2:]
                    t = _rope(t, cos_ref[...], sin_ref[...])
                c = (pi * per + s) * LANES
                o_ref[:, c:c + LANES] = t.astype(o_ref.dtype)

    kinds = sorted(set(roles))
    if len(kinds) == 1:
        emit(kinds[0])
        return
    for kind in kinds:
        hit = functools.reduce(jnp.logical_or, [n == t for t, r in enumerate(roles) if r == kind])
        pl.when(hit)(functools.partial(emit, kind))


def _proj(h, w_t, layer, row0, roles, out_dtype, *, tn, tables=(), name="proj"):
    S, D = h.shape
    tm = PROJ_ROWS if S % PROJ_ROWS == 0 else S
    nt = len(roles)
    tab = pl.BlockSpec((tm, HEAD_DIM), lambda n, m: (m, 0))
    return pl.pallas_call(
        functools.partial(_proj_kernel, roles=tuple(roles), tn=tn),
        grid=(nt, S // tm),
        in_specs=[pl.BlockSpec((tm, D), lambda n, m: (m, 0)),
                  pl.BlockSpec((pl.Element(1), pl.Element(tn), pl.Element(D)),
                               lambda n, m: (layer, pl.multiple_of(row0 + n * tn, SUBLANES), 0))]
                 + [tab] * len(tables),
        out_specs=pl.BlockSpec((tm, tn), lambda n, m: (m, n)),
        out_shape=jax.ShapeDtypeStruct((S, nt * tn), out_dtype),
        scratch_shapes=[pltpu.VMEM((tn, D), BF16)],
        compiler_params=_cparams(("arbitrary", "arbitrary")),
        name=name,
    )(h, w_t, *tables)


A_TILE = 1024
_A_SLABS = (("rq",) * (A_WIDTH // LANES)
            + (("rk",) * A_KV_GROUPS + ("id",) * A_KV_GROUPS) * 3
            + ("sg",) * (-(A_WIDTH // LANES + 6 * A_KV_GROUPS) % (A_TILE // LANES)))
A_ROLES = tuple(_A_SLABS[i:i + A_TILE // LANES] for i in range(0, len(_A_SLABS), A_TILE // LANES))
_KV_BLOCK0 = A_WIDTH // LANES
BLK_KC, BLK_VC, BLK_KS, BLK_VS, BLK_KW, BLK_VW = [_KV_BLOCK0 + j * A_KV_GROUPS for j in range(6)]
BLK_GATE = _KV_BLOCK0 + 6 * A_KV_GROUPS


def _compress_kernel(t_ref, pe_ref, w_ref, o_ref, t32_ref, *, n_chunk):
    half = CMP_BLOCK // 2
    t32_ref[...] = t_ref[...].astype(F32)
    a = jnp.concatenate([t32_ref[pl.ds(j, n_chunk, stride=CMP_STRIDE), :] for j in range(half)],
                        axis=1).astype(BF16)
    kw = half * HEAD_DIM
    w = w_ref[0].astype(BF16)
    y_top = _dot(a, w[:kw])
    y_bot = _dot(a, w[kw:])
    pe = jnp.concatenate([pe_ref[0, j:j + 1, :] for j in range(CMP_BLOCK)], axis=1)
    pe_term = jnp.zeros((1, HEAD_DIM), F32)
    for piece in _split3(pe):
        pe_term = pe_term + _dot(piece, w)
    o_ref[0, 0] = y_top + pltpu.roll(y_bot, n_chunk - 1, axis=0) + pe_term


def _compress(a, pe2, w2):
    S = a.shape[0]
    n_chunk = S // CMP_STRIDE
    G = A_KV_GROUPS
    return pl.pallas_call(
        functools.partial(_compress_kernel, n_chunk=n_chunk),
        grid=(2, G),
        in_specs=[
            pl.BlockSpec((S, HEAD_DIM), lambda kv, g: (0, BLK_KC + kv * G + g)),
            pl.BlockSpec((1, CMP_BLOCK, HEAD_DIM), lambda kv, g: (kv, 0, 0)),
            pl.BlockSpec((1, CMP_BLOCK * HEAD_DIM, HEAD_DIM), lambda kv, g: (kv, 0, 0)),
        ],
        out_specs=pl.BlockSpec((1, 1, n_chunk, HEAD_DIM), lambda kv, g: (kv, g, 0, 0)),
        out_shape=jax.ShapeDtypeStruct((2, G, n_chunk, HEAD_DIM), F32),
        scratch_shapes=[pltpu.VMEM((S, HEAD_DIM), F32)],
        compiler_params=_cparams(("arbitrary", "arbitrary")),
        name="compress",
    )(a, pe2, w2)


NSA_QB = 256
SEL_BODY = 8
SEL_GROUP = 4
_N_FORCED = 3


def _nsa_kernel(q_ref, ks_ref, kw_ref, vs_ref, vw_ref, kc_ref, vc_ref, ovt_ref, gate_ref,
                o_ref, kaug_ref, vt_ref, vwt_ref, vct_ref, acc_ref, m_ref, l_ref, *, seq):
    grp = pl.program_id(0)
    i = pl.program_id(1)
    t0 = i * NSA_QB
    n_cmp = seq // CMP_STRIDE
    cols = A_HPG * NSA_QB

    @pl.when(i == 0)
    def _():
        def build(r, c):
            r0 = pl.multiple_of(r * KEY_TILE, KEY_TILE)
            kaug_ref[r, :, 0:HEAD_DIM] = ks_ref[pl.ds(r0, KEY_TILE), :]
            blk = (r0 + lax.broadcasted_iota(jnp.int32, (KEY_TILE, SEL_SLOTS), 0)) // SEL_BLOCK
            slot = lax.broadcasted_iota(jnp.int32, (KEY_TILE, SEL_SLOTS), 1)
            kaug_ref[r, :, HEAD_DIM:HEAD_DIM + SEL_SLOTS] = jnp.where(blk == slot, 1.0, 0.0).astype(BF16)
            vt_ref[r] = vs_ref[pl.ds(r0, KEY_TILE), :].astype(F32).T.astype(BF16)
            vwt_ref[r] = vw_ref[pl.ds(r0, KEY_TILE), :].astype(F32).T.astype(BF16)
            return c
        lax.fori_loop(0, seq // KEY_TILE, build, 0)
        vct_ref[...] = vc_ref[0, 0].T.astype(BF16)

    qt = jnp.concatenate(
        [q_ref[:, h * HEAD_DIM:(h + 1) * HEAD_DIM].astype(F32).T.astype(BF16) for h in range(A_HPG)],
        axis=1)
    tq = t0 + lax.broadcasted_iota(jnp.int32, (1, cols), 1) % NSA_QB

    def softmax0(s):
        m = jnp.max(s, axis=0, keepdims=True)
        p = jnp.exp2(s - m)
        return p, jnp.sum(p, axis=0, keepdims=True)

    s_c = _dot(kc_ref[0, 0].astype(BF16), qt)
    nrow = lax.broadcasted_iota(jnp.int32, (n_cmp, 1), 0)
    vis = (nrow * CMP_STRIDE + (CMP_BLOCK - 1)) <= tq
    p_c, den = softmax0(jnp.where(vis, s_c, NEG_BIG))
    p_c = p_c * jnp.where(tq >= CMP_BLOCK - 1, 1.0 / den, 0.0)
    o_ct = _dot(vct_ref[...], p_c.astype(BF16))

    psum = p_c[:, 0:NSA_QB]
    for h in range(1, A_HPG):
        psum = psum + p_c[:, h * NSA_QB:(h + 1) * NSA_QB]
    ovt = ovt_ref[...]
    imp_t = jnp.zeros((SEL_SLOTS, NSA_QB), F32)
    for piece in _split3(psum):
        imp_t = imp_t + _dot(ovt, piece)
    slot_i = lax.broadcasted_iota(jnp.int32, (SEL_SLOTS, NSA_QB), 0)
    tq_l = t0 + lax.broadcasted_iota(jnp.int32, (SEL_SLOTS, NSA_QB), 1)
    cur = tq_l // SEL_BLOCK
    forced = jnp.where(slot_i == 0, 1, 0) + jnp.where(slot_i == cur, 1, 0) + jnp.where(slot_i == cur - 1, 1, 0)
    valid = slot_i * SEL_BLOCK <= tq_l
    key = jnp.where(forced > 0, -3e38, jnp.where(valid, imp_t, NEG_BIG))
    slot_f = slot_i.astype(F32)
    bias_t = jnp.where(forced > 0, 0.0, NEG_BIG)
    for _ in range(SEL_TOPK - _N_FORCED):
        mx = jnp.max(key, axis=0, keepdims=True)
        first = jnp.min(jnp.where(key == mx, slot_f, 1e9), axis=0, keepdims=True)
        pick = slot_f == first
        bias_t = jnp.where(pick, 0.0, bias_t)
        key = jnp.where(pick, -3e38, key)

    n_wt = WINDOW // KEY_TILE + 1
    assert WINDOW % KEY_TILE == 0 and n_wt == 3
    kk = lax.broadcasted_iota(jnp.int32, (KEY_TILE, 1), 0)
    qq = tq - t0
    tri = kk <= qq
    tri_far = kk <= jnp.where(i >= 2, qq, KEY_TILE)
    w_idx = [jnp.maximum(i - 2, 0), jnp.maximum(i - 1, 0), i]
    s_w = [_dot(kw_ref[pl.ds(pl.multiple_of(w * KEY_TILE, KEY_TILE), KEY_TILE), :], qt) for w in w_idx]
    s_w = jnp.concatenate([jnp.where(tri_far, NEG_BIG, s_w[0]),
                           jnp.where(i >= 1, s_w[1], NEG_BIG),
                           jnp.where(tri, s_w[2], NEG_BIG)], axis=0)
    p_w, l_w = softmax0(s_w)
    p_w = p_w.astype(BF16)
    o_wt = _dot(vwt_ref[w_idx[0]], p_w[0:KEY_TILE])
    for k in range(1, n_wt):
        o_wt = o_wt + _dot(vwt_ref[w_idx[k]], p_w[k * KEY_TILE:(k + 1) * KEY_TILE])
    o_wt = o_wt * (1.0 / l_w)

    qt_aug = jnp.concatenate([qt, jnp.concatenate([bias_t.astype(BF16)] * A_HPG, axis=1)], axis=0)
    acc_ref[...] = jnp.zeros_like(acc_ref)
    m_ref[...] = jnp.full(m_ref.shape, 0.5 * NEG_BIG, F32)
    l_ref[...] = jnp.zeros_like(l_ref)

    def sel_body(base, causal):
        n = len(causal)
        m, l = m_ref[...], l_ref[...]
        gsz = min(SEL_GROUP, max(n // 2, 1))
        groups = [list(range(g0, min(g0 + gsz, n))) for g0 in range(0, n, gsz)]

        def scores(ks):
            out = {}
            for k in ks:
                sk = _dot(kaug_ref[base + k], qt_aug)
                if causal[k]:
                    kpos = (base + k) * KEY_TILE + lax.broadcasted_iota(jnp.int32, (KEY_TILE, 1), 0)
                    sk = jnp.where(kpos <= tq, sk, NEG_BIG)
                out[k] = sk
            return out

        s = scores(groups[0])
        for gi, ks in enumerate(groups):
            if gi + 1 < len(groups):
                s.update(scores(groups[gi + 1]))
            m_new = m
            for k in ks:
                m_new = jnp.maximum(m_new, jnp.max(s[k], axis=0, keepdims=True))
            alpha = jnp.exp2(m - m_new)
            l = alpha * l
            pv = None
            for k in ks:
                p = jnp.exp2(s.pop(k) - m_new)
                l = l + jnp.sum(p, axis=0, keepdims=True)
                d = _dot(vt_ref[base + k], p.astype(BF16))
                pv = d if pv is None else pv + d
            acc_ref[...] = alpha * acc_ref[...] + pv
            m = m_new
        m_ref[...] = m
        l_ref[...] = l

    n_bulk = i // SEL_BODY
    rem = i % SEL_BODY

    def bulk(b, c):
        sel_body(b * SEL_BODY, (False,) * SEL_BODY)
        return c
    lax.fori_loop(0, n_bulk, bulk, 0)

    base = n_bulk * SEL_BODY
    part = SEL_BODY // 2
    while part >= 2:
        pl.when(rem & part != 0)(functools.partial(sel_body, base, (False,) * part))
        base = base + (rem & part)
        part //= 2

    @pl.when(rem % 2 == 1)
    def _():
        sel_body(i - 1, (False, True))

    @pl.when(rem % 2 == 0)
    def _():
        sel_body(i, (True,))

    o_st = acc_ref[...] * (1.0 / l_ref[...])

    gates_t = gate_ref[...].astype(F32).T
    per_group = 3 * A_HPG

    def gate(h, b):
        r = 3 * h + b
        return jnp.where(grp == 0, gates_t[r:r + 1], gates_t[per_group + r:per_group + r + 1])

    for h in range(A_HPG):
        c = slice(h * NSA_QB, (h + 1) * NSA_QB)
        o = gate(h, 0) * o_ct[:, c] + gate(h, 1) * o_st[:, c] + gate(h, 2) * o_wt[:, c]
        o_ref[:, h * HEAD_DIM:(h + 1) * HEAD_DIM] = o.T.astype(o_ref.dtype)


def _overlap_matrix(seq):
    n_chunk = seq // CMP_STRIDE
    cs = np.arange(n_chunk)[:, None] * CMP_STRIDE
    ss = np.arange(SEL_SLOTS)[None, :] * SEL_BLOCK
    ov = np.maximum(np.minimum(cs + CMP_BLOCK, ss + SEL_BLOCK) - np.maximum(cs, ss), 0) / CMP_BLOCK
    n_cmp = (seq - CMP_BLOCK) // CMP_STRIDE + 1
    ov[n_cmp:] = 0.0
    ov[:, seq // SEL_BLOCK:] = 0.0
    return ov.astype(np.float32)


def _nsa(a, cmp_kv):
    S = a.shape[0]
    G = A_KV_GROUPS
    n_tiles = S // KEY_TILE
    assert S // SEL_BLOCK <= SEL_SLOTS and S % KEY_TILE == 0 and S >= WINDOW + NSA_QB
    assert NSA_QB == KEY_TILE and G == 2
    n_chunk = S // CMP_STRIDE
    ovt = jnp.asarray(_overlap_matrix(S).T, BF16)
    col = lambda b: pl.BlockSpec((S, HEAD_DIM), lambda g, i, b=b: (0, b + g))
    return pl.pallas_call(
        functools.partial(_nsa_kernel, seq=S),
        grid=(G, S // NSA_QB),
        in_specs=[
            pl.BlockSpec((NSA_QB, A_HPG * HEAD_DIM), lambda g, i: (i, g)),
            col(BLK_KS), col(BLK_KW), col(BLK_VS), col(BLK_VW),
            pl.BlockSpec((1, 1, n_chunk, HEAD_DIM), lambda g, i: (0, g, 0, 0)),
            pl.BlockSpec((1, 1, n_chunk, HEAD_DIM), lambda g, i: (1, g, 0, 0)),
            pl.BlockSpec((SEL_SLOTS, n_chunk), lambda g, i: (0, 0)),
            pl.BlockSpec((NSA_QB, LANES), lambda g, i: (i, BLK_GATE)),
        ],
        out_specs=pl.BlockSpec((NSA_QB, A_HPG * HEAD_DIM), lambda g, i: (i, g)),
        out_shape=jax.ShapeDtypeStruct((S, A_WIDTH), BF16),
        scratch_shapes=[pltpu.VMEM((n_tiles, KEY_TILE, HEAD_DIM + SEL_SLOTS), BF16),
                        pltpu.VMEM((n_tiles, HEAD_DIM, KEY_TILE), BF16),
                        pltpu.VMEM((n_tiles, HEAD_DIM, KEY_TILE), BF16),
                        pltpu.VMEM((HEAD_DIM, n_chunk), BF16),
                        pltpu.VMEM((HEAD_DIM, A_HPG * NSA_QB), F32),
                        pltpu.VMEM((1, A_HPG * NSA_QB), F32),
                        pltpu.VMEM((1, A_HPG * NSA_QB), F32)],
        compiler_params=_cparams(("arbitrary", "arbitrary")),
        name="nsa",
    )(a, a, a, a, a, cmp_kv, cmp_kv, ovt, a)


_ROW_MASKS = 18
_SCORE_MASKS = 10
HG_HEADS_PER_STEP = 8


def _hgrn_tables():
    t = np.arange(HG_CHUNK)
    p4, i4, i16 = t % 4, (t // 4) % 4, t // 16
    conds = [p4 >= 1, p4 >= 2, p4 >= 3, p4 <= 2, p4 <= 1, p4 == 0,
             i4 >= 1, i4 >= 2, i4 >= 3, i4 <= 2, i4 <= 1, i4 == 0,
             i16 >= 1, i16 >= 2, i16 >= 3, i16 <= 2, i16 <= 1, i16 == 0]
    rm = np.stack([np.broadcast_to(c[:, None], (HG_CHUNK, HG_DIM)) for c in conds]).astype(np.float32)
    tt, ss = t[:, None], t[None, :]
    lag16 = tt // 16 - ss // 16
    lag4 = tt // 4 - ss // 4
    same16, same4 = lag16 == 0, lag4 == 0
    sm = [lag16 == 1, lag16 == 2, lag16 == 3,
          same16 & (lag4 == 1), same16 & (lag4 == 2), same16 & (lag4 == 3),
          same4 & (tt - ss == 0), same4 & (tt - ss == 1), same4 & (tt - ss == 2), same4 & (tt - ss == 3)]
    return rm, np.stack(sm).astype(np.float32)


def _hgrn_kernel(q_ref, f_ref, i_ref, og_ref, loglb_ref, log1m_ref, gn_ref, rm_ref, sm_ref,
                 o_ref, st_ref, g_scr, k_scr, o_scr, *, tb, nh):
    C = HG_CHUNK
    heads = [slice(h * HG_DIM, (h + 1) * HG_DIM) for h in range(nh)]

    @pl.when(pl.program_id(1) == 0)
    def _():
        st_ref[...] = jnp.zeros_like(st_ref)

    x = f_ref[...]
    log_sig = jnp.minimum(x, 0.0) - jnp.log(1.0 + jnp.exp(-jnp.abs(x)))
    a = loglb_ref[...]
    c = log1m_ref[...] + log_sig
    log_f = jnp.maximum(a, c) + jnp.log(1.0 + jnp.exp(-jnp.abs(a - c)))
    g_scr[...] = log_f
    k_scr[...] = 1.0 - jnp.exp(log_f)

    def rmask(n):
        return jnp.concatenate([rm_ref[n]] * nh, axis=1)

    def roll(v, s):
        return pltpu.roll(v, s % C, axis=0)

    def b16(z):
        return z.astype(BF16)

    def chunk(ci, carry):
        r0 = pl.multiple_of(ci * C, C)
        g = g_scr[pl.ds(r0, C), :]
        q = q_ref[pl.ds(r0, C), :]
        k = k_scr[pl.ds(r0, C), :]
        v = i_ref[pl.ds(r0, C), :].astype(BF16)
        pre4 = g + rmask(0) * roll(g, 1) + rmask(1) * roll(g, 2) + rmask(2) * roll(g, 3)
        suf4 = rmask(3) * roll(g, -1) + rmask(4) * roll(g, -2) + rmask(5) * roll(g, -3)
        tot4 = pre4 + suf4
        a1, a2, a3 = roll(tot4, 4), roll(tot4, 8), roll(tot4, 12)
        pre16 = pre4 + rmask(6) * a1 + rmask(7) * a2 + rmask(8) * a3
        suf16 = suf4 + rmask(9) * roll(tot4, -4) + rmask(10) * roll(tot4, -8) + rmask(11) * roll(tot4, -12)
        tot16 = pre16 + suf16
        d1, d2, d3 = roll(tot16, 16), roll(tot16, 32), roll(tot16, 48)
        pre64 = pre16 + rmask(12) * d1 + rmask(13) * d2 + rmask(14) * d3
        suf64 = suf16 + rmask(15) * roll(tot16, -16) + rmask(16) * roll(tot16, -32) + rmask(17) * roll(tot16, -48)
        tot64 = pre64 + suf64
        g1, g2 = roll(g, 1), roll(g, 2)

        q_a = q * jnp.exp(pre16)
        lhs_a = [b16(q_a), b16(q_a * jnp.exp(d1)), b16(q_a * jnp.exp(d1 + d2))]
        k_a = b16(k * jnp.exp(suf16))
        q_b = q * jnp.exp(pre4)
        lhs_b = [b16(q_b), b16(q_b * jnp.exp(a1)), b16(q_b * jnp.exp(a1 + a2))]
        k_b = b16(k * jnp.exp(suf4))
        q_c = q * jnp.exp(g)
        lhs_c = [b16(q), b16(q_c), b16(q_c * jnp.exp(g1)), b16(q_c * jnp.exp(g1 + g2))]
        k_c = b16(k)
        q_s = b16(q * jnp.exp(pre64))
        k_s = b16(k * jnp.exp(suf64))
        decay = jnp.exp(tot64[0:1, :])

        def level(lhs, rhs):
            return [_dot_nt(jnp.concatenate([z[:, hh] for z in lhs], axis=0), rhs[:, hh]) for hh in heads]

        s_a, s_b, s_c = level(lhs_a, k_a), level(lhs_b, k_b), level(lhs_c, k_c)
        o_inter = [_dot_nt(q_s[:, hh], b16(st_ref[h])) for h, hh in enumerate(heads)]
        kv = [_dot_tn(v[:, hh], k_s[:, hh]) for hh in heads]
        for h, hh in enumerate(heads):
            sa, sb, sc = s_a[h], s_b[h], s_c[h]
            scores = (sm_ref[0] * sa[0:C] + sm_ref[1] * sa[C:2 * C] + sm_ref[2] * sa[2 * C:3 * C]
                      + sm_ref[3] * sb[0:C] + sm_ref[4] * sb[C:2 * C] + sm_ref[5] * sb[2 * C:3 * C]
                      + sm_ref[6] * sc[0:C] + sm_ref[7] * sc[C:2 * C] + sm_ref[8] * sc[2 * C:3 * C]
                      + sm_ref[9] * sc[3 * C:4 * C])
            o_scr[pl.ds(r0, C), hh] = _dot(b16(scores), v[:, hh]) + o_inter[h]
            st_ref[h] = st_ref[h] * decay[:, hh] + kv[h]
        return carry

    lax.fori_loop(0, tb // C, chunk, 0)

    for hh in heads:
        o = o_scr[:, hh]
        ms = jnp.mean(o * o, axis=-1, keepdims=True)
        og = og_ref[:, hh]
        y = o * lax.rsqrt(ms + NORM_EPS) * gn_ref[:, hh] * (og * jax.nn.sigmoid(og))
        o_ref[:, hh] = y.astype(o_ref.dtype)


def _hgrn(hz, loglb, log1mlb, gnorm):
    S = hz.shape[0]
    tb = HG_ROWS if S % HG_ROWS == 0 else S
    nh = HG_HEADS_PER_STEP
    ng = HG_HEADS // nh
    w = nh * HG_DIM
    rm, sm = _hgrn_tables()
    blk = lambda part: pl.BlockSpec((tb, w), lambda h, c, part=part: (c, part * ng + h))
    row = pl.BlockSpec((1, w), lambda h, c: (0, h))
    return pl.pallas_call(
        functools.partial(_hgrn_kernel, tb=tb, nh=nh),
        grid=(ng, S // tb),
        in_specs=[blk(0), blk(1), blk(2), blk(3), row, row, row,
                  pl.BlockSpec((_ROW_MASKS, HG_CHUNK, HG_DIM), lambda h, c: (0, 0, 0)),
                  pl.BlockSpec((_SCORE_MASKS, HG_CHUNK, HG_CHUNK), lambda h, c: (0, 0, 0))],
        out_specs=pl.BlockSpec((tb, w), lambda h, c: (c, h)),
        out_shape=jax.ShapeDtypeStruct((S, HG_WIDTH), BF16),
        scratch_shapes=[pltpu.VMEM((nh, HG_DIM, HG_DIM), F32), pltpu.VMEM((tb, w), F32),
                        pltpu.VMEM((tb, w), F32), pltpu.VMEM((tb, w), F32)],
        compiler_params=_cparams(("arbitrary", "arbitrary")),
        name="hgrn",
    )(hz, hz, hz, hz, loglb, log1mlb, gnorm, jnp.asarray(rm), jnp.asarray(sm))


_MERGE_PART = 512


def _merge_kernel(at_ref, hg_ref, ga_ref, gh_ref, wa_ref, wh_ref, wo_ref, x_ref, gt_ref, gp_ref, o_ref):
    at, hg = at_ref[...], hg_ref[...]
    d = wo_ref.shape[1]
    parts = [slice(c0, c0 + _MERGE_PART) for c0 in range(0, d, _MERGE_PART)]
    branch = [(_dot(at, wa_ref[0, :, cs]), _dot(hg, wh_ref[0, :, cs])) for cs in parts]
    y = None
    for cs, (ma, mh) in zip(parts, branch):
        mixed = ga_ref[:, cs].astype(F32) * ma + gh_ref[:, cs].astype(F32) * mh
        yp = _dot(mixed.astype(BF16), wo_ref[0, cs, :])
        y = yp if y is None else y + yp
    ms = jnp.mean(y * y, axis=-1, keepdims=True)
    yn = y * lax.rsqrt(ms + NORM_EPS) * gp_ref[...]
    o_ref[...] = x_ref[...] + gt_ref[...] * yn


def _merge(attn, hg, mg, wa, wh, wo, layer, x, gt, gpost):
    S, D = x.shape
    tm = MERGE_ROWS
    const = lambda shape: pl.BlockSpec((1,) + shape, lambda m: (layer, 0, 0), pipeline_mode=pl.Buffered(1))
    row = pl.BlockSpec((1, D), lambda m: (0, 0))
    return pl.pallas_call(
        _merge_kernel,
        grid=(S // tm,),
        in_specs=[
            pl.BlockSpec((tm, A_WIDTH), lambda m: (m, 0)),
            pl.BlockSpec((tm, HG_WIDTH), lambda m: (m, 0)),
            pl.BlockSpec((tm, D), lambda m: (m, 0)),
            pl.BlockSpec((tm, D), lambda m: (m, 1)),
            const((A_WIDTH, D)), const((HG_WIDTH, D)), const((D, D)),
            pl.BlockSpec((tm, D), lambda m: (m, 0)),
            row, row,
        ],
        out_specs=pl.BlockSpec((tm, D), lambda m: (m, 0)),
        out_shape=jax.ShapeDtypeStruct((S, D), F32),
        compiler_params=_cparams(("arbitrary",)),
        name="merge",
    )(attn, hg, mg, mg, wa, wh, wo, x, gt, gpost.reshape(1, D))


_HALO = 16
_FF_SUB = 256


def _norm_mod(x, g_ref, sc_ref, sh_ref):
    ms = jnp.mean(x * x, axis=-1, keepdims=True)
    y = x * lax.rsqrt(ms + NORM_EPS) * g_ref[...]
    return (y * (1.0 + sc_ref[...]) + sh_ref[...]).astype(BF16)


def _ffn_kernel(x_ref, xh_ref, g_ref, sc_ref, sh_ref, wg_ref, wv_ref, cwg_ref, cwv_ref, cbg_ref, cbv_ref,
                wd_ref, gt_ref, gp_ref, *rest, tm, emit_next):
    if emit_next:
        gn_ref, scn_ref, shn_ref, o_ref, hn_ref, h_scr, ug_scr, uv_scr = rest
    else:
        o_ref, h_scr, ug_scr, uv_scr = rest
    m = pl.program_id(0)
    f = pl.program_id(1)
    norm_mod = functools.partial(_norm_mod, g_ref=g_ref, sc_ref=sc_ref, sh_ref=sh_ref)

    @pl.when(f == 0)
    def _():
        h_scr[0:_HALO, :] = jnp.where(m > 0, norm_mod(xh_ref[...]), jnp.zeros((), BF16))
        h_scr[_HALO:_HALO + tm, :] = norm_mod(x_ref[...])

    tf = wg_ref.shape[2]
    subs = [slice(s * _FF_SUB, (s + 1) * _FF_SUB) for s in range(tf // _FF_SUB)]
    h = h_scr[...]
    ups = [(_dot(h, wg_ref[0, :, cs]), _dot(h, wv_ref[0, :, cs])) for cs in subs]

    def conv(u_scr, cw_ref, cb_ref, cs):
        out = cb_ref[:, cs] + cw_ref[0:1, cs] * u_scr[pl.ds(_HALO - 2, tm), cs]
        out = out + cw_ref[1:2, cs] * u_scr[pl.ds(_HALO - 1, tm), cs]
        return out + cw_ref[2:3, cs] * u_scr[pl.ds(_HALO, tm), cs]

    acc = jnp.where(f > 0, o_ref[...], 0.0)
    for cs, (ug, uv) in zip(subs, ups):
        ug_scr[:, cs] = ug
        uv_scr[:, cs] = uv
        cg = conv(ug_scr, cwg_ref, cbg_ref, cs)
        cv = conv(uv_scr, cwv_ref, cbv_ref, cs)
        act = (cg * jax.nn.sigmoid(cg) * cv).astype(BF16)
        acc = acc + _dot(act, wd_ref[0, cs, :])
    o_ref[...] = acc

    @pl.when(f == pl.num_programs(1) - 1)
    def _():
        y = o_ref[...]
        ms = jnp.mean(y * y, axis=-1, keepdims=True)
        yn = y * lax.rsqrt(ms + NORM_EPS) * gp_ref[...]
        x_new = x_ref[...] + gt_ref[...] * yn
        o_ref[...] = x_new
        if emit_next:
            hn_ref[...] = _norm_mod(x_new, gn_ref, scn_ref, shn_ref)


def _ffn(x, gpre, sc, sh, wup, convw, convb, wdown, layer, gt, gpost, nxt=None):
    S, D = x.shape
    F = wdown.shape[1]
    tm = FFN_ROWS if S % FFN_ROWS == 0 else S
    tf = FFN_COLS
    nf = F // tf
    hb = tm // _HALO
    row = pl.BlockSpec((1, D), lambda m, f: (0, 0))
    tile = pl.BlockSpec((tm, D), lambda m, f: (m, 0))
    extra = () if nxt is None else (nxt[0].reshape(1, D), nxt[1], nxt[2])
    return pl.pallas_call(
        functools.partial(_ffn_kernel, tm=tm, emit_next=nxt is not None),
        grid=(S // tm, nf),
        in_specs=[
            pl.BlockSpec((tm, D), lambda m, f: (m, 0)),
            pl.BlockSpec((_HALO, D), lambda m, f: (jnp.maximum(m * hb - 1, 0), 0)),
            row, row, row,
            pl.BlockSpec((1, D, tf), lambda m, f: (layer, 0, f)),
            pl.BlockSpec((1, D, tf), lambda m, f: (layer, 0, nf + f)),
            pl.BlockSpec((CONV_WIDTH, tf), lambda m, f: (0, f)),
            pl.BlockSpec((CONV_WIDTH, tf), lambda m, f: (0, nf + f)),
            pl.BlockSpec((1, tf), lambda m, f: (0, f)),
            pl.BlockSpec((1, tf), lambda m, f: (0, nf + f)),
            pl.BlockSpec((1, tf, D), lambda m, f: (layer, f, 0)),
            row, row,
        ] + ([row] * 3 if nxt is not None else []),
        out_specs=[tile, tile] if nxt is not None else tile,
        out_shape=([jax.ShapeDtypeStruct((S, D), F32), jax.ShapeDtypeStruct((S, D), BF16)]
                   if nxt is not None else jax.ShapeDtypeStruct((S, D), F32)),
        scratch_shapes=[pltpu.VMEM((tm + _HALO, D), BF16),
                        pltpu.VMEM((tm + _HALO, tf), F32), pltpu.VMEM((tm + _HALO, tf), F32)],
        compiler_params=_cparams(("arbitrary", "arbitrary")),
        name="ffn",
    )(x, x, gpre.reshape(1, D), sc, sh, wup, wup, convw, convw, convb.reshape(1, -1), convb.reshape(1, -1),
      wdown, gt, gpost.reshape(1, D), *extra)


def _rope_tables(positions):
    inv_freq = 1.0 / (ROPE_THETA ** (jnp.arange(0, HEAD_DIM, 2, dtype=F32) / HEAD_DIM))
    ang = positions.astype(F32)[:, None] * inv_freq
    cos, sin = jnp.cos(ang), jnp.sin(ang)
    return jnp.concatenate([cos, cos], axis=-1), jnp.concatenate([-sin, sin], axis=-1)


def kernel(x, c, positions, w_ada, b_ada, g_pre_mix, w_in, pe_kc, w_kc, pe_vc, w_vc, lb_logits, g_hg_norm,
           w_br_attn, w_br_hgrn, w_out, g_post_mix, g_pre_ffn, w_up, conv_w, conv_b, w_down, g_post_ffn):
    B, S, D = x.shape
    assert B == 1, "kernel is written for one sequence"
    L = w_ada.shape[0]
    xs = x[0]
    cosf, sinf = _rope_tables(positions[0])
    scale = HEAD_DIM ** -0.5 * np.log2(np.e)
    tables = (cosf * scale, sinf * scale, cosf, sinf)
    lb_cum = jnp.cumsum(jax.nn.softmax(lb_logits.astype(F32), axis=0), axis=0)
    lower = lb_cum - lb_cum[0:1]
    log_lb = jnp.log(lower)
    log_1m = jnp.log1p(-lower)
    ada = _ada_all(c, w_ada, b_ada)

    w_in_t = jnp.swapaxes(w_in, 1, 2)
    hg_roles = (("id",) * (TAIL_TILE // LANES),) * (4 * HG_WIDTH // TAIL_TILE)
    mg_roles = (("sg",) * (TAIL_TILE // LANES),) * (2 * D // TAIL_TILE)

    wa_b, wh_b, wo_b = w_br_attn.astype(BF16), w_br_hgrn.astype(BF16), w_out.astype(BF16)
    wup_b, wdown_b = w_up.astype(BF16), w_down.astype(BF16)

    mods = [[ada[l, :, j * D:(j + 1) * D] for j in range(N_ADA)] for l in range(L)]
    h = _prenorm(xs, g_pre_mix[0], mods[0][1], mods[0][0])
    for l in range(L):
        sh1, sc1, gt1, sh2, sc2, gt2 = mods[l]

        a = _proj(h, w_in_t, l, 0, A_ROLES, BF16, tn=A_TILE, tables=tables, name="proj_attn")
        hz = _proj(h, w_in_t, l, A_COLS, hg_roles, F32, tn=TAIL_TILE, name="proj_hgrn")
        mg = _proj(h, w_in_t, l, A_COLS + 4 * HG_WIDTH, mg_roles, BF16, tn=TAIL_TILE, name="proj_merge_gate")

        cmp_kv = _compress(a, jnp.stack([pe_kc[l], pe_vc[l]]), jnp.stack([w_kc[l], w_vc[l]]))
        attn = _nsa(a, cmp_kv)
        hg = _hgrn(hz, log_lb[l:l + 1], log_1m[l:l + 1], g_hg_norm[l:l + 1])
        xs = _merge(attn, hg, mg, wa_b, wh_b, wo_b, l, xs, gt1, g_post_mix[l])

        nxt = (g_pre_mix[l + 1], mods[l + 1][1], mods[l + 1][0]) if l + 1 < L else None
        out = _ffn(xs, g_pre_ffn[l], sc2, sh2, wup_b, conv_w[l], conv_b[l], wdown_b, l, gt2, g_post_ffn[l], nxt)
        xs, h = out if nxt is not None else (out, None)
    return xs[None]
```

```python
import functools

import jax
import jax.numpy as jnp
import numpy as np
from jax import lax
from jax.experimental import pallas as pl
from jax.experimental.pallas import tpu as pltpu

F32 = jnp.float32
BF16 = jnp.bfloat16

A_HEADS = 8
A_KV_GROUPS = 2
A_HPG = A_HEADS // A_KV_GROUPS
HEAD_DIM = 128
A_WIDTH = A_HEADS * HEAD_DIM
A_KV_WIDTH = A_KV_GROUPS * HEAD_DIM
CMP_BLOCK = 32
CMP_STRIDE = 16
SEL_BLOCK = 64
SEL_TOPK = 16
WINDOW = 512
ROPE_THETA = 10000.0
HG_HEADS = 8
HG_DIM = 128
HG_WIDTH = HG_HEADS * HG_DIM
HG_CHUNK = 64
CONV_WIDTH = 3
NORM_EPS = 1e-6
N_ADA = 6
A_COLS = A_WIDTH + 6 * A_KV_WIDTH + 3 * A_HEADS

LANES = 128
SUBLANES = 8
SEL_SLOTS = LANES
KEY_TILE = 256
NEG_BIG = -1e30
VMEM_LIMIT = 56 * 1024 * 1024

ADA_COLS = 1024
NORM_ROWS = 512
PROJ_ROWS = 1024
TAIL_TILE = 1024
HG_ROWS = 512
MERGE_ROWS = 256
FFN_ROWS = 512
FFN_COLS = 512


def _cparams(sem):
    return pltpu.CompilerParams(dimension_semantics=sem, vmem_limit_bytes=VMEM_LIMIT)


def _split3(a):
    hi = a.astype(BF16)
    r1 = a - hi.astype(F32)
    mid = r1.astype(BF16)
    lo = (r1 - mid.astype(F32)).astype(BF16)
    return hi, mid, lo


def _dot(a, b):
    return jnp.dot(a, b, preferred_element_type=F32)


def _dot_nt(a, b):
    return lax.dot_general(a, b, (((1,), (1,)), ((), ())), preferred_element_type=F32)


def _dot_tn(a, b):
    return lax.dot_general(a, b, (((0,), (0,)), ((), ())), preferred_element_type=F32)


def _ada_kernel(c_ref, w_ref, b_ref, o_ref):
    c = c_ref[...]
    ca = c * jax.nn.sigmoid(c)
    o_ref[0] = jnp.sum(ca * w_ref[0], axis=0, keepdims=True) + b_ref[0]


def _ada_all(c, w_ada, b_ada):
    L, D, N = w_ada.shape
    tn = ADA_COLS
    return pl.pallas_call(
        _ada_kernel,
        grid=(L, N // tn),
        in_specs=[
            pl.BlockSpec((D, 1), lambda l, n: (0, 0)),
            pl.BlockSpec((1, D, tn), lambda l, n: (l, 0, n)),
            pl.BlockSpec((1, 1, tn), lambda l, n: (l, 0, n)),
        ],
        out_specs=pl.BlockSpec((1, 1, tn), lambda l, n: (l, 0, n)),
        out_shape=jax.ShapeDtypeStruct((L, 1, N), F32),
        compiler_params=_cparams(("arbitrary", "arbitrary")),
        name="ada",
    )(c.reshape(D, 1), w_ada, b_ada.reshape(L, 1, N))


def _prenorm_kernel(x_ref, g_ref, sc_ref, sh_ref, o_ref):
    x = x_ref[...]
    ms = jnp.mean(x * x, axis=-1, keepdims=True)
    y = x * lax.rsqrt(ms + NORM_EPS) * g_ref[...]
    o_ref[...] = (y * (1.0 + sc_ref[...]) + sh_ref[...]).astype(o_ref.dtype)


def _prenorm(x, g, sc, sh):
    S, D = x.shape
    tm = NORM_ROWS
    row = pl.BlockSpec((1, D), lambda m: (0, 0))
    return pl.pallas_call(
        _prenorm_kernel,
        grid=(S // tm,),
        in_specs=[pl.BlockSpec((tm, D), lambda m: (m, 0)), row, row, row],
        out_specs=pl.BlockSpec((tm, D), lambda m: (m, 0)),
        out_shape=jax.ShapeDtypeStruct((S, D), BF16),
        compiler_params=_cparams(("arbitrary",)),
        name="prenorm",
    )(x, g.reshape(1, D), sc, sh)


def _rope(t, cosf, sinf):
    return t * cosf + pltpu.roll(t, HEAD_DIM // 2, axis=1) * sinf


_PROJ_PART = 256


def _proj_kernel(h_ref, w_ref, *rest, roles, tn):
    o_ref, w_scr = rest[-2:]
    tabs = rest[:-2]
    n = pl.program_id(0)

    @pl.when(pl.program_id(1) == 0)
    def _():
        w_scr[...] = w_ref[0].astype(BF16)

    def emit(slabs):
        h = h_ref[...]
        per = _PROJ_PART // LANES
        parts = [_dot_nt(h, w_scr[c0:c0 + _PROJ_PART, :]) for c0 in range(0, tn, _PROJ_PART)]
        for pi, acc in enumerate(parts):
            for s in range(per):
                kind = slabs[pi * per + s]
                t = acc[:, s * LANES:(s + 1) * LANES]
                if kind == "sg":
                    t = jax.nn.sigmoid(t)
                elif kind != "id":
                    cos_ref, sin_ref = tabs[0:2] if kind == "rq" else tabs[-2:]
                    t = _rope(t, cos_ref[...], sin_ref[...])
                c = (pi * per + s) * LANES
                o_ref[:, c:c + LANES] = t.astype(o_ref.dtype)

    kinds = sorted(set(roles))
    if len(kinds) == 1:
        emit(kinds[0])
        return
    for kind in kinds:
        hit = functools.reduce(jnp.logical_or, [n == t for t, r in enumerate(roles) if r == kind])
        pl.when(hit)(functools.partial(emit, kind))


def _proj(h, w_t, layer, row0, roles, out_dtype, *, tn, tables=(), name="proj"):
    S, D = h.shape
    tm = PROJ_ROWS if S % PROJ_ROWS == 0 else S
    nt = len(roles)
    tab = pl.BlockSpec((tm, HEAD_DIM), lambda n, m: (m, 0))
    return pl.pallas_call(
        functools.partial(_proj_kernel, roles=tuple(roles), tn=tn),
        grid=(nt, S // tm),
        in_specs=[pl.BlockSpec((tm, D), lambda n, m: (m, 0)),
                  pl.BlockSpec((pl.Element(1), pl.Element(tn), pl.Element(D)),
                               lambda n, m: (layer, pl.multiple_of(row0 + n * tn, SUBLANES), 0))]
                 + [tab] * len(tables),
        out_specs=pl.BlockSpec((tm, tn), lambda n, m: (m, n)),
        out_shape=jax.ShapeDtypeStruct((S, nt * tn), out_dtype),
        scratch_shapes=[pltpu.VMEM((tn, D), BF16)],
        compiler_params=_cparams(("arbitrary", "arbitrary")),
        name=name,
    )(h, w_t, *tables)


A_TILE = 1024
_A_SLABS = (("rq",) * (A_WIDTH // LANES)
            + (("rk",) * A_KV_GROUPS + ("id",) * A_KV_GROUPS) * 3
            + ("sg",) * (-(A_WIDTH // LANES + 6 * A_KV_GROUPS) % (A_TILE // LANES)))
A_ROLES = tuple(_A_SLABS[i:i + A_TILE // LANES] for i in range(0, len(_A_SLABS), A_TILE // LANES))
_KV_BLOCK0 = A_WIDTH // LANES
BLK_KC, BLK_VC, BLK_KS, BLK_VS, BLK_KW, BLK_VW = [_KV_BLOCK0 + j * A_KV_GROUPS for j in range(6)]
BLK_GATE = _KV_BLOCK0 + 6 * A_KV_GROUPS


def _compress_kernel(t_ref, pe_ref, w_ref, o_ref, t32_ref, *, n_chunk):
    half = CMP_BLOCK // 2
    t32_ref[...] = t_ref[...].astype(F32)
    a = jnp.concatenate([t32_ref[pl.ds(j, n_chunk, stride=CMP_STRIDE), :] for j in range(half)],
                        axis=1).astype(BF16)
    kw = half * HEAD_DIM
    w = w_ref[0].astype(BF16)
    y_top = _dot(a, w[:kw])
    y_bot = _dot(a, w[kw:])
    pe = jnp.concatenate([pe_ref[0, j:j + 1, :] for j in range(CMP_BLOCK)], axis=1)
    pe_term = jnp.zeros((1, HEAD_DIM), F32)
    for piece in _split3(pe):
        pe_term = pe_term + _dot(piece, w)
    o_ref[0, 0] = y_top + pltpu.roll(y_bot, n_chunk - 1, axis=0) + pe_term


def _compress(a, pe2, w2):
    S = a.shape[0]
    n_chunk = S // CMP_STRIDE
    G = A_KV_GROUPS
    return pl.pallas_call(
        functools.partial(_compress_kernel, n_chunk=n_chunk),
        grid=(2, G),
        in_specs=[
            pl.BlockSpec((S, HEAD_DIM), lambda kv, g: (0, BLK_KC + kv * G + g)),
            pl.BlockSpec((1, CMP_BLOCK, HEAD_DIM), lambda kv, g: (kv, 0, 0)),
            pl.BlockSpec((1, CMP_BLOCK * HEAD_DIM, HEAD_DIM), lambda kv, g: (kv, 0, 0)),
        ],
        out_specs=pl.BlockSpec((1, 1, n_chunk, HEAD_DIM), lambda kv, g: (kv, g, 0, 0)),
        out_shape=jax.ShapeDtypeStruct((2, G, n_chunk, HEAD_DIM), F32),
        scratch_shapes=[pltpu.VMEM((S, HEAD_DIM), F32)],
        compiler_params=_cparams(("arbitrary", "arbitrary")),
        name="compress",
    )(a, pe2, w2)


NSA_QB = 256
SEL_BODY = 8
SEL_GROUP = 4
_N_FORCED = 3


def _nsa_kernel(q_ref, ks_ref, kw_ref, vs_ref, vw_ref, kc_ref, vc_ref, ovt_ref, gate_ref,
                o_ref, kaug_ref, vt_ref, vwt_ref, vct_ref, acc_ref, m_ref, l_ref, *, seq):
    grp = pl.program_id(0)
    i = pl.program_id(1)
    t0 = i * NSA_QB
    n_cmp = seq // CMP_STRIDE
    cols = A_HPG * NSA_QB

    @pl.when(i == 0)
    def _():
        def build(r, c):
            r0 = pl.multiple_of(r * KEY_TILE, KEY_TILE)
            kaug_ref[r, :, 0:HEAD_DIM] = ks_ref[pl.ds(r0, KEY_TILE), :]
            blk = (r0 + lax.broadcasted_iota(jnp.int32, (KEY_TILE, SEL_SLOTS), 0)) // SEL_BLOCK
            slot = lax.broadcasted_iota(jnp.int32, (KEY_TILE, SEL_SLOTS), 1)
            kaug_ref[r, :, HEAD_DIM:HEAD_DIM + SEL_SLOTS] = jnp.where(blk == slot, 1.0, 0.0).astype(BF16)
            vt_ref[r] = vs_ref[pl.ds(r0, KEY_TILE), :].astype(F32).T.astype(BF16)
            vwt_ref[r] = vw_ref[pl.ds(r0, KEY_TILE), :].astype(F32).T.astype(BF16)
            return c
        lax.fori_loop(0, seq // KEY_TILE, build, 0)
        vct_ref[...] = vc_ref[0, 0].T.astype(BF16)

    qt = jnp.concatenate(
        [q_ref[:, h * HEAD_DIM:(h + 1) * HEAD_DIM].astype(F32).T.astype(BF16) for h in range(A_HPG)],
        axis=1)
    tq = t0 + lax.broadcasted_iota(jnp.int32, (1, cols), 1) % NSA_QB

    def softmax0(s):
        m = jnp.max(s, axis=0, keepdims=True)
        p = jnp.exp2(s - m)
        return p, jnp.sum(p, axis=0, keepdims=True)

    s_c = _dot(kc_ref[0, 0].astype(BF16), qt)
    nrow = lax.broadcasted_iota(jnp.int32, (n_cmp, 1), 0)
    vis = (nrow * CMP_STRIDE + (CMP_BLOCK - 1)) <= tq
    p_c, den = softmax0(jnp.where(vis, s_c, NEG_BIG))
    p_c = p_c * jnp.where(tq >= CMP_BLOCK - 1, 1.0 / den, 0.0)
    o_ct = _dot(vct_ref[...], p_c.astype(BF16))

    psum = p_c[:, 0:NSA_QB]
    for h in range(1, A_HPG):
        psum = psum + p_c[:, h * NSA_QB:(h + 1) * NSA_QB]
    ovt = ovt_ref[...]
    imp_t = jnp.zeros((SEL_SLOTS, NSA_QB), F32)
    for piece in _split3(psum):
        imp_t = imp_t + _dot(ovt, piece)
    slot_i = lax.broadcasted_iota(jnp.int32, (SEL_SLOTS, NSA_QB), 0)
    tq_l = t0 + lax.broadcasted_iota(jnp.int32, (SEL_SLOTS, NSA_QB), 1)
    cur = tq_l // SEL_BLOCK
    forced = jnp.where(slot_i == 0, 1, 0) + jnp.where(slot_i == cur, 1, 0) + jnp.where(slot_i == cur - 1, 1, 0)
    valid = slot_i * SEL_BLOCK <= tq_l
    key = jnp.where(forced > 0, -3e38, jnp.where(valid, imp_t, NEG_BIG))
    slot_f = slot_i.astype(F32)
    bias_t = jnp.where(forced > 0, 0.0, NEG_BIG)
    for _ in range(SEL_TOPK - _N_FORCED):
        mx = jnp.max(key, axis=0, keepdims=True)
        first = jnp.min(jnp.where(key == mx, slot_f, 1e9), axis=0, keepdims=True)
        pick = slot_f == first
        bias_t = jnp.where(pick, 0.0, bias_t)
        key = jnp.where(pick, -3e38, key)

    n_wt = WINDOW // KEY_TILE + 1
    assert WINDOW % KEY_TILE == 0 and n_wt == 3
    kk = lax.broadcasted_iota(jnp.int32, (KEY_TILE, 1), 0)
    qq = tq - t0
    tri = kk <= qq
    tri_far = kk <= jnp.where(i >= 2, qq, KEY_TILE)
    w_idx = [jnp.maximum(i - 2, 0), jnp.maximum(i - 1, 0), i]
    s_w = [_dot(kw_ref[pl.ds(pl.multiple_of(w * KEY_TILE, KEY_TILE), KEY_TILE), :], qt) for w in w_idx]
    s_w = jnp.concatenate([jnp.where(tri_far, NEG_BIG, s_w[0]),
                           jnp.where(i >= 1, s_w[1], NEG_BIG),
                           jnp.where(tri, s_w[2], NEG_BIG)], axis=0)
    p_w, l_w = softmax0(s_w)
    p_w = p_w.astype(BF16)
    o_wt = _dot(vwt_ref[w_idx[0]], p_w[0:KEY_TILE])
    for k in range(1, n_wt):
        o_wt = o_wt + _dot(vwt_ref[w_idx[k]], p_w[k * KEY_TILE:(k + 1) * KEY_TILE])
    o_wt = o_wt * (1.0 / l_w)

    qt_aug = jnp.concatenate([qt, jnp.concatenate([bias_t.astype(BF16)] * A_HPG, axis=1)], axis=0)
    acc_ref[...] = jnp.zeros_like(acc_ref)
    m_ref[...] = jnp.full(m_ref.shape, 0.5 * NEG_BIG, F32)
    l_ref[...] = jnp.zeros_like(l_ref)

    def sel_body(base, causal):
        n = len(causal)
        m, l = m_ref[...], l_ref[...]
        gsz = min(SEL_GROUP, max(n // 2, 1))
        groups = [list(range(g0, min(g0 + gsz, n))) for g0 in range(0, n, gsz)]

        def scores(ks):
            out = {}
            for k in ks:
                sk = _dot(kaug_ref[base + k], qt_aug)
                if causal[k]:
                    kpos = (base + k) * KEY_TILE + lax.broadcasted_iota(jnp.int32, (KEY_TILE, 1), 0)
                    sk = jnp.where(kpos <= tq, sk, NEG_BIG)
                out[k] = sk
            return out

        s = scores(groups[0])
        for gi, ks in enumerate(groups):
            if gi + 1 < len(groups):
                s.update(scores(groups[gi + 1]))
            m_new = m
            for k in ks:
                m_new = jnp.maximum(m_new, jnp.max(s[k], axis=0, keepdims=True))
            alpha = jnp.exp2(m - m_new)
            l = alpha * l
            pv = None
            for k in ks:
                p = jnp.exp2(s.pop(k) - m_new)
                l = l + jnp.sum(p, axis=0, keepdims=True)
                d = _dot(vt_ref[base + k], p.astype(BF16))
                pv = d if pv is None else pv + d
            acc_ref[...] = alpha * acc_ref[...] + pv
            m = m_new
        m_ref[...] = m
        l_ref[...] = l

    n_bulk = i // SEL_BODY
    rem = i % SEL_BODY

    def bulk(b, c):
        sel_body(b * SEL_BODY, (False,) * SEL_BODY)
        return c
    lax.fori_loop(0, n_bulk, bulk, 0)

    base = n_bulk * SEL_BODY
    part = SEL_BODY // 2
    while part >= 2:
        pl.when(rem & part != 0)(functools.partial(sel_body, base, (False,) * part))
        base = base + (rem & part)
        part //= 2

    @pl.when(rem % 2 == 1)
    def _():
        sel_body(i - 1, (False, True))

    @pl.when(rem % 2 == 0)
    def _():
        sel_body(i, (True,))

    o_st = acc_ref[...] * (1.0 / l_ref[...])

    gates_t = gate_ref[...].astype(F32).T
    per_group = 3 * A_HPG

    def gate(h, b):
        r = 3 * h + b
        return jnp.where(grp == 0, gates_t[r:r + 1], gates_t[per_group + r:per_group + r + 1])

    for h in range(A_HPG):
        c = slice(h * NSA_QB, (h + 1) * NSA_QB)
        o = gate(h, 0) * o_ct[:, c] + gate(h, 1) * o_st[:, c] + gate(h, 2) * o_wt[:, c]
        o_ref[:, h * HEAD_DIM:(h + 1) * HEAD_DIM] = o.T.astype(o_ref.dtype)


def _overlap_matrix(seq):
    n_chunk = seq // CMP_STRIDE
    cs = np.arange(n_chunk)[:, None] * CMP_STRIDE
    ss = np.arange(SEL_SLOTS)[None, :] * SEL_BLOCK
    ov = np.maximum(np.minimum(cs + CMP_BLOCK, ss + SEL_BLOCK) - np.maximum(cs, ss), 0) / CMP_BLOCK
    n_cmp = (seq - CMP_BLOCK) // CMP_STRIDE + 1
    ov[n_cmp:] = 0.0
    ov[:, seq // SEL_BLOCK:] = 0.0
    return ov.astype(np.float32)


def _nsa(a, cmp_kv):
    S = a.shape[0]
    G = A_KV_GROUPS
    n_tiles = S // KEY_TILE
    assert S // SEL_BLOCK <= SEL_SLOTS and S % KEY_TILE == 0 and S >= WINDOW + NSA_QB
    assert NSA_QB == KEY_TILE and G == 2
    n_chunk = S // CMP_STRIDE
    ovt = jnp.asarray(_overlap_matrix(S).T, BF16)
    col = lambda b: pl.BlockSpec((S, HEAD_DIM), lambda g, i, b=b: (0, b + g))
    return pl.pallas_call(
        functools.partial(_nsa_kernel, seq=S),
        grid=(G, S // NSA_QB),
        in_specs=[
            pl.BlockSpec((NSA_QB, A_HPG * HEAD_DIM), lambda g, i: (i, g)),
            col(BLK_KS), col(BLK_KW), col(BLK_VS), col(BLK_VW),
            pl.BlockSpec((1, 1, n_chunk, HEAD_DIM), lambda g, i: (0, g, 0, 0)),
            pl.BlockSpec((1, 1, n_chunk, HEAD_DIM), lambda g, i: (1, g, 0, 0)),
            pl.BlockSpec((SEL_SLOTS, n_chunk), lambda g, i: (0, 0)),
            pl.BlockSpec((NSA_QB, LANES), lambda g, i: (i, BLK_GATE)),
        ],
        out_specs=pl.BlockSpec((NSA_QB, A_HPG * HEAD_DIM), lambda g, i: (i, g)),
        out_shape=jax.ShapeDtypeStruct((S, A_WIDTH), BF16),
        scratch_shapes=[pltpu.VMEM((n_tiles, KEY_TILE, HEAD_DIM + SEL_SLOTS), BF16),
                        pltpu.VMEM((n_tiles, HEAD_DIM, KEY_TILE), BF16),
                        pltpu.VMEM((n_tiles, HEAD_DIM, KEY_TILE), BF16),
                        pltpu.VMEM((HEAD_DIM, n_chunk), BF16),
                        pltpu.VMEM((HEAD_DIM, A_HPG * NSA_QB), F32),
                        pltpu.VMEM((1, A_HPG * NSA_QB), F32),
                        pltpu.VMEM((1, A_HPG * NSA_QB), F32)],
        compiler_params=_cparams(("arbitrary", "arbitrary")),
        name="nsa",
    )(a, a, a, a, a, cmp_kv, cmp_kv, ovt, a)


_ROW_MASKS = 18
_SCORE_MASKS = 10
HG_HEADS_PER_STEP = 8


def _hgrn_tables():
    t = np.arange(HG_CHUNK)
    p4, i4, i16 = t % 4, (t // 4) % 4, t // 16
    conds = [p4 >= 1, p4 >= 2, p4 >= 3, p4 <= 2, p4 <= 1, p4 == 0,
             i4 >= 1, i4 >= 2, i4 >= 3, i4 <= 2, i4 <= 1, i4 == 0,
             i16 >= 1, i16 >= 2, i16 >= 3, i16 <= 2, i16 <= 1, i16 == 0]
    rm = np.stack([np.broadcast_to(c[:, None], (HG_CHUNK, HG_DIM)) for c in conds]).astype(np.float32)
    tt, ss = t[:, None], t[None, :]
    lag16 = tt // 16 - ss // 16
    lag4 = tt // 4 - ss // 4
    same16, same4 = lag16 == 0, lag4 == 0
    sm = [lag16 == 1, lag16 == 2, lag16 == 3,
          same16 & (lag4 == 1), same16 & (lag4 == 2), same16 & (lag4 == 3),
          same4 & (tt - ss == 0), same4 & (tt - ss == 1), same4 & (tt - ss == 2), same4 & (tt - ss == 3)]
    return rm, np.stack(sm).astype(np.float32)


def _hgrn_kernel(q_ref, f_ref, i_ref, og_ref, loglb_ref, log1m_ref, gn_ref, rm_ref, sm_ref,
                 o_ref, st_ref, g_scr, k_scr, o_scr, *, tb, nh):
    C = HG_CHUNK
    heads = [slice(h * HG_DIM, (h + 1) * HG_DIM) for h in range(nh)]

    @pl.when(pl.program_id(1) == 0)
    def _():
        st_ref[...] = jnp.zeros_like(st_ref)

    x = f_ref[...]
    log_sig = jnp.minimum(x, 0.0) - jnp.log(1.0 + jnp.exp(-jnp.abs(x)))
    a = loglb_ref[...]
    c = log1m_ref[...] + log_sig
    log_f = jnp.maximum(a, c) + jnp.log(1.0 + jnp.exp(-jnp.abs(a - c)))
    g_scr[...] = log_f
    k_scr[...] = jnp.exp(log1m_ref[...] + (log_sig - x))

    def rmask(n):
        return jnp.concatenate([rm_ref[n]] * nh, axis=1)

    def roll(v, s):
        return pltpu.roll(v, s % C, axis=0)

    def b16(z):
        return z.astype(BF16)

    def chunk(ci, carry):
        r0 = pl.multiple_of(ci * C, C)
        g = g_scr[pl.ds(r0, C), :]
        q = q_ref[pl.ds(r0, C), :]
        k = k_scr[pl.ds(r0, C), :]
        v = i_ref[pl.ds(r0, C), :].astype(BF16)
        pre4 = g + rmask(0) * roll(g, 1) + rmask(1) * roll(g, 2) + rmask(2) * roll(g, 3)
        suf4 = rmask(3) * roll(g, -1) + rmask(4) * roll(g, -2) + rmask(5) * roll(g, -3)
        tot4 = pre4 + suf4
        a1, a2, a3 = roll(tot4, 4), roll(tot4, 8), roll(tot4, 12)
        pre16 = pre4 + rmask(6) * a1 + rmask(7) * a2 + rmask(8) * a3
        suf16 = suf4 + rmask(9) * roll(tot4, -4) + rmask(10) * roll(tot4, -8) + rmask(11) * roll(tot4, -12)
        tot16 = pre16 + suf16
        d1, d2, d3 = roll(tot16, 16), roll(tot16, 32), roll(tot16, 48)
        pre64 = pre16 + rmask(12) * d1 + rmask(13) * d2 + rmask(14) * d3
        suf64 = suf16 + rmask(15) * roll(tot16, -16) + rmask(16) * roll(tot16, -32) + rmask(17) * roll(tot16, -48)
        tot64 = pre64 + suf64
        g1, g2 = roll(g, 1), roll(g, 2)

        q_a = q * jnp.exp(pre16)
        lhs_a = [b16(q_a), b16(q_a * jnp.exp(d1)), b16(q_a * jnp.exp(d1 + d2))]
        k_a = b16(k * jnp.exp(suf16))
        q_b = q * jnp.exp(pre4)
        lhs_b = [b16(q_b), b16(q_b * jnp.exp(a1)), b16(q_b * jnp.exp(a1 + a2))]
        k_b = b16(k * jnp.exp(suf4))
        q_c = q * jnp.exp(g)
        lhs_c = [b16(q), b16(q_c), b16(q_c * jnp.exp(g1)), b16(q_c * jnp.exp(g1 + g2))]
        k_c = b16(k)
        q_s = b16(q * jnp.exp(pre64))
        k_s = b16(k * jnp.exp(suf64))
        decay = jnp.exp(tot64[0:1, :])

        def level(lhs, rhs):
            return [_dot_nt(jnp.concatenate([z[:, hh] for z in lhs], axis=0), rhs[:, hh]) for hh in heads]

        s_a, s_b, s_c = level(lhs_a, k_a), level(lhs_b, k_b), level(lhs_c, k_c)
        o_inter = [_dot_nt(q_s[:, hh], b16(st_ref[h])) for h, hh in enumerate(heads)]
        kv = [_dot_tn(v[:, hh], k_s[:, hh]) for hh in heads]
        for h, hh in enumerate(heads):
            sa, sb, sc = s_a[h], s_b[h], s_c[h]
            scores = (sm_ref[0] * sa[0:C] + sm_ref[1] * sa[C:2 * C] + sm_ref[2] * sa[2 * C:3 * C]
                      + sm_ref[3] * sb[0:C] + sm_ref[4] * sb[C:2 * C] + sm_ref[5] * sb[2 * C:3 * C]
                      + sm_ref[6] * sc[0:C] + sm_ref[7] * sc[C:2 * C] + sm_ref[8] * sc[2 * C:3 * C]
                      + sm_ref[9] * sc[3 * C:4 * C])
            o_scr[pl.ds(r0, C), hh] = _dot(b16(scores), v[:, hh]) + o_inter[h]
            st_ref[h] = st_ref[h] * decay[:, hh] + kv[h]
        return carry

    lax.fori_loop(0, tb // C, chunk, 0)

    for hh in heads:
        o = o_scr[:, hh]
        ms = jnp.mean(o * o, axis=-1, keepdims=True)
        og = og_ref[:, hh]
        y = o * lax.rsqrt(ms + NORM_EPS) * gn_ref[:, hh] * (og * jax.nn.sigmoid(og))
        o_ref[:, hh] = y.astype(o_ref.dtype)


def _hgrn(hz, loglb, log1mlb, gnorm):
    S = hz.shape[0]
    tb = HG_ROWS if S % HG_ROWS == 0 else S
    nh = HG_HEADS_PER_STEP
    ng = HG_HEADS // nh
    w = nh * HG_DIM
    rm, sm = _hgrn_tables()
    blk = lambda part: pl.BlockSpec((tb, w), lambda h, c, part=part: (c, part * ng + h))
    row = pl.BlockSpec((1, w), lambda h, c: (0, h))
    return pl.pallas_call(
        functools.partial(_hgrn_kernel, tb=tb, nh=nh),
        grid=(ng, S // tb),
        in_specs=[blk(0), blk(1), blk(2), blk(3), row, row, row,
                  pl.BlockSpec((_ROW_MASKS, HG_CHUNK, HG_DIM), lambda h, c: (0, 0, 0)),
                  pl.BlockSpec((_SCORE_MASKS, HG_CHUNK, HG_CHUNK), lambda h, c: (0, 0, 0))],
        out_specs=pl.BlockSpec((tb, w), lambda h, c: (c, h)),
        out_shape=jax.ShapeDtypeStruct((S, HG_WIDTH), BF16),
        scratch_shapes=[pltpu.VMEM((nh, HG_DIM, HG_DIM), F32), pltpu.VMEM((tb, w), F32),
                        pltpu.VMEM((tb, w), F32), pltpu.VMEM((tb, w), F32)],
        compiler_params=_cparams(("arbitrary", "arbitrary")),
        name="hgrn",
    )(hz, hz, hz, hz, loglb, log1mlb, gnorm, jnp.asarray(rm), jnp.asarray(sm))


_MERGE_PART = 512


def _merge_kernel(at_ref, hg_ref, ga_ref, gh_ref, wa_ref, wh_ref, wo_ref, x_ref, gt_ref, gp_ref, o_ref):
    at, hg = at_ref[...], hg_ref[...]
    d = wo_ref.shape[1]
    parts = [slice(c0, c0 + _MERGE_PART) for c0 in range(0, d, _MERGE_PART)]
    branch = [(_dot(at, wa_ref[0, :, cs]), _dot(hg, wh_ref[0, :, cs])) for cs in parts]
    y = None
    for cs, (ma, mh) in zip(parts, branch):
        mixed = ga_ref[:, cs].astype(F32) * ma + gh_ref[:, cs].astype(F32) * mh
        yp = _dot(mixed.astype(BF16), wo_ref[0, cs, :])
        y = yp if y is None else y + yp
    ms = jnp.mean(y * y, axis=-1, keepdims=True)
    yn = y * lax.rsqrt(ms + NORM_EPS) * gp_ref[...]
    o_ref[...] = x_ref[...] + gt_ref[...] * yn


def _merge(attn, hg, mg, wa, wh, wo, layer, x, gt, gpost):
    S, D = x.shape
    tm = MERGE_ROWS
    const = lambda shape: pl.BlockSpec((1,) + shape, lambda m: (layer, 0, 0), pipeline_mode=pl.Buffered(1))
    row = pl.BlockSpec((1, D), lambda m: (0, 0))
    return pl.pallas_call(
        _merge_kernel,
        grid=(S // tm,),
        in_specs=[
            pl.BlockSpec((tm, A_WIDTH), lambda m: (m, 0)),
            pl.BlockSpec((tm, HG_WIDTH), lambda m: (m, 0)),
            pl.BlockSpec((tm, D), lambda m: (m, 0)),
            pl.BlockSpec((tm, D), lambda m: (m, 1)),
            const((A_WIDTH, D)), const((HG_WIDTH, D)), const((D, D)),
            pl.BlockSpec((tm, D), lambda m: (m, 0)),
            row, row,
        ],
        out_specs=pl.BlockSpec((tm, D), lambda m: (m, 0)),
        out_shape=jax.ShapeDtypeStruct((S, D), F32),
        compiler_params=_cparams(("arbitrary",)),
        name="merge",
    )(attn, hg, mg, mg, wa, wh, wo, x, gt, gpost.reshape(1, D))


_HALO = 16
_FF_SUB = 256


def _norm_mod(x, g_ref, sc_ref, sh_ref):
    ms = jnp.mean(x * x, axis=-1, keepdims=True)
    y = x * lax.rsqrt(ms + NORM_EPS) * g_ref[...]
    return (y * (1.0 + sc_ref[...]) + sh_ref[...]).astype(BF16)


def _ffn_kernel(x_ref, xh_ref, g_ref, sc_ref, sh_ref, wg_ref, wv_ref, cwg_ref, cwv_ref, cbg_ref, cbv_ref,
                wd_ref, gt_ref, gp_ref, *rest, tm, emit_next):
    if emit_next:
        gn_ref, scn_ref, shn_ref, o_ref, hn_ref, h_scr, ug_scr, uv_scr = rest
    else:
        o_ref, h_scr, ug_scr, uv_scr = rest
    m = pl.program_id(0)
    f = pl.program_id(1)
    norm_mod = functools.partial(_norm_mod, g_ref=g_ref, sc_ref=sc_ref, sh_ref=sh_ref)

    @pl.when(f == 0)
    def _():
        h_scr[0:_HALO, :] = jnp.where(m > 0, norm_mod(xh_ref[...]), jnp.zeros((), BF16))
        h_scr[_HALO:_HALO + tm, :] = norm_mod(x_ref[...])

    tf = wg_ref.shape[2]
    subs = [slice(s * _FF_SUB, (s + 1) * _FF_SUB) for s in range(tf // _FF_SUB)]
    h = h_scr[...]
    ups = [(_dot(h, wg_ref[0, :, cs]), _dot(h, wv_ref[0, :, cs])) for cs in subs]

    def conv(u_scr, cw_ref, cb_ref, cs):
        out = cb_ref[:, cs] + cw_ref[0:1, cs] * u_scr[pl.ds(_HALO - 2, tm), cs]
        out = out + cw_ref[1:2, cs] * u_scr[pl.ds(_HALO - 1, tm), cs]
        return out + cw_ref[2:3, cs] * u_scr[pl.ds(_HALO, tm), cs]

    acc = jnp.where(f > 0, o_ref[...], 0.0)
    for cs, (ug, uv) in zip(subs, ups):
        ug_scr[:, cs] = ug
        uv_scr[:, cs] = uv
        cg = conv(ug_scr, cwg_ref, cbg_ref, cs)
        cv = conv(uv_scr, cwv_ref, cbv_ref, cs)
        act = (cg * jax.nn.sigmoid(cg) * cv).astype(BF16)
        acc = acc + _dot(act, wd_ref[0, cs, :])
    o_ref[...] = acc

    @pl.when(f == pl.num_programs(1) - 1)
    def _():
        y = o_ref[...]
        ms = jnp.mean(y * y, axis=-1, keepdims=True)
        yn = y * lax.rsqrt(ms + NORM_EPS) * gp_ref[...]
        x_new = x_ref[...] + gt_ref[...] * yn
        o_ref[...] = x_new
        if emit_next:
            hn_ref[...] = _norm_mod(x_new, gn_ref, scn_ref, shn_ref)


def _ffn(x, gpre, sc, sh, wup, convw, convb, wdown, layer, gt, gpost, nxt=None):
    S, D = x.shape
    F = wdown.shape[1]
    tm = FFN_ROWS if S % FFN_ROWS == 0 else S
    tf = FFN_COLS
    nf = F // tf
    hb = tm // _HALO
    row = pl.BlockSpec((1, D), lambda m, f: (0, 0))
    tile = pl.BlockSpec((tm, D), lambda m, f: (m, 0))
    extra = () if nxt is None else (nxt[0].reshape(1, D), nxt[1], nxt[2])
    return pl.pallas_call(
        functools.partial(_ffn_kernel, tm=tm, emit_next=nxt is not None),
        grid=(S // tm, nf),
        in_specs=[
            pl.BlockSpec((tm, D), lambda m, f: (m, 0)),
            pl.BlockSpec((_HALO, D), lambda m, f: (jnp.maximum(m * hb - 1, 0), 0)),
            row, row, row,
            pl.BlockSpec((1, D, tf), lambda m, f: (layer, 0, f)),
            pl.BlockSpec((1, D, tf), lambda m, f: (layer, 0, nf + f)),
            pl.BlockSpec((CONV_WIDTH, tf), lambda m, f: (0, f)),
            pl.BlockSpec((CONV_WIDTH, tf), lambda m, f: (0, nf + f)),
            pl.BlockSpec((1, tf), lambda m, f: (0, f)),
            pl.BlockSpec((1, tf), lambda m, f: (0, nf + f)),
            pl.BlockSpec((1, tf, D), lambda m, f: (layer, f, 0)),
            row, row,
        ] + ([row] * 3 if nxt is not None else []),
        out_specs=[tile, tile] if nxt is not None else tile,
        out_shape=([jax.ShapeDtypeStruct((S, D), F32), jax.ShapeDtypeStruct((S, D), BF16)]
                   if nxt is not None else jax.ShapeDtypeStruct((S, D), F32)),
        scratch_shapes=[pltpu.VMEM((tm + _HALO, D), BF16),
                        pltpu.VMEM((tm + _HALO, tf), F32), pltpu.VMEM((tm + _HALO, tf), F32)],
        compiler_params=_cparams(("arbitrary", "arbitrary")),
        name="ffn",
    )(x, x, gpre.reshape(1, D), sc, sh, wup, wup, convw, convw, convb.reshape(1, -1), convb.reshape(1, -1),
      wdown, gt, gpost.reshape(1, D), *extra)


def _rope_tables(positions):
    inv_freq = 1.0 / (ROPE_THETA ** (jnp.arange(0, HEAD_DIM, 2, dtype=F32) / HEAD_DIM))
    ang = positions.astype(F32)[:, None] * inv_freq
    cos, sin = jnp.cos(ang), jnp.sin(ang)
    return jnp.concatenate([cos, cos], axis=-1), jnp.concatenate([-sin, sin], axis=-1)


def kernel(x, c, positions, w_ada, b_ada, g_pre_mix, w_in, pe_kc, w_kc, pe_vc, w_vc, lb_logits, g_hg_norm,
           w_br_attn, w_br_hgrn, w_out, g_post_mix, g_pre_ffn, w_up, conv_w, conv_b, w_down, g_post_ffn):
    B, S, D = x.shape
    assert B == 1, "kernel is written for one sequence"
    L = w_ada.shape[0]
    xs = x[0]
    cosf, sinf = _rope_tables(positions[0])
    scale = HEAD_DIM ** -0.5 * np.log2(np.e)
    tables = (cosf * scale, sinf * scale, cosf, sinf)
    lb_cum = jnp.cumsum(jax.nn.softmax(lb_logits.astype(F32), axis=0), axis=0)
    lower = lb_cum - lb_cum[0:1]
    log_lb = jnp.log(lower)
    log_1m = jnp.log1p(-lower)
    ada = _ada_all(c, w_ada, b_ada)

    w_in_t = jnp.swapaxes(w_in, 1, 2)
    hg_roles = (("id",) * (TAIL_TILE // LANES),) * (4 * HG_WIDTH // TAIL_TILE)
    mg_roles = (("sg",) * (TAIL_TILE // LANES),) * (2 * D // TAIL_TILE)

    wa_b, wh_b, wo_b = w_br_attn.astype(BF16), w_br_hgrn.astype(BF16), w_out.astype(BF16)
    wup_b, wdown_b = w_up.astype(BF16), w_down.astype(BF16)

    mods = [[ada[l, :, j * D:(j + 1) * D] for j in range(N_ADA)] for l in range(L)]
    h = _prenorm(xs, g_pre_mix[0], mods[0][1], mods[0][0])
    for l in range(L):
        sh1, sc1, gt1, sh2, sc2, gt2 = mods[l]

        a = _proj(h, w_in_t, l, 0, A_ROLES, BF16, tn=A_TILE, tables=tables, name="proj_attn")
        hz = _proj(h, w_in_t, l, A_COLS, hg_roles, F32, tn=TAIL_TILE, name="proj_hgrn")
        mg = _proj(h, w_in_t, l, A_COLS + 4 * HG_WIDTH, mg_roles, BF16, tn=TAIL_TILE, name="proj_merge_gate")

        cmp_kv = _compress(a, jnp.stack([pe_kc[l], pe_vc[l]]), jnp.stack([w_kc[l], w_vc[l]]))
        attn = _nsa(a, cmp_kv)
        hg = _hgrn(hz, log_lb[l:l + 1], log_1m[l:l + 1], g_hg_norm[l:l + 1])
        xs = _merge(attn, hg, mg, wa_b, wh_b, wo_b, l, xs, gt1, g_post_mix[l])

        nxt = (g_pre_mix[l + 1], mods[l + 1][1], mods[l + 1][0]) if l + 1 < L else None
        out = _ffn(xs, g_pre_ffn[l], sc2, sh2, wup_b, conv_w[l], conv_b[l], wdown_b, l, gt2, g_post_ffn[l], nxt)
        xs, h = out if nxt is not None else (out, None)
    return xs[None]
```

```python
import functools

import jax
import jax.numpy as jnp
import numpy as np
from jax import lax
from jax.experimental import pallas as pl
from jax.experimental.pallas import tpu as pltpu

F32 = jnp.float32
BF16 = jnp.bfloat16

A_HEADS = 8
A_KV_GROUPS = 2
A_HPG = A_HEADS // A_KV_GROUPS
HEAD_DIM = 128
A_WIDTH = A_HEADS * HEAD_DIM
A_KV_WIDTH = A_KV_GROUPS * HEAD_DIM
CMP_BLOCK = 32
CMP_STRIDE = 16
SEL_BLOCK = 64
SEL_TOPK = 16
WINDOW = 512
ROPE_THETA = 10000.0
HG_HEADS = 8
HG_DIM = 128
HG_WIDTH = HG_HEADS * HG_DIM
HG_CHUNK = 64
CONV_WIDTH = 3
NORM_EPS = 1e-6
N_ADA = 6
A_COLS = A_WIDTH + 6 * A_KV_WIDTH + 3 * A_HEADS

LANES = 128
SUBLANES = 8
SEL_SLOTS = LANES
KEY_TILE = 256
NEG_BIG = -1e30
VMEM_LIMIT = 56 * 1024 * 1024

ADA_COLS = 1024
NORM_ROWS = 512
PROJ_ROWS = 1024
TAIL_TILE = 1024
HG_ROWS = 512
MERGE_ROWS = 256
FFN_ROWS = 512
FFN_COLS = 512


def _cparams(sem):
    return pltpu.CompilerParams(dimension_semantics=sem, vmem_limit_bytes=VMEM_LIMIT)


def _split3(a):
    hi = a.astype(BF16)
    r1 = a - hi.astype(F32)
    mid = r1.astype(BF16)
    lo = (r1 - mid.astype(F32)).astype(BF16)
    return hi, mid, lo


def _dot(a, b):
    return jnp.dot(a, b, preferred_element_type=F32)


def _dot_nt(a, b):
    return lax.dot_general(a, b, (((1,), (1,)), ((), ())), preferred_element_type=F32)


def _dot_tn(a, b):
    return lax.dot_general(a, b, (((0,), (0,)), ((), ())), preferred_element_type=F32)


def _ada_kernel(c_ref, w_ref, b_ref, o_ref):
    c = c_ref[...]
    ca = c * jax.nn.sigmoid(c)
    o_ref[0] = jnp.sum(ca * w_ref[0], axis=0, keepdims=True) + b_ref[0]


def _ada_all(c, w_ada, b_ada):
    L, D, N = w_ada.shape
    tn = ADA_COLS
    return pl.pallas_call(
        _ada_kernel,
        grid=(L, N // tn),
        in_specs=[
            pl.BlockSpec((D, 1), lambda l, n: (0, 0)),
            pl.BlockSpec((1, D, tn), lambda l, n: (l, 0, n)),
            pl.BlockSpec((1, 1, tn), lambda l, n: (l, 0, n)),
        ],
        out_specs=pl.BlockSpec((1, 1, tn), lambda l, n: (l, 0, n)),
        out_shape=jax.ShapeDtypeStruct((L, 1, N), F32),
        compiler_params=_cparams(("arbitrary", "arbitrary")),
        name="ada",
    )(c.reshape(D, 1), w_ada, b_ada.reshape(L, 1, N))


def _prenorm_kernel(x_ref, g_ref, sc_ref, sh_ref, o_ref):
    x = x_ref[...]
    ms = jnp.mean(x * x, axis=-1, keepdims=True)
    y = x * lax.rsqrt(ms + NORM_EPS) * g_ref[...]
    o_ref[...] = (y * (1.0 + sc_ref[...]) + sh_ref[...]).astype(o_ref.dtype)


def _prenorm(x, g, sc, sh):
    S, D = x.shape
    tm = NORM_ROWS
    row = pl.BlockSpec((1, D), lambda m: (0, 0))
    return pl.pallas_call(
        _prenorm_kernel,
        grid=(S // tm,),
        in_specs=[pl.BlockSpec((tm, D), lambda m: (m, 0)), row, row, row],
        out_specs=pl.BlockSpec((tm, D), lambda m: (m, 0)),
        out_shape=jax.ShapeDtypeStruct((S, D), BF16),
        compiler_params=_cparams(("arbitrary",)),
        name="prenorm",
    )(x, g.reshape(1, D), sc, sh)


def _rope(t, cosf, sinf):
    return t * cosf + pltpu.roll(t, HEAD_DIM // 2, axis=1) * sinf


_PROJ_PART = 256


def _proj_kernel(h_ref, w_ref, *rest, roles, tn):
    o_ref, w_scr = rest[-2:]
    tabs = rest[:-2]
    n = pl.program_id(0)

    @pl.when(pl.program_id(1) == 0)
    def _():
        w_scr[...] = w_ref[0].astype(BF16)

    def emit(slabs):
        h = h_ref[...]
        per = _PROJ_PART // LANES
        parts = [_dot_nt(h, w_scr[c0:c0 + _PROJ_PART, :]) for c0 in range(0, tn, _PROJ_PART)]
        for pi, acc in enumerate(parts):
            for s in range(per):
                kind = slabs[pi * per + s]
                t = acc[:, s * LANES:(s + 1) * LANES]
                if kind == "sg":
                    t = jax.nn.sigmoid(t)
                elif kind != "id":
                    cos_ref, sin_ref = tabs[0:2] if kind == "rq" else tabs[-2:]
                    t = _rope(t, cos_ref[...], sin_ref[...])
                c = (pi * per + s) * LANES
                o_ref[:, c:c + LANES] = t.astype(o_ref.dtype)

    kinds = sorted(set(roles))
    if len(kinds) == 1:
        emit(kinds[0])
        return
    for kind in kinds:
        hit = functools.reduce(jnp.logical_or, [n == t for t, r in enumerate(roles) if r == kind])
        pl.when(hit)(functools.partial(emit, kind))


def _proj(h, w_t, layer, row0, roles, out_dtype, *, tn, tables=(), name="proj"):
    S, D = h.shape
    tm = PROJ_ROWS if S % PROJ_ROWS == 0 else S
    nt = len(roles)
    tab = pl.BlockSpec((tm, HEAD_DIM), lambda n, m: (m, 0))
    return pl.pallas_call(
        functools.partial(_proj_kernel, roles=tuple(roles), tn=tn),
        grid=(nt, S // tm),
        in_specs=[pl.BlockSpec((tm, D), lambda n, m: (m, 0)),
                  pl.BlockSpec((pl.Element(1), pl.Element(tn), pl.Element(D)),
                               lambda n, m: (layer, pl.multiple_of(row0 + n * tn, SUBLANES), 0))]
                 + [tab] * len(tables),
        out_specs=pl.BlockSpec((tm, tn), lambda n, m: (m, n)),
        out_shape=jax.ShapeDtypeStruct((S, nt * tn), out_dtype),
        scratch_shapes=[pltpu.VMEM((tn, D), BF16)],
        compiler_params=_cparams(("arbitrary", "arbitrary")),
        name=name,
    )(h, w_t, *tables)


A_TILE = 1024
_A_SLABS = (("rq",) * (A_WIDTH // LANES)
            + (("rk",) * A_KV_GROUPS + ("id",) * A_KV_GROUPS) * 3
            + ("sg",) * (-(A_WIDTH // LANES + 6 * A_KV_GROUPS) % (A_TILE // LANES)))
A_ROLES = tuple(_A_SLABS[i:i + A_TILE // LANES] for i in range(0, len(_A_SLABS), A_TILE // LANES))
_KV_BLOCK0 = A_WIDTH // LANES
BLK_KC, BLK_VC, BLK_KS, BLK_VS, BLK_KW, BLK_VW = [_KV_BLOCK0 + j * A_KV_GROUPS for j in range(6)]
BLK_GATE = _KV_BLOCK0 + 6 * A_KV_GROUPS


def _compress_kernel(t_ref, pe_ref, w_ref, o_ref, t32_ref, *, n_chunk):
    half = CMP_BLOCK // 2
    t32_ref[...] = t_ref[...].astype(F32)
    a = jnp.concatenate([t32_ref[pl.ds(j, n_chunk, stride=CMP_STRIDE), :] for j in range(half)],
                        axis=1).astype(BF16)
    kw = half * HEAD_DIM
    w = w_ref[0].astype(BF16)
    y_top = _dot(a, w[:kw])
    y_bot = _dot(a, w[kw:])
    pe = jnp.concatenate([pe_ref[0, j:j + 1, :] for j in range(CMP_BLOCK)], axis=1)
    pe_term = jnp.zeros((1, HEAD_DIM), F32)
    for piece in _split3(pe):
        pe_term = pe_term + _dot(piece, w)
    o_ref[0, 0] = y_top + pltpu.roll(y_bot, n_chunk - 1, axis=0) + pe_term


def _compress(a, pe2, w2):
    S = a.shape[0]
    n_chunk = S // CMP_STRIDE
    G = A_KV_GROUPS
    return pl.pallas_call(
        functools.partial(_compress_kernel, n_chunk=n_chunk),
        grid=(2, G),
        in_specs=[
            pl.BlockSpec((S, HEAD_DIM), lambda kv, g: (0, BLK_KC + kv * G + g)),
            pl.BlockSpec((1, CMP_BLOCK, HEAD_DIM), lambda kv, g: (kv, 0, 0)),
            pl.BlockSpec((1, CMP_BLOCK * HEAD_DIM, HEAD_DIM), lambda kv, g: (kv, 0, 0)),
        ],
        out_specs=pl.BlockSpec((1, 1, n_chunk, HEAD_DIM), lambda kv, g: (kv, g, 0, 0)),
        out_shape=jax.ShapeDtypeStruct((2, G, n_chunk, HEAD_DIM), F32),
        scratch_shapes=[pltpu.VMEM((S, HEAD_DIM), F32)],
        compiler_params=_cparams(("arbitrary", "arbitrary")),
        name="compress",
    )(a, pe2, w2)


NSA_QB = 256
SEL_BODY = 8
SEL_GROUP = 4
_N_FORCED = 3


def _nsa_kernel(q_ref, ks_ref, kw_ref, vs_ref, vw_ref, kc_ref, vc_ref, ovt_ref, gate_ref, *rest, seq, n_cast):
    cast_in, rest = rest[:n_cast], rest[n_cast:]
    o_ref, rest = rest[0], rest[1:]
    cast_out, rest = rest[:n_cast], rest[n_cast:]
    kaug_ref, vt_ref, vwt_ref, vct_ref, acc_ref, m_ref, l_ref = rest
    for src, dst in zip(cast_in, cast_out):
        dst[...] = src[0].astype(dst.dtype)
    grp = pl.program_id(0)
    i = pl.program_id(1)
    t0 = i * NSA_QB
    n_cmp = seq // CMP_STRIDE
    cols = A_HPG * NSA_QB

    @pl.when(i == 0)
    def _():
        def build(r, c):
            r0 = pl.multiple_of(r * KEY_TILE, KEY_TILE)
            kaug_ref[r, :, 0:HEAD_DIM] = ks_ref[pl.ds(r0, KEY_TILE), :]
            blk = (r0 + lax.broadcasted_iota(jnp.int32, (KEY_TILE, SEL_SLOTS), 0)) // SEL_BLOCK
            slot = lax.broadcasted_iota(jnp.int32, (KEY_TILE, SEL_SLOTS), 1)
            kaug_ref[r, :, HEAD_DIM:HEAD_DIM + SEL_SLOTS] = jnp.where(blk == slot, 1.0, 0.0).astype(BF16)
            vt_ref[r] = vs_ref[pl.ds(r0, KEY_TILE), :].astype(F32).T.astype(BF16)
            vwt_ref[r] = vw_ref[pl.ds(r0, KEY_TILE), :].astype(F32).T.astype(BF16)
            return c
        lax.fori_loop(0, seq // KEY_TILE, build, 0)
        vct_ref[...] = vc_ref[0, 0].T.astype(BF16)

    qt = jnp.concatenate(
        [q_ref[:, h * HEAD_DIM:(h + 1) * HEAD_DIM].astype(F32).T.astype(BF16) for h in range(A_HPG)],
        axis=1)
    tq = t0 + lax.broadcasted_iota(jnp.int32, (1, cols), 1) % NSA_QB

    def softmax0(s):
        m = jnp.max(s, axis=0, keepdims=True)
        p = jnp.exp2(s - m)
        return p, jnp.sum(p, axis=0, keepdims=True)

    s_c = _dot(kc_ref[0, 0].astype(BF16), qt)
    nrow = lax.broadcasted_iota(jnp.int32, (n_cmp, 1), 0)
    vis = (nrow * CMP_STRIDE + (CMP_BLOCK - 1)) <= tq
    p_c, den = softmax0(jnp.where(vis, s_c, NEG_BIG))
    p_c = p_c * jnp.where(tq >= CMP_BLOCK - 1, 1.0 / den, 0.0)
    o_ct = _dot(vct_ref[...], p_c.astype(BF16))

    psum = p_c[:, 0:NSA_QB]
    for h in range(1, A_HPG):
        psum = psum + p_c[:, h * NSA_QB:(h + 1) * NSA_QB]
    ovt = ovt_ref[...]
    imp_t = jnp.zeros((SEL_SLOTS, NSA_QB), F32)
    for piece in _split3(psum):
        imp_t = imp_t + _dot(ovt, piece)
    slot_i = lax.broadcasted_iota(jnp.int32, (SEL_SLOTS, NSA_QB), 0)
    tq_l = t0 + lax.broadcasted_iota(jnp.int32, (SEL_SLOTS, NSA_QB), 1)
    cur = tq_l // SEL_BLOCK
    forced = jnp.where(slot_i == 0, 1, 0) + jnp.where(slot_i == cur, 1, 0) + jnp.where(slot_i == cur - 1, 1, 0)
    valid = slot_i * SEL_BLOCK <= tq_l
    key = jnp.where(forced > 0, -3e38, jnp.where(valid, imp_t, NEG_BIG))
    slot_f = slot_i.astype(F32)
    bias_t = jnp.where(forced > 0, 0.0, NEG_BIG)
    for _ in range(SEL_TOPK - _N_FORCED):
        mx = jnp.max(key, axis=0, keepdims=True)
        first = jnp.min(jnp.where(key == mx, slot_f, 1e9), axis=0, keepdims=True)
        pick = slot_f == first
        bias_t = jnp.where(pick, 0.0, bias_t)
        key = jnp.where(pick, -3e38, key)

    n_wt = WINDOW // KEY_TILE + 1
    assert WINDOW % KEY_TILE == 0 and n_wt == 3
    kk = lax.broadcasted_iota(jnp.int32, (KEY_TILE, 1), 0)
    qq = tq - t0
    tri = kk <= qq
    tri_far = kk <= jnp.where(i >= 2, qq, KEY_TILE)
    w_idx = [jnp.maximum(i - 2, 0), jnp.maximum(i - 1, 0), i]
    s_w = [_dot(kw_ref[pl.ds(pl.multiple_of(w * KEY_TILE, KEY_TILE), KEY_TILE), :], qt) for w in w_idx]
    s_w = jnp.concatenate([jnp.where(tri_far, NEG_BIG, s_w[0]),
                           jnp.where(i >= 1, s_w[1], NEG_BIG),
                           jnp.where(tri, s_w[2], NEG_BIG)], axis=0)
    p_w, l_w = softmax0(s_w)
    p_w = p_w.astype(BF16)
    o_wt = _dot(vwt_ref[w_idx[0]], p_w[0:KEY_TILE])
    for k in range(1, n_wt):
        o_wt = o_wt + _dot(vwt_ref[w_idx[k]], p_w[k * KEY_TILE:(k + 1) * KEY_TILE])
    o_wt = o_wt * (1.0 / l_w)

    qt_aug = jnp.concatenate([qt, jnp.concatenate([bias_t.astype(BF16)] * A_HPG, axis=1)], axis=0)
    acc_ref[...] = jnp.zeros_like(acc_ref)
    m_ref[...] = jnp.full(m_ref.shape, 0.5 * NEG_BIG, F32)
    l_ref[...] = jnp.zeros_like(l_ref)

    def sel_body(base, causal):
        n = len(causal)
        m, l = m_ref[...], l_ref[...]
        gsz = min(SEL_GROUP, max(n // 2, 1))
        groups = [list(range(g0, min(g0 + gsz, n))) for g0 in range(0, n, gsz)]

        def scores(ks):
            out = {}
            for k in ks:
                sk = _dot(kaug_ref[base + k], qt_aug)
                if causal[k]:
                    kpos = (base + k) * KEY_TILE + lax.broadcasted_iota(jnp.int32, (KEY_TILE, 1), 0)
                    sk = jnp.where(kpos <= tq, sk, NEG_BIG)
                out[k] = sk
            return out

        s = scores(groups[0])
        for gi, ks in enumerate(groups):
            if gi + 1 < len(groups):
                s.update(scores(groups[gi + 1]))
            m_new = m
            for k in ks:
                m_new = jnp.maximum(m_new, jnp.max(s[k], axis=0, keepdims=True))
            alpha = jnp.exp2(m - m_new)
            l = alpha * l
            pv = None
            for k in ks:
                p = jnp.exp2(s.pop(k) - m_new)
                l = l + jnp.sum(p, axis=0, keepdims=True)
                d = _dot(vt_ref[base + k], p.astype(BF16))
                pv = d if pv is None else pv + d
            acc_ref[...] = alpha * acc_ref[...] + pv
            m = m_new
        m_ref[...] = m
        l_ref[...] = l

    n_bulk = i // SEL_BODY
    rem = i % SEL_BODY

    def bulk(b, c):
        sel_body(b * SEL_BODY, (False,) * SEL_BODY)
        return c
    lax.fori_loop(0, n_bulk, bulk, 0)

    base = n_bulk * SEL_BODY
    part = SEL_BODY // 2
    while part >= 2:
        pl.when(rem & part != 0)(functools.partial(sel_body, base, (False,) * part))
        base = base + (rem & part)
        part //= 2

    @pl.when(rem % 2 == 1)
    def _():
        sel_body(i - 1, (False, True))

    @pl.when(rem % 2 == 0)
    def _():
        sel_body(i, (True,))

    o_st = acc_ref[...] * (1.0 / l_ref[...])

    gates_t = gate_ref[...].astype(F32).T
    per_group = 3 * A_HPG

    def gate(h, b):
        r = 3 * h + b
        return jnp.where(grp == 0, gates_t[r:r + 1], gates_t[per_group + r:per_group + r + 1])

    for h in range(A_HPG):
        c = slice(h * NSA_QB, (h + 1) * NSA_QB)
        o = gate(h, 0) * o_ct[:, c] + gate(h, 1) * o_st[:, c] + gate(h, 2) * o_wt[:, c]
        o_ref[:, h * HEAD_DIM:(h + 1) * HEAD_DIM] = o.T.astype(o_ref.dtype)


def _overlap_matrix(seq):
    n_chunk = seq // CMP_STRIDE
    cs = np.arange(n_chunk)[:, None] * CMP_STRIDE
    ss = np.arange(SEL_SLOTS)[None, :] * SEL_BLOCK
    ov = np.maximum(np.minimum(cs + CMP_BLOCK, ss + SEL_BLOCK) - np.maximum(cs, ss), 0) / CMP_BLOCK
    n_cmp = (seq - CMP_BLOCK) // CMP_STRIDE + 1
    ov[n_cmp:] = 0.0
    ov[:, seq // SEL_BLOCK:] = 0.0
    return ov.astype(np.float32)


def _nsa(a, cmp_kv, cast=(), layer=0):
    S = a.shape[0]
    G = A_KV_GROUPS
    steps = S // NSA_QB
    for w in cast:
        assert w.shape[1] % (steps * 2 * SUBLANES) == 0
    slab = lambda g, i: jnp.where(g == 0, i, steps - 1)
    cast_in = [pl.BlockSpec((1, w.shape[1] // steps, w.shape[2]), lambda g, i: (layer, slab(g, i), 0))
               for w in cast]
    cast_out = [pl.BlockSpec((w.shape[1] // steps, w.shape[2]), lambda g, i: (slab(g, i), 0)) for w in cast]
    cast_shape = [jax.ShapeDtypeStruct(w.shape[1:], BF16) for w in cast]
    n_tiles = S // KEY_TILE
    assert S // SEL_BLOCK <= SEL_SLOTS and S % KEY_TILE == 0 and S >= WINDOW + NSA_QB
    assert NSA_QB == KEY_TILE and G == 2
    n_chunk = S // CMP_STRIDE
    ovt = jnp.asarray(_overlap_matrix(S).T, BF16)
    col = lambda b: pl.BlockSpec((S, HEAD_DIM), lambda g, i, b=b: (0, b + g))
    out = pl.pallas_call(
        functools.partial(_nsa_kernel, seq=S, n_cast=len(cast)),
        grid=(G, steps),
        in_specs=[
            pl.BlockSpec((NSA_QB, A_HPG * HEAD_DIM), lambda g, i: (i, g)),
            col(BLK_KS), col(BLK_KW), col(BLK_VS), col(BLK_VW),
            pl.BlockSpec((1, 1, n_chunk, HEAD_DIM), lambda g, i: (0, g, 0, 0)),
            pl.BlockSpec((1, 1, n_chunk, HEAD_DIM), lambda g, i: (1, g, 0, 0)),
            pl.BlockSpec((SEL_SLOTS, n_chunk), lambda g, i: (0, 0)),
            pl.BlockSpec((NSA_QB, LANES), lambda g, i: (i, BLK_GATE)),
        ] + cast_in,
        out_specs=[pl.BlockSpec((NSA_QB, A_HPG * HEAD_DIM), lambda g, i: (i, g))] + cast_out,
        out_shape=[jax.ShapeDtypeStruct((S, A_WIDTH), BF16)] + cast_shape,
        scratch_shapes=[pltpu.VMEM((n_tiles, KEY_TILE, HEAD_DIM + SEL_SLOTS), BF16),
                        pltpu.VMEM((n_tiles, HEAD_DIM, KEY_TILE), BF16),
                        pltpu.VMEM((n_tiles, HEAD_DIM, KEY_TILE), BF16),
                        pltpu.VMEM((HEAD_DIM, n_chunk), BF16),
                        pltpu.VMEM((HEAD_DIM, A_HPG * NSA_QB), F32),
                        pltpu.VMEM((1, A_HPG * NSA_QB), F32),
                        pltpu.VMEM((1, A_HPG * NSA_QB), F32)],
        compiler_params=_cparams(("arbitrary", "arbitrary")),
        name="nsa",
    )(a, a, a, a, a, cmp_kv, cmp_kv, ovt, a, *cast)
    return out[0] if not cast else out


_ROW_MASKS = 18
_SCORE_MASKS = 10
HG_HEADS_PER_STEP = 8


def _hgrn_tables():
    t = np.arange(HG_CHUNK)
    p4, i4, i16 = t % 4, (t // 4) % 4, t // 16
    conds = [p4 >= 1, p4 >= 2, p4 >= 3, p4 <= 2, p4 <= 1, p4 == 0,
             i4 >= 1, i4 >= 2, i4 >= 3, i4 <= 2, i4 <= 1, i4 == 0,
             i16 >= 1, i16 >= 2, i16 >= 3, i16 <= 2, i16 <= 1, i16 == 0]
    rm = np.stack([np.broadcast_to(c[:, None], (HG_CHUNK, HG_DIM)) for c in conds]).astype(np.float32)
    tt, ss = t[:, None], t[None, :]
    lag16 = tt // 16 - ss // 16
    lag4 = tt // 4 - ss // 4
    same16, same4 = lag16 == 0, lag4 == 0
    sm = [lag16 == 1, lag16 == 2, lag16 == 3,
          same16 & (lag4 == 1), same16 & (lag4 == 2), same16 & (lag4 == 3),
          same4 & (tt - ss == 0), same4 & (tt - ss == 1), same4 & (tt - ss == 2), same4 & (tt - ss == 3)]
    return rm, np.stack(sm).astype(np.float32)


def _hgrn_kernel(q_ref, f_ref, i_ref, og_ref, loglb_ref, log1m_ref, gn_ref, rm_ref, sm_ref,
                 o_ref, st_ref, g_scr, k_scr, o_scr, *, tb, nh):
    C = HG_CHUNK
    heads = [slice(h * HG_DIM, (h + 1) * HG_DIM) for h in range(nh)]

    @pl.when(pl.program_id(1) == 0)
    def _():
        st_ref[...] = jnp.zeros_like(st_ref)

    x = f_ref[...]
    log_sig = jnp.minimum(x, 0.0) - jnp.log(1.0 + jnp.exp(-jnp.abs(x)))
    a = loglb_ref[...]
    c = log1m_ref[...] + log_sig
    log_f = jnp.maximum(a, c) + jnp.log(1.0 + jnp.exp(-jnp.abs(a - c)))
    g_scr[...] = log_f
    k_scr[...] = jnp.exp(log1m_ref[...] + (log_sig - x))

    def rmask(n):
        return jnp.concatenate([rm_ref[n]] * nh, axis=1)

    def roll(v, s):
        return pltpu.roll(v, s % C, axis=0)

    def b16(z):
        return z.astype(BF16)

    def chunk(ci, carry):
        r0 = pl.multiple_of(ci * C, C)
        g = g_scr[pl.ds(r0, C), :]
        q = q_ref[pl.ds(r0, C), :]
        k = k_scr[pl.ds(r0, C), :]
        v = i_ref[pl.ds(r0, C), :].astype(BF16)
        pre4 = g + rmask(0) * roll(g, 1) + rmask(1) * roll(g, 2) + rmask(2) * roll(g, 3)
        suf4 = rmask(3) * roll(g, -1) + rmask(4) * roll(g, -2) + rmask(5) * roll(g, -3)
        tot4 = pre4 + suf4
        a1, a2, a3 = roll(tot4, 4), roll(tot4, 8), roll(tot4, 12)
        pre16 = pre4 + rmask(6) * a1 + rmask(7) * a2 + rmask(8) * a3
        suf16 = suf4 + rmask(9) * roll(tot4, -4) + rmask(10) * roll(tot4, -8) + rmask(11) * roll(tot4, -12)
        tot16 = pre16 + suf16
        d1, d2, d3 = roll(tot16, 16), roll(tot16, 32), roll(tot16, 48)
        pre64 = pre16 + rmask(12) * d1 + rmask(13) * d2 + rmask(14) * d3
        suf64 = suf16 + rmask(15) * roll(tot16, -16) + rmask(16) * roll(tot16, -32) + rmask(17) * roll(tot16, -48)
        tot64 = pre64 + suf64
        g1, g2 = roll(g, 1), roll(g, 2)

        q_a = q * jnp.exp(pre16)
        lhs_a = [b16(q_a), b16(q_a * jnp.exp(d1)), b16(q_a * jnp.exp(d1 + d2))]
        k_a = b16(k * jnp.exp(suf16))
        q_b = q * jnp.exp(pre4)
        lhs_b = [b16(q_b), b16(q_b * jnp.exp(a1)), b16(q_b * jnp.exp(a1 + a2))]
        k_b = b16(k * jnp.exp(suf4))
        q_c = q * jnp.exp(g)
        lhs_c = [b16(q), b16(q_c), b16(q_c * jnp.exp(g1)), b16(q_c * jnp.exp(g1 + g2))]
        k_c = b16(k)
        q_s = b16(q * jnp.exp(pre64))
        k_s = b16(k * jnp.exp(suf64))
        decay = jnp.exp(tot64[0:1, :])

        def level(lhs, rhs):
            return [_dot_nt(jnp.concatenate([z[:, hh] for z in lhs], axis=0), rhs[:, hh]) for hh in heads]

        s_a, s_b, s_c = level(lhs_a, k_a), level(lhs_b, k_b), level(lhs_c, k_c)
        o_inter = [_dot_nt(q_s[:, hh], b16(st_ref[h])) for h, hh in enumerate(heads)]
        kv = [_dot_tn(v[:, hh], k_s[:, hh]) for hh in heads]
        for h, hh in enumerate(heads):
            sa, sb, sc = s_a[h], s_b[h], s_c[h]
            scores = (sm_ref[0] * sa[0:C] + sm_ref[1] * sa[C:2 * C] + sm_ref[2] * sa[2 * C:3 * C]
                      + sm_ref[3] * sb[0:C] + sm_ref[4] * sb[C:2 * C] + sm_ref[5] * sb[2 * C:3 * C]
                      + sm_ref[6] * sc[0:C] + sm_ref[7] * sc[C:2 * C] + sm_ref[8] * sc[2 * C:3 * C]
                      + sm_ref[9] * sc[3 * C:4 * C])
            o_scr[pl.ds(r0, C), hh] = _dot(b16(scores), v[:, hh]) + o_inter[h]
            st_ref[h] = st_ref[h] * decay[:, hh] + kv[h]
        return carry

    lax.fori_loop(0, tb // C, chunk, 0)

    for hh in heads:
        o = o_scr[:, hh]
        ms = jnp.mean(o * o, axis=-1, keepdims=True)
        og = og_ref[:, hh]
        y = o * lax.rsqrt(ms + NORM_EPS) * gn_ref[:, hh] * (og * jax.nn.sigmoid(og))
        o_ref[:, hh] = y.astype(o_ref.dtype)


def _hgrn(hz, loglb, log1mlb, gnorm):
    S = hz.shape[0]
    tb = HG_ROWS if S % HG_ROWS == 0 else S
    nh = HG_HEADS_PER_STEP
    ng = HG_HEADS // nh
    w = nh * HG_DIM
    rm, sm = _hgrn_tables()
    blk = lambda part: pl.BlockSpec((tb, w), lambda h, c, part=part: (c, part * ng + h))
    row = pl.BlockSpec((1, w), lambda h, c: (0, h))
    return pl.pallas_call(
        functools.partial(_hgrn_kernel, tb=tb, nh=nh),
        grid=(ng, S // tb),
        in_specs=[blk(0), blk(1), blk(2), blk(3), row, row, row,
                  pl.BlockSpec((_ROW_MASKS, HG_CHUNK, HG_DIM), lambda h, c: (0, 0, 0)),
                  pl.BlockSpec((_SCORE_MASKS, HG_CHUNK, HG_CHUNK), lambda h, c: (0, 0, 0))],
        out_specs=pl.BlockSpec((tb, w), lambda h, c: (c, h)),
        out_shape=jax.ShapeDtypeStruct((S, HG_WIDTH), BF16),
        scratch_shapes=[pltpu.VMEM((nh, HG_DIM, HG_DIM), F32), pltpu.VMEM((tb, w), F32),
                        pltpu.VMEM((tb, w), F32), pltpu.VMEM((tb, w), F32)],
        compiler_params=_cparams(("arbitrary", "arbitrary")),
        name="hgrn",
    )(hz, hz, hz, hz, loglb, log1mlb, gnorm, jnp.asarray(rm), jnp.asarray(sm))


_MERGE_PART = 512


def _merge_kernel(at_ref, hg_ref, ga_ref, gh_ref, wa_ref, wh_ref, wo_ref, x_ref, gt_ref, gp_ref, o_ref):
    at, hg = at_ref[...], hg_ref[...]
    d = wo_ref.shape[1]
    parts = [slice(c0, c0 + _MERGE_PART) for c0 in range(0, d, _MERGE_PART)]
    branch = [(_dot(at, wa_ref[0, :, cs]), _dot(hg, wh_ref[0, :, cs])) for cs in parts]
    y = None
    for cs, (ma, mh) in zip(parts, branch):
        mixed = ga_ref[:, cs].astype(F32) * ma + gh_ref[:, cs].astype(F32) * mh
        yp = _dot(mixed.astype(BF16), wo_ref[0, cs, :])
        y = yp if y is None else y + yp
    ms = jnp.mean(y * y, axis=-1, keepdims=True)
    yn = y * lax.rsqrt(ms + NORM_EPS) * gp_ref[...]
    o_ref[...] = x_ref[...] + gt_ref[...] * yn


def _merge(attn, hg, mg, wa, wh, wo, layer, x, gt, gpost):
    S, D = x.shape
    tm = MERGE_ROWS
    const = lambda shape: pl.BlockSpec((1,) + shape, lambda m: (layer, 0, 0), pipeline_mode=pl.Buffered(1))
    row = pl.BlockSpec((1, D), lambda m: (0, 0))
    return pl.pallas_call(
        _merge_kernel,
        grid=(S // tm,),
        in_specs=[
            pl.BlockSpec((tm, A_WIDTH), lambda m: (m, 0)),
            pl.BlockSpec((tm, HG_WIDTH), lambda m: (m, 0)),
            pl.BlockSpec((tm, D), lambda m: (m, 0)),
            pl.BlockSpec((tm, D), lambda m: (m, 1)),
            const((A_WIDTH, D)), const((HG_WIDTH, D)), const((D, D)),
            pl.BlockSpec((tm, D), lambda m: (m, 0)),
            row, row,
        ],
        out_specs=pl.BlockSpec((tm, D), lambda m: (m, 0)),
        out_shape=jax.ShapeDtypeStruct((S, D), F32),
        compiler_params=_cparams(("arbitrary",)),
        name="merge",
    )(attn, hg, mg, mg, wa, wh, wo, x, gt, gpost.reshape(1, D))


_HALO = 16
_FF_SUB = 256


def _norm_mod(x, g_ref, sc_ref, sh_ref):
    ms = jnp.mean(x * x, axis=-1, keepdims=True)
    y = x * lax.rsqrt(ms + NORM_EPS) * g_ref[...]
    return (y * (1.0 + sc_ref[...]) + sh_ref[...]).astype(BF16)


def _ffn_kernel(x_ref, xh_ref, g_ref, sc_ref, sh_ref, wg_ref, wv_ref, cwg_ref, cwv_ref, cbg_ref, cbv_ref,
                wd_ref, gt_ref, gp_ref, *rest, tm, emit_next):
    if emit_next:
        gn_ref, scn_ref, shn_ref, o_ref, hn_ref, h_scr, ug_scr, uv_scr = rest
    else:
        o_ref, h_scr, ug_scr, uv_scr = rest
    m = pl.program_id(0)
    f = pl.program_id(1)
    norm_mod = functools.partial(_norm_mod, g_ref=g_ref, sc_ref=sc_ref, sh_ref=sh_ref)

    @pl.when(f == 0)
    def _():
        h_scr[0:_HALO, :] = jnp.where(m > 0, norm_mod(xh_ref[...]), jnp.zeros((), BF16))
        h_scr[_HALO:_HALO + tm, :] = norm_mod(x_ref[...])

    tf = wg_ref.shape[2]
    subs = [slice(s * _FF_SUB, (s + 1) * _FF_SUB) for s in range(tf // _FF_SUB)]
    h = h_scr[...]
    ups = [(_dot(h, wg_ref[0, :, cs]), _dot(h, wv_ref[0, :, cs])) for cs in subs]

    def conv(u_scr, cw_ref, cb_ref, cs):
        out = cb_ref[:, cs] + cw_ref[0:1, cs] * u_scr[pl.ds(_HALO - 2, tm), cs]
        out = out + cw_ref[1:2, cs] * u_scr[pl.ds(_HALO - 1, tm), cs]
        return out + cw_ref[2:3, cs] * u_scr[pl.ds(_HALO, tm), cs]

    acc = jnp.where(f > 0, o_ref[...], 0.0)
    for cs, (ug, uv) in zip(subs, ups):
        ug_scr[:, cs] = ug
        uv_scr[:, cs] = uv
        cg = conv(ug_scr, cwg_ref, cbg_ref, cs)
        cv = conv(uv_scr, cwv_ref, cbv_ref, cs)
        act = (cg * jax.nn.sigmoid(cg) * cv).astype(BF16)
        acc = acc + _dot(act, wd_ref[0, cs, :])
    o_ref[...] = acc

    @pl.when(f == pl.num_programs(1) - 1)
    def _():
        y = o_ref[...]
        ms = jnp.mean(y * y, axis=-1, keepdims=True)
        yn = y * lax.rsqrt(ms + NORM_EPS) * gp_ref[...]
        x_new = x_ref[...] + gt_ref[...] * yn
        o_ref[...] = x_new
        if emit_next:
            hn_ref[...] = _norm_mod(x_new, gn_ref, scn_ref, shn_ref)


def _ffn(x, gpre, sc, sh, wup, convw, convb, wdown, layer, gt, gpost, nxt=None):
    S, D = x.shape
    F = wdown.shape[1]
    tm = FFN_ROWS if S % FFN_ROWS == 0 else S
    tf = FFN_COLS
    nf = F // tf
    hb = tm // _HALO
    row = pl.BlockSpec((1, D), lambda m, f: (0, 0))
    tile = pl.BlockSpec((tm, D), lambda m, f: (m, 0))
    extra = () if nxt is None else (nxt[0].reshape(1, D), nxt[1], nxt[2])
    return pl.pallas_call(
        functools.partial(_ffn_kernel, tm=tm, emit_next=nxt is not None),
        grid=(S // tm, nf),
        in_specs=[
            pl.BlockSpec((tm, D), lambda m, f: (m, 0)),
            pl.BlockSpec((_HALO, D), lambda m, f: (jnp.maximum(m * hb - 1, 0), 0)),
            row, row, row,
            pl.BlockSpec((1, D, tf), lambda m, f: (layer, 0, f)),
            pl.BlockSpec((1, D, tf), lambda m, f: (layer, 0, nf + f)),
            pl.BlockSpec((CONV_WIDTH, tf), lambda m, f: (0, f)),
            pl.BlockSpec((CONV_WIDTH, tf), lambda m, f: (0, nf + f)),
            pl.BlockSpec((1, tf), lambda m, f: (0, f)),
            pl.BlockSpec((1, tf), lambda m, f: (0, nf + f)),
            pl.BlockSpec((1, tf, D), lambda m, f: (layer, f, 0)),
            row, row,
        ] + ([row] * 3 if nxt is not None else []),
        out_specs=[tile, tile] if nxt is not None else tile,
        out_shape=([jax.ShapeDtypeStruct((S, D), F32), jax.ShapeDtypeStruct((S, D), BF16)]
                   if nxt is not None else jax.ShapeDtypeStruct((S, D), F32)),
        scratch_shapes=[pltpu.VMEM((tm + _HALO, D), BF16),
                        pltpu.VMEM((tm + _HALO, tf), F32), pltpu.VMEM((tm + _HALO, tf), F32)],
        compiler_params=_cparams(("arbitrary", "arbitrary")),
        name="ffn",
    )(x, x, gpre.reshape(1, D), sc, sh, wup, wup, convw, convw, convb.reshape(1, -1), convb.reshape(1, -1),
      wdown, gt, gpost.reshape(1, D), *extra)


def _rope_tables(positions):
    inv_freq = 1.0 / (ROPE_THETA ** (jnp.arange(0, HEAD_DIM, 2, dtype=F32) / HEAD_DIM))
    ang = positions.astype(F32)[:, None] * inv_freq
    cos, sin = jnp.cos(ang), jnp.sin(ang)
    return jnp.concatenate([cos, cos], axis=-1), jnp.concatenate([-sin, sin], axis=-1)


def kernel(x, c, positions, w_ada, b_ada, g_pre_mix, w_in, pe_kc, w_kc, pe_vc, w_vc, lb_logits, g_hg_norm,
           w_br_attn, w_br_hgrn, w_out, g_post_mix, g_pre_ffn, w_up, conv_w, conv_b, w_down, g_post_ffn):
    B, S, D = x.shape
    assert B == 1, "kernel is written for one sequence"
    L = w_ada.shape[0]
    xs = x[0]
    cosf, sinf = _rope_tables(positions[0])
    scale = HEAD_DIM ** -0.5 * np.log2(np.e)
    tables = (cosf * scale, sinf * scale, cosf, sinf)
    lb_cum = jnp.cumsum(jax.nn.softmax(lb_logits.astype(F32), axis=0), axis=0)
    lower = lb_cum - lb_cum[0:1]
    log_lb = jnp.log(lower)
    log_1m = jnp.log1p(-lower)
    ada = _ada_all(c, w_ada, b_ada)

    w_in_t = jnp.swapaxes(w_in, 1, 2)
    hg_roles = (("id",) * (TAIL_TILE // LANES),) * (4 * HG_WIDTH // TAIL_TILE)
    mg_roles = (("sg",) * (TAIL_TILE // LANES),) * (2 * D // TAIL_TILE)

    wa_b, wh_b, wo_b = w_br_attn.astype(BF16), w_br_hgrn.astype(BF16), w_out.astype(BF16)

    mods = [[ada[l, :, j * D:(j + 1) * D] for j in range(N_ADA)] for l in range(L)]
    h = _prenorm(xs, g_pre_mix[0], mods[0][1], mods[0][0])
    for l in range(L):
        sh1, sc1, gt1, sh2, sc2, gt2 = mods[l]

        a = _proj(h, w_in_t, l, 0, A_ROLES, BF16, tn=A_TILE, tables=tables, name="proj_attn")
        hz = _proj(h, w_in_t, l, A_COLS, hg_roles, F32, tn=TAIL_TILE, name="proj_hgrn")
        mg = _proj(h, w_in_t, l, A_COLS + 4 * HG_WIDTH, mg_roles, BF16, tn=TAIL_TILE, name="proj_merge_gate")

        cmp_kv = _compress(a, jnp.stack([pe_kc[l], pe_vc[l]]), jnp.stack([w_kc[l], w_vc[l]]))
        attn, wup_b, wdown_b = _nsa(a, cmp_kv, cast=(w_up, w_down), layer=l)
        hg = _hgrn(hz, log_lb[l:l + 1], log_1m[l:l + 1], g_hg_norm[l:l + 1])
        xs = _merge(attn, hg, mg, wa_b, wh_b, wo_b, l, xs, gt1, g_post_mix[l])

        nxt = (g_pre_mix[l + 1], mods[l + 1][1], mods[l + 1][0]) if l + 1 < L else None
        out = _ffn(xs, g_pre_ffn[l], sc2, sh2, wup_b[None], conv_w[l], conv_b[l], wdown_b[None], 0, gt2,
                   g_post_ffn[l], nxt)
        xs, h = out if nxt is not None else (out, None)
    return xs[None]
```

```python
import functools

import jax
import jax.numpy as jnp
import numpy as np
from jax import lax
from jax.experimental import pallas as pl
from jax.experimental.pallas import tpu as pltpu

F32 = jnp.float32
BF16 = jnp.bfloat16

A_HEADS = 8
A_KV_GROUPS = 2
A_HPG = A_HEADS // A_KV_GROUPS
HEAD_DIM = 128
A_WIDTH = A_HEADS * HEAD_DIM
A_KV_WIDTH = A_KV_GROUPS * HEAD_DIM
CMP_BLOCK = 32
CMP_STRIDE = 16
SEL_BLOCK = 64
SEL_TOPK = 16
WINDOW = 512
ROPE_THETA = 10000.0
HG_HEADS = 8
HG_DIM = 128
HG_WIDTH = HG_HEADS * HG_DIM
HG_CHUNK = 64
CONV_WIDTH = 3
NORM_EPS = 1e-6
N_ADA = 6
A_COLS = A_WIDTH + 6 * A_KV_WIDTH + 3 * A_HEADS

LANES = 128
SUBLANES = 8
SEL_SLOTS = LANES
KEY_TILE = 256
NEG_BIG = -1e30
VMEM_LIMIT = 56 * 1024 * 1024

ADA_COLS = 1024
NORM_ROWS = 512
PROJ_ROWS = 1024
TAIL_TILE = 1024
HG_ROWS = 512
MERGE_ROWS = 256
FFN_ROWS = 512
FFN_COLS = 512


def _cparams(sem):
    return pltpu.CompilerParams(dimension_semantics=sem, vmem_limit_bytes=VMEM_LIMIT)


def _split3(a):
    hi = a.astype(BF16)
    r1 = a - hi.astype(F32)
    mid = r1.astype(BF16)
    lo = (r1 - mid.astype(F32)).astype(BF16)
    return hi, mid, lo


def _dot(a, b):
    return jnp.dot(a, b, preferred_element_type=F32)


def _dot_nt(a, b):
    return lax.dot_general(a, b, (((1,), (1,)), ((), ())), preferred_element_type=F32)


def _dot_tn(a, b):
    return lax.dot_general(a, b, (((0,), (0,)), ((), ())), preferred_element_type=F32)


def _ada_kernel(c_ref, w_ref, b_ref, o_ref):
    c = c_ref[...]
    ca = c * jax.nn.sigmoid(c)
    o_ref[0] = jnp.sum(ca * w_ref[0], axis=0, keepdims=True) + b_ref[0]


def _ada_all(c, w_ada, b_ada):
    L, D, N = w_ada.shape
    tn = ADA_COLS
    return pl.pallas_call(
        _ada_kernel,
        grid=(L, N // tn),
        in_specs=[
            pl.BlockSpec((D, 1), lambda l, n: (0, 0)),
            pl.BlockSpec((1, D, tn), lambda l, n: (l, 0, n)),
            pl.BlockSpec((1, 1, tn), lambda l, n: (l, 0, n)),
        ],
        out_specs=pl.BlockSpec((1, 1, tn), lambda l, n: (l, 0, n)),
        out_shape=jax.ShapeDtypeStruct((L, 1, N), F32),
        compiler_params=_cparams(("arbitrary", "arbitrary")),
        name="ada",
    )(c.reshape(D, 1), w_ada, b_ada.reshape(L, 1, N))


def _prenorm_kernel(x_ref, g_ref, sc_ref, sh_ref, o_ref):
    x = x_ref[...]
    ms = jnp.mean(x * x, axis=-1, keepdims=True)
    y = x * lax.rsqrt(ms + NORM_EPS) * g_ref[...]
    o_ref[...] = (y * (1.0 + sc_ref[...]) + sh_ref[...]).astype(o_ref.dtype)


def _prenorm(x, g, sc, sh):
    S, D = x.shape
    tm = NORM_ROWS
    row = pl.BlockSpec((1, D), lambda m: (0, 0))
    return pl.pallas_call(
        _prenorm_kernel,
        grid=(S // tm,),
        in_specs=[pl.BlockSpec((tm, D), lambda m: (m, 0)), row, row, row],
        out_specs=pl.BlockSpec((tm, D), lambda m: (m, 0)),
        out_shape=jax.ShapeDtypeStruct((S, D), BF16),
        compiler_params=_cparams(("arbitrary",)),
        name="prenorm",
    )(x, g.reshape(1, D), sc, sh)


def _rope(t, cosf, sinf):
    return t * cosf + pltpu.roll(t, HEAD_DIM // 2, axis=1) * sinf


_PROJ_PART = 256


def _proj_kernel(h_ref, w_ref, *rest, roles, tn):
    o_ref, w_scr = rest[-2:]
    tabs = rest[:-2]
    n = pl.program_id(0)

    @pl.when(pl.program_id(1) == 0)
    def _():
        w_scr[...] = w_ref[0].astype(BF16)

    def emit(slabs):
        h = h_ref[...]
        per = _PROJ_PART // LANES
        parts = [_dot_nt(h, w_scr[c0:c0 + _PROJ_PART, :]) for c0 in range(0, tn, _PROJ_PART)]
        for pi, acc in enumerate(parts):
            for s in range(per):
                kind = slabs[pi * per + s]
                t = acc[:, s * LANES:(s + 1) * LANES]
                if kind == "sg":
                    t = jax.nn.sigmoid(t)
                elif kind != "id":
                    cos_ref, sin_ref = tabs[0:2] if kind == "rq" else tabs[-2:]
                    t = _rope(t, cos_ref[...], sin_ref[...])
                c = (pi * per + s) * LANES
                o_ref[:, c:c + LANES] = t.astype(o_ref.dtype)

    kinds = sorted(set(roles))
    if len(kinds) == 1:
        emit(kinds[0])
        return
    for kind in kinds:
        hit = functools.reduce(jnp.logical_or, [n == t for t, r in enumerate(roles) if r == kind])
        pl.when(hit)(functools.partial(emit, kind))


def _proj(h, w_t, layer, row0, roles, out_dtype, *, tn, tables=(), name="proj"):
    S, D = h.shape
    tm = PROJ_ROWS if S % PROJ_ROWS == 0 else S
    nt = len(roles)
    tab = pl.BlockSpec((tm, HEAD_DIM), lambda n, m: (m, 0))
    return pl.pallas_call(
        functools.partial(_proj_kernel, roles=tuple(roles), tn=tn),
        grid=(nt, S // tm),
        in_specs=[pl.BlockSpec((tm, D), lambda n, m: (m, 0)),
                  pl.BlockSpec((pl.Element(1), pl.Element(tn), pl.Element(D)),
                               lambda n, m: (layer, pl.multiple_of(row0 + n * tn, SUBLANES), 0))]
                 + [tab] * len(tables),
        out_specs=pl.BlockSpec((tm, tn), lambda n, m: (m, n)),
        out_shape=jax.ShapeDtypeStruct((S, nt * tn), out_dtype),
        scratch_shapes=[pltpu.VMEM((tn, D), BF16)],
        compiler_params=_cparams(("arbitrary", "arbitrary")),
        name=name,
    )(h, w_t, *tables)


A_TILE = 1024
_A_SLABS = (("rq",) * (A_WIDTH // LANES)
            + (("rk",) * A_KV_GROUPS + ("id",) * A_KV_GROUPS) * 3
            + ("sg",) * (-(A_WIDTH // LANES + 6 * A_KV_GROUPS) % (A_TILE // LANES)))
A_ROLES = tuple(_A_SLABS[i:i + A_TILE // LANES] for i in range(0, len(_A_SLABS), A_TILE // LANES))
_KV_BLOCK0 = A_WIDTH // LANES
BLK_KC, BLK_VC, BLK_KS, BLK_VS, BLK_KW, BLK_VW = [_KV_BLOCK0 + j * A_KV_GROUPS for j in range(6)]
BLK_GATE = _KV_BLOCK0 + 6 * A_KV_GROUPS


def _compress_kernel(t_ref, pe_ref, w_ref, o_ref, t32_ref, *, n_chunk):
    half = CMP_BLOCK // 2
    t32_ref[...] = t_ref[...].astype(F32)
    a = jnp.concatenate([t32_ref[pl.ds(j, n_chunk, stride=CMP_STRIDE), :] for j in range(half)],
                        axis=1).astype(BF16)
    kw = half * HEAD_DIM
    w = w_ref[0].astype(BF16)
    y_top = _dot(a, w[:kw])
    y_bot = _dot(a, w[kw:])
    pe = jnp.concatenate([pe_ref[0, j:j + 1, :] for j in range(CMP_BLOCK)], axis=1)
    pe_term = jnp.zeros((1, HEAD_DIM), F32)
    for piece in _split3(pe):
        pe_term = pe_term + _dot(piece, w)
    o_ref[0, 0] = y_top + pltpu.roll(y_bot, n_chunk - 1, axis=0) + pe_term


def _compress(a, pe2, w2):
    S = a.shape[0]
    n_chunk = S // CMP_STRIDE
    G = A_KV_GROUPS
    return pl.pallas_call(
        functools.partial(_compress_kernel, n_chunk=n_chunk),
        grid=(2, G),
        in_specs=[
            pl.BlockSpec((S, HEAD_DIM), lambda kv, g: (0, BLK_KC + kv * G + g)),
            pl.BlockSpec((1, CMP_BLOCK, HEAD_DIM), lambda kv, g: (kv, 0, 0)),
            pl.BlockSpec((1, CMP_BLOCK * HEAD_DIM, HEAD_DIM), lambda kv, g: (kv, 0, 0)),
        ],
        out_specs=pl.BlockSpec((1, 1, n_chunk, HEAD_DIM), lambda kv, g: (kv, g, 0, 0)),
        out_shape=jax.ShapeDtypeStruct((2, G, n_chunk, HEAD_DIM), F32),
        scratch_shapes=[pltpu.VMEM((S, HEAD_DIM), F32)],
        compiler_params=_cparams(("arbitrary", "arbitrary")),
        name="compress",
    )(a, pe2, w2)


NSA_QB = 256
SEL_BODY = 8
SEL_GROUP = 4
_N_FORCED = 3


def _nsa_kernel(q_ref, ks_ref, kw_ref, vs_ref, vw_ref, kc_ref, vc_ref, ovt_ref, gate_ref, *rest, seq, n_cast):
    cast_in, rest = rest[:n_cast], rest[n_cast:]
    o_ref, rest = rest[0], rest[1:]
    cast_out, rest = rest[:n_cast], rest[n_cast:]
    kaug_ref, vt_ref, vwt_ref, vct_ref, acc_ref, m_ref, l_ref = rest
    for src, dst in zip(cast_in, cast_out):
        dst[...] = src[0].astype(dst.dtype)
    grp = pl.program_id(0)
    i = pl.program_id(1)
    t0 = i * NSA_QB
    n_cmp = seq // CMP_STRIDE
    cols = A_HPG * NSA_QB

    @pl.when(i == 0)
    def _():
        def build(r, c):
            r0 = pl.multiple_of(r * KEY_TILE, KEY_TILE)
            kaug_ref[r, :, 0:HEAD_DIM] = ks_ref[pl.ds(r0, KEY_TILE), :]
            blk = (r0 + lax.broadcasted_iota(jnp.int32, (KEY_TILE, SEL_SLOTS), 0)) // SEL_BLOCK
            slot = lax.broadcasted_iota(jnp.int32, (KEY_TILE, SEL_SLOTS), 1)
            kaug_ref[r, :, HEAD_DIM:HEAD_DIM + SEL_SLOTS] = jnp.where(blk == slot, 1.0, 0.0).astype(BF16)
            vt_ref[r] = vs_ref[pl.ds(r0, KEY_TILE), :].astype(F32).T.astype(BF16)
            vwt_ref[r] = vw_ref[pl.ds(r0, KEY_TILE), :].astype(F32).T.astype(BF16)
            return c
        lax.fori_loop(0, seq // KEY_TILE, build, 0)
        vct_ref[...] = vc_ref[0, 0].T.astype(BF16)

    qt = jnp.concatenate(
        [q_ref[:, h * HEAD_DIM:(h + 1) * HEAD_DIM].astype(F32).T.astype(BF16) for h in range(A_HPG)],
        axis=1)
    tq = t0 + lax.broadcasted_iota(jnp.int32, (1, cols), 1) % NSA_QB

    def softmax0(s):
        m = jnp.max(s, axis=0, keepdims=True)
        p = jnp.exp2(s - m)
        return p, jnp.sum(p, axis=0, keepdims=True)

    s_c = _dot(kc_ref[0, 0].astype(BF16), qt)
    nrow = lax.broadcasted_iota(jnp.int32, (n_cmp, 1), 0)
    vis = (nrow * CMP_STRIDE + (CMP_BLOCK - 1)) <= tq
    p_c, den = softmax0(jnp.where(vis, s_c, NEG_BIG))
    p_c = p_c * jnp.where(tq >= CMP_BLOCK - 1, 1.0 / den, 0.0)
    o_ct = _dot(vct_ref[...], p_c.astype(BF16))

    psum = p_c[:, 0:NSA_QB]
    for h in range(1, A_HPG):
        psum = psum + p_c[:, h * NSA_QB:(h + 1) * NSA_QB]
    ovt = ovt_ref[...]
    imp_t = jnp.zeros((SEL_SLOTS, NSA_QB), F32)
    for piece in _split3(psum):
        imp_t = imp_t + _dot(ovt, piece)
    slot_i = lax.broadcasted_iota(jnp.int32, (SEL_SLOTS, NSA_QB), 0)
    tq_l = t0 + lax.broadcasted_iota(jnp.int32, (SEL_SLOTS, NSA_QB), 1)
    cur = tq_l // SEL_BLOCK
    forced = jnp.where(slot_i == 0, 1, 0) + jnp.where(slot_i == cur, 1, 0) + jnp.where(slot_i == cur - 1, 1, 0)
    valid = slot_i * SEL_BLOCK <= tq_l
    key = jnp.where(forced > 0, -3e38, jnp.where(valid, imp_t, NEG_BIG))
    slot_f = slot_i.astype(F32)
    bias_t = jnp.where(forced > 0, 0.0, NEG_BIG)
    for _ in range(SEL_TOPK - _N_FORCED):
        mx = jnp.max(key, axis=0, keepdims=True)
        first = jnp.min(jnp.where(key == mx, slot_f, 1e9), axis=0, keepdims=True)
        pick = slot_f == first
        bias_t = jnp.where(pick, 0.0, bias_t)
        key = jnp.where(pick, -3e38, key)

    n_wt = WINDOW // KEY_TILE + 1
    assert WINDOW % KEY_TILE == 0 and n_wt == 3
    kk = lax.broadcasted_iota(jnp.int32, (KEY_TILE, 1), 0)
    qq = tq - t0
    tri = kk <= qq
    tri_far = kk <= jnp.where(i >= 2, qq, KEY_TILE)
    w_idx = [jnp.maximum(i - 2, 0), jnp.maximum(i - 1, 0), i]
    s_w = [_dot(kw_ref[pl.ds(pl.multiple_of(w * KEY_TILE, KEY_TILE), KEY_TILE), :], qt) for w in w_idx]
    s_w = jnp.concatenate([jnp.where(tri_far, NEG_BIG, s_w[0]),
                           jnp.where(i >= 1, s_w[1], NEG_BIG),
                           jnp.where(tri, s_w[2], NEG_BIG)], axis=0)
    p_w, l_w = softmax0(s_w)
    p_w = p_w.astype(BF16)
    o_wt = _dot(vwt_ref[w_idx[0]], p_w[0:KEY_TILE])
    for k in range(1, n_wt):
        o_wt = o_wt + _dot(vwt_ref[w_idx[k]], p_w[k * KEY_TILE:(k + 1) * KEY_TILE])
    o_wt = o_wt * (1.0 / l_w)

    qt_aug = jnp.concatenate([qt, jnp.concatenate([bias_t.astype(BF16)] * A_HPG, axis=1)], axis=0)
    acc_ref[...] = jnp.zeros_like(acc_ref)
    m_ref[...] = jnp.full(m_ref.shape, 0.5 * NEG_BIG, F32)
    l_ref[...] = jnp.zeros_like(l_ref)

    def sel_body(base, causal):
        n = len(causal)
        m, l = m_ref[...], l_ref[...]
        gsz = min(SEL_GROUP, max(n // 2, 1))
        groups = [list(range(g0, min(g0 + gsz, n))) for g0 in range(0, n, gsz)]

        def scores(ks):
            out = {}
            for k in ks:
                sk = _dot(kaug_ref[base + k], qt_aug)
                if causal[k]:
                    kpos = (base + k) * KEY_TILE + lax.broadcasted_iota(jnp.int32, (KEY_TILE, 1), 0)
                    sk = jnp.where(kpos <= tq, sk, NEG_BIG)
                out[k] = sk
            return out

        s = scores(groups[0])
        for gi, ks in enumerate(groups):
            if gi + 1 < len(groups):
                s.update(scores(groups[gi + 1]))
            m_new = m
            for k in ks:
                m_new = jnp.maximum(m_new, jnp.max(s[k], axis=0, keepdims=True))
            alpha = jnp.exp2(m - m_new)
            l = alpha * l
            pv = None
            for k in ks:
                p = jnp.exp2(s.pop(k) - m_new)
                l = l + jnp.sum(p, axis=0, keepdims=True)
                d = _dot(vt_ref[base + k], p.astype(BF16))
                pv = d if pv is None else pv + d
            acc_ref[...] = alpha * acc_ref[...] + pv
            m = m_new
        m_ref[...] = m
        l_ref[...] = l

    n_bulk = i // SEL_BODY
    rem = i % SEL_BODY

    def bulk(b, c):
        sel_body(b * SEL_BODY, (False,) * SEL_BODY)
        return c
    lax.fori_loop(0, n_bulk, bulk, 0)

    base = n_bulk * SEL_BODY
    part = SEL_BODY // 2
    while part >= 2:
        pl.when(rem & part != 0)(functools.partial(sel_body, base, (False,) * part))
        base = base + (rem & part)
        part //= 2

    @pl.when(rem % 2 == 1)
    def _():
        sel_body(i - 1, (False, True))

    @pl.when(rem % 2 == 0)
    def _():
        sel_body(i, (True,))

    o_st = acc_ref[...] * (1.0 / l_ref[...])

    gates_t = gate_ref[...].astype(F32).T
    per_group = 3 * A_HPG

    def gate(h, b):
        r = 3 * h + b
        return jnp.where(grp == 0, gates_t[r:r + 1], gates_t[per_group + r:per_group + r + 1])

    for h in range(A_HPG):
        c = slice(h * NSA_QB, (h + 1) * NSA_QB)
        o = gate(h, 0) * o_ct[:, c] + gate(h, 1) * o_st[:, c] + gate(h, 2) * o_wt[:, c]
        o_ref[:, h * HEAD_DIM:(h + 1) * HEAD_DIM] = o.T.astype(o_ref.dtype)


def _overlap_matrix(seq):
    n_chunk = seq // CMP_STRIDE
    cs = np.arange(n_chunk)[:, None] * CMP_STRIDE
    ss = np.arange(SEL_SLOTS)[None, :] * SEL_BLOCK
    ov = np.maximum(np.minimum(cs + CMP_BLOCK, ss + SEL_BLOCK) - np.maximum(cs, ss), 0) / CMP_BLOCK
    n_cmp = (seq - CMP_BLOCK) // CMP_STRIDE + 1
    ov[n_cmp:] = 0.0
    ov[:, seq // SEL_BLOCK:] = 0.0
    return ov.astype(np.float32)


def _nsa(a, cmp_kv, cast=(), layer=0):
    S = a.shape[0]
    G = A_KV_GROUPS
    steps = S // NSA_QB
    for w in cast:
        assert w.shape[1] % (steps * 2 * SUBLANES) == 0
    slab = lambda g, i: jnp.where(g == 0, i, steps - 1)
    cast_in = [pl.BlockSpec((1, w.shape[1] // steps, w.shape[2]), lambda g, i: (layer, slab(g, i), 0))
               for w in cast]
    cast_out = [pl.BlockSpec((w.shape[1] // steps, w.shape[2]), lambda g, i: (slab(g, i), 0)) for w in cast]
    cast_shape = [jax.ShapeDtypeStruct(w.shape[1:], BF16) for w in cast]
    n_tiles = S // KEY_TILE
    assert S // SEL_BLOCK <= SEL_SLOTS and S % KEY_TILE == 0 and S >= WINDOW + NSA_QB
    assert NSA_QB == KEY_TILE and G == 2
    n_chunk = S // CMP_STRIDE
    ovt = jnp.asarray(_overlap_matrix(S).T, BF16)
    col = lambda b: pl.BlockSpec((S, HEAD_DIM), lambda g, i, b=b: (0, b + g))
    out = pl.pallas_call(
        functools.partial(_nsa_kernel, seq=S, n_cast=len(cast)),
        grid=(G, steps),
        in_specs=[
            pl.BlockSpec((NSA_QB, A_HPG * HEAD_DIM), lambda g, i: (i, g)),
            col(BLK_KS), col(BLK_KW), col(BLK_VS), col(BLK_VW),
            pl.BlockSpec((1, 1, n_chunk, HEAD_DIM), lambda g, i: (0, g, 0, 0)),
            pl.BlockSpec((1, 1, n_chunk, HEAD_DIM), lambda g, i: (1, g, 0, 0)),
            pl.BlockSpec((SEL_SLOTS, n_chunk), lambda g, i: (0, 0)),
            pl.BlockSpec((NSA_QB, LANES), lambda g, i: (i, BLK_GATE)),
        ] + cast_in,
        out_specs=[pl.BlockSpec((NSA_QB, A_HPG * HEAD_DIM), lambda g, i: (i, g))] + cast_out,
        out_shape=[jax.ShapeDtypeStruct((S, A_WIDTH), BF16)] + cast_shape,
        scratch_shapes=[pltpu.VMEM((n_tiles, KEY_TILE, HEAD_DIM + SEL_SLOTS), BF16),
                        pltpu.VMEM((n_tiles, HEAD_DIM, KEY_TILE), BF16),
                        pltpu.VMEM((n_tiles, HEAD_DIM, KEY_TILE), BF16),
                        pltpu.VMEM((HEAD_DIM, n_chunk), BF16),
                        pltpu.VMEM((HEAD_DIM, A_HPG * NSA_QB), F32),
                        pltpu.VMEM((1, A_HPG * NSA_QB), F32),
                        pltpu.VMEM((1, A_HPG * NSA_QB), F32)],
        compiler_params=_cparams(("arbitrary", "arbitrary")),
        name="nsa",
    )(a, a, a, a, a, cmp_kv, cmp_kv, ovt, a, *cast)
    return out[0] if not cast else out


_ROW_MASKS = 18
_SCORE_MASKS = 10
HG_HEADS_PER_STEP = 8


def _hgrn_tables():
    t = np.arange(HG_CHUNK)
    p4, i4, i16 = t % 4, (t // 4) % 4, t // 16
    conds = [p4 >= 1, p4 >= 2, p4 >= 3, p4 <= 2, p4 <= 1, p4 == 0,
             i4 >= 1, i4 >= 2, i4 >= 3, i4 <= 2, i4 <= 1, i4 == 0,
             i16 >= 1, i16 >= 2, i16 >= 3, i16 <= 2, i16 <= 1, i16 == 0]
    rm = np.stack([np.broadcast_to(c[:, None], (HG_CHUNK, HG_DIM)) for c in conds]).astype(np.float32)
    tt, ss = t[:, None], t[None, :]
    lag16 = tt // 16 - ss // 16
    lag4 = tt // 4 - ss // 4
    same16, same4 = lag16 == 0, lag4 == 0
    sm = [lag16 == 1, lag16 == 2, lag16 == 3,
          same16 & (lag4 == 1), same16 & (lag4 == 2), same16 & (lag4 == 3),
          same4 & (tt - ss == 0), same4 & (tt - ss == 1), same4 & (tt - ss == 2), same4 & (tt - ss == 3)]
    return rm, np.stack(sm).astype(np.float32)


def _hgrn_kernel(q_ref, f_ref, i_ref, og_ref, loglb_ref, log1m_ref, gn_ref, rm_ref, sm_ref, *rest, tb, nh, n_cast):
    cast_in, rest = rest[:n_cast], rest[n_cast:]
    o_ref, rest = rest[0], rest[1:]
    cast_out, rest = rest[:n_cast], rest[n_cast:]
    st_ref, g_scr, k_scr, o_scr = rest
    for src, dst in zip(cast_in, cast_out):
        dst[...] = src[0].astype(dst.dtype)
    C = HG_CHUNK
    heads = [slice(h * HG_DIM, (h + 1) * HG_DIM) for h in range(nh)]

    @pl.when(pl.program_id(1) == 0)
    def _():
        st_ref[...] = jnp.zeros_like(st_ref)

    x = f_ref[...]
    log_sig = jnp.minimum(x, 0.0) - jnp.log(1.0 + jnp.exp(-jnp.abs(x)))
    a = loglb_ref[...]
    c = log1m_ref[...] + log_sig
    log_f = jnp.maximum(a, c) + jnp.log(1.0 + jnp.exp(-jnp.abs(a - c)))
    g_scr[...] = log_f
    k_scr[...] = jnp.exp(log1m_ref[...] + (log_sig - x))

    def rmask(n):
        return jnp.concatenate([rm_ref[n]] * nh, axis=1)

    def roll(v, s):
        return pltpu.roll(v, s % C, axis=0)

    def b16(z):
        return z.astype(BF16)

    def chunk(ci, carry):
        r0 = pl.multiple_of(ci * C, C)
        g = g_scr[pl.ds(r0, C), :]
        q = q_ref[pl.ds(r0, C), :]
        k = k_scr[pl.ds(r0, C), :]
        v = i_ref[pl.ds(r0, C), :].astype(BF16)
        pre4 = g + rmask(0) * roll(g, 1) + rmask(1) * roll(g, 2) + rmask(2) * roll(g, 3)
        suf4 = rmask(3) * roll(g, -1) + rmask(4) * roll(g, -2) + rmask(5) * roll(g, -3)
        tot4 = pre4 + suf4
        a1, a2, a3 = roll(tot4, 4), roll(tot4, 8), roll(tot4, 12)
        pre16 = pre4 + rmask(6) * a1 + rmask(7) * a2 + rmask(8) * a3
        suf16 = suf4 + rmask(9) * roll(tot4, -4) + rmask(10) * roll(tot4, -8) + rmask(11) * roll(tot4, -12)
        tot16 = pre16 + suf16
        d1, d2, d3 = roll(tot16, 16), roll(tot16, 32), roll(tot16, 48)
        pre64 = pre16 + rmask(12) * d1 + rmask(13) * d2 + rmask(14) * d3
        suf64 = suf16 + rmask(15) * roll(tot16, -16) + rmask(16) * roll(tot16, -32) + rmask(17) * roll(tot16, -48)
        tot64 = pre64 + suf64
        g1, g2 = roll(g, 1), roll(g, 2)

        q_a = q * jnp.exp(pre16)
        lhs_a = [b16(q_a), b16(q_a * jnp.exp(d1)), b16(q_a * jnp.exp(d1 + d2))]
        k_a = b16(k * jnp.exp(suf16))
        q_b = q * jnp.exp(pre4)
        lhs_b = [b16(q_b), b16(q_b * jnp.exp(a1)), b16(q_b * jnp.exp(a1 + a2))]
        k_b = b16(k * jnp.exp(suf4))
        q_c = q * jnp.exp(g)
        lhs_c = [b16(q), b16(q_c), b16(q_c * jnp.exp(g1)), b16(q_c * jnp.exp(g1 + g2))]
        k_c = b16(k)
        q_s = b16(q * jnp.exp(pre64))
        k_s = b16(k * jnp.exp(suf64))
        decay = jnp.exp(tot64[0:1, :])

        def level(lhs, rhs):
            return [_dot_nt(jnp.concatenate([z[:, hh] for z in lhs], axis=0), rhs[:, hh]) for hh in heads]

        s_a, s_b, s_c = level(lhs_a, k_a), level(lhs_b, k_b), level(lhs_c, k_c)
        o_inter = [_dot_nt(q_s[:, hh], b16(st_ref[h])) for h, hh in enumerate(heads)]
        kv = [_dot_tn(v[:, hh], k_s[:, hh]) for hh in heads]
        for h, hh in enumerate(heads):
            sa, sb, sc = s_a[h], s_b[h], s_c[h]
            scores = (sm_ref[0] * sa[0:C] + sm_ref[1] * sa[C:2 * C] + sm_ref[2] * sa[2 * C:3 * C]
                      + sm_ref[3] * sb[0:C] + sm_ref[4] * sb[C:2 * C] + sm_ref[5] * sb[2 * C:3 * C]
                      + sm_ref[6] * sc[0:C] + sm_ref[7] * sc[C:2 * C] + sm_ref[8] * sc[2 * C:3 * C]
                      + sm_ref[9] * sc[3 * C:4 * C])
            o_scr[pl.ds(r0, C), hh] = _dot(b16(scores), v[:, hh]) + o_inter[h]
            st_ref[h] = st_ref[h] * decay[:, hh] + kv[h]
        return carry

    lax.fori_loop(0, tb // C, chunk, 0)

    for hh in heads:
        o = o_scr[:, hh]
        ms = jnp.mean(o * o, axis=-1, keepdims=True)
        og = og_ref[:, hh]
        y = o * lax.rsqrt(ms + NORM_EPS) * gn_ref[:, hh] * (og * jax.nn.sigmoid(og))
        o_ref[:, hh] = y.astype(o_ref.dtype)


def _hgrn(hz, loglb, log1mlb, gnorm, cast=(), layer=0):
    S = hz.shape[0]
    tb = HG_ROWS if S % HG_ROWS == 0 else S
    nh = HG_HEADS_PER_STEP
    ng = HG_HEADS // nh
    steps = S // tb
    for w in cast:
        assert w.shape[1] % (steps * 2 * SUBLANES) == 0
    slab = lambda h, c: jnp.where(h == 0, c, steps - 1)
    cast_in = [pl.BlockSpec((1, w.shape[1] // steps, w.shape[2]), lambda h, c: (layer, slab(h, c), 0))
               for w in cast]
    cast_out = [pl.BlockSpec((w.shape[1] // steps, w.shape[2]), lambda h, c: (slab(h, c), 0)) for w in cast]
    cast_shape = [jax.ShapeDtypeStruct(w.shape[1:], BF16) for w in cast]
    w = nh * HG_DIM
    rm, sm = _hgrn_tables()
    blk = lambda part: pl.BlockSpec((tb, w), lambda h, c, part=part: (c, part * ng + h))
    row = pl.BlockSpec((1, w), lambda h, c: (0, h))
    out = pl.pallas_call(
        functools.partial(_hgrn_kernel, tb=tb, nh=nh, n_cast=len(cast)),
        grid=(ng, steps),
        in_specs=[blk(0), blk(1), blk(2), blk(3), row, row, row,
                  pl.BlockSpec((_ROW_MASKS, HG_CHUNK, HG_DIM), lambda h, c: (0, 0, 0)),
                  pl.BlockSpec((_SCORE_MASKS, HG_CHUNK, HG_CHUNK), lambda h, c: (0, 0, 0))] + cast_in,
        out_specs=[pl.BlockSpec((tb, w), lambda h, c: (c, h))] + cast_out,
        out_shape=[jax.ShapeDtypeStruct((S, HG_WIDTH), BF16)] + cast_shape,
        scratch_shapes=[pltpu.VMEM((nh, HG_DIM, HG_DIM), F32), pltpu.VMEM((tb, w), F32),
                        pltpu.VMEM((tb, w), F32), pltpu.VMEM((tb, w), F32)],
        compiler_params=_cparams(("arbitrary", "arbitrary")),
        name="hgrn",
    )(hz, hz, hz, hz, loglb, log1mlb, gnorm, jnp.asarray(rm), jnp.asarray(sm), *cast)
    return out[0] if not cast else out


_MERGE_PART = 512


def _merge_kernel(at_ref, hg_ref, ga_ref, gh_ref, wa_ref, wh_ref, wo_ref, x_ref, gt_ref, gp_ref, o_ref):
    at, hg = at_ref[...], hg_ref[...]
    d = wo_ref.shape[1]
    parts = [slice(c0, c0 + _MERGE_PART) for c0 in range(0, d, _MERGE_PART)]
    branch = [(_dot(at, wa_ref[0, :, cs]), _dot(hg, wh_ref[0, :, cs])) for cs in parts]
    y = None
    for cs, (ma, mh) in zip(parts, branch):
        mixed = ga_ref[:, cs].astype(F32) * ma + gh_ref[:, cs].astype(F32) * mh
        yp = _dot(mixed.astype(BF16), wo_ref[0, cs, :])
        y = yp if y is None else y + yp
    ms = jnp.mean(y * y, axis=-1, keepdims=True)
    yn = y * lax.rsqrt(ms + NORM_EPS) * gp_ref[...]
    o_ref[...] = x_ref[...] + gt_ref[...] * yn


def _merge(attn, hg, mg, wa, wh, wo, layer, x, gt, gpost):
    S, D = x.shape
    tm = MERGE_ROWS
    const = lambda shape: pl.BlockSpec((1,) + shape, lambda m: (layer, 0, 0), pipeline_mode=pl.Buffered(1))
    row = pl.BlockSpec((1, D), lambda m: (0, 0))
    return pl.pallas_call(
        _merge_kernel,
        grid=(S // tm,),
        in_specs=[
            pl.BlockSpec((tm, A_WIDTH), lambda m: (m, 0)),
            pl.BlockSpec((tm, HG_WIDTH), lambda m: (m, 0)),
            pl.BlockSpec((tm, D), lambda m: (m, 0)),
            pl.BlockSpec((tm, D), lambda m: (m, 1)),
            const((A_WIDTH, D)), const((HG_WIDTH, D)), const((D, D)),
            pl.BlockSpec((tm, D), lambda m: (m, 0)),
            row, row,
        ],
        out_specs=pl.BlockSpec((tm, D), lambda m: (m, 0)),
        out_shape=jax.ShapeDtypeStruct((S, D), F32),
        compiler_params=_cparams(("arbitrary",)),
        name="merge",
    )(attn, hg, mg, mg, wa, wh, wo, x, gt, gpost.reshape(1, D))


_HALO = 16
_FF_SUB = 256


def _norm_mod(x, g_ref, sc_ref, sh_ref):
    ms = jnp.mean(x * x, axis=-1, keepdims=True)
    y = x * lax.rsqrt(ms + NORM_EPS) * g_ref[...]
    return (y * (1.0 + sc_ref[...]) + sh_ref[...]).astype(BF16)


def _ffn_kernel(x_ref, xh_ref, g_ref, sc_ref, sh_ref, wg_ref, wv_ref, cwg_ref, cwv_ref, cbg_ref, cbv_ref,
                wd_ref, gt_ref, gp_ref, *rest, tm, emit_next):
    if emit_next:
        gn_ref, scn_ref, shn_ref, o_ref, hn_ref, h_scr, ug_scr, uv_scr = rest
    else:
        o_ref, h_scr, ug_scr, uv_scr = rest
    m = pl.program_id(0)
    f = pl.program_id(1)
    norm_mod = functools.partial(_norm_mod, g_ref=g_ref, sc_ref=sc_ref, sh_ref=sh_ref)

    @pl.when(f == 0)
    def _():
        h_scr[0:_HALO, :] = jnp.where(m > 0, norm_mod(xh_ref[...]), jnp.zeros((), BF16))
        h_scr[_HALO:_HALO + tm, :] = norm_mod(x_ref[...])

    tf = wg_ref.shape[2]
    subs = [slice(s * _FF_SUB, (s + 1) * _FF_SUB) for s in range(tf // _FF_SUB)]
    h = h_scr[...]
    ups = [(_dot(h, wg_ref[0, :, cs]), _dot(h, wv_ref[0, :, cs])) for cs in subs]

    def conv(u_scr, cw_ref, cb_ref, cs):
        out = cb_ref[:, cs] + cw_ref[0:1, cs] * u_scr[pl.ds(_HALO - 2, tm), cs]
        out = out + cw_ref[1:2, cs] * u_scr[pl.ds(_HALO - 1, tm), cs]
        return out + cw_ref[2:3, cs] * u_scr[pl.ds(_HALO, tm), cs]

    acc = jnp.where(f > 0, o_ref[...], 0.0)
    for cs, (ug, uv) in zip(subs, ups):
        ug_scr[:, cs] = ug
        uv_scr[:, cs] = uv
        cg = conv(ug_scr, cwg_ref, cbg_ref, cs)
        cv = conv(uv_scr, cwv_ref, cbv_ref, cs)
        act = (cg * jax.nn.sigmoid(cg) * cv).astype(BF16)
        acc = acc + _dot(act, wd_ref[0, cs, :])
    o_ref[...] = acc

    @pl.when(f == pl.num_programs(1) - 1)
    def _():
        y = o_ref[...]
        ms = jnp.mean(y * y, axis=-1, keepdims=True)
        yn = y * lax.rsqrt(ms + NORM_EPS) * gp_ref[...]
        x_new = x_ref[...] + gt_ref[...] * yn
        o_ref[...] = x_new
        if emit_next:
            hn_ref[...] = _norm_mod(x_new, gn_ref, scn_ref, shn_ref)


def _ffn(x, gpre, sc, sh, wup, convw, convb, wdown, layer, gt, gpost, nxt=None):
    S, D = x.shape
    F = wdown.shape[1]
    tm = FFN_ROWS if S % FFN_ROWS == 0 else S
    tf = FFN_COLS
    nf = F // tf
    hb = tm // _HALO
    row = pl.BlockSpec((1, D), lambda m, f: (0, 0))
    tile = pl.BlockSpec((tm, D), lambda m, f: (m, 0))
    extra = () if nxt is None else (nxt[0].reshape(1, D), nxt[1], nxt[2])
    return pl.pallas_call(
        functools.partial(_ffn_kernel, tm=tm, emit_next=nxt is not None),
        grid=(S // tm, nf),
        in_specs=[
            pl.BlockSpec((tm, D), lambda m, f: (m, 0)),
            pl.BlockSpec((_HALO, D), lambda m, f: (jnp.maximum(m * hb - 1, 0), 0)),
            row, row, row,
            pl.BlockSpec((1, D, tf), lambda m, f: (layer, 0, f)),
            pl.BlockSpec((1, D, tf), lambda m, f: (layer, 0, nf + f)),
            pl.BlockSpec((CONV_WIDTH, tf), lambda m, f: (0, f)),
            pl.BlockSpec((CONV_WIDTH, tf), lambda m, f: (0, nf + f)),
            pl.BlockSpec((1, tf), lambda m, f: (0, f)),
            pl.BlockSpec((1, tf), lambda m, f: (0, nf + f)),
            pl.BlockSpec((1, tf, D), lambda m, f: (layer, f, 0)),
            row, row,
        ] + ([row] * 3 if nxt is not None else []),
        out_specs=[tile, tile] if nxt is not None else tile,
        out_shape=([jax.ShapeDtypeStruct((S, D), F32), jax.ShapeDtypeStruct((S, D), BF16)]
                   if nxt is not None else jax.ShapeDtypeStruct((S, D), F32)),
        scratch_shapes=[pltpu.VMEM((tm + _HALO, D), BF16),
                        pltpu.VMEM((tm + _HALO, tf), F32), pltpu.VMEM((tm + _HALO, tf), F32)],
        compiler_params=_cparams(("arbitrary", "arbitrary")),
        name="ffn",
    )(x, x, gpre.reshape(1, D), sc, sh, wup, wup, convw, convw, convb.reshape(1, -1), convb.reshape(1, -1),
      wdown, gt, gpost.reshape(1, D), *extra)


def _rope_tables(positions):
    inv_freq = 1.0 / (ROPE_THETA ** (jnp.arange(0, HEAD_DIM, 2, dtype=F32) / HEAD_DIM))
    ang = positions.astype(F32)[:, None] * inv_freq
    cos, sin = jnp.cos(ang), jnp.sin(ang)
    return jnp.concatenate([cos, cos], axis=-1), jnp.concatenate([-sin, sin], axis=-1)


def kernel(x, c, positions, w_ada, b_ada, g_pre_mix, w_in, pe_kc, w_kc, pe_vc, w_vc, lb_logits, g_hg_norm,
           w_br_attn, w_br_hgrn, w_out, g_post_mix, g_pre_ffn, w_up, conv_w, conv_b, w_down, g_post_ffn):
    B, S, D = x.shape
    assert B == 1, "kernel is written for one sequence"
    L = w_ada.shape[0]
    xs = x[0]
    cosf, sinf = _rope_tables(positions[0])
    scale = HEAD_DIM ** -0.5 * np.log2(np.e)
    tables = (cosf * scale, sinf * scale, cosf, sinf)
    lb_cum = jnp.cumsum(jax.nn.softmax(lb_logits.astype(F32), axis=0), axis=0)
    lower = lb_cum - lb_cum[0:1]
    log_lb = jnp.log(lower)
    log_1m = jnp.log1p(-lower)
    ada = _ada_all(c, w_ada, b_ada)

    w_in_t = jnp.swapaxes(w_in, 1, 2)
    hg_roles = (("id",) * (TAIL_TILE // LANES),) * (4 * HG_WIDTH // TAIL_TILE)
    mg_roles = (("sg",) * (TAIL_TILE // LANES),) * (2 * D // TAIL_TILE)


    mods = [[ada[l, :, j * D:(j + 1) * D] for j in range(N_ADA)] for l in range(L)]
    h = _prenorm(xs, g_pre_mix[0], mods[0][1], mods[0][0])
    for l in range(L):
        sh1, sc1, gt1, sh2, sc2, gt2 = mods[l]

        a = _proj(h, w_in_t, l, 0, A_ROLES, BF16, tn=A_TILE, tables=tables, name="proj_attn")
        hz = _proj(h, w_in_t, l, A_COLS, hg_roles, F32, tn=TAIL_TILE, name="proj_hgrn")
        mg = _proj(h, w_in_t, l, A_COLS + 4 * HG_WIDTH, mg_roles, BF16, tn=TAIL_TILE, name="proj_merge_gate")

        cmp_kv = _compress(a, jnp.stack([pe_kc[l], pe_vc[l]]), jnp.stack([w_kc[l], w_vc[l]]))
        attn, wup_b, wdown_b = _nsa(a, cmp_kv, cast=(w_up, w_down), layer=l)
        hg, wa_b, wh_b, wo_b = _hgrn(hz, log_lb[l:l + 1], log_1m[l:l + 1], g_hg_norm[l:l + 1],
                                     cast=(w_br_attn, w_br_hgrn, w_out), layer=l)
        xs = _merge(attn, hg, mg, wa_b[None], wh_b[None], wo_b[None], 0, xs, gt1, g_post_mix[l])

        nxt = (g_pre_mix[l + 1], mods[l + 1][1], mods[l + 1][0]) if l + 1 < L else None
        out = _ffn(xs, g_pre_ffn[l], sc2, sh2, wup_b[None], conv_w[l], conv_b[l], wdown_b[None], 0, gt2,
                   g_post_ffn[l], nxt)
        xs, h = out if nxt is not None else (out, None)
    return xs[None]
```

```python
import functools

import jax
import jax.numpy as jnp
import numpy as np
from jax import lax
from jax.experimental import pallas as pl
from jax.experimental.pallas import tpu as pltpu

F32 = jnp.float32
BF16 = jnp.bfloat16

A_HEADS = 8
A_KV_GROUPS = 2
A_HPG = A_HEADS // A_KV_GROUPS
HEAD_DIM = 128
A_WIDTH = A_HEADS * HEAD_DIM
A_KV_WIDTH = A_KV_GROUPS * HEAD_DIM
CMP_BLOCK = 32
CMP_STRIDE = 16
SEL_BLOCK = 64
SEL_TOPK = 16
WINDOW = 512
ROPE_THETA = 10000.0
HG_HEADS = 8
HG_DIM = 128
HG_WIDTH = HG_HEADS * HG_DIM
HG_CHUNK = 64
CONV_WIDTH = 3
NORM_EPS = 1e-6
N_ADA = 6
A_COLS = A_WIDTH + 6 * A_KV_WIDTH + 3 * A_HEADS

LANES = 128
SUBLANES = 8
SEL_SLOTS = LANES
KEY_TILE = 256
NEG_BIG = -1e30
VMEM_LIMIT = 56 * 1024 * 1024

ADA_COLS = 1024
NORM_ROWS = 512
PROJ_ROWS = 1024
TAIL_TILE = 1024
HG_ROWS = 512
MERGE_ROWS = 512
FFN_ROWS = 512
FFN_COLS = 512


def _cparams(sem):
    return pltpu.CompilerParams(dimension_semantics=sem, vmem_limit_bytes=VMEM_LIMIT)


def _split3(a):
    hi = a.astype(BF16)
    r1 = a - hi.astype(F32)
    mid = r1.astype(BF16)
    lo = (r1 - mid.astype(F32)).astype(BF16)
    return hi, mid, lo


def _dot(a, b):
    return jnp.dot(a, b, preferred_element_type=F32)


def _dot_nt(a, b):
    return lax.dot_general(a, b, (((1,), (1,)), ((), ())), preferred_element_type=F32)


def _dot_tn(a, b):
    return lax.dot_general(a, b, (((0,), (0,)), ((), ())), preferred_element_type=F32)


def _ada_kernel(c_ref, w_ref, b_ref, o_ref):
    c = c_ref[...]
    ca = c * jax.nn.sigmoid(c)
    o_ref[0] = jnp.sum(ca * w_ref[0], axis=0, keepdims=True) + b_ref[0]


def _ada_all(c, w_ada, b_ada):
    L, D, N = w_ada.shape
    tn = ADA_COLS
    return pl.pallas_call(
        _ada_kernel,
        grid=(L, N // tn),
        in_specs=[
            pl.BlockSpec((D, 1), lambda l, n: (0, 0)),
            pl.BlockSpec((1, D, tn), lambda l, n: (l, 0, n)),
            pl.BlockSpec((1, 1, tn), lambda l, n: (l, 0, n)),
        ],
        out_specs=pl.BlockSpec((1, 1, tn), lambda l, n: (l, 0, n)),
        out_shape=jax.ShapeDtypeStruct((L, 1, N), F32),
        compiler_params=_cparams(("arbitrary", "arbitrary")),
        name="ada",
    )(c.reshape(D, 1), w_ada, b_ada.reshape(L, 1, N))


def _prenorm_kernel(x_ref, g_ref, sc_ref, sh_ref, o_ref):
    x = x_ref[...]
    ms = jnp.mean(x * x, axis=-1, keepdims=True)
    y = x * lax.rsqrt(ms + NORM_EPS) * g_ref[...]
    o_ref[...] = (y * (1.0 + sc_ref[...]) + sh_ref[...]).astype(o_ref.dtype)


def _prenorm(x, g, sc, sh):
    S, D = x.shape
    tm = NORM_ROWS
    row = pl.BlockSpec((1, D), lambda m: (0, 0))
    return pl.pallas_call(
        _prenorm_kernel,
        grid=(S // tm,),
        in_specs=[pl.BlockSpec((tm, D), lambda m: (m, 0)), row, row, row],
        out_specs=pl.BlockSpec((tm, D), lambda m: (m, 0)),
        out_shape=jax.ShapeDtypeStruct((S, D), BF16),
        compiler_params=_cparams(("arbitrary",)),
        name="prenorm",
    )(x, g.reshape(1, D), sc, sh)


def _rope(t, cosf, sinf):
    return t * cosf + pltpu.roll(t, HEAD_DIM // 2, axis=1) * sinf


_PROJ_PART = 256


def _proj_kernel(h_ref, w_ref, *rest, roles, tn):
    o_ref, w_scr = rest[-2:]
    tabs = rest[:-2]
    n = pl.program_id(0)

    @pl.when(pl.program_id(1) == 0)
    def _():
        w_scr[...] = w_ref[0].astype(BF16)

    def emit(slabs):
        h = h_ref[...]
        per = _PROJ_PART // LANES
        parts = [_dot_nt(h, w_scr[c0:c0 + _PROJ_PART, :]) for c0 in range(0, tn, _PROJ_PART)]
        for pi, acc in enumerate(parts):
            for s in range(per):
                kind = slabs[pi * per + s]
                t = acc[:, s * LANES:(s + 1) * LANES]
                if kind == "sg":
                    t = jax.nn.sigmoid(t)
                elif kind != "id":
                    cos_ref, sin_ref = tabs[0:2] if kind == "rq" else tabs[-2:]
                    t = _rope(t, cos_ref[...], sin_ref[...])
                c = (pi * per + s) * LANES
                o_ref[:, c:c + LANES] = t.astype(o_ref.dtype)

    kinds = sorted(set(roles))
    if len(kinds) == 1:
        emit(kinds[0])
        return
    for kind in kinds:
        hit = functools.reduce(jnp.logical_or, [n == t for t, r in enumerate(roles) if r == kind])
        pl.when(hit)(functools.partial(emit, kind))


def _proj(h, w_t, layer, row0, roles, out_dtype, *, tn, tables=(), name="proj"):
    S, D = h.shape
    tm = PROJ_ROWS if S % PROJ_ROWS == 0 else S
    nt = len(roles)
    tab = pl.BlockSpec((tm, HEAD_DIM), lambda n, m: (m, 0))
    return pl.pallas_call(
        functools.partial(_proj_kernel, roles=tuple(roles), tn=tn),
        grid=(nt, S // tm),
        in_specs=[pl.BlockSpec((tm, D), lambda n, m: (m, 0)),
                  pl.BlockSpec((pl.Element(1), pl.Element(tn), pl.Element(D)),
                               lambda n, m: (layer, pl.multiple_of(row0 + n * tn, SUBLANES), 0))]
                 + [tab] * len(tables),
        out_specs=pl.BlockSpec((tm, tn), lambda n, m: (m, n)),
        out_shape=jax.ShapeDtypeStruct((S, nt * tn), out_dtype),
        scratch_shapes=[pltpu.VMEM((tn, D), BF16)],
        compiler_params=_cparams(("arbitrary", "arbitrary")),
        name=name,
    )(h, w_t, *tables)


A_TILE = 1024
_A_SLABS = (("rq",) * (A_WIDTH // LANES)
            + (("rk",) * A_KV_GROUPS + ("id",) * A_KV_GROUPS) * 3
            + ("sg",) * (-(A_WIDTH // LANES + 6 * A_KV_GROUPS) % (A_TILE // LANES)))
A_ROLES = tuple(_A_SLABS[i:i + A_TILE // LANES] for i in range(0, len(_A_SLABS), A_TILE // LANES))
_KV_BLOCK0 = A_WIDTH // LANES
BLK_KC, BLK_VC, BLK_KS, BLK_VS, BLK_KW, BLK_VW = [_KV_BLOCK0 + j * A_KV_GROUPS for j in range(6)]
BLK_GATE = _KV_BLOCK0 + 6 * A_KV_GROUPS


def _compress_kernel(t_ref, pe_ref, w_ref, o_ref, t32_ref, *, n_chunk):
    half = CMP_BLOCK // 2
    t32_ref[...] = t_ref[...].astype(F32)
    a = jnp.concatenate([t32_ref[pl.ds(j, n_chunk, stride=CMP_STRIDE), :] for j in range(half)],
                        axis=1).astype(BF16)
    kw = half * HEAD_DIM
    w = w_ref[0].astype(BF16)
    y_top = _dot(a, w[:kw])
    y_bot = _dot(a, w[kw:])
    pe = jnp.concatenate([pe_ref[0, j:j + 1, :] for j in range(CMP_BLOCK)], axis=1)
    pe_term = jnp.zeros((1, HEAD_DIM), F32)
    for piece in _split3(pe):
        pe_term = pe_term + _dot(piece, w)
    o_ref[0, 0] = y_top + pltpu.roll(y_bot, n_chunk - 1, axis=0) + pe_term


def _compress(a, pe2, w2):
    S = a.shape[0]
    n_chunk = S // CMP_STRIDE
    G = A_KV_GROUPS
    return pl.pallas_call(
        functools.partial(_compress_kernel, n_chunk=n_chunk),
        grid=(2, G),
        in_specs=[
            pl.BlockSpec((S, HEAD_DIM), lambda kv, g: (0, BLK_KC + kv * G + g)),
            pl.BlockSpec((1, CMP_BLOCK, HEAD_DIM), lambda kv, g: (kv, 0, 0)),
            pl.BlockSpec((1, CMP_BLOCK * HEAD_DIM, HEAD_DIM), lambda kv, g: (kv, 0, 0)),
        ],
        out_specs=pl.BlockSpec((1, 1, n_chunk, HEAD_DIM), lambda kv, g: (kv, g, 0, 0)),
        out_shape=jax.ShapeDtypeStruct((2, G, n_chunk, HEAD_DIM), F32),
        scratch_shapes=[pltpu.VMEM((S, HEAD_DIM), F32)],
        compiler_params=_cparams(("arbitrary", "arbitrary")),
        name="compress",
    )(a, pe2, w2)


NSA_QB = 256
SEL_BODY = 8
SEL_GROUP = 4
_N_FORCED = 3


def _nsa_kernel(q_ref, ks_ref, kw_ref, vs_ref, vw_ref, kc_ref, vc_ref, ovt_ref, gate_ref, *rest, seq, n_cast):
    cast_in, rest = rest[:n_cast], rest[n_cast:]
    o_ref, rest = rest[0], rest[1:]
    cast_out, rest = rest[:n_cast], rest[n_cast:]
    kaug_ref, vt_ref, vwt_ref, vct_ref, acc_ref, m_ref, l_ref = rest
    for src, dst in zip(cast_in, cast_out):
        dst[...] = src[0].astype(dst.dtype)
    grp = pl.program_id(0)
    i = pl.program_id(1)
    t0 = i * NSA_QB
    n_cmp = seq // CMP_STRIDE
    cols = A_HPG * NSA_QB

    @pl.when(i == 0)
    def _():
        def build(r, c):
            r0 = pl.multiple_of(r * KEY_TILE, KEY_TILE)
            kaug_ref[r, :, 0:HEAD_DIM] = ks_ref[pl.ds(r0, KEY_TILE), :]
            blk = (r0 + lax.broadcasted_iota(jnp.int32, (KEY_TILE, SEL_SLOTS), 0)) // SEL_BLOCK
            slot = lax.broadcasted_iota(jnp.int32, (KEY_TILE, SEL_SLOTS), 1)
            kaug_ref[r, :, HEAD_DIM:HEAD_DIM + SEL_SLOTS] = jnp.where(blk == slot, 1.0, 0.0).astype(BF16)
            vt_ref[r] = vs_ref[pl.ds(r0, KEY_TILE), :].astype(F32).T.astype(BF16)
            vwt_ref[r] = vw_ref[pl.ds(r0, KEY_TILE), :].astype(F32).T.astype(BF16)
            return c
        lax.fori_loop(0, seq // KEY_TILE, build, 0)
        vct_ref[...] = vc_ref[0, 0].T.astype(BF16)

    qt = jnp.concatenate(
        [q_ref[:, h * HEAD_DIM:(h + 1) * HEAD_DIM].astype(F32).T.astype(BF16) for h in range(A_HPG)],
        axis=1)
    tq = t0 + lax.broadcasted_iota(jnp.int32, (1, cols), 1) % NSA_QB

    def softmax0(s):
        m = jnp.max(s, axis=0, keepdims=True)
        p = jnp.exp2(s - m)
        return p, jnp.sum(p, axis=0, keepdims=True)

    s_c = _dot(kc_ref[0, 0].astype(BF16), qt)
    nrow = lax.broadcasted_iota(jnp.int32, (n_cmp, 1), 0)
    vis = (nrow * CMP_STRIDE + (CMP_BLOCK - 1)) <= tq
    p_c, den = softmax0(jnp.where(vis, s_c, NEG_BIG))
    p_c = p_c * jnp.where(tq >= CMP_BLOCK - 1, 1.0 / den, 0.0)
    o_ct = _dot(vct_ref[...], p_c.astype(BF16))

    psum = p_c[:, 0:NSA_QB]
    for h in range(1, A_HPG):
        psum = psum + p_c[:, h * NSA_QB:(h + 1) * NSA_QB]
    ovt = ovt_ref[...]
    imp_t = jnp.zeros((SEL_SLOTS, NSA_QB), F32)
    for piece in _split3(psum):
        imp_t = imp_t + _dot(ovt, piece)
    slot_i = lax.broadcasted_iota(jnp.int32, (SEL_SLOTS, NSA_QB), 0)
    tq_l = t0 + lax.broadcasted_iota(jnp.int32, (SEL_SLOTS, NSA_QB), 1)
    cur = tq_l // SEL_BLOCK
    forced = jnp.where(slot_i == 0, 1, 0) + jnp.where(slot_i == cur, 1, 0) + jnp.where(slot_i == cur - 1, 1, 0)
    valid = slot_i * SEL_BLOCK <= tq_l
    key = jnp.where(forced > 0, -3e38, jnp.where(valid, imp_t, NEG_BIG))
    slot_f = slot_i.astype(F32)
    bias_t = jnp.where(forced > 0, 0.0, NEG_BIG)
    for _ in range(SEL_TOPK - _N_FORCED):
        mx = jnp.max(key, axis=0, keepdims=True)
        first = jnp.min(jnp.where(key == mx, slot_f, 1e9), axis=0, keepdims=True)
        pick = slot_f == first
        bias_t = jnp.where(pick, 0.0, bias_t)
        key = jnp.where(pick, -3e38, key)

    n_wt = WINDOW // KEY_TILE + 1
    assert WINDOW % KEY_TILE == 0 and n_wt == 3
    kk = lax.broadcasted_iota(jnp.int32, (KEY_TILE, 1), 0)
    qq = tq - t0
    tri = kk <= qq
    tri_far = kk <= jnp.where(i >= 2, qq, KEY_TILE)
    w_idx = [jnp.maximum(i - 2, 0), jnp.maximum(i - 1, 0), i]
    s_w = [_dot(kw_ref[pl.ds(pl.multiple_of(w * KEY_TILE, KEY_TILE), KEY_TILE), :], qt) for w in w_idx]
    s_w = jnp.concatenate([jnp.where(tri_far, NEG_BIG, s_w[0]),
                           jnp.where(i >= 1, s_w[1], NEG_BIG),
                           jnp.where(tri, s_w[2], NEG_BIG)], axis=0)
    p_w, l_w = softmax0(s_w)
    p_w = p_w.astype(BF16)
    o_wt = _dot(vwt_ref[w_idx[0]], p_w[0:KEY_TILE])
    for k in range(1, n_wt):
        o_wt = o_wt + _dot(vwt_ref[w_idx[k]], p_w[k * KEY_TILE:(k + 1) * KEY_TILE])
    o_wt = o_wt * (1.0 / l_w)

    qt_aug = jnp.concatenate([qt, jnp.concatenate([bias_t.astype(BF16)] * A_HPG, axis=1)], axis=0)
    acc_ref[...] = jnp.zeros_like(acc_ref)
    m_ref[...] = jnp.full(m_ref.shape, 0.5 * NEG_BIG, F32)
    l_ref[...] = jnp.zeros_like(l_ref)

    def sel_body(base, causal):
        n = len(causal)
        m, l = m_ref[...], l_ref[...]
        gsz = min(SEL_GROUP, max(n // 2, 1))
        groups = [list(range(g0, min(g0 + gsz, n))) for g0 in range(0, n, gsz)]

        def scores(ks):
            out = {}
            for k in ks:
                sk = _dot(kaug_ref[base + k], qt_aug)
                if causal[k]:
                    kpos = (base + k) * KEY_TILE + lax.broadcasted_iota(jnp.int32, (KEY_TILE, 1), 0)
                    sk = jnp.where(kpos <= tq, sk, NEG_BIG)
                out[k] = sk
            return out

        s = scores(groups[0])
        for gi, ks in enumerate(groups):
            if gi + 1 < len(groups):
                s.update(scores(groups[gi + 1]))
            m_new = m
            for k in ks:
                m_new = jnp.maximum(m_new, jnp.max(s[k], axis=0, keepdims=True))
            alpha = jnp.exp2(m - m_new)
            l = alpha * l
            pv = None
            for k in ks:
                p = jnp.exp2(s.pop(k) - m_new)
                l = l + jnp.sum(p, axis=0, keepdims=True)
                d = _dot(vt_ref[base + k], p.astype(BF16))
                pv = d if pv is None else pv + d
            acc_ref[...] = alpha * acc_ref[...] + pv
            m = m_new
        m_ref[...] = m
        l_ref[...] = l

    n_bulk = i // SEL_BODY
    rem = i % SEL_BODY

    def bulk(b, c):
        sel_body(b * SEL_BODY, (False,) * SEL_BODY)
        return c
    lax.fori_loop(0, n_bulk, bulk, 0)

    base = n_bulk * SEL_BODY
    part = SEL_BODY // 2
    while part >= 2:
        pl.when(rem & part != 0)(functools.partial(sel_body, base, (False,) * part))
        base = base + (rem & part)
        part //= 2

    @pl.when(rem % 2 == 1)
    def _():
        sel_body(i - 1, (False, True))

    @pl.when(rem % 2 == 0)
    def _():
        sel_body(i, (True,))

    o_st = acc_ref[...] * (1.0 / l_ref[...])

    gates_t = gate_ref[...].astype(F32).T
    per_group = 3 * A_HPG

    def gate(h, b):
        r = 3 * h + b
        return jnp.where(grp == 0, gates_t[r:r + 1], gates_t[per_group + r:per_group + r + 1])

    for h in range(A_HPG):
        c = slice(h * NSA_QB, (h + 1) * NSA_QB)
        o = gate(h, 0) * o_ct[:, c] + gate(h, 1) * o_st[:, c] + gate(h, 2) * o_wt[:, c]
        o_ref[:, h * HEAD_DIM:(h + 1) * HEAD_DIM] = o.T.astype(o_ref.dtype)


def _overlap_matrix(seq):
    n_chunk = seq // CMP_STRIDE
    cs = np.arange(n_chunk)[:, None] * CMP_STRIDE
    ss = np.arange(SEL_SLOTS)[None, :] * SEL_BLOCK
    ov = np.maximum(np.minimum(cs + CMP_BLOCK, ss + SEL_BLOCK) - np.maximum(cs, ss), 0) / CMP_BLOCK
    n_cmp = (seq - CMP_BLOCK) // CMP_STRIDE + 1
    ov[n_cmp:] = 0.0
    ov[:, seq // SEL_BLOCK:] = 0.0
    return ov.astype(np.float32)


def _nsa(a, cmp_kv, cast=(), layer=0):
    S = a.shape[0]
    G = A_KV_GROUPS
    steps = S // NSA_QB
    for w in cast:
        assert w.shape[1] % (steps * 2 * SUBLANES) == 0
    slab = lambda g, i: jnp.where(g == 0, i, steps - 1)
    cast_in = [pl.BlockSpec((1, w.shape[1] // steps, w.shape[2]), lambda g, i: (layer, slab(g, i), 0))
               for w in cast]
    cast_out = [pl.BlockSpec((w.shape[1] // steps, w.shape[2]), lambda g, i: (slab(g, i), 0)) for w in cast]
    cast_shape = [jax.ShapeDtypeStruct(w.shape[1:], BF16) for w in cast]
    n_tiles = S // KEY_TILE
    assert S // SEL_BLOCK <= SEL_SLOTS and S % KEY_TILE == 0 and S >= WINDOW + NSA_QB
    assert NSA_QB == KEY_TILE and G == 2
    n_chunk = S // CMP_STRIDE
    ovt = jnp.asarray(_overlap_matrix(S).T, BF16)
    col = lambda b: pl.BlockSpec((S, HEAD_DIM), lambda g, i, b=b: (0, b + g))
    out = pl.pallas_call(
        functools.partial(_nsa_kernel, seq=S, n_cast=len(cast)),
        grid=(G, steps),
        in_specs=[
            pl.BlockSpec((NSA_QB, A_HPG * HEAD_DIM), lambda g, i: (i, g)),
            col(BLK_KS), col(BLK_KW), col(BLK_VS), col(BLK_VW),
            pl.BlockSpec((1, 1, n_chunk, HEAD_DIM), lambda g, i: (0, g, 0, 0)),
            pl.BlockSpec((1, 1, n_chunk, HEAD_DIM), lambda g, i: (1, g, 0, 0)),
            pl.BlockSpec((SEL_SLOTS, n_chunk), lambda g, i: (0, 0)),
            pl.BlockSpec((NSA_QB, LANES), lambda g, i: (i, BLK_GATE)),
        ] + cast_in,
        out_specs=[pl.BlockSpec((NSA_QB, A_HPG * HEAD_DIM), lambda g, i: (i, g))] + cast_out,
        out_shape=[jax.ShapeDtypeStruct((S, A_WIDTH), BF16)] + cast_shape,
        scratch_shapes=[pltpu.VMEM((n_tiles, KEY_TILE, HEAD_DIM + SEL_SLOTS), BF16),
                        pltpu.VMEM((n_tiles, HEAD_DIM, KEY_TILE), BF16),
                        pltpu.VMEM((n_tiles, HEAD_DIM, KEY_TILE), BF16),
                        pltpu.VMEM((HEAD_DIM, n_chunk), BF16),
                        pltpu.VMEM((HEAD_DIM, A_HPG * NSA_QB), F32),
                        pltpu.VMEM((1, A_HPG * NSA_QB), F32),
                        pltpu.VMEM((1, A_HPG * NSA_QB), F32)],
        compiler_params=_cparams(("arbitrary", "arbitrary")),
        name="nsa",
    )(a, a, a, a, a, cmp_kv, cmp_kv, ovt, a, *cast)
    return out[0] if not cast else out


_ROW_MASKS = 18
_SCORE_MASKS = 10
HG_HEADS_PER_STEP = 8


def _hgrn_tables():
    t = np.arange(HG_CHUNK)
    p4, i4, i16 = t % 4, (t // 4) % 4, t // 16
    conds = [p4 >= 1, p4 >= 2, p4 >= 3, p4 <= 2, p4 <= 1, p4 == 0,
             i4 >= 1, i4 >= 2, i4 >= 3, i4 <= 2, i4 <= 1, i4 == 0,
             i16 >= 1, i16 >= 2, i16 >= 3, i16 <= 2, i16 <= 1, i16 == 0]
    rm = np.stack([np.broadcast_to(c[:, None], (HG_CHUNK, HG_DIM)) for c in conds]).astype(np.float32)
    tt, ss = t[:, None], t[None, :]
    lag16 = tt // 16 - ss // 16
    lag4 = tt // 4 - ss // 4
    same16, same4 = lag16 == 0, lag4 == 0
    sm = [lag16 == 1, lag16 == 2, lag16 == 3,
          same16 & (lag4 == 1), same16 & (lag4 == 2), same16 & (lag4 == 3),
          same4 & (tt - ss == 0), same4 & (tt - ss == 1), same4 & (tt - ss == 2), same4 & (tt - ss == 3)]
    return rm, np.stack(sm).astype(np.float32)


def _hgrn_kernel(q_ref, f_ref, i_ref, og_ref, loglb_ref, log1m_ref, gn_ref, rm_ref, sm_ref, *rest, tb, nh, n_cast):
    cast_in, rest = rest[:n_cast], rest[n_cast:]
    o_ref, rest = rest[0], rest[1:]
    cast_out, rest = rest[:n_cast], rest[n_cast:]
    st_ref, g_scr, k_scr, o_scr = rest
    for src, dst in zip(cast_in, cast_out):
        dst[...] = src[0].astype(dst.dtype)
    C = HG_CHUNK
    heads = [slice(h * HG_DIM, (h + 1) * HG_DIM) for h in range(nh)]

    @pl.when(pl.program_id(1) == 0)
    def _():
        st_ref[...] = jnp.zeros_like(st_ref)

    x = f_ref[...]
    log_sig = jnp.minimum(x, 0.0) - jnp.log(1.0 + jnp.exp(-jnp.abs(x)))
    a = loglb_ref[...]
    c = log1m_ref[...] + log_sig
    log_f = jnp.maximum(a, c) + jnp.log(1.0 + jnp.exp(-jnp.abs(a - c)))
    g_scr[...] = log_f
    k_scr[...] = jnp.exp(log1m_ref[...] + (log_sig - x))

    def rmask(n):
        return jnp.concatenate([rm_ref[n]] * nh, axis=1)

    def roll(v, s):
        return pltpu.roll(v, s % C, axis=0)

    def b16(z):
        return z.astype(BF16)

    def chunk(ci, carry):
        r0 = pl.multiple_of(ci * C, C)
        g = g_scr[pl.ds(r0, C), :]
        q = q_ref[pl.ds(r0, C), :]
        k = k_scr[pl.ds(r0, C), :]
        v = i_ref[pl.ds(r0, C), :].astype(BF16)
        pre4 = g + rmask(0) * roll(g, 1) + rmask(1) * roll(g, 2) + rmask(2) * roll(g, 3)
        suf4 = rmask(3) * roll(g, -1) + rmask(4) * roll(g, -2) + rmask(5) * roll(g, -3)
        tot4 = pre4 + suf4
        a1, a2, a3 = roll(tot4, 4), roll(tot4, 8), roll(tot4, 12)
        pre16 = pre4 + rmask(6) * a1 + rmask(7) * a2 + rmask(8) * a3
        suf16 = suf4 + rmask(9) * roll(tot4, -4) + rmask(10) * roll(tot4, -8) + rmask(11) * roll(tot4, -12)
        tot16 = pre16 + suf16
        d1, d2, d3 = roll(tot16, 16), roll(tot16, 32), roll(tot16, 48)
        pre64 = pre16 + rmask(12) * d1 + rmask(13) * d2 + rmask(14) * d3
        suf64 = suf16 + rmask(15) * roll(tot16, -16) + rmask(16) * roll(tot16, -32) + rmask(17) * roll(tot16, -48)
        tot64 = pre64 + suf64
        g1, g2 = roll(g, 1), roll(g, 2)

        q_a = q * jnp.exp(pre16)
        lhs_a = [b16(q_a), b16(q_a * jnp.exp(d1)), b16(q_a * jnp.exp(d1 + d2))]
        k_a = b16(k * jnp.exp(suf16))
        q_b = q * jnp.exp(pre4)
        lhs_b = [b16(q_b), b16(q_b * jnp.exp(a1)), b16(q_b * jnp.exp(a1 + a2))]
        k_b = b16(k * jnp.exp(suf4))
        q_c = q * jnp.exp(g)
        lhs_c = [b16(q), b16(q_c), b16(q_c * jnp.exp(g1)), b16(q_c * jnp.exp(g1 + g2))]
        k_c = b16(k)
        q_s = b16(q * jnp.exp(pre64))
        k_s = b16(k * jnp.exp(suf64))
        decay = jnp.exp(tot64[0:1, :])

        def level(lhs, rhs):
            return [_dot_nt(jnp.concatenate([z[:, hh] for z in lhs], axis=0), rhs[:, hh]) for hh in heads]

        s_a, s_b, s_c = level(lhs_a, k_a), level(lhs_b, k_b), level(lhs_c, k_c)
        o_inter = [_dot_nt(q_s[:, hh], b16(st_ref[h])) for h, hh in enumerate(heads)]
        kv = [_dot_tn(v[:, hh], k_s[:, hh]) for hh in heads]
        for h, hh in enumerate(heads):
            sa, sb, sc = s_a[h], s_b[h], s_c[h]
            scores = (sm_ref[0] * sa[0:C] + sm_ref[1] * sa[C:2 * C] + sm_ref[2] * sa[2 * C:3 * C]
                      + sm_ref[3] * sb[0:C] + sm_ref[4] * sb[C:2 * C] + sm_ref[5] * sb[2 * C:3 * C]
                      + sm_ref[6] * sc[0:C] + sm_ref[7] * sc[C:2 * C] + sm_ref[8] * sc[2 * C:3 * C]
                      + sm_ref[9] * sc[3 * C:4 * C])
            o_scr[pl.ds(r0, C), hh] = _dot(b16(scores), v[:, hh]) + o_inter[h]
            st_ref[h] = st_ref[h] * decay[:, hh] + kv[h]
        return carry

    lax.fori_loop(0, tb // C, chunk, 0)

    for hh in heads:
        o = o_scr[:, hh]
        ms = jnp.mean(o * o, axis=-1, keepdims=True)
        og = og_ref[:, hh]
        y = o * lax.rsqrt(ms + NORM_EPS) * gn_ref[:, hh] * (og * jax.nn.sigmoid(og))
        o_ref[:, hh] = y.astype(o_ref.dtype)


def _hgrn(hz, loglb, log1mlb, gnorm, cast=(), layer=0):
    S = hz.shape[0]
    tb = HG_ROWS if S % HG_ROWS == 0 else S
    nh = HG_HEADS_PER_STEP
    ng = HG_HEADS // nh
    steps = S // tb
    for w in cast:
        assert w.shape[1] % (steps * 2 * SUBLANES) == 0
    slab = lambda h, c: jnp.where(h == 0, c, steps - 1)
    cast_in = [pl.BlockSpec((1, w.shape[1] // steps, w.shape[2]), lambda h, c: (layer, slab(h, c), 0))
               for w in cast]
    cast_out = [pl.BlockSpec((w.shape[1] // steps, w.shape[2]), lambda h, c: (slab(h, c), 0)) for w in cast]
    cast_shape = [jax.ShapeDtypeStruct(w.shape[1:], BF16) for w in cast]
    w = nh * HG_DIM
    rm, sm = _hgrn_tables()
    blk = lambda part: pl.BlockSpec((tb, w), lambda h, c, part=part: (c, part * ng + h))
    row = pl.BlockSpec((1, w), lambda h, c: (0, h))
    out = pl.pallas_call(
        functools.partial(_hgrn_kernel, tb=tb, nh=nh, n_cast=len(cast)),
        grid=(ng, steps),
        in_specs=[blk(0), blk(1), blk(2), blk(3), row, row, row,
                  pl.BlockSpec((_ROW_MASKS, HG_CHUNK, HG_DIM), lambda h, c: (0, 0, 0)),
                  pl.BlockSpec((_SCORE_MASKS, HG_CHUNK, HG_CHUNK), lambda h, c: (0, 0, 0))] + cast_in,
        out_specs=[pl.BlockSpec((tb, w), lambda h, c: (c, h))] + cast_out,
        out_shape=[jax.ShapeDtypeStruct((S, HG_WIDTH), BF16)] + cast_shape,
        scratch_shapes=[pltpu.VMEM((nh, HG_DIM, HG_DIM), F32), pltpu.VMEM((tb, w), F32),
                        pltpu.VMEM((tb, w), F32), pltpu.VMEM((tb, w), F32)],
        compiler_params=_cparams(("arbitrary", "arbitrary")),
        name="hgrn",
    )(hz, hz, hz, hz, loglb, log1mlb, gnorm, jnp.asarray(rm), jnp.asarray(sm), *cast)
    return out[0] if not cast else out


_MERGE_PART = 512


def _merge_kernel(at_ref, hg_ref, ga_ref, gh_ref, wa_ref, wh_ref, wo_ref, x_ref, gt_ref, gp_ref, o_ref):
    at, hg = at_ref[...], hg_ref[...]
    d = wo_ref.shape[1]
    parts = [slice(c0, c0 + _MERGE_PART) for c0 in range(0, d, _MERGE_PART)]
    branch = [(_dot(at, wa_ref[0, :, cs]), _dot(hg, wh_ref[0, :, cs])) for cs in parts]
    y = None
    for cs, (ma, mh) in zip(parts, branch):
        mixed = ga_ref[:, cs].astype(F32) * ma + gh_ref[:, cs].astype(F32) * mh
        yp = _dot(mixed.astype(BF16), wo_ref[0, cs, :])
        y = yp if y is None else y + yp
    ms = jnp.mean(y * y, axis=-1, keepdims=True)
    yn = y * lax.rsqrt(ms + NORM_EPS) * gp_ref[...]
    o_ref[...] = x_ref[...] + gt_ref[...] * yn


def _merge(attn, hg, mg, wa, wh, wo, layer, x, gt, gpost):
    S, D = x.shape
    tm = MERGE_ROWS
    const = lambda shape: pl.BlockSpec((1,) + shape, lambda m: (layer, 0, 0), pipeline_mode=pl.Buffered(1))
    row = pl.BlockSpec((1, D), lambda m: (0, 0))
    return pl.pallas_call(
        _merge_kernel,
        grid=(S // tm,),
        in_specs=[
            pl.BlockSpec((tm, A_WIDTH), lambda m: (m, 0)),
            pl.BlockSpec((tm, HG_WIDTH), lambda m: (m, 0)),
            pl.BlockSpec((tm, D), lambda m: (m, 0)),
            pl.BlockSpec((tm, D), lambda m: (m, 1)),
            const((A_WIDTH, D)), const((HG_WIDTH, D)), const((D, D)),
            pl.BlockSpec((tm, D), lambda m: (m, 0)),
            row, row,
        ],
        out_specs=pl.BlockSpec((tm, D), lambda m: (m, 0)),
        out_shape=jax.ShapeDtypeStruct((S, D), F32),
        compiler_params=_cparams(("arbitrary",)),
        name="merge",
    )(attn, hg, mg, mg, wa, wh, wo, x, gt, gpost.reshape(1, D))


_HALO = 16
_FF_SUB = 256


def _norm_mod(x, g_ref, sc_ref, sh_ref):
    ms = jnp.mean(x * x, axis=-1, keepdims=True)
    y = x * lax.rsqrt(ms + NORM_EPS) * g_ref[...]
    return (y * (1.0 + sc_ref[...]) + sh_ref[...]).astype(BF16)


def _ffn_kernel(x_ref, xh_ref, g_ref, sc_ref, sh_ref, wg_ref, wv_ref, cwg_ref, cwv_ref, cbg_ref, cbv_ref,
                wd_ref, gt_ref, gp_ref, *rest, tm, emit_next):
    if emit_next:
        gn_ref, scn_ref, shn_ref, o_ref, hn_ref, h_scr, ug_scr, uv_scr = rest
    else:
        o_ref, h_scr, ug_scr, uv_scr = rest
    m = pl.program_id(0)
    f = pl.program_id(1)
    norm_mod = functools.partial(_norm_mod, g_ref=g_ref, sc_ref=sc_ref, sh_ref=sh_ref)

    @pl.when(f == 0)
    def _():
        h_scr[0:_HALO, :] = jnp.where(m > 0, norm_mod(xh_ref[...]), jnp.zeros((), BF16))
        h_scr[_HALO:_HALO + tm, :] = norm_mod(x_ref[...])

    tf = wg_ref.shape[2]
    subs = [slice(s * _FF_SUB, (s + 1) * _FF_SUB) for s in range(tf // _FF_SUB)]
    h = h_scr[...]
    ups = [(_dot(h, wg_ref[0, :, cs]), _dot(h, wv_ref[0, :, cs])) for cs in subs]

    def conv(u_scr, cw_ref, cb_ref, cs):
        out = cb_ref[:, cs] + cw_ref[0:1, cs] * u_scr[pl.ds(_HALO - 2, tm), cs]
        out = out + cw_ref[1:2, cs] * u_scr[pl.ds(_HALO - 1, tm), cs]
        return out + cw_ref[2:3, cs] * u_scr[pl.ds(_HALO, tm), cs]

    acc = jnp.where(f > 0, o_ref[...], 0.0)
    for cs, (ug, uv) in zip(subs, ups):
        ug_scr[:, cs] = ug
        uv_scr[:, cs] = uv
        cg = conv(ug_scr, cwg_ref, cbg_ref, cs)
        cv = conv(uv_scr, cwv_ref, cbv_ref, cs)
        act = (cg * jax.nn.sigmoid(cg) * cv).astype(BF16)
        acc = acc + _dot(act, wd_ref[0, cs, :])
    o_ref[...] = acc

    @pl.when(f == pl.num_programs(1) - 1)
    def _():
        y = o_ref[...]
        ms = jnp.mean(y * y, axis=-1, keepdims=True)
        yn = y * lax.rsqrt(ms + NORM_EPS) * gp_ref[...]
        x_new = x_ref[...] + gt_ref[...] * yn
        o_ref[...] = x_new
        if emit_next:
            hn_ref[...] = _norm_mod(x_new, gn_ref, scn_ref, shn_ref)


def _ffn(x, gpre, sc, sh, wup, convw, convb, wdown, layer, gt, gpost, nxt=None):
    S, D = x.shape
    F = wdown.shape[1]
    tm = FFN_ROWS if S % FFN_ROWS == 0 else S
    tf = FFN_COLS
    nf = F // tf
    hb = tm // _HALO
    row = pl.BlockSpec((1, D), lambda m, f: (0, 0))
    tile = pl.BlockSpec((tm, D), lambda m, f: (m, 0))
    extra = () if nxt is None else (nxt[0].reshape(1, D), nxt[1], nxt[2])
    return pl.pallas_call(
        functools.partial(_ffn_kernel, tm=tm, emit_next=nxt is not None),
        grid=(S // tm, nf),
        in_specs=[
            pl.BlockSpec((tm, D), lambda m, f: (m, 0)),
            pl.BlockSpec((_HALO, D), lambda m, f: (jnp.maximum(m * hb - 1, 0), 0)),
            row, row, row,
            pl.BlockSpec((1, D, tf), lambda m, f: (layer, 0, f)),
            pl.BlockSpec((1, D, tf), lambda m, f: (layer, 0, nf + f)),
            pl.BlockSpec((CONV_WIDTH, tf), lambda m, f: (0, f)),
            pl.BlockSpec((CONV_WIDTH, tf), lambda m, f: (0, nf + f)),
            pl.BlockSpec((1, tf), lambda m, f: (0, f)),
            pl.BlockSpec((1, tf), lambda m, f: (0, nf + f)),
            pl.BlockSpec((1, tf, D), lambda m, f: (layer, f, 0)),
            row, row,
        ] + ([row] * 3 if nxt is not None else []),
        out_specs=[tile, tile] if nxt is not None else tile,
        out_shape=([jax.ShapeDtypeStruct((S, D), F32), jax.ShapeDtypeStruct((S, D), BF16)]
                   if nxt is not None else jax.ShapeDtypeStruct((S, D), F32)),
        scratch_shapes=[pltpu.VMEM((tm + _HALO, D), BF16),
                        pltpu.VMEM((tm + _HALO, tf), F32), pltpu.VMEM((tm + _HALO, tf), F32)],
        compiler_params=_cparams(("arbitrary", "arbitrary")),
        name="ffn",
    )(x, x, gpre.reshape(1, D), sc, sh, wup, wup, convw, convw, convb.reshape(1, -1), convb.reshape(1, -1),
      wdown, gt, gpost.reshape(1, D), *extra)


def _rope_tables(positions):
    inv_freq = 1.0 / (ROPE_THETA ** (jnp.arange(0, HEAD_DIM, 2, dtype=F32) / HEAD_DIM))
    ang = positions.astype(F32)[:, None] * inv_freq
    cos, sin = jnp.cos(ang), jnp.sin(ang)
    return jnp.concatenate([cos, cos], axis=-1), jnp.concatenate([-sin, sin], axis=-1)


def kernel(x, c, positions, w_ada, b_ada, g_pre_mix, w_in, pe_kc, w_kc, pe_vc, w_vc, lb_logits, g_hg_norm,
           w_br_attn, w_br_hgrn, w_out, g_post_mix, g_pre_ffn, w_up, conv_w, conv_b, w_down, g_post_ffn):
    B, S, D = x.shape
    assert B == 1, "kernel is written for one sequence"
    L = w_ada.shape[0]
    xs = x[0]
    cosf, sinf = _rope_tables(positions[0])
    scale = HEAD_DIM ** -0.5 * np.log2(np.e)
    tables = (cosf * scale, sinf * scale, cosf, sinf)
    lb_cum = jnp.cumsum(jax.nn.softmax(lb_logits.astype(F32), axis=0), axis=0)
    lower = lb_cum - lb_cum[0:1]
    log_lb = jnp.log(lower)
    log_1m = jnp.log1p(-lower)
    ada = _ada_all(c, w_ada, b_ada)

    w_in_t = jnp.swapaxes(w_in, 1, 2)
    hg_roles = (("id",) * (TAIL_TILE // LANES),) * (4 * HG_WIDTH // TAIL_TILE)
    mg_roles = (("sg",) * (TAIL_TILE // LANES),) * (2 * D // TAIL_TILE)


    mods = [[ada[l, :, j * D:(j + 1) * D] for j in range(N_ADA)] for l in range(L)]
    h = _prenorm(xs, g_pre_mix[0], mods[0][1], mods[0][0])
    for l in range(L):
        sh1, sc1, gt1, sh2, sc2, gt2 = mods[l]

        a = _proj(h, w_in_t, l, 0, A_ROLES, BF16, tn=A_TILE, tables=tables, name="proj_attn")
        hz = _proj(h, w_in_t, l, A_COLS, hg_roles, F32, tn=TAIL_TILE, name="proj_hgrn")
        mg = _proj(h, w_in_t, l, A_COLS + 4 * HG_WIDTH, mg_roles, BF16, tn=TAIL_TILE, name="proj_merge_gate")

        cmp_kv = _compress(a, jnp.stack([pe_kc[l], pe_vc[l]]), jnp.stack([w_kc[l], w_vc[l]]))
        attn, wup_b, wdown_b = _nsa(a, cmp_kv, cast=(w_up, w_down), layer=l)
        hg, wa_b, wh_b, wo_b = _hgrn(hz, log_lb[l:l + 1], log_1m[l:l + 1], g_hg_norm[l:l + 1],
                                     cast=(w_br_attn, w_br_hgrn, w_out), layer=l)
        xs = _merge(attn, hg, mg, wa_b[None], wh_b[None], wo_b[None], 0, xs, gt1, g_post_mix[l])

        nxt = (g_pre_mix[l + 1], mods[l + 1][1], mods[l + 1][0]) if l + 1 < L else None
        out = _ffn(xs, g_pre_ffn[l], sc2, sh2, wup_b[None], conv_w[l], conv_b[l], wdown_b[None], 0, gt2,
                   g_post_ffn[l], nxt)
        xs, h = out if nxt is not None else (out, None)
    return xs[None]
```

```python
import functools

import jax
import jax.numpy as jnp
import numpy as np
from jax import lax
from jax.experimental import pallas as pl
from jax.experimental.pallas import tpu as pltpu

F32 = jnp.float32
BF16 = jnp.bfloat16

A_HEADS = 8
A_KV_GROUPS = 2
A_HPG = A_HEADS // A_KV_GROUPS
HEAD_DIM = 128
A_WIDTH = A_HEADS * HEAD_DIM
A_KV_WIDTH = A_KV_GROUPS * HEAD_DIM
CMP_BLOCK = 32
CMP_STRIDE = 16
SEL_BLOCK = 64
SEL_TOPK = 16
WINDOW = 512
ROPE_THETA = 10000.0
HG_HEADS = 8
HG_DIM = 128
HG_WIDTH = HG_HEADS * HG_DIM
HG_CHUNK = 64
CONV_WIDTH = 3
NORM_EPS = 1e-6
N_ADA = 6
A_COLS = A_WIDTH + 6 * A_KV_WIDTH + 3 * A_HEADS

LANES = 128
SUBLANES = 8
SEL_SLOTS = LANES
KEY_TILE = 256
NEG_BIG = -1e30
VMEM_LIMIT = 56 * 1024 * 1024

ADA_COLS = 1024
NORM_ROWS = 512
PROJ_ROWS = 1024
TAIL_TILE = 1024
HG_ROWS = 512
MERGE_ROWS = 512
FFN_ROWS = 512
FFN_COLS = 512


def _cparams(sem):
    return pltpu.CompilerParams(dimension_semantics=sem, vmem_limit_bytes=VMEM_LIMIT)


def _split3(a):
    hi = a.astype(BF16)
    r1 = a - hi.astype(F32)
    mid = r1.astype(BF16)
    lo = (r1 - mid.astype(F32)).astype(BF16)
    return hi, mid, lo


def _dot(a, b):
    return jnp.dot(a, b, preferred_element_type=F32)


def _dot_nt(a, b):
    return lax.dot_general(a, b, (((1,), (1,)), ((), ())), preferred_element_type=F32)


def _dot_tn(a, b):
    return lax.dot_general(a, b, (((0,), (0,)), ((), ())), preferred_element_type=F32)


def _ada_kernel(c_ref, w_ref, b_ref, o_ref):
    c = c_ref[...]
    ca = c * jax.nn.sigmoid(c)
    o_ref[0] = jnp.sum(ca * w_ref[0], axis=0, keepdims=True) + b_ref[0]


def _ada_all(c, w_ada, b_ada, layers):
    _, D, N = w_ada.shape
    L = layers
    tn = ADA_COLS
    return pl.pallas_call(
        _ada_kernel,
        grid=(L, N // tn),
        in_specs=[
            pl.BlockSpec((D, 1), lambda l, n: (0, 0)),
            pl.BlockSpec((1, D, tn), lambda l, n: (l, 0, n)),
            pl.BlockSpec((1, 1, tn), lambda l, n: (l, 0, n)),
        ],
        out_specs=pl.BlockSpec((1, 1, tn), lambda l, n: (l, 0, n)),
        out_shape=jax.ShapeDtypeStruct((L, 1, N), F32),
        compiler_params=_cparams(("arbitrary", "arbitrary")),
        name="ada",
    )(c.reshape(D, 1), w_ada, b_ada.reshape(b_ada.shape[0], 1, N))


def _prenorm_kernel(x_ref, g_ref, sc_ref, sh_ref, o_ref):
    x = x_ref[...]
    ms = jnp.mean(x * x, axis=-1, keepdims=True)
    y = x * lax.rsqrt(ms + NORM_EPS) * g_ref[...]
    o_ref[...] = (y * (1.0 + sc_ref[...]) + sh_ref[...]).astype(o_ref.dtype)


def _prenorm(x, g, sc, sh):
    S, D = x.shape
    tm = NORM_ROWS
    row = pl.BlockSpec((1, D), lambda m: (0, 0))
    return pl.pallas_call(
        _prenorm_kernel,
        grid=(S // tm,),
        in_specs=[pl.BlockSpec((tm, D), lambda m: (m, 0)), row, row, row],
        out_specs=pl.BlockSpec((tm, D), lambda m: (m, 0)),
        out_shape=jax.ShapeDtypeStruct((S, D), BF16),
        compiler_params=_cparams(("arbitrary",)),
        name="prenorm",
    )(x, g.reshape(1, D), sc, sh)


def _rope(t, cosf, sinf):
    return t * cosf + pltpu.roll(t, HEAD_DIM // 2, axis=1) * sinf


_PROJ_PART = 256


def _proj_kernel(h_ref, w_ref, *rest, roles, tn):
    o_ref, w_scr = rest[-2:]
    tabs = rest[:-2]
    n = pl.program_id(0)

    @pl.when(pl.program_id(1) == 0)
    def _():
        w_scr[...] = w_ref[0].astype(BF16)

    def emit(slabs):
        h = h_ref[...]
        per = _PROJ_PART // LANES
        parts = [_dot_nt(h, w_scr[c0:c0 + _PROJ_PART, :]) for c0 in range(0, tn, _PROJ_PART)]
        for pi, acc in enumerate(parts):
            for s in range(per):
                kind = slabs[pi * per + s]
                t = acc[:, s * LANES:(s + 1) * LANES]
                if kind == "sg":
                    t = jax.nn.sigmoid(t)
                elif kind != "id":
                    cos_ref, sin_ref = tabs[0:2] if kind == "rq" else tabs[-2:]
                    t = _rope(t, cos_ref[...], sin_ref[...])
                c = (pi * per + s) * LANES
                o_ref[:, c:c + LANES] = t.astype(o_ref.dtype)

    kinds = sorted(set(roles))
    if len(kinds) == 1:
        emit(kinds[0])
        return
    for kind in kinds:
        hit = functools.reduce(jnp.logical_or, [n == t for t, r in enumerate(roles) if r == kind])
        pl.when(hit)(functools.partial(emit, kind))


def _proj(h, w_t, layer, row0, roles, out_dtype, *, tn, tables=(), name="proj"):
    S, D = h.shape
    tm = PROJ_ROWS if S % PROJ_ROWS == 0 else S
    nt = len(roles)
    tab = pl.BlockSpec((tm, HEAD_DIM), lambda n, m: (m, 0))
    return pl.pallas_call(
        functools.partial(_proj_kernel, roles=tuple(roles), tn=tn),
        grid=(nt, S // tm),
        in_specs=[pl.BlockSpec((tm, D), lambda n, m: (m, 0)),
                  pl.BlockSpec((pl.Element(1), pl.Element(tn), pl.Element(D)),
                               lambda n, m: (layer, pl.multiple_of(row0 + n * tn, SUBLANES), 0))]
                 + [tab] * len(tables),
        out_specs=pl.BlockSpec((tm, tn), lambda n, m: (m, n)),
        out_shape=jax.ShapeDtypeStruct((S, nt * tn), out_dtype),
        scratch_shapes=[pltpu.VMEM((tn, D), BF16)],
        compiler_params=_cparams(("arbitrary", "arbitrary")),
        name=name,
    )(h, w_t, *tables)


A_TILE = 1024
_A_SLABS = (("rq",) * (A_WIDTH // LANES)
            + (("rk",) * A_KV_GROUPS + ("id",) * A_KV_GROUPS) * 3
            + ("sg",) * (-(A_WIDTH // LANES + 6 * A_KV_GROUPS) % (A_TILE // LANES)))
A_ROLES = tuple(_A_SLABS[i:i + A_TILE // LANES] for i in range(0, len(_A_SLABS), A_TILE // LANES))
_KV_BLOCK0 = A_WIDTH // LANES
BLK_KC, BLK_VC, BLK_KS, BLK_VS, BLK_KW, BLK_VW = [_KV_BLOCK0 + j * A_KV_GROUPS for j in range(6)]
BLK_GATE = _KV_BLOCK0 + 6 * A_KV_GROUPS


def _compress_kernel(t_ref, pe_ref, w_ref, o_ref, t32_ref, *, n_chunk):
    half = CMP_BLOCK // 2
    t32_ref[...] = t_ref[...].astype(F32)
    a = jnp.concatenate([t32_ref[pl.ds(j, n_chunk, stride=CMP_STRIDE), :] for j in range(half)],
                        axis=1).astype(BF16)
    kw = half * HEAD_DIM
    w = w_ref[0].astype(BF16)
    y_top = _dot(a, w[:kw])
    y_bot = _dot(a, w[kw:])
    pe = jnp.concatenate([pe_ref[0, j:j + 1, :] for j in range(CMP_BLOCK)], axis=1)
    pe_term = jnp.zeros((1, HEAD_DIM), F32)
    for piece in _split3(pe):
        pe_term = pe_term + _dot(piece, w)
    o_ref[0, 0] = y_top + pltpu.roll(y_bot, n_chunk - 1, axis=0) + pe_term


def _compress(a, pe2, w2):
    S = a.shape[0]
    n_chunk = S // CMP_STRIDE
    G = A_KV_GROUPS
    return pl.pallas_call(
        functools.partial(_compress_kernel, n_chunk=n_chunk),
        grid=(2, G),
        in_specs=[
            pl.BlockSpec((S, HEAD_DIM), lambda kv, g: (0, BLK_KC + kv * G + g)),
            pl.BlockSpec((1, CMP_BLOCK, HEAD_DIM), lambda kv, g: (kv, 0, 0)),
            pl.BlockSpec((1, CMP_BLOCK * HEAD_DIM, HEAD_DIM), lambda kv, g: (kv, 0, 0)),
        ],
        out_specs=pl.BlockSpec((1, 1, n_chunk, HEAD_DIM), lambda kv, g: (kv, g, 0, 0)),
        out_shape=jax.ShapeDtypeStruct((2, G, n_chunk, HEAD_DIM), F32),
        scratch_shapes=[pltpu.VMEM((S, HEAD_DIM), F32)],
        compiler_params=_cparams(("arbitrary", "arbitrary")),
        name="compress",
    )(a, pe2, w2)


NSA_QB = 256
SEL_BODY = 8
SEL_GROUP = 4
_N_FORCED = 3


def _nsa_kernel(q_ref, ks_ref, kw_ref, vs_ref, vw_ref, kc_ref, vc_ref, ovt_ref, gate_ref, *rest, seq, n_cast):
    cast_in, rest = rest[:n_cast], rest[n_cast:]
    o_ref, rest = rest[0], rest[1:]
    cast_out, rest = rest[:n_cast], rest[n_cast:]
    kaug_ref, vt_ref, vwt_ref, vct_ref, acc_ref, m_ref, l_ref = rest
    for src, dst in zip(cast_in, cast_out):
        dst[...] = src[0].astype(dst.dtype)
    grp = pl.program_id(0)
    i = pl.program_id(1)
    t0 = i * NSA_QB
    n_cmp = seq // CMP_STRIDE
    cols = A_HPG * NSA_QB

    @pl.when(i == 0)
    def _():
        def build(r, c):
            r0 = pl.multiple_of(r * KEY_TILE, KEY_TILE)
            kaug_ref[r, :, 0:HEAD_DIM] = ks_ref[pl.ds(r0, KEY_TILE), :]
            blk = (r0 + lax.broadcasted_iota(jnp.int32, (KEY_TILE, SEL_SLOTS), 0)) // SEL_BLOCK
            slot = lax.broadcasted_iota(jnp.int32, (KEY_TILE, SEL_SLOTS), 1)
            kaug_ref[r, :, HEAD_DIM:HEAD_DIM + SEL_SLOTS] = jnp.where(blk == slot, 1.0, 0.0).astype(BF16)
            vt_ref[r] = vs_ref[pl.ds(r0, KEY_TILE), :].astype(F32).T.astype(BF16)
            vwt_ref[r] = vw_ref[pl.ds(r0, KEY_TILE), :].astype(F32).T.astype(BF16)
            return c
        lax.fori_loop(0, seq // KEY_TILE, build, 0)
        vct_ref[...] = vc_ref[0, 0].T.astype(BF16)

    qt = jnp.concatenate(
        [q_ref[:, h * HEAD_DIM:(h + 1) * HEAD_DIM].astype(F32).T.astype(BF16) for h in range(A_HPG)],
        axis=1)
    tq = t0 + lax.broadcasted_iota(jnp.int32, (1, cols), 1) % NSA_QB

    def softmax0(s):
        m = jnp.max(s, axis=0, keepdims=True)
        p = jnp.exp2(s - m)
        return p, jnp.sum(p, axis=0, keepdims=True)

    s_c = _dot(kc_ref[0, 0].astype(BF16), qt)
    nrow = lax.broadcasted_iota(jnp.int32, (n_cmp, 1), 0)
    vis = (nrow * CMP_STRIDE + (CMP_BLOCK - 1)) <= tq
    p_c, den = softmax0(jnp.where(vis, s_c, NEG_BIG))
    p_c = p_c * jnp.where(tq >= CMP_BLOCK - 1, 1.0 / den, 0.0)
    o_ct = _dot(vct_ref[...], p_c.astype(BF16))

    psum = p_c[:, 0:NSA_QB]
    for h in range(1, A_HPG):
        psum = psum + p_c[:, h * NSA_QB:(h + 1) * NSA_QB]
    ovt = ovt_ref[...]
    imp_t = jnp.zeros((SEL_SLOTS, NSA_QB), F32)
    for piece in _split3(psum):
        imp_t = imp_t + _dot(ovt, piece)
    slot_i = lax.broadcasted_iota(jnp.int32, (SEL_SLOTS, NSA_QB), 0)
    tq_l = t0 + lax.broadcasted_iota(jnp.int32, (SEL_SLOTS, NSA_QB), 1)
    cur = tq_l // SEL_BLOCK
    forced = jnp.where(slot_i == 0, 1, 0) + jnp.where(slot_i == cur, 1, 0) + jnp.where(slot_i == cur - 1, 1, 0)
    valid = slot_i * SEL_BLOCK <= tq_l
    key = jnp.where(forced > 0, -3e38, jnp.where(valid, imp_t, NEG_BIG))
    slot_f = slot_i.astype(F32)
    bias_t = jnp.where(forced > 0, 0.0, NEG_BIG)
    for _ in range(SEL_TOPK - _N_FORCED):
        mx = jnp.max(key, axis=0, keepdims=True)
        first = jnp.min(jnp.where(key == mx, slot_f, 1e9), axis=0, keepdims=True)
        pick = slot_f == first
        bias_t = jnp.where(pick, 0.0, bias_t)
        key = jnp.where(pick, -3e38, key)

    n_wt = WINDOW // KEY_TILE + 1
    assert WINDOW % KEY_TILE == 0 and n_wt == 3
    kk = lax.broadcasted_iota(jnp.int32, (KEY_TILE, 1), 0)
    qq = tq - t0
    tri = kk <= qq
    tri_far = kk <= jnp.where(i >= 2, qq, KEY_TILE)
    w_idx = [jnp.maximum(i - 2, 0), jnp.maximum(i - 1, 0), i]
    s_w = [_dot(kw_ref[pl.ds(pl.multiple_of(w * KEY_TILE, KEY_TILE), KEY_TILE), :], qt) for w in w_idx]
    s_w = jnp.concatenate([jnp.where(tri_far, NEG_BIG, s_w[0]),
                           jnp.where(i >= 1, s_w[1], NEG_BIG),
                           jnp.where(tri, s_w[2], NEG_BIG)], axis=0)
    p_w, l_w = softmax0(s_w)
    p_w = p_w.astype(BF16)
    o_wt = _dot(vwt_ref[w_idx[0]], p_w[0:KEY_TILE])
    for k in range(1, n_wt):
        o_wt = o_wt + _dot(vwt_ref[w_idx[k]], p_w[k * KEY_TILE:(k + 1) * KEY_TILE])
    o_wt = o_wt * (1.0 / l_w)

    qt_aug = jnp.concatenate([qt, jnp.concatenate([bias_t.astype(BF16)] * A_HPG, axis=1)], axis=0)
    acc_ref[...] = jnp.zeros_like(acc_ref)
    m_ref[...] = jnp.full(m_ref.shape, 0.5 * NEG_BIG, F32)
    l_ref[...] = jnp.zeros_like(l_ref)

    def sel_body(base, causal):
        n = len(causal)
        m, l = m_ref[...], l_ref[...]
        gsz = min(SEL_GROUP, max(n // 2, 1))
        groups = [list(range(g0, min(g0 + gsz, n))) for g0 in range(0, n, gsz)]

        def scores(ks):
            out = {}
            for k in ks:
                sk = _dot(kaug_ref[base + k], qt_aug)
                if causal[k]:
                    kpos = (base + k) * KEY_TILE + lax.broadcasted_iota(jnp.int32, (KEY_TILE, 1), 0)
                    sk = jnp.where(kpos <= tq, sk, NEG_BIG)
                out[k] = sk
            return out

        s = scores(groups[0])
        for gi, ks in enumerate(groups):
            if gi + 1 < len(groups):
                s.update(scores(groups[gi + 1]))
            m_new = m
            for k in ks:
                m_new = jnp.maximum(m_new, jnp.max(s[k], axis=0, keepdims=True))
            alpha = jnp.exp2(m - m_new)
            l = alpha * l
            pv = None
            for k in ks:
                p = jnp.exp2(s.pop(k) - m_new)
                l = l + jnp.sum(p, axis=0, keepdims=True)
                d = _dot(vt_ref[base + k], p.astype(BF16))
                pv = d if pv is None else pv + d
            acc_ref[...] = alpha * acc_ref[...] + pv
            m = m_new
        m_ref[...] = m
        l_ref[...] = l

    n_bulk = i // SEL_BODY
    rem = i % SEL_BODY

    def bulk(b, c):
        sel_body(b * SEL_BODY, (False,) * SEL_BODY)
        return c
    lax.fori_loop(0, n_bulk, bulk, 0)

    base = n_bulk * SEL_BODY
    part = SEL_BODY // 2
    while part >= 2:
        pl.when(rem & part != 0)(functools.partial(sel_body, base, (False,) * part))
        base = base + (rem & part)
        part //= 2

    @pl.when(rem % 2 == 1)
    def _():
        sel_body(i - 1, (False, True))

    @pl.when(rem % 2 == 0)
    def _():
        sel_body(i, (True,))

    o_st = acc_ref[...] * (1.0 / l_ref[...])

    gates_t = gate_ref[...].astype(F32).T
    per_group = 3 * A_HPG

    def gate(h, b):
        r = 3 * h + b
        return jnp.where(grp == 0, gates_t[r:r + 1], gates_t[per_group + r:per_group + r + 1])

    for h in range(A_HPG):
        c = slice(h * NSA_QB, (h + 1) * NSA_QB)
        o = gate(h, 0) * o_ct[:, c] + gate(h, 1) * o_st[:, c] + gate(h, 2) * o_wt[:, c]
        o_ref[:, h * HEAD_DIM:(h + 1) * HEAD_DIM] = o.T.astype(o_ref.dtype)


def _overlap_matrix(seq):
    n_chunk = seq // CMP_STRIDE
    cs = np.arange(n_chunk)[:, None] * CMP_STRIDE
    ss = np.arange(SEL_SLOTS)[None, :] * SEL_BLOCK
    ov = np.maximum(np.minimum(cs + CMP_BLOCK, ss + SEL_BLOCK) - np.maximum(cs, ss), 0) / CMP_BLOCK
    n_cmp = (seq - CMP_BLOCK) // CMP_STRIDE + 1
    ov[n_cmp:] = 0.0
    ov[:, seq // SEL_BLOCK:] = 0.0
    return ov.astype(np.float32)


def _nsa(a, cmp_kv, cast=(), layer=0):
    S = a.shape[0]
    G = A_KV_GROUPS
    steps = S // NSA_QB
    for w in cast:
        assert w.shape[1] % (steps * 2 * SUBLANES) == 0
    slab = lambda g, i: jnp.where(g == 0, i, steps - 1)
    cast_in = [pl.BlockSpec((1, w.shape[1] // steps, w.shape[2]), lambda g, i: (layer, slab(g, i), 0))
               for w in cast]
    cast_out = [pl.BlockSpec((w.shape[1] // steps, w.shape[2]), lambda g, i: (slab(g, i), 0)) for w in cast]
    cast_shape = [jax.ShapeDtypeStruct(w.shape[1:], BF16) for w in cast]
    n_tiles = S // KEY_TILE
    assert S // SEL_BLOCK <= SEL_SLOTS and S % KEY_TILE == 0 and S >= WINDOW + NSA_QB
    assert NSA_QB == KEY_TILE and G == 2
    n_chunk = S // CMP_STRIDE
    ovt = jnp.asarray(_overlap_matrix(S).T, BF16)
    col = lambda b: pl.BlockSpec((S, HEAD_DIM), lambda g, i, b=b: (0, b + g))
    out = pl.pallas_call(
        functools.partial(_nsa_kernel, seq=S, n_cast=len(cast)),
        grid=(G, steps),
        in_specs=[
            pl.BlockSpec((NSA_QB, A_HPG * HEAD_DIM), lambda g, i: (i, g)),
            col(BLK_KS), col(BLK_KW), col(BLK_VS), col(BLK_VW),
            pl.BlockSpec((1, 1, n_chunk, HEAD_DIM), lambda g, i: (0, g, 0, 0)),
            pl.BlockSpec((1, 1, n_chunk, HEAD_DIM), lambda g, i: (1, g, 0, 0)),
            pl.BlockSpec((SEL_SLOTS, n_chunk), lambda g, i: (0, 0)),
            pl.BlockSpec((NSA_QB, LANES), lambda g, i: (i, BLK_GATE)),
        ] + cast_in,
        out_specs=[pl.BlockSpec((NSA_QB, A_HPG * HEAD_DIM), lambda g, i: (i, g))] + cast_out,
        out_shape=[jax.ShapeDtypeStruct((S, A_WIDTH), BF16)] + cast_shape,
        scratch_shapes=[pltpu.VMEM((n_tiles, KEY_TILE, HEAD_DIM + SEL_SLOTS), BF16),
                        pltpu.VMEM((n_tiles, HEAD_DIM, KEY_TILE), BF16),
                        pltpu.VMEM((n_tiles, HEAD_DIM, KEY_TILE), BF16),
                        pltpu.VMEM((HEAD_DIM, n_chunk), BF16),
                        pltpu.VMEM((HEAD_DIM, A_HPG * NSA_QB), F32),
                        pltpu.VMEM((1, A_HPG * NSA_QB), F32),
                        pltpu.VMEM((1, A_HPG * NSA_QB), F32)],
        compiler_params=_cparams(("arbitrary", "arbitrary")),
        name="nsa",
    )(a, a, a, a, a, cmp_kv, cmp_kv, ovt, a, *cast)
    return out[0] if not cast else out


_ROW_MASKS = 18
_SCORE_MASKS = 10
HG_HEADS_PER_STEP = 8


def _hgrn_tables():
    t = np.arange(HG_CHUNK)
    p4, i4, i16 = t % 4, (t // 4) % 4, t // 16
    conds = [p4 >= 1, p4 >= 2, p4 >= 3, p4 <= 2, p4 <= 1, p4 == 0,
             i4 >= 1, i4 >= 2, i4 >= 3, i4 <= 2, i4 <= 1, i4 == 0,
             i16 >= 1, i16 >= 2, i16 >= 3, i16 <= 2, i16 <= 1, i16 == 0]
    rm = np.stack([np.broadcast_to(c[:, None], (HG_CHUNK, HG_DIM)) for c in conds]).astype(np.float32)
    tt, ss = t[:, None], t[None, :]
    lag16 = tt // 16 - ss // 16
    lag4 = tt // 4 - ss // 4
    same16, same4 = lag16 == 0, lag4 == 0
    sm = [lag16 == 1, lag16 == 2, lag16 == 3,
          same16 & (lag4 == 1), same16 & (lag4 == 2), same16 & (lag4 == 3),
          same4 & (tt - ss == 0), same4 & (tt - ss == 1), same4 & (tt - ss == 2), same4 & (tt - ss == 3)]
    return rm, np.stack(sm).astype(np.float32)


def _hgrn_kernel(q_ref, f_ref, i_ref, og_ref, loglb_ref, log1m_ref, gn_ref, rm_ref, sm_ref, *rest, tb, nh, n_cast,
                 with_ada):
    cast_in, rest = rest[:n_cast], rest[n_cast:]
    if with_ada:
        (c_ref, wa_ref, ba_ref), rest = rest[:3], rest[3:]
    o_ref, rest = rest[0], rest[1:]
    cast_out, rest = rest[:n_cast], rest[n_cast:]
    if with_ada:
        ada_ref, rest = rest[0], rest[1:]
        cc = c_ref[...]
        ada_ref[...] = jnp.sum(cc * jax.nn.sigmoid(cc) * wa_ref[0], axis=0, keepdims=True) + ba_ref[0]
    st_ref, g_scr, k_scr, o_scr = rest
    for src, dst in zip(cast_in, cast_out):
        dst[...] = src[0].astype(dst.dtype)
    C = HG_CHUNK
    heads = [slice(h * HG_DIM, (h + 1) * HG_DIM) for h in range(nh)]

    @pl.when(pl.program_id(1) == 0)
    def _():
        st_ref[...] = jnp.zeros_like(st_ref)

    x = f_ref[...]
    log_sig = jnp.minimum(x, 0.0) - jnp.log(1.0 + jnp.exp(-jnp.abs(x)))
    a = loglb_ref[...]
    c = log1m_ref[...] + log_sig
    log_f = jnp.maximum(a, c) + jnp.log(1.0 + jnp.exp(-jnp.abs(a - c)))
    g_scr[...] = log_f
    k_scr[...] = jnp.exp(log1m_ref[...] + (log_sig - x))

    def rmask(n):
        return jnp.concatenate([rm_ref[n]] * nh, axis=1)

    def roll(v, s):
        return pltpu.roll(v, s % C, axis=0)

    def b16(z):
        return z.astype(BF16)

    def chunk(ci, carry):
        r0 = pl.multiple_of(ci * C, C)
        g = g_scr[pl.ds(r0, C), :]
        q = q_ref[pl.ds(r0, C), :]
        k = k_scr[pl.ds(r0, C), :]
        v = i_ref[pl.ds(r0, C), :].astype(BF16)
        pre4 = g + rmask(0) * roll(g, 1) + rmask(1) * roll(g, 2) + rmask(2) * roll(g, 3)
        suf4 = rmask(3) * roll(g, -1) + rmask(4) * roll(g, -2) + rmask(5) * roll(g, -3)
        tot4 = pre4 + suf4
        a1, a2, a3 = roll(tot4, 4), roll(tot4, 8), roll(tot4, 12)
        pre16 = pre4 + rmask(6) * a1 + rmask(7) * a2 + rmask(8) * a3
        suf16 = suf4 + rmask(9) * roll(tot4, -4) + rmask(10) * roll(tot4, -8) + rmask(11) * roll(tot4, -12)
        tot16 = pre16 + suf16
        d1, d2, d3 = roll(tot16, 16), roll(tot16, 32), roll(tot16, 48)
        pre64 = pre16 + rmask(12) * d1 + rmask(13) * d2 + rmask(14) * d3
        suf64 = suf16 + rmask(15) * roll(tot16, -16) + rmask(16) * roll(tot16, -32) + rmask(17) * roll(tot16, -48)
        tot64 = pre64 + suf64
        g1, g2 = roll(g, 1), roll(g, 2)

        q_a = q * jnp.exp(pre16)
        lhs_a = [b16(q_a), b16(q_a * jnp.exp(d1)), b16(q_a * jnp.exp(d1 + d2))]
        k_a = b16(k * jnp.exp(suf16))
        q_b = q * jnp.exp(pre4)
        lhs_b = [b16(q_b), b16(q_b * jnp.exp(a1)), b16(q_b * jnp.exp(a1 + a2))]
        k_b = b16(k * jnp.exp(suf4))
        q_c = q * jnp.exp(g)
        lhs_c = [b16(q), b16(q_c), b16(q_c * jnp.exp(g1)), b16(q_c * jnp.exp(g1 + g2))]
        k_c = b16(k)
        q_s = b16(q * jnp.exp(pre64))
        k_s = b16(k * jnp.exp(suf64))
        decay = jnp.exp(tot64[0:1, :])

        def level(lhs, rhs):
            return [_dot_nt(jnp.concatenate([z[:, hh] for z in lhs], axis=0), rhs[:, hh]) for hh in heads]

        s_a, s_b, s_c = level(lhs_a, k_a), level(lhs_b, k_b), level(lhs_c, k_c)
        o_inter = [_dot_nt(q_s[:, hh], b16(st_ref[h])) for h, hh in enumerate(heads)]
        kv = [_dot_tn(v[:, hh], k_s[:, hh]) for hh in heads]
        for h, hh in enumerate(heads):
            sa, sb, sc = s_a[h], s_b[h], s_c[h]
            scores = (sm_ref[0] * sa[0:C] + sm_ref[1] * sa[C:2 * C] + sm_ref[2] * sa[2 * C:3 * C]
                      + sm_ref[3] * sb[0:C] + sm_ref[4] * sb[C:2 * C] + sm_ref[5] * sb[2 * C:3 * C]
                      + sm_ref[6] * sc[0:C] + sm_ref[7] * sc[C:2 * C] + sm_ref[8] * sc[2 * C:3 * C]
                      + sm_ref[9] * sc[3 * C:4 * C])
            o_scr[pl.ds(r0, C), hh] = _dot(b16(scores), v[:, hh]) + o_inter[h]
            st_ref[h] = st_ref[h] * decay[:, hh] + kv[h]
        return carry

    lax.fori_loop(0, tb // C, chunk, 0)

    for hh in heads:
        o = o_scr[:, hh]
        ms = jnp.mean(o * o, axis=-1, keepdims=True)
        og = og_ref[:, hh]
        y = o * lax.rsqrt(ms + NORM_EPS) * gn_ref[:, hh] * (og * jax.nn.sigmoid(og))
        o_ref[:, hh] = y.astype(o_ref.dtype)


def _hgrn(hz, loglb, log1mlb, gnorm, cast=(), layer=0, ada=None):
    S = hz.shape[0]
    tb = HG_ROWS if S % HG_ROWS == 0 else S
    nh = HG_HEADS_PER_STEP
    ng = HG_HEADS // nh
    steps = S // tb
    for w in cast:
        assert w.shape[1] % (steps * 2 * SUBLANES) == 0
    slab = lambda h, c: jnp.where(h == 0, c, steps - 1)
    cast_in = [pl.BlockSpec((1, w.shape[1] // steps, w.shape[2]), lambda h, c: (layer, slab(h, c), 0))
               for w in cast]
    cast_out = [pl.BlockSpec((w.shape[1] // steps, w.shape[2]), lambda h, c: (slab(h, c), 0)) for w in cast]
    cast_shape = [jax.ShapeDtypeStruct(w.shape[1:], BF16) for w in cast]
    ada_args, ada_in, ada_out, ada_shape = [], [], [], []
    if ada is not None:
        c, w_ada, b_ada, la = ada
        D, n_ada = w_ada.shape[1], w_ada.shape[2]
        ck = n_ada // steps
        assert n_ada % steps == 0 and ck % LANES == 0
        ada_args = [c.reshape(D, 1), w_ada, b_ada.reshape(b_ada.shape[0], 1, n_ada)]
        ada_in = [pl.BlockSpec((D, 1), lambda h, c: (0, 0)),
                  pl.BlockSpec((1, D, ck), lambda h, c: (la, 0, slab(h, c))),
                  pl.BlockSpec((1, 1, ck), lambda h, c: (la, 0, slab(h, c)))]
        ada_out = [pl.BlockSpec((1, ck), lambda h, c: (0, slab(h, c)))]
        ada_shape = [jax.ShapeDtypeStruct((1, n_ada), F32)]
    w = nh * HG_DIM
    rm, sm = _hgrn_tables()
    blk = lambda part: pl.BlockSpec((tb, w), lambda h, c, part=part: (c, part * ng + h))
    row = pl.BlockSpec((1, w), lambda h, c: (0, h))
    out = pl.pallas_call(
        functools.partial(_hgrn_kernel, tb=tb, nh=nh, n_cast=len(cast), with_ada=ada is not None),
        grid=(ng, steps),
        in_specs=[blk(0), blk(1), blk(2), blk(3), row, row, row,
                  pl.BlockSpec((_ROW_MASKS, HG_CHUNK, HG_DIM), lambda h, c: (0, 0, 0)),
                  pl.BlockSpec((_SCORE_MASKS, HG_CHUNK, HG_CHUNK), lambda h, c: (0, 0, 0))] + cast_in + ada_in,
        out_specs=[pl.BlockSpec((tb, w), lambda h, c: (c, h))] + cast_out + ada_out,
        out_shape=[jax.ShapeDtypeStruct((S, HG_WIDTH), BF16)] + cast_shape + ada_shape,
        scratch_shapes=[pltpu.VMEM((nh, HG_DIM, HG_DIM), F32), pltpu.VMEM((tb, w), F32),
                        pltpu.VMEM((tb, w), F32), pltpu.VMEM((tb, w), F32)],
        compiler_params=_cparams(("arbitrary", "arbitrary")),
        name="hgrn",
    )(hz, hz, hz, hz, loglb, log1mlb, gnorm, jnp.asarray(rm), jnp.asarray(sm), *cast, *ada_args)
    return out[0] if len(out) == 1 else out


_MERGE_PART = 512


def _merge_kernel(at_ref, hg_ref, ga_ref, gh_ref, wa_ref, wh_ref, wo_ref, x_ref, gt_ref, gp_ref, o_ref):
    at, hg = at_ref[...], hg_ref[...]
    d = wo_ref.shape[1]
    parts = [slice(c0, c0 + _MERGE_PART) for c0 in range(0, d, _MERGE_PART)]
    branch = [(_dot(at, wa_ref[0, :, cs]), _dot(hg, wh_ref[0, :, cs])) for cs in parts]
    y = None
    for cs, (ma, mh) in zip(parts, branch):
        mixed = ga_ref[:, cs].astype(F32) * ma + gh_ref[:, cs].astype(F32) * mh
        yp = _dot(mixed.astype(BF16), wo_ref[0, cs, :])
        y = yp if y is None else y + yp
    ms = jnp.mean(y * y, axis=-1, keepdims=True)
    yn = y * lax.rsqrt(ms + NORM_EPS) * gp_ref[...]
    o_ref[...] = x_ref[...] + gt_ref[...] * yn


def _merge(attn, hg, mg, wa, wh, wo, layer, x, gt, gpost):
    S, D = x.shape
    tm = MERGE_ROWS
    const = lambda shape: pl.BlockSpec((1,) + shape, lambda m: (layer, 0, 0), pipeline_mode=pl.Buffered(1))
    row = pl.BlockSpec((1, D), lambda m: (0, 0))
    return pl.pallas_call(
        _merge_kernel,
        grid=(S // tm,),
        in_specs=[
            pl.BlockSpec((tm, A_WIDTH), lambda m: (m, 0)),
            pl.BlockSpec((tm, HG_WIDTH), lambda m: (m, 0)),
            pl.BlockSpec((tm, D), lambda m: (m, 0)),
            pl.BlockSpec((tm, D), lambda m: (m, 1)),
            const((A_WIDTH, D)), const((HG_WIDTH, D)), const((D, D)),
            pl.BlockSpec((tm, D), lambda m: (m, 0)),
            row, row,
        ],
        out_specs=pl.BlockSpec((tm, D), lambda m: (m, 0)),
        out_shape=jax.ShapeDtypeStruct((S, D), F32),
        compiler_params=_cparams(("arbitrary",)),
        name="merge",
    )(attn, hg, mg, mg, wa, wh, wo, x, gt, gpost.reshape(1, D))


_HALO = 16
_FF_SUB = 256


def _norm_mod(x, g_ref, sc_ref, sh_ref):
    ms = jnp.mean(x * x, axis=-1, keepdims=True)
    y = x * lax.rsqrt(ms + NORM_EPS) * g_ref[...]
    return (y * (1.0 + sc_ref[...]) + sh_ref[...]).astype(BF16)


def _ffn_kernel(x_ref, xh_ref, g_ref, sc_ref, sh_ref, wg_ref, wv_ref, cwg_ref, cwv_ref, cbg_ref, cbv_ref,
                wd_ref, gt_ref, gp_ref, *rest, tm, emit_next):
    if emit_next:
        gn_ref, scn_ref, shn_ref, o_ref, hn_ref, h_scr, ug_scr, uv_scr = rest
    else:
        o_ref, h_scr, ug_scr, uv_scr = rest
    m = pl.program_id(0)
    f = pl.program_id(1)
    norm_mod = functools.partial(_norm_mod, g_ref=g_ref, sc_ref=sc_ref, sh_ref=sh_ref)

    @pl.when(f == 0)
    def _():
        h_scr[0:_HALO, :] = jnp.where(m > 0, norm_mod(xh_ref[...]), jnp.zeros((), BF16))
        h_scr[_HALO:_HALO + tm, :] = norm_mod(x_ref[...])

    tf = wg_ref.shape[2]
    subs = [slice(s * _FF_SUB, (s + 1) * _FF_SUB) for s in range(tf // _FF_SUB)]
    h = h_scr[...]
    ups = [(_dot(h, wg_ref[0, :, cs]), _dot(h, wv_ref[0, :, cs])) for cs in subs]

    def conv(u_scr, cw_ref, cb_ref, cs):
        out = cb_ref[:, cs] + cw_ref[0:1, cs] * u_scr[pl.ds(_HALO - 2, tm), cs]
        out = out + cw_ref[1:2, cs] * u_scr[pl.ds(_HALO - 1, tm), cs]
        return out + cw_ref[2:3, cs] * u_scr[pl.ds(_HALO, tm), cs]

    acc = jnp.where(f > 0, o_ref[...], 0.0)
    for cs, (ug, uv) in zip(subs, ups):
        ug_scr[:, cs] = ug
        uv_scr[:, cs] = uv
        cg = conv(ug_scr, cwg_ref, cbg_ref, cs)
        cv = conv(uv_scr, cwv_ref, cbv_ref, cs)
        act = (cg * jax.nn.sigmoid(cg) * cv).astype(BF16)
        acc = acc + _dot(act, wd_ref[0, cs, :])
    o_ref[...] = acc

    @pl.when(f == pl.num_programs(1) - 1)
    def _():
        y = o_ref[...]
        ms = jnp.mean(y * y, axis=-1, keepdims=True)
        yn = y * lax.rsqrt(ms + NORM_EPS) * gp_ref[...]
        x_new = x_ref[...] + gt_ref[...] * yn
        o_ref[...] = x_new
        if emit_next:
            hn_ref[...] = _norm_mod(x_new, gn_ref, scn_ref, shn_ref)


def _ffn(x, gpre, sc, sh, wup, convw, convb, wdown, layer, gt, gpost, nxt=None):
    S, D = x.shape
    F = wdown.shape[1]
    tm = FFN_ROWS if S % FFN_ROWS == 0 else S
    tf = FFN_COLS
    nf = F // tf
    hb = tm // _HALO
    row = pl.BlockSpec((1, D), lambda m, f: (0, 0))
    tile = pl.BlockSpec((tm, D), lambda m, f: (m, 0))
    extra = () if nxt is None else (nxt[0].reshape(1, D), nxt[1], nxt[2])
    return pl.pallas_call(
        functools.partial(_ffn_kernel, tm=tm, emit_next=nxt is not None),
        grid=(S // tm, nf),
        in_specs=[
            pl.BlockSpec((tm, D), lambda m, f: (m, 0)),
            pl.BlockSpec((_HALO, D), lambda m, f: (jnp.maximum(m * hb - 1, 0), 0)),
            row, row, row,
            pl.BlockSpec((1, D, tf), lambda m, f: (layer, 0, f)),
            pl.BlockSpec((1, D, tf), lambda m, f: (layer, 0, nf + f)),
            pl.BlockSpec((CONV_WIDTH, tf), lambda m, f: (0, f)),
            pl.BlockSpec((CONV_WIDTH, tf), lambda m, f: (0, nf + f)),
            pl.BlockSpec((1, tf), lambda m, f: (0, f)),
            pl.BlockSpec((1, tf), lambda m, f: (0, nf + f)),
            pl.BlockSpec((1, tf, D), lambda m, f: (layer, f, 0)),
            row, row,
        ] + ([row] * 3 if nxt is not None else []),
        out_specs=[tile, tile] if nxt is not None else tile,
        out_shape=([jax.ShapeDtypeStruct((S, D), F32), jax.ShapeDtypeStruct((S, D), BF16)]
                   if nxt is not None else jax.ShapeDtypeStruct((S, D), F32)),
        scratch_shapes=[pltpu.VMEM((tm + _HALO, D), BF16),
                        pltpu.VMEM((tm + _HALO, tf), F32), pltpu.VMEM((tm + _HALO, tf), F32)],
        compiler_params=_cparams(("arbitrary", "arbitrary")),
        name="ffn",
    )(x, x, gpre.reshape(1, D), sc, sh, wup, wup, convw, convw, convb.reshape(1, -1), convb.reshape(1, -1),
      wdown, gt, gpost.reshape(1, D), *extra)


def _rope_tables(positions):
    inv_freq = 1.0 / (ROPE_THETA ** (jnp.arange(0, HEAD_DIM, 2, dtype=F32) / HEAD_DIM))
    ang = positions.astype(F32)[:, None] * inv_freq
    cos, sin = jnp.cos(ang), jnp.sin(ang)
    return jnp.concatenate([cos, cos], axis=-1), jnp.concatenate([-sin, sin], axis=-1)


def kernel(x, c, positions, w_ada, b_ada, g_pre_mix, w_in, pe_kc, w_kc, pe_vc, w_vc, lb_logits, g_hg_norm,
           w_br_attn, w_br_hgrn, w_out, g_post_mix, g_pre_ffn, w_up, conv_w, conv_b, w_down, g_post_ffn):
    B, S, D = x.shape
    assert B == 1, "kernel is written for one sequence"
    L = w_ada.shape[0]
    xs = x[0]
    cosf, sinf = _rope_tables(positions[0])
    scale = HEAD_DIM ** -0.5 * np.log2(np.e)
    tables = (cosf * scale, sinf * scale, cosf, sinf)
    lb_cum = jnp.cumsum(jax.nn.softmax(lb_logits.astype(F32), axis=0), axis=0)
    lower = lb_cum - lb_cum[0:1]
    log_lb = jnp.log(lower)
    log_1m = jnp.log1p(-lower)
    split = lambda ada_l: [ada_l[:, j * D:(j + 1) * D] for j in range(N_ADA)]
    mods = split(_ada_all(c, w_ada, b_ada, 1)[0])

    w_in_t = jnp.swapaxes(w_in, 1, 2)
    hg_roles = (("id",) * (TAIL_TILE // LANES),) * (4 * HG_WIDTH // TAIL_TILE)
    mg_roles = (("sg",) * (TAIL_TILE // LANES),) * (2 * D // TAIL_TILE)


    h = _prenorm(xs, g_pre_mix[0], mods[1], mods[0])
    for l in range(L):
        sh1, sc1, gt1, sh2, sc2, gt2 = mods

        a = _proj(h, w_in_t, l, 0, A_ROLES, BF16, tn=A_TILE, tables=tables, name="proj_attn")
        hz = _proj(h, w_in_t, l, A_COLS, hg_roles, F32, tn=TAIL_TILE, name="proj_hgrn")
        mg = _proj(h, w_in_t, l, A_COLS + 4 * HG_WIDTH, mg_roles, BF16, tn=TAIL_TILE, name="proj_merge_gate")

        cmp_kv = _compress(a, jnp.stack([pe_kc[l], pe_vc[l]]), jnp.stack([w_kc[l], w_vc[l]]))
        attn, wup_b, wdown_b = _nsa(a, cmp_kv, cast=(w_up, w_down), layer=l)
        res = _hgrn(hz, log_lb[l:l + 1], log_1m[l:l + 1], g_hg_norm[l:l + 1],
                    cast=(w_br_attn, w_br_hgrn, w_out), layer=l,
                    ada=(c, w_ada, b_ada, l + 1) if l + 1 < L else None)
        hg, wa_b, wh_b, wo_b = res[:4]
        mods_next = split(res[4]) if l + 1 < L else None
        xs = _merge(attn, hg, mg, wa_b[None], wh_b[None], wo_b[None], 0, xs, gt1, g_post_mix[l])

        nxt = (g_pre_mix[l + 1], mods_next[1], mods_next[0]) if l + 1 < L else None
        out = _ffn(xs, g_pre_ffn[l], sc2, sh2, wup_b[None], conv_w[l], conv_b[l], wdown_b[None], 0, gt2,
                   g_post_ffn[l], nxt)
        xs, h = out if nxt is not None else (out, None)
        mods = mods_next
    return xs[None]
```

```python
import functools

import jax
import jax.numpy as jnp
import numpy as np
from jax import lax
from jax.experimental import pallas as pl
from jax.experimental.pallas import tpu as pltpu

F32 = jnp.float32
BF16 = jnp.bfloat16

A_HEADS = 8
A_KV_GROUPS = 2
A_HPG = A_HEADS // A_KV_GROUPS
HEAD_DIM = 128
A_WIDTH = A_HEADS * HEAD_DIM
A_KV_WIDTH = A_KV_GROUPS * HEAD_DIM
CMP_BLOCK = 32
CMP_STRIDE = 16
SEL_BLOCK = 64
SEL_TOPK = 16
WINDOW = 512
ROPE_THETA = 10000.0
HG_HEADS = 8
HG_DIM = 128
HG_WIDTH = HG_HEADS * HG_DIM
HG_CHUNK = 64
CONV_WIDTH = 3
NORM_EPS = 1e-6
N_ADA = 6
A_COLS = A_WIDTH + 6 * A_KV_WIDTH + 3 * A_HEADS

LANES = 128
SUBLANES = 8
SEL_SLOTS = LANES
KEY_TILE = 256
NEG_BIG = -1e30
VMEM_LIMIT = 56 * 1024 * 1024

ADA_COLS = 1024
NORM_ROWS = 512
PROJ_ROWS = 1024
TAIL_TILE = 1024
HG_ROWS = 512
MERGE_ROWS = 512
FFN_ROWS = 512
FFN_COLS = 512


def _cparams(sem):
    return pltpu.CompilerParams(dimension_semantics=sem, vmem_limit_bytes=VMEM_LIMIT)


def _split3(a):
    hi = a.astype(BF16)
    r1 = a - hi.astype(F32)
    mid = r1.astype(BF16)
    lo = (r1 - mid.astype(F32)).astype(BF16)
    return hi, mid, lo


def _dot(a, b):
    return jnp.dot(a, b, preferred_element_type=F32)


def _dot_nt(a, b):
    return lax.dot_general(a, b, (((1,), (1,)), ((), ())), preferred_element_type=F32)


def _dot_tn(a, b):
    return lax.dot_general(a, b, (((0,), (0,)), ((), ())), preferred_element_type=F32)


def _ada_kernel(c_ref, w_ref, b_ref, o_ref):
    c = c_ref[...]
    ca = c * jax.nn.sigmoid(c)
    o_ref[0] = jnp.sum(ca * w_ref[0], axis=0, keepdims=True) + b_ref[0]


def _ada_all(c, w_ada, b_ada, layers):
    _, D, N = w_ada.shape
    L = layers
    tn = ADA_COLS
    return pl.pallas_call(
        _ada_kernel,
        grid=(L, N // tn),
        in_specs=[
            pl.BlockSpec((D, 1), lambda l, n: (0, 0)),
            pl.BlockSpec((1, D, tn), lambda l, n: (l, 0, n)),
            pl.BlockSpec((1, 1, tn), lambda l, n: (l, 0, n)),
        ],
        out_specs=pl.BlockSpec((1, 1, tn), lambda l, n: (l, 0, n)),
        out_shape=jax.ShapeDtypeStruct((L, 1, N), F32),
        compiler_params=_cparams(("arbitrary", "arbitrary")),
        name="ada",
    )(c.reshape(D, 1), w_ada, b_ada.reshape(b_ada.shape[0], 1, N))


def _prenorm_kernel(x_ref, g_ref, sc_ref, sh_ref, o_ref):
    x = x_ref[...]
    ms = jnp.mean(x * x, axis=-1, keepdims=True)
    y = x * lax.rsqrt(ms + NORM_EPS) * g_ref[...]
    o_ref[...] = (y * (1.0 + sc_ref[...]) + sh_ref[...]).astype(o_ref.dtype)


def _prenorm(x, g, sc, sh):
    S, D = x.shape
    tm = NORM_ROWS
    row = pl.BlockSpec((1, D), lambda m: (0, 0))
    return pl.pallas_call(
        _prenorm_kernel,
        grid=(S // tm,),
        in_specs=[pl.BlockSpec((tm, D), lambda m: (m, 0)), row, row, row],
        out_specs=pl.BlockSpec((tm, D), lambda m: (m, 0)),
        out_shape=jax.ShapeDtypeStruct((S, D), BF16),
        compiler_params=_cparams(("arbitrary",)),
        name="prenorm",
    )(x, g.reshape(1, D), sc, sh)


def _rope(t, cosf, sinf):
    return t * cosf + pltpu.roll(t, HEAD_DIM // 2, axis=1) * sinf


_PROJ_PART = 256


def _proj_kernel(h_ref, w_ref, *rest, roles, tn, with_ada=False):
    if with_ada:
        o_ref, ada_ref, w_scr = rest[-3:]
        c_ref, wa_ref, ba_ref = rest[-6:-3]
        tabs = rest[:-6]
        cc = c_ref[...]
        ada_ref[...] = jnp.sum(cc * jax.nn.sigmoid(cc) * wa_ref[0], axis=0, keepdims=True) + ba_ref[0]
    else:
        o_ref, w_scr = rest[-2:]
        tabs = rest[:-2]
    n = pl.program_id(0)

    @pl.when(pl.program_id(1) == 0)
    def _():
        w_scr[...] = w_ref[0].astype(BF16)

    def emit(slabs):
        h = h_ref[...]
        per = _PROJ_PART // LANES
        parts = [_dot_nt(h, w_scr[c0:c0 + _PROJ_PART, :]) for c0 in range(0, tn, _PROJ_PART)]
        for pi, acc in enumerate(parts):
            for s in range(per):
                kind = slabs[pi * per + s]
                t = acc[:, s * LANES:(s + 1) * LANES]
                if kind == "sg":
                    t = jax.nn.sigmoid(t)
                elif kind != "id":
                    cos_ref, sin_ref = tabs[0:2] if kind == "rq" else tabs[-2:]
                    t = _rope(t, cos_ref[...], sin_ref[...])
                c = (pi * per + s) * LANES
                o_ref[:, c:c + LANES] = t.astype(o_ref.dtype)

    kinds = sorted(set(roles))
    if len(kinds) == 1:
        emit(kinds[0])
        return
    for kind in kinds:
        hit = functools.reduce(jnp.logical_or, [n == t for t, r in enumerate(roles) if r == kind])
        pl.when(hit)(functools.partial(emit, kind))


def _proj(h, w_t, layer, row0, roles, out_dtype, *, tn, tables=(), name="proj", ada=None):
    S, D = h.shape
    tm = PROJ_ROWS if S % PROJ_ROWS == 0 else S
    nt = len(roles)
    tab = pl.BlockSpec((tm, HEAD_DIM), lambda n, m: (m, 0))
    ada_args, ada_in, out_specs, out_shape = [], [], pl.BlockSpec((tm, tn), lambda n, m: (m, n)), \
        jax.ShapeDtypeStruct((S, nt * tn), out_dtype)
    if ada is not None:
        c, w_ada, b_ada, la = ada
        n_ada, nm = w_ada.shape[2], S // tm
        ck = n_ada // (nt * nm)
        assert n_ada % (nt * nm) == 0 and ck % LANES == 0
        ada_args = [c.reshape(D, 1), w_ada, b_ada.reshape(b_ada.shape[0], 1, n_ada)]
        ada_in = [pl.BlockSpec((D, 1), lambda n, m: (0, 0)),
                  pl.BlockSpec((1, D, ck), lambda n, m: (la, 0, n * nm + m)),
                  pl.BlockSpec((1, 1, ck), lambda n, m: (la, 0, n * nm + m))]
        out_specs = [out_specs, pl.BlockSpec((1, ck), lambda n, m: (0, n * nm + m))]
        out_shape = [out_shape, jax.ShapeDtypeStruct((1, n_ada), F32)]
    return pl.pallas_call(
        functools.partial(_proj_kernel, roles=tuple(roles), tn=tn, with_ada=ada is not None),
        grid=(nt, S // tm),
        in_specs=[pl.BlockSpec((tm, D), lambda n, m: (m, 0)),
                  pl.BlockSpec((pl.Element(1), pl.Element(tn), pl.Element(D)),
                               lambda n, m: (layer, pl.multiple_of(row0 + n * tn, SUBLANES), 0))]
                 + [tab] * len(tables) + ada_in,
        out_specs=out_specs,
        out_shape=out_shape,
        scratch_shapes=[pltpu.VMEM((tn, D), BF16)],
        compiler_params=_cparams(("arbitrary", "arbitrary")),
        name=name,
    )(h, w_t, *tables, *ada_args)


A_TILE = 1024
_A_SLABS = (("rq",) * (A_WIDTH // LANES)
            + (("rk",) * A_KV_GROUPS + ("id",) * A_KV_GROUPS) * 3
            + ("sg",) * (-(A_WIDTH // LANES + 6 * A_KV_GROUPS) % (A_TILE // LANES)))
A_ROLES = tuple(_A_SLABS[i:i + A_TILE // LANES] for i in range(0, len(_A_SLABS), A_TILE // LANES))
_KV_BLOCK0 = A_WIDTH // LANES
BLK_KC, BLK_VC, BLK_KS, BLK_VS, BLK_KW, BLK_VW = [_KV_BLOCK0 + j * A_KV_GROUPS for j in range(6)]
BLK_GATE = _KV_BLOCK0 + 6 * A_KV_GROUPS


def _compress_kernel(t_ref, pe_ref, w_ref, o_ref, t32_ref, *, n_chunk):
    half = CMP_BLOCK // 2
    t32_ref[...] = t_ref[...].astype(F32)
    a = jnp.concatenate([t32_ref[pl.ds(j, n_chunk, stride=CMP_STRIDE), :] for j in range(half)],
                        axis=1).astype(BF16)
    kw = half * HEAD_DIM
    w = w_ref[0].astype(BF16)
    y_top = _dot(a, w[:kw])
    y_bot = _dot(a, w[kw:])
    pe = jnp.concatenate([pe_ref[0, j:j + 1, :] for j in range(CMP_BLOCK)], axis=1)
    pe_term = jnp.zeros((1, HEAD_DIM), F32)
    for piece in _split3(pe):
        pe_term = pe_term + _dot(piece, w)
    o_ref[0, 0] = y_top + pltpu.roll(y_bot, n_chunk - 1, axis=0) + pe_term


def _compress(a, pe2, w2):
    S = a.shape[0]
    n_chunk = S // CMP_STRIDE
    G = A_KV_GROUPS
    return pl.pallas_call(
        functools.partial(_compress_kernel, n_chunk=n_chunk),
        grid=(2, G),
        in_specs=[
            pl.BlockSpec((S, HEAD_DIM), lambda kv, g: (0, BLK_KC + kv * G + g)),
            pl.BlockSpec((1, CMP_BLOCK, HEAD_DIM), lambda kv, g: (kv, 0, 0)),
            pl.BlockSpec((1, CMP_BLOCK * HEAD_DIM, HEAD_DIM), lambda kv, g: (kv, 0, 0)),
        ],
        out_specs=pl.BlockSpec((1, 1, n_chunk, HEAD_DIM), lambda kv, g: (kv, g, 0, 0)),
        out_shape=jax.ShapeDtypeStruct((2, G, n_chunk, HEAD_DIM), F32),
        scratch_shapes=[pltpu.VMEM((S, HEAD_DIM), F32)],
        compiler_params=_cparams(("arbitrary", "arbitrary")),
        name="compress",
    )(a, pe2, w2)


NSA_QB = 256
SEL_BODY = 8
SEL_GROUP = 4
_N_FORCED = 3


def _nsa_kernel(q_ref, ks_ref, kw_ref, vs_ref, vw_ref, kc_ref, vc_ref, ovt_ref, gate_ref, *rest, seq, n_cast):
    cast_in, rest = rest[:n_cast], rest[n_cast:]
    o_ref, rest = rest[0], rest[1:]
    cast_out, rest = rest[:n_cast], rest[n_cast:]
    kaug_ref, vt_ref, vwt_ref, vct_ref, acc_ref, m_ref, l_ref = rest
    for src, dst in zip(cast_in, cast_out):
        dst[...] = src[0].astype(dst.dtype)
    grp = pl.program_id(0)
    i = pl.program_id(1)
    t0 = i * NSA_QB
    n_cmp = seq // CMP_STRIDE
    cols = A_HPG * NSA_QB

    @pl.when(i == 0)
    def _():
        def build(r, c):
            r0 = pl.multiple_of(r * KEY_TILE, KEY_TILE)
            kaug_ref[r, :, 0:HEAD_DIM] = ks_ref[pl.ds(r0, KEY_TILE), :]
            blk = (r0 + lax.broadcasted_iota(jnp.int32, (KEY_TILE, SEL_SLOTS), 0)) // SEL_BLOCK
            slot = lax.broadcasted_iota(jnp.int32, (KEY_TILE, SEL_SLOTS), 1)
            kaug_ref[r, :, HEAD_DIM:HEAD_DIM + SEL_SLOTS] = jnp.where(blk == slot, 1.0, 0.0).astype(BF16)
            vt_ref[r] = vs_ref[pl.ds(r0, KEY_TILE), :].astype(F32).T.astype(BF16)
            vwt_ref[r] = vw_ref[pl.ds(r0, KEY_TILE), :].astype(F32).T.astype(BF16)
            return c
        lax.fori_loop(0, seq // KEY_TILE, build, 0)
        vct_ref[...] = vc_ref[0, 0].T.astype(BF16)

    qt = jnp.concatenate(
        [q_ref[:, h * HEAD_DIM:(h + 1) * HEAD_DIM].astype(F32).T.astype(BF16) for h in range(A_HPG)],
        axis=1)
    tq = t0 + lax.broadcasted_iota(jnp.int32, (1, cols), 1) % NSA_QB

    def softmax0(s):
        m = jnp.max(s, axis=0, keepdims=True)
        p = jnp.exp2(s - m)
        return p, jnp.sum(p, axis=0, keepdims=True)

    s_c = _dot(kc_ref[0, 0].astype(BF16), qt)
    nrow = lax.broadcasted_iota(jnp.int32, (n_cmp, 1), 0)
    vis = (nrow * CMP_STRIDE + (CMP_BLOCK - 1)) <= tq
    p_c, den = softmax0(jnp.where(vis, s_c, NEG_BIG))
    p_c = p_c * jnp.where(tq >= CMP_BLOCK - 1, 1.0 / den, 0.0)
    o_ct = _dot(vct_ref[...], p_c.astype(BF16))

    psum = p_c[:, 0:NSA_QB]
    for h in range(1, A_HPG):
        psum = psum + p_c[:, h * NSA_QB:(h + 1) * NSA_QB]
    ovt = ovt_ref[...]
    imp_t = jnp.zeros((SEL_SLOTS, NSA_QB), F32)
    for piece in _split3(psum):
        imp_t = imp_t + _dot(ovt, piece)
    slot_i = lax.broadcasted_iota(jnp.int32, (SEL_SLOTS, NSA_QB), 0)
    tq_l = t0 + lax.broadcasted_iota(jnp.int32, (SEL_SLOTS, NSA_QB), 1)
    cur = tq_l // SEL_BLOCK
    forced = jnp.where(slot_i == 0, 1, 0) + jnp.where(slot_i == cur, 1, 0) + jnp.where(slot_i == cur - 1, 1, 0)
    valid = slot_i * SEL_BLOCK <= tq_l
    key = jnp.where(forced > 0, -3e38, jnp.where(valid, imp_t, NEG_BIG))
    slot_f = slot_i.astype(F32)
    bias_t = jnp.where(forced > 0, 0.0, NEG_BIG)
    for _ in range(SEL_TOPK - _N_FORCED):
        mx = jnp.max(key, axis=0, keepdims=True)
        first = jnp.min(jnp.where(key == mx, slot_f, 1e9), axis=0, keepdims=True)
        pick = slot_f == first
        bias_t = jnp.where(pick, 0.0, bias_t)
        key = jnp.where(pick, -3e38, key)

    n_wt = WINDOW // KEY_TILE + 1
    assert WINDOW % KEY_TILE == 0 and n_wt == 3
    kk = lax.broadcasted_iota(jnp.int32, (KEY_TILE, 1), 0)
    qq = tq - t0
    tri = kk <= qq
    tri_far = kk <= jnp.where(i >= 2, qq, KEY_TILE)
    w_idx = [jnp.maximum(i - 2, 0), jnp.maximum(i - 1, 0), i]
    s_w = [_dot(kw_ref[pl.ds(pl.multiple_of(w * KEY_TILE, KEY_TILE), KEY_TILE), :], qt) for w in w_idx]
    s_w = jnp.concatenate([jnp.where(tri_far, NEG_BIG, s_w[0]),
                           jnp.where(i >= 1, s_w[1], NEG_BIG),
                           jnp.where(tri, s_w[2], NEG_BIG)], axis=0)
    p_w, l_w = softmax0(s_w)
    p_w = p_w.astype(BF16)
    o_wt = _dot(vwt_ref[w_idx[0]], p_w[0:KEY_TILE])
    for k in range(1, n_wt):
        o_wt = o_wt + _dot(vwt_ref[w_idx[k]], p_w[k * KEY_TILE:(k + 1) * KEY_TILE])
    o_wt = o_wt * (1.0 / l_w)

    qt_aug = jnp.concatenate([qt, jnp.concatenate([bias_t.astype(BF16)] * A_HPG, axis=1)], axis=0)
    acc_ref[...] = jnp.zeros_like(acc_ref)
    m_ref[...] = jnp.full(m_ref.shape, 0.5 * NEG_BIG, F32)
    l_ref[...] = jnp.zeros_like(l_ref)

    def sel_body(base, causal):
        n = len(causal)
        m, l = m_ref[...], l_ref[...]
        gsz = min(SEL_GROUP, max(n // 2, 1))
        groups = [list(range(g0, min(g0 + gsz, n))) for g0 in range(0, n, gsz)]

        def scores(ks):
            out = {}
            for k in ks:
                sk = _dot(kaug_ref[base + k], qt_aug)
                if causal[k]:
                    kpos = (base + k) * KEY_TILE + lax.broadcasted_iota(jnp.int32, (KEY_TILE, 1), 0)
                    sk = jnp.where(kpos <= tq, sk, NEG_BIG)
                out[k] = sk
            return out

        s = scores(groups[0])
        for gi, ks in enumerate(groups):
            if gi + 1 < len(groups):
                s.update(scores(groups[gi + 1]))
            m_new = m
            for k in ks:
                m_new = jnp.maximum(m_new, jnp.max(s[k], axis=0, keepdims=True))
            alpha = jnp.exp2(m - m_new)
            l = alpha * l
            pv = None
            for k in ks:
                p = jnp.exp2(s.pop(k) - m_new)
                l = l + jnp.sum(p, axis=0, keepdims=True)
                d = _dot(vt_ref[base + k], p.astype(BF16))
                pv = d if pv is None else pv + d
            acc_ref[...] = alpha * acc_ref[...] + pv
            m = m_new
        m_ref[...] = m
        l_ref[...] = l

    n_bulk = i // SEL_BODY
    rem = i % SEL_BODY

    def bulk(b, c):
        sel_body(b * SEL_BODY, (False,) * SEL_BODY)
        return c
    lax.fori_loop(0, n_bulk, bulk, 0)

    base = n_bulk * SEL_BODY
    part = SEL_BODY // 2
    while part >= 2:
        pl.when(rem & part != 0)(functools.partial(sel_body, base, (False,) * part))
        base = base + (rem & part)
        part //= 2

    @pl.when(rem % 2 == 1)
    def _():
        sel_body(i - 1, (False, True))

    @pl.when(rem % 2 == 0)
    def _():
        sel_body(i, (True,))

    o_st = acc_ref[...] * (1.0 / l_ref[...])

    gates_t = gate_ref[...].astype(F32).T
    per_group = 3 * A_HPG

    def gate(h, b):
        r = 3 * h + b
        return jnp.where(grp == 0, gates_t[r:r + 1], gates_t[per_group + r:per_group + r + 1])

    for h in range(A_HPG):
        c = slice(h * NSA_QB, (h + 1) * NSA_QB)
        o = gate(h, 0) * o_ct[:, c] + gate(h, 1) * o_st[:, c] + gate(h, 2) * o_wt[:, c]
        o_ref[:, h * HEAD_DIM:(h + 1) * HEAD_DIM] = o.T.astype(o_ref.dtype)


def _overlap_matrix(seq):
    n_chunk = seq // CMP_STRIDE
    cs = np.arange(n_chunk)[:, None] * CMP_STRIDE
    ss = np.arange(SEL_SLOTS)[None, :] * SEL_BLOCK
    ov = np.maximum(np.minimum(cs + CMP_BLOCK, ss + SEL_BLOCK) - np.maximum(cs, ss), 0) / CMP_BLOCK
    n_cmp = (seq - CMP_BLOCK) // CMP_STRIDE + 1
    ov[n_cmp:] = 0.0
    ov[:, seq // SEL_BLOCK:] = 0.0
    return ov.astype(np.float32)


def _nsa(a, cmp_kv, cast=(), layer=0):
    S = a.shape[0]
    G = A_KV_GROUPS
    steps = S // NSA_QB
    for w in cast:
        assert w.shape[1] % (steps * 2 * SUBLANES) == 0
    slab = lambda g, i: jnp.where(g == 0, i, steps - 1)
    cast_in = [pl.BlockSpec((1, w.shape[1] // steps, w.shape[2]), lambda g, i: (layer, slab(g, i), 0))
               for w in cast]
    cast_out = [pl.BlockSpec((w.shape[1] // steps, w.shape[2]), lambda g, i: (slab(g, i), 0)) for w in cast]
    cast_shape = [jax.ShapeDtypeStruct(w.shape[1:], BF16) for w in cast]
    n_tiles = S // KEY_TILE
    assert S // SEL_BLOCK <= SEL_SLOTS and S % KEY_TILE == 0 and S >= WINDOW + NSA_QB
    assert NSA_QB == KEY_TILE and G == 2
    n_chunk = S // CMP_STRIDE
    ovt = jnp.asarray(_overlap_matrix(S).T, BF16)
    col = lambda b: pl.BlockSpec((S, HEAD_DIM), lambda g, i, b=b: (0, b + g))
    out = pl.pallas_call(
        functools.partial(_nsa_kernel, seq=S, n_cast=len(cast)),
        grid=(G, steps),
        in_specs=[
            pl.BlockSpec((NSA_QB, A_HPG * HEAD_DIM), lambda g, i: (i, g)),
            col(BLK_KS), col(BLK_KW), col(BLK_VS), col(BLK_VW),
            pl.BlockSpec((1, 1, n_chunk, HEAD_DIM), lambda g, i: (0, g, 0, 0)),
            pl.BlockSpec((1, 1, n_chunk, HEAD_DIM), lambda g, i: (1, g, 0, 0)),
            pl.BlockSpec((SEL_SLOTS, n_chunk), lambda g, i: (0, 0)),
            pl.BlockSpec((NSA_QB, LANES), lambda g, i: (i, BLK_GATE)),
        ] + cast_in,
        out_specs=[pl.BlockSpec((NSA_QB, A_HPG * HEAD_DIM), lambda g, i: (i, g))] + cast_out,
        out_shape=[jax.ShapeDtypeStruct((S, A_WIDTH), BF16)] + cast_shape,
        scratch_shapes=[pltpu.VMEM((n_tiles, KEY_TILE, HEAD_DIM + SEL_SLOTS), BF16),
                        pltpu.VMEM((n_tiles, HEAD_DIM, KEY_TILE), BF16),
                        pltpu.VMEM((n_tiles, HEAD_DIM, KEY_TILE), BF16),
                        pltpu.VMEM((HEAD_DIM, n_chunk), BF16),
                        pltpu.VMEM((HEAD_DIM, A_HPG * NSA_QB), F32),
                        pltpu.VMEM((1, A_HPG * NSA_QB), F32),
                        pltpu.VMEM((1, A_HPG * NSA_QB), F32)],
        compiler_params=_cparams(("arbitrary", "arbitrary")),
        name="nsa",
    )(a, a, a, a, a, cmp_kv, cmp_kv, ovt, a, *cast)
    return out[0] if not cast else out


_ROW_MASKS = 18
_SCORE_MASKS = 10
HG_HEADS_PER_STEP = 8


def _hgrn_tables():
    t = np.arange(HG_CHUNK)
    p4, i4, i16 = t % 4, (t // 4) % 4, t // 16
    conds = [p4 >= 1, p4 >= 2, p4 >= 3, p4 <= 2, p4 <= 1, p4 == 0,
             i4 >= 1, i4 >= 2, i4 >= 3, i4 <= 2, i4 <= 1, i4 == 0,
             i16 >= 1, i16 >= 2, i16 >= 3, i16 <= 2, i16 <= 1, i16 == 0]
    rm = np.stack([np.broadcast_to(c[:, None], (HG_CHUNK, HG_DIM)) for c in conds]).astype(np.float32)
    tt, ss = t[:, None], t[None, :]
    lag16 = tt // 16 - ss // 16
    lag4 = tt // 4 - ss // 4
    same16, same4 = lag16 == 0, lag4 == 0
    sm = [lag16 == 1, lag16 == 2, lag16 == 3,
          same16 & (lag4 == 1), same16 & (lag4 == 2), same16 & (lag4 == 3),
          same4 & (tt - ss == 0), same4 & (tt - ss == 1), same4 & (tt - ss == 2), same4 & (tt - ss == 3)]
    return rm, np.stack(sm).astype(np.float32)


def _hgrn_kernel(q_ref, f_ref, i_ref, og_ref, loglb_ref, log1m_ref, gn_ref, rm_ref, sm_ref, *rest, tb, nh, n_cast,
                 with_ada):
    cast_in, rest = rest[:n_cast], rest[n_cast:]
    if with_ada:
        (c_ref, wa_ref, ba_ref), rest = rest[:3], rest[3:]
    o_ref, rest = rest[0], rest[1:]
    cast_out, rest = rest[:n_cast], rest[n_cast:]
    if with_ada:
        ada_ref, rest = rest[0], rest[1:]
        cc = c_ref[...]
        ada_ref[...] = jnp.sum(cc * jax.nn.sigmoid(cc) * wa_ref[0], axis=0, keepdims=True) + ba_ref[0]
    st_ref, g_scr, k_scr, o_scr = rest
    for src, dst in zip(cast_in, cast_out):
        dst[...] = src[0].astype(dst.dtype)
    C = HG_CHUNK
    heads = [slice(h * HG_DIM, (h + 1) * HG_DIM) for h in range(nh)]

    @pl.when(pl.program_id(1) == 0)
    def _():
        st_ref[...] = jnp.zeros_like(st_ref)

    x = f_ref[...]
    log_sig = jnp.minimum(x, 0.0) - jnp.log(1.0 + jnp.exp(-jnp.abs(x)))
    a = loglb_ref[...]
    c = log1m_ref[...] + log_sig
    log_f = jnp.maximum(a, c) + jnp.log(1.0 + jnp.exp(-jnp.abs(a - c)))
    g_scr[...] = log_f
    k_scr[...] = jnp.exp(log1m_ref[...] + (log_sig - x))

    def rmask(n):
        return jnp.concatenate([rm_ref[n]] * nh, axis=1)

    def roll(v, s):
        return pltpu.roll(v, s % C, axis=0)

    def b16(z):
        return z.astype(BF16)

    def chunk(ci, carry):
        r0 = pl.multiple_of(ci * C, C)
        g = g_scr[pl.ds(r0, C), :]
        q = q_ref[pl.ds(r0, C), :]
        k = k_scr[pl.ds(r0, C), :]
        v = i_ref[pl.ds(r0, C), :].astype(BF16)
        pre4 = g + rmask(0) * roll(g, 1) + rmask(1) * roll(g, 2) + rmask(2) * roll(g, 3)
        suf4 = rmask(3) * roll(g, -1) + rmask(4) * roll(g, -2) + rmask(5) * roll(g, -3)
        tot4 = pre4 + suf4
        a1, a2, a3 = roll(tot4, 4), roll(tot4, 8), roll(tot4, 12)
        pre16 = pre4 + rmask(6) * a1 + rmask(7) * a2 + rmask(8) * a3
        suf16 = suf4 + rmask(9) * roll(tot4, -4) + rmask(10) * roll(tot4, -8) + rmask(11) * roll(tot4, -12)
        tot16 = pre16 + suf16
        d1, d2, d3 = roll(tot16, 16), roll(tot16, 32), roll(tot16, 48)
        pre64 = pre16 + rmask(12) * d1 + rmask(13) * d2 + rmask(14) * d3
        suf64 = suf16 + rmask(15) * roll(tot16, -16) + rmask(16) * roll(tot16, -32) + rmask(17) * roll(tot16, -48)
        tot64 = pre64 + suf64
        g1, g2 = roll(g, 1), roll(g, 2)

        q_a = q * jnp.exp(pre16)
        lhs_a = [b16(q_a), b16(q_a * jnp.exp(d1)), b16(q_a * jnp.exp(d1 + d2))]
        k_a = b16(k * jnp.exp(suf16))
        q_b = q * jnp.exp(pre4)
        lhs_b = [b16(q_b), b16(q_b * jnp.exp(a1)), b16(q_b * jnp.exp(a1 + a2))]
        k_b = b16(k * jnp.exp(suf4))
        q_c = q * jnp.exp(g)
        lhs_c = [b16(q), b16(q_c), b16(q_c * jnp.exp(g1)), b16(q_c * jnp.exp(g1 + g2))]
        k_c = b16(k)
        q_s = b16(q * jnp.exp(pre64))
        k_s = b16(k * jnp.exp(suf64))
        decay = jnp.exp(tot64[0:1, :])

        def level(lhs, rhs):
            return [_dot_nt(jnp.concatenate([z[:, hh] for z in lhs], axis=0), rhs[:, hh]) for hh in heads]

        s_a, s_b, s_c = level(lhs_a, k_a), level(lhs_b, k_b), level(lhs_c, k_c)
        o_inter = [_dot_nt(q_s[:, hh], b16(st_ref[h])) for h, hh in enumerate(heads)]
        kv = [_dot_tn(v[:, hh], k_s[:, hh]) for hh in heads]
        for h, hh in enumerate(heads):
            sa, sb, sc = s_a[h], s_b[h], s_c[h]
            scores = (sm_ref[0] * sa[0:C] + sm_ref[1] * sa[C:2 * C] + sm_ref[2] * sa[2 * C:3 * C]
                      + sm_ref[3] * sb[0:C] + sm_ref[4] * sb[C:2 * C] + sm_ref[5] * sb[2 * C:3 * C]
                      + sm_ref[6] * sc[0:C] + sm_ref[7] * sc[C:2 * C] + sm_ref[8] * sc[2 * C:3 * C]
                      + sm_ref[9] * sc[3 * C:4 * C])
            o_scr[pl.ds(r0, C), hh] = _dot(b16(scores), v[:, hh]) + o_inter[h]
            st_ref[h] = st_ref[h] * decay[:, hh] + kv[h]
        return carry

    lax.fori_loop(0, tb // C, chunk, 0)

    for hh in heads:
        o = o_scr[:, hh]
        ms = jnp.mean(o * o, axis=-1, keepdims=True)
        og = og_ref[:, hh]
        y = o * lax.rsqrt(ms + NORM_EPS) * gn_ref[:, hh] * (og * jax.nn.sigmoid(og))
        o_ref[:, hh] = y.astype(o_ref.dtype)


def _hgrn(hz, loglb, log1mlb, gnorm, cast=(), layer=0, ada=None):
    S = hz.shape[0]
    tb = HG_ROWS if S % HG_ROWS == 0 else S
    nh = HG_HEADS_PER_STEP
    ng = HG_HEADS // nh
    steps = S // tb
    for w in cast:
        assert w.shape[1] % (steps * 2 * SUBLANES) == 0
    slab = lambda h, c: jnp.where(h == 0, c, steps - 1)
    cast_in = [pl.BlockSpec((1, w.shape[1] // steps, w.shape[2]), lambda h, c: (layer, slab(h, c), 0))
               for w in cast]
    cast_out = [pl.BlockSpec((w.shape[1] // steps, w.shape[2]), lambda h, c: (slab(h, c), 0)) for w in cast]
    cast_shape = [jax.ShapeDtypeStruct(w.shape[1:], BF16) for w in cast]
    ada_args, ada_in, ada_out, ada_shape = [], [], [], []
    if ada is not None:
        c, w_ada, b_ada, la = ada
        D, n_ada = w_ada.shape[1], w_ada.shape[2]
        ck = n_ada // steps
        assert n_ada % steps == 0 and ck % LANES == 0
        ada_args = [c.reshape(D, 1), w_ada, b_ada.reshape(b_ada.shape[0], 1, n_ada)]
        ada_in = [pl.BlockSpec((D, 1), lambda h, c: (0, 0)),
                  pl.BlockSpec((1, D, ck), lambda h, c: (la, 0, slab(h, c))),
                  pl.BlockSpec((1, 1, ck), lambda h, c: (la, 0, slab(h, c)))]
        ada_out = [pl.BlockSpec((1, ck), lambda h, c: (0, slab(h, c)))]
        ada_shape = [jax.ShapeDtypeStruct((1, n_ada), F32)]
    w = nh * HG_DIM
    rm, sm = _hgrn_tables()
    blk = lambda part: pl.BlockSpec((tb, w), lambda h, c, part=part: (c, part * ng + h))
    row = pl.BlockSpec((1, w), lambda h, c: (0, h))
    out = pl.pallas_call(
        functools.partial(_hgrn_kernel, tb=tb, nh=nh, n_cast=len(cast), with_ada=ada is not None),
        grid=(ng, steps),
        in_specs=[blk(0), blk(1), blk(2), blk(3), row, row, row,
                  pl.BlockSpec((_ROW_MASKS, HG_CHUNK, HG_DIM), lambda h, c: (0, 0, 0)),
                  pl.BlockSpec((_SCORE_MASKS, HG_CHUNK, HG_CHUNK), lambda h, c: (0, 0, 0))] + cast_in + ada_in,
        out_specs=[pl.BlockSpec((tb, w), lambda h, c: (c, h))] + cast_out + ada_out,
        out_shape=[jax.ShapeDtypeStruct((S, HG_WIDTH), BF16)] + cast_shape + ada_shape,
        scratch_shapes=[pltpu.VMEM((nh, HG_DIM, HG_DIM), F32), pltpu.VMEM((tb, w), F32),
                        pltpu.VMEM((tb, w), F32), pltpu.VMEM((tb, w), F32)],
        compiler_params=_cparams(("arbitrary", "arbitrary")),
        name="hgrn",
    )(hz, hz, hz, hz, loglb, log1mlb, gnorm, jnp.asarray(rm), jnp.asarray(sm), *cast, *ada_args)
    return out[0] if len(out) == 1 else out


_MERGE_PART = 512


def _merge_kernel(at_ref, hg_ref, ga_ref, gh_ref, wa_ref, wh_ref, wo_ref, x_ref, gt_ref, gp_ref, o_ref):
    at, hg = at_ref[...], hg_ref[...]
    d = wo_ref.shape[1]
    parts = [slice(c0, c0 + _MERGE_PART) for c0 in range(0, d, _MERGE_PART)]
    branch = [(_dot(at, wa_ref[0, :, cs]), _dot(hg, wh_ref[0, :, cs])) for cs in parts]
    y = None
    for cs, (ma, mh) in zip(parts, branch):
        mixed = ga_ref[:, cs].astype(F32) * ma + gh_ref[:, cs].astype(F32) * mh
        yp = _dot(mixed.astype(BF16), wo_ref[0, cs, :])
        y = yp if y is None else y + yp
    ms = jnp.mean(y * y, axis=-1, keepdims=True)
    yn = y * lax.rsqrt(ms + NORM_EPS) * gp_ref[...]
    o_ref[...] = x_ref[...] + gt_ref[...] * yn


def _merge(attn, hg, mg, wa, wh, wo, layer, x, gt, gpost):
    S, D = x.shape
    tm = MERGE_ROWS
    const = lambda shape: pl.BlockSpec((1,) + shape, lambda m: (layer, 0, 0), pipeline_mode=pl.Buffered(1))
    row = pl.BlockSpec((1, D), lambda m: (0, 0))
    return pl.pallas_call(
        _merge_kernel,
        grid=(S // tm,),
        in_specs=[
            pl.BlockSpec((tm, A_WIDTH), lambda m: (m, 0)),
            pl.BlockSpec((tm, HG_WIDTH), lambda m: (m, 0)),
            pl.BlockSpec((tm, D), lambda m: (m, 0)),
            pl.BlockSpec((tm, D), lambda m: (m, 1)),
            const((A_WIDTH, D)), const((HG_WIDTH, D)), const((D, D)),
            pl.BlockSpec((tm, D), lambda m: (m, 0)),
            row, row,
        ],
        out_specs=pl.BlockSpec((tm, D), lambda m: (m, 0)),
        out_shape=jax.ShapeDtypeStruct((S, D), F32),
        compiler_params=_cparams(("arbitrary",)),
        name="merge",
    )(attn, hg, mg, mg, wa, wh, wo, x, gt, gpost.reshape(1, D))


_HALO = 16
_FF_SUB = 256


def _norm_mod(x, g_ref, sc_ref, sh_ref):
    ms = jnp.mean(x * x, axis=-1, keepdims=True)
    y = x * lax.rsqrt(ms + NORM_EPS) * g_ref[...]
    return (y * (1.0 + sc_ref[...]) + sh_ref[...]).astype(BF16)


def _ffn_kernel(x_ref, xh_ref, g_ref, sc_ref, sh_ref, wg_ref, wv_ref, cwg_ref, cwv_ref, cbg_ref, cbv_ref,
                wd_ref, gt_ref, gp_ref, *rest, tm, emit_next):
    if emit_next:
        gn_ref, scn_ref, shn_ref, o_ref, hn_ref, h_scr, ug_scr, uv_scr = rest
    else:
        o_ref, h_scr, ug_scr, uv_scr = rest
    m = pl.program_id(0)
    f = pl.program_id(1)
    norm_mod = functools.partial(_norm_mod, g_ref=g_ref, sc_ref=sc_ref, sh_ref=sh_ref)

    @pl.when(f == 0)
    def _():
        h_scr[0:_HALO, :] = jnp.where(m > 0, norm_mod(xh_ref[...]), jnp.zeros((), BF16))
        h_scr[_HALO:_HALO + tm, :] = norm_mod(x_ref[...])

    tf = wg_ref.shape[2]
    subs = [slice(s * _FF_SUB, (s + 1) * _FF_SUB) for s in range(tf // _FF_SUB)]
    h = h_scr[...]
    ups = [(_dot(h, wg_ref[0, :, cs]), _dot(h, wv_ref[0, :, cs])) for cs in subs]

    def conv(u_scr, cw_ref, cb_ref, cs):
        out = cb_ref[:, cs] + cw_ref[0:1, cs] * u_scr[pl.ds(_HALO - 2, tm), cs]
        out = out + cw_ref[1:2, cs] * u_scr[pl.ds(_HALO - 1, tm), cs]
        return out + cw_ref[2:3, cs] * u_scr[pl.ds(_HALO, tm), cs]

    acc = jnp.where(f > 0, o_ref[...], 0.0)
    for cs, (ug, uv) in zip(subs, ups):
        ug_scr[:, cs] = ug
        uv_scr[:, cs] = uv
        cg = conv(ug_scr, cwg_ref, cbg_ref, cs)
        cv = conv(uv_scr, cwv_ref, cbv_ref, cs)
        act = (cg * jax.nn.sigmoid(cg) * cv).astype(BF16)
        acc = acc + _dot(act, wd_ref[0, cs, :])
    o_ref[...] = acc

    @pl.when(f == pl.num_programs(1) - 1)
    def _():
        y = o_ref[...]
        ms = jnp.mean(y * y, axis=-1, keepdims=True)
        yn = y * lax.rsqrt(ms + NORM_EPS) * gp_ref[...]
        x_new = x_ref[...] + gt_ref[...] * yn
        o_ref[...] = x_new
        if emit_next:
            hn_ref[...] = _norm_mod(x_new, gn_ref, scn_ref, shn_ref)


def _ffn(x, gpre, sc, sh, wup, convw, convb, wdown, layer, gt, gpost, nxt=None):
    S, D = x.shape
    F = wdown.shape[1]
    tm = FFN_ROWS if S % FFN_ROWS == 0 else S
    tf = FFN_COLS
    nf = F // tf
    hb = tm // _HALO
    row = pl.BlockSpec((1, D), lambda m, f: (0, 0))
    tile = pl.BlockSpec((tm, D), lambda m, f: (m, 0))
    extra = () if nxt is None else (nxt[0].reshape(1, D), nxt[1], nxt[2])
    return pl.pallas_call(
        functools.partial(_ffn_kernel, tm=tm, emit_next=nxt is not None),
        grid=(S // tm, nf),
        in_specs=[
            pl.BlockSpec((tm, D), lambda m, f: (m, 0)),
            pl.BlockSpec((_HALO, D), lambda m, f: (jnp.maximum(m * hb - 1, 0), 0)),
            row, row, row,
            pl.BlockSpec((1, D, tf), lambda m, f: (layer, 0, f)),
            pl.BlockSpec((1, D, tf), lambda m, f: (layer, 0, nf + f)),
            pl.BlockSpec((CONV_WIDTH, tf), lambda m, f: (0, f)),
            pl.BlockSpec((CONV_WIDTH, tf), lambda m, f: (0, nf + f)),
            pl.BlockSpec((1, tf), lambda m, f: (0, f)),
            pl.BlockSpec((1, tf), lambda m, f: (0, nf + f)),
            pl.BlockSpec((1, tf, D), lambda m, f: (layer, f, 0)),
            row, row,
        ] + ([row] * 3 if nxt is not None else []),
        out_specs=[tile, tile] if nxt is not None else tile,
        out_shape=([jax.ShapeDtypeStruct((S, D), F32), jax.ShapeDtypeStruct((S, D), BF16)]
                   if nxt is not None else jax.ShapeDtypeStruct((S, D), F32)),
        scratch_shapes=[pltpu.VMEM((tm + _HALO, D), BF16),
                        pltpu.VMEM((tm + _HALO, tf), F32), pltpu.VMEM((tm + _HALO, tf), F32)],
        compiler_params=_cparams(("arbitrary", "arbitrary")),
        name="ffn",
    )(x, x, gpre.reshape(1, D), sc, sh, wup, wup, convw, convw, convb.reshape(1, -1), convb.reshape(1, -1),
      wdown, gt, gpost.reshape(1, D), *extra)


def _rope_tables(positions):
    inv_freq = 1.0 / (ROPE_THETA ** (jnp.arange(0, HEAD_DIM, 2, dtype=F32) / HEAD_DIM))
    ang = positions.astype(F32)[:, None] * inv_freq
    cos, sin = jnp.cos(ang), jnp.sin(ang)
    return jnp.concatenate([cos, cos], axis=-1), jnp.concatenate([-sin, sin], axis=-1)


def kernel(x, c, positions, w_ada, b_ada, g_pre_mix, w_in, pe_kc, w_kc, pe_vc, w_vc, lb_logits, g_hg_norm,
           w_br_attn, w_br_hgrn, w_out, g_post_mix, g_pre_ffn, w_up, conv_w, conv_b, w_down, g_post_ffn):
    B, S, D = x.shape
    assert B == 1, "kernel is written for one sequence"
    L = w_ada.shape[0]
    xs = x[0]
    cosf, sinf = _rope_tables(positions[0])
    scale = HEAD_DIM ** -0.5 * np.log2(np.e)
    tables = (cosf * scale, sinf * scale, cosf, sinf)
    lb_cum = jnp.cumsum(jax.nn.softmax(lb_logits.astype(F32), axis=0), axis=0)
    lower = lb_cum - lb_cum[0:1]
    log_lb = jnp.log(lower)
    log_1m = jnp.log1p(-lower)
    split = lambda ada_l: [ada_l[:, j * D:(j + 1) * D] for j in range(N_ADA)]
    mods = split(_ada_all(c, w_ada, b_ada, 1)[0])

    w_in_t = jnp.swapaxes(w_in, 1, 2)
    hg_roles = (("id",) * (TAIL_TILE // LANES),) * (4 * HG_WIDTH // TAIL_TILE)
    mg_roles = (("sg",) * (TAIL_TILE // LANES),) * (2 * D // TAIL_TILE)


    h = _prenorm(xs, g_pre_mix[0], mods[1], mods[0])
    for l in range(L):
        sh1, sc1, gt1, sh2, sc2, gt2 = mods

        a = _proj(h, w_in_t, l, 0, A_ROLES, BF16, tn=A_TILE, tables=tables, name="proj_attn")
        hz = _proj(h, w_in_t, l, A_COLS, hg_roles, F32, tn=TAIL_TILE, name="proj_hgrn")
        mg = _proj(h, w_in_t, l, A_COLS + 4 * HG_WIDTH, mg_roles, BF16, tn=TAIL_TILE, name="proj_merge_gate",
                   ada=(c, w_ada, b_ada, l + 1) if l + 1 < L else None)
        mods_next = None
        if l + 1 < L:
            mg, ada_next = mg
            mods_next = split(ada_next)

        cmp_kv = _compress(a, jnp.stack([pe_kc[l], pe_vc[l]]), jnp.stack([w_kc[l], w_vc[l]]))
        attn, wup_b, wdown_b = _nsa(a, cmp_kv, cast=(w_up, w_down), layer=l)
        hg, wa_b, wh_b, wo_b = _hgrn(hz, log_lb[l:l + 1], log_1m[l:l + 1], g_hg_norm[l:l + 1],
                                     cast=(w_br_attn, w_br_hgrn, w_out), layer=l)
        xs = _merge(attn, hg, mg, wa_b[None], wh_b[None], wo_b[None], 0, xs, gt1, g_post_mix[l])

        nxt = (g_pre_mix[l + 1], mods_next[1], mods_next[0]) if l + 1 < L else None
        out = _ffn(xs, g_pre_ffn[l], sc2, sh2, wup_b[None], conv_w[l], conv_b[l], wdown_b[None], 0, gt2,
                   g_post_ffn[l], nxt)
        xs, h = out if nxt is not None else (out, None)
        mods = mods_next
    return xs[None]
```
